```python
import jax
import jax.numpy as jnp
from jax import lax
import numpy as np

D_MODEL = 2048
BATCH = 4
SEQ = 4096
DEPTH = 2

GRID_W = 64
CTX_LEN = 256
F32 = jnp.float32
NORM_EPS = 1e-6

N_Q_HEADS = 8
N_KV_HEADS = 2
HEAD_DIM = 128
ATTN_W = N_Q_HEADS * HEAD_DIM
KV_W = N_KV_HEADS * HEAD_DIM
ROPE_THETA = 10000.0
Q_BLOCK = 128
ATTN_SCALE = HEAD_DIM ** -0.5

POOL_WINDOWS = (2, 4, 8, 16)
POOL_GROUPS = len(POOL_WINDOWS)
POOL_W = 1024
POOL_GW = POOL_W // POOL_GROUPS

RWKV_W = 1024
RWKV_HEAD = 64
RWKV_HEADS = RWKV_W // RWKV_HEAD
DECAY_LORA = 64
AAA_LORA = 64
GATE_LORA = 320
N_DIR = 2
GN_EPS = 64e-5

N_BRANCH = 3

N_EXPERTS = 16
EC_FACTOR = 2
EXPERT_FF = 1024

COL_Q = 0
COL_K = COL_Q + ATTN_W
COL_V = COL_K + KV_W
COL_POOL = COL_V + KV_W
COL_R = COL_POOL + POOL_W
COL_GATE = COL_R + 3 * RWKV_W + N_DIR * DECAY_LORA + N_DIR * AAA_LORA + GATE_LORA
N_SHIFT = COL_GATE - COL_R
N_IN = COL_GATE + N_BRANCH * D_MODEL

kernel_name = 'hybrid_gqa_pool_rwkv7_ec_moe_prefix_dit'


def rms_norm(x, g):
    xf = x.astype(F32)
    y = xf * lax.rsqrt(jnp.mean(xf * xf, axis=-1, keepdims=True) + NORM_EPS)
    return (y * g.astype(F32)).astype(x.dtype)


def modulated_norm(x, g, shift, scale):
    return rms_norm(x, g) * (1.0 + scale) + shift


def axial_rope_tables(n_lat):
    rows = n_lat // GRID_W
    row = jnp.repeat(jnp.arange(rows), GRID_W).astype(F32)
    col = (jnp.arange(rows * GRID_W) % GRID_W).astype(F32)
    half = HEAD_DIM // 2
    inv = ROPE_THETA ** (-jnp.arange(0, half, 2, dtype=F32) / half)
    ang = jnp.stack([row[:, None] * inv, col[:, None] * inv], axis=1)
    return jnp.cos(ang), jnp.sin(ang)


def apply_axial_rope(x, cos, sin):
    lead = x.shape[:-1]
    xs = x.astype(F32).reshape(lead + (2, 2, HEAD_DIM // 4))
    bshape = (1, x.shape[1]) + (1,) * (x.ndim - 3) + (2, HEAD_DIM // 4)
    cs, sn = cos.reshape(bshape), sin.reshape(bshape)
    x1, x2 = xs[..., 0, :], xs[..., 1, :]
    out = jnp.stack([x1 * cs - x2 * sn, x2 * cs + x1 * sn], axis=-2)
    return out.reshape(x.shape).astype(x.dtype)


def blocked_attention(q, k, v):
    B, Tq = q.shape[:2]
    nb = Tq // Q_BLOCK
    qb = jnp.swapaxes(q.reshape((B, nb, Q_BLOCK) + q.shape[2:]), 0, 1)

    def one_block(qblk):
        s = jnp.einsum('bqkgd,bskd->bkgqs', qblk, k).astype(F32) * ATTN_SCALE
        pr = jax.nn.softmax(s, axis=-1).astype(v.dtype)
        return jnp.einsum('bkgqs,bskd->bqkgd', pr, v)

    o = lax.map(one_block, qb)
    return jnp.swapaxes(o, 0, 1).reshape(B, Tq, ATTN_W)


def multiscale_pool(u, w_group, scale):
    B, T, _ = u.shape
    uf = u.astype(F32).reshape(B, T, POOL_GROUPS, POOL_GW)
    cs = jnp.pad(jnp.cumsum(uf, axis=1), ((0, 0), (1, 0), (0, 0), (0, 0)))
    t = jnp.arange(T)
    groups = []
    for gi, win in enumerate(POOL_WINDOWS):
        lo = jnp.clip(t - win // 2, 0, T)
        hi = jnp.clip(t + win - win // 2, 0, T)
        csg = cs[:, :, gi]
        mean = (csg[:, hi] - csg[:, lo]) / (hi - lo).astype(F32)[None, :, None]
        groups.append(mean - uf[:, :, gi])
    pooled = jnp.stack(groups, axis=2).astype(u.dtype)
    y = jnp.einsum('btgc,gcd->btgd', pooled, w_group).reshape(B, T, POOL_W)
    return y * scale


def token_shift(u, mu):
    prev = jnp.pad(u[:, :-1], ((0, 0), (1, 0), (0, 0)))
    nxt = jnp.pad(u[:, 1:], ((0, 0), (0, 1), (0, 0)))
    return u + (0.5 * (prev + nxt) - u) * mu


def _rwkv7_step(state, inp):
    r, w, k, v, kk, b = inp
    sa = jnp.einsum('bhij,bhj->bhi', state, -kk)
    state = state * w[:, :, None, :] + sa[..., None] * b[:, :, None, :] + v[..., None] * k[:, :, None, :]
    return state, jnp.einsum('bhij,bhj->bhi', state, r)


def rwkv7_mixer(u, n_ctx, mu, w0, w2, a0, a2, g2, k_k, k_a, r_k, ln_w, ln_b):
    B, T, _ = u.shape
    u = jnp.concatenate([token_shift(u[:, :n_ctx], mu), token_shift(u[:, n_ctx:], mu)], axis=1)
    o1 = 3 * RWKV_W
    o2 = o1 + N_DIR * DECAY_LORA
    o3 = o2 + N_DIR * AAA_LORA
    r, k, v, wd, ad, gd = jnp.split(u, [RWKV_W, 2 * RWKV_W, o1, o2, o3], axis=-1)
    wd = wd.reshape(B, T, N_DIR, DECAY_LORA)
    ad = ad.reshape(B, T, N_DIR, AAA_LORA)
    w_log = -jax.nn.softplus(-(w0 + jnp.einsum('btzr,zrc->btzc', jnp.tanh(wd), w2))) - 0.5
    decay = jnp.exp(-jnp.exp(w_log.astype(F32)))
    a = jax.nn.sigmoid(a0 + jnp.einsum('btzr,zrc->btzc', ad, a2)).astype(F32)
    g = jax.nn.sigmoid(gd) @ g2

    def heads(z):
        return z.reshape(z.shape[:-1] + (RWKV_HEADS, RWKV_HEAD))

    rf, kf, vf = r.astype(F32), k.astype(F32), v.astype(F32)
    kk = heads(kf * k_k)
    kk = kk / jnp.maximum(jnp.sqrt(jnp.sum(kk * kk, axis=-1, keepdims=True)), 1e-12)
    k_dir = heads(kf[:, :, None] * (1.0 + (a - 1.0) * k_a))
    b_dir = kk[:, :, None] * heads(a)
    decay = heads(decay)
    r_h, v_h = heads(rf), heads(vf)

    s0 = jnp.zeros((B, RWKV_HEADS, RWKV_HEAD, RWKV_HEAD), F32)
    y = 0.0
    for d, rev in enumerate((False, True)):
        seq = tuple(jnp.moveaxis(z, 1, 0) for z in
                    (r_h, decay[:, :, d], k_dir[:, :, d], v_h, kk, b_dir[:, :, d]))
        s_ctx, y_ctx = lax.scan(_rwkv7_step, s0, tuple(z[:n_ctx] for z in seq), reverse=rev)
        _, y_lat = lax.scan(_rwkv7_step, s_ctx, tuple(z[n_ctx:] for z in seq), reverse=rev)
        y = y + jnp.concatenate([y_ctx, y_lat], axis=0)
    y = jnp.moveaxis(y, 0, 1)
    mean = jnp.mean(y, axis=-1, keepdims=True)
    var = jnp.mean(jnp.square(y - mean), axis=-1, keepdims=True)
    yn = ((y - mean) * lax.rsqrt(var + GN_EPS)).reshape(B, T, RWKV_W) * ln_w + ln_b
    bonus = jnp.sum(jnp.sum(r_h[:, :, None] * k_dir * r_k, axis=-1), axis=2)[..., None] * v_h
    out = (yn + bonus.reshape(B, T, RWKV_W)) * g
    return out.astype(u.dtype)


def token_mixing(p, n_ctx, cos, sin, keep_ctx, q_norm, k_norm, w_attn_o, w_pool_group, pool_scale, w_pool_o,
                 rwkv_mu, rwkv_w0, rwkv_w2, rwkv_a0, rwkv_a2, rwkv_g2, rwkv_k_k, rwkv_k_a, rwkv_r_k,
                 rwkv_ln_w, rwkv_ln_b, w_rwkv_o, w_out):
    B, T, _ = p.shape
    G = N_Q_HEADS // N_KV_HEADS
    q = rms_norm(p[..., COL_Q:COL_K].reshape(B, T, N_KV_HEADS, G, HEAD_DIM), q_norm)
    k = rms_norm(p[..., COL_K:COL_V].reshape(B, T, N_KV_HEADS, HEAD_DIM), k_norm)
    v = p[..., COL_V:COL_POOL].reshape(B, T, N_KV_HEADS, HEAD_DIM)
    q_lat = apply_axial_rope(q[:, n_ctx:], cos, sin)
    k_lat = apply_axial_rope(k[:, n_ctx:], cos, sin)
    k_all = jnp.concatenate([k_lat, k[:, :n_ctx]], axis=1)
    v_all = jnp.concatenate([v[:, n_ctx:], v[:, :n_ctx]], axis=1)
    attn = blocked_attention(q_lat, k_all, v_all)
    u_pool = p[..., COL_POOL:COL_R]
    pool = multiscale_pool(u_pool[:, n_ctx:], w_pool_group, pool_scale)
    rwkv = rwkv7_mixer(p[..., COL_R:COL_GATE], n_ctx, rwkv_mu, rwkv_w0, rwkv_w2, rwkv_a0, rwkv_a2, rwkv_g2,
                       rwkv_k_k, rwkv_k_a, rwkv_r_k, rwkv_ln_w, rwkv_ln_b)
    if keep_ctx:
        attn = jnp.concatenate([blocked_attention(q[:, :n_ctx], k[:, :n_ctx], v[:, :n_ctx]), attn], axis=1)
        pool = jnp.concatenate([multiscale_pool(u_pool[:, :n_ctx], w_group=w_pool_group, scale=pool_scale), pool], axis=1)
        start = 0
    else:
        rwkv = rwkv[:, n_ctx:]
        start = n_ctx
    gates = jax.nn.sigmoid(p[:, start:, COL_GATE:].reshape(B, T - start, N_BRANCH, D_MODEL))
    merged = (gates[:, :, 0] * (attn @ w_attn_o) + gates[:, :, 1] * (pool @ w_pool_o)
              + gates[:, :, 2] * (rwkv @ w_rwkv_o))
    return merged @ w_out


def expert_choice_ffn(h, w_router, w_gate, w_up, w_down):
    B, n, D = h.shape
    cap = EC_FACTOR * n // N_EXPERTS
    aff = jax.nn.softmax(jnp.einsum('bnd,de->bne', h, w_router).astype(F32), axis=-1)
    top_aff, idx = lax.top_k(jnp.swapaxes(aff, 1, 2), cap)
    xe = jax.vmap(lambda hb, ib: hb[ib])(h, idx)
    hid = jax.nn.silu(jnp.einsum('becd,edf->becf', xe, w_gate)) * jnp.einsum('becd,edf->becf', xe, w_up)
    ye = jnp.einsum('becf,efd->becd', hid, w_down) * top_aff[..., None].astype(h.dtype)
    return jax.vmap(lambda ib, yb: jnp.zeros((n, D), h.dtype).at[ib.reshape(-1)].add(yb.reshape(-1, D)))(idx, ye)


def setup_inputs(seed: int = 0) -> dict:
    key = jax.random.key(seed)
    keys = iter(jax.random.split(key, 48))

    def nrm(shape, std):
        return jax.random.normal(next(keys), shape, F32) * std

    def unif(shape, lo, hi):
        return jax.random.uniform(next(keys), shape, F32, lo, hi)

    L, D, E = DEPTH, D_MODEL, N_EXPERTS
    return {
        'x': nrm((BATCH, SEQ, D), 1.0),
        'c': nrm((BATCH, D), 1.0),
        'ctx': nrm((BATCH, CTX_LEN, D), 1.0),
        'c_ctx': nrm((D,), 1.0),
        'w_mod': nrm((L, D, 6 * D), 0.5 * D ** -0.5),
        'b_mod': nrm((L, 6 * D), 0.02),
        'norm_pre': 1.0 + nrm((L, 2, D), 0.02),
        'norm_post': 1.0 + nrm((L, 2, D), 0.02),
        'w_in': nrm((L, D, N_IN), D ** -0.5),
        'q_norm': 1.0 + nrm((L, HEAD_DIM), 0.02),
        'k_norm': 1.0 + nrm((L, HEAD_DIM), 0.02),
        'w_attn_o': nrm((L, ATTN_W, D), ATTN_W ** -0.5),
        'w_pool_group': nrm((L, POOL_GROUPS, POOL_GW, POOL_GW), POOL_GW ** -0.5),
        'pool_scale': 1.0 + nrm((L, POOL_W), 0.1),
        'w_pool_o': nrm((L, POOL_W, D), POOL_W ** -0.5),
        'rwkv_mu': unif((L, N_SHIFT), 0.0, 1.0),
        'rwkv_w0': unif((L, N_DIR, RWKV_W), -5.0, -0.5),
        'rwkv_w2': nrm((L, N_DIR, DECAY_LORA, RWKV_W), 0.1 * DECAY_LORA ** -0.5),
        'rwkv_a0': nrm((L, N_DIR, RWKV_W), 0.5),
        'rwkv_a2': nrm((L, N_DIR, AAA_LORA, RWKV_W), 0.5 * AAA_LORA ** -0.5),
        'rwkv_g2': nrm((L, GATE_LORA, RWKV_W), GATE_LORA ** -0.5),
        'rwkv_k_k': 1.0 + nrm((L, RWKV_W), 0.1),
        'rwkv_k_a': 1.0 + nrm((L, RWKV_W), 0.1),
        'rwkv_r_k': nrm((L, RWKV_HEADS, RWKV_HEAD), 0.1),
        'rwkv_ln_w': 1.0 + nrm((L, RWKV_W), 0.02),
        'rwkv_ln_b': nrm((L, RWKV_W), 0.02),
        'w_rwkv_o': nrm((L, RWKV_W, D), RWKV_W ** -0.5),
        'w_out': nrm((L, D, D), D ** -0.5),
        'w_router': nrm((L, D, E), D ** -0.5),
        'w_exp_gate': nrm((L, E, D, EXPERT_FF), D ** -0.5),
        'w_exp_up': nrm((L, E, D, EXPERT_FF), D ** -0.5),
        'w_exp_down': nrm((L, E, EXPERT_FF, D), EXPERT_FF ** -0.5),
    }


def reference(x, c, ctx, c_ctx, w_mod, b_mod, norm_pre, norm_post, w_in, q_norm, k_norm, w_attn_o,
              w_pool_group, pool_scale, w_pool_o, rwkv_mu, rwkv_w0, rwkv_w2, rwkv_a0, rwkv_a2, rwkv_g2,
              rwkv_k_k, rwkv_k_a, rwkv_r_k, rwkv_ln_w, rwkv_ln_b, w_rwkv_o, w_out, w_router,
              w_exp_gate, w_exp_up, w_exp_down):
    B, n_lat, _ = x.shape
    n_ctx = ctx.shape[1]
    cos, sin = axial_rope_tables(n_lat)
    for l in range(DEPTH):
        keep_ctx = l < DEPTH - 1
        m_lat = (jax.nn.silu(c) @ w_mod[l] + b_mod[l]).reshape(B, 1, 6, D_MODEL)
        m_ctx = (jax.nn.silu(c_ctx) @ w_mod[l] + b_mod[l]).reshape(1, 1, 6, D_MODEL)
        h = jnp.concatenate([
            modulated_norm(ctx, norm_pre[l, 0], m_ctx[:, :, 0], m_ctx[:, :, 1]),
            modulated_norm(x, norm_pre[l, 0], m_lat[:, :, 0], m_lat[:, :, 1])], axis=1)
        p = h @ w_in[l]
        mix = token_mixing(p, n_ctx, cos, sin, keep_ctx, q_norm[l], k_norm[l], w_attn_o[l], w_pool_group[l],
                           pool_scale[l], w_pool_o[l], rwkv_mu[l], rwkv_w0[l], rwkv_w2[l], rwkv_a0[l],
                           rwkv_a2[l], rwkv_g2[l], rwkv_k_k[l], rwkv_k_a[l], rwkv_r_k[l], rwkv_ln_w[l],
                           rwkv_ln_b[l], w_rwkv_o[l], w_out[l])
        x = x + m_lat[:, :, 2] * rms_norm(mix[:, mix.shape[1] - n_lat:], norm_post[l, 0])
        f_lat = expert_choice_ffn(modulated_norm(x, norm_pre[l, 1], m_lat[:, :, 3], m_lat[:, :, 4]),
                                  w_router[l], w_exp_gate[l], w_exp_up[l], w_exp_down[l])
        x = x + m_lat[:, :, 5] * rms_norm(f_lat, norm_post[l, 1])
        if keep_ctx:
            ctx = ctx + m_ctx[:, :, 2] * rms_norm(mix[:, :n_ctx], norm_post[l, 0])
            f_ctx = expert_choice_ffn(modulated_norm(ctx, norm_pre[l, 1], m_ctx[:, :, 3], m_ctx[:, :, 4]),
                                      w_router[l], w_exp_gate[l], w_exp_up[l], w_exp_down[l])
            ctx = ctx + m_ctx[:, :, 5] * rms_norm(f_ctx, norm_post[l, 1])
    return x
```

```python
import functools

import jax
import jax.numpy as jnp
from jax import lax
from jax.experimental import pallas as pl
from jax.experimental.pallas import tpu as pltpu

F32 = jnp.float32
BF16 = jnp.bfloat16
HIGHEST = lax.Precision.HIGHEST

GRID_W = 64
NORM_EPS = 1e-6
ROPE_THETA = 10000.0
POOL_WINDOWS = (2, 4, 8, 16)
GN_EPS = 64e-5
EC_FACTOR = 2
N_DIR = 2
N_BRANCH = 3
RWKV_CHUNK = 64
POOL_HALO = 16

V7X_LANES = 128
V7X_MXU = 256
VMEM_LIMIT = 52 * 1024 * 1024

NT = (((1,), (1,)), ((), ()))
TN = (((0,), (0,)), ((), ()))


def _pick(n, cands):
    for c in cands:
        if c <= n and n % c == 0:
            return c
    return n


def _cparams(*sem):
    return pltpu.CompilerParams(dimension_semantics=sem, vmem_limit_bytes=VMEM_LIMIT)


def _bdot(x, y, dn=None):
    x = x.astype(BF16)
    y = y.astype(BF16)
    if dn is None:
        return jnp.dot(x, y, preferred_element_type=F32)
    return lax.dot_general(x, y, dn, preferred_element_type=F32)


def _mm_kernel(a_ref, w_ref, o_ref):
    o_ref[...] = _bdot(a_ref[...], w_ref[...]).astype(o_ref.dtype)


def _mm(a, w, out_dtype):
    M, K = a.shape
    N = w.shape[1]
    tm = _pick(M, (1024, 512, 256, 128, 64, 32, 16, 8))
    tn = _pick(N, (1024, 768, 512, 384, 256, 128))
    return pl.pallas_call(
        _mm_kernel,
        out_shape=jax.ShapeDtypeStruct((M, N), out_dtype),
        grid=(N // tn, M // tm),
        in_specs=[pl.BlockSpec((tm, K), lambda j, i: (i, 0)),
                  pl.BlockSpec((K, tn), lambda j, i: (0, j))],
        out_specs=pl.BlockSpec((tm, tn), lambda j, i: (i, j)),
        compiler_params=_cparams("parallel", "parallel"),
        name="proj",
    )(a, w)


def _mod_kernel(s_ref, w_ref, b_ref, o_ref):
    o_ref[0] = jnp.dot(s_ref[...], w_ref[0], precision=HIGHEST,
                       preferred_element_type=F32) + b_ref[0]


def _modulation(s, w_mod, b_mod):
    L, D, N = w_mod.shape
    R = s.shape[0]
    tn = _pick(N, (1024, 512, 256, 128))
    return pl.pallas_call(
        _mod_kernel,
        out_shape=jax.ShapeDtypeStruct((L, R, N), F32),
        grid=(L, N // tn),
        in_specs=[pl.BlockSpec((R, D), lambda l, j: (0, 0)),
                  pl.BlockSpec((1, D, tn), lambda l, j: (l, 0, j)),
                  pl.BlockSpec((1, 1, tn), lambda l, j: (l, 0, j))],
        out_specs=pl.BlockSpec((1, R, tn), lambda l, j: (l, 0, j)),
        compiler_params=_cparams("parallel", "parallel"),
        name="adaln_mod",
    )(s, w_mod, b_mod.reshape(L, 1, N))


def _attn_kernel(q_ref, k_ref, v_ref, o_ref):
    s = _bdot(q_ref[0], k_ref[0], NT)
    m = jnp.max(s, axis=-1, keepdims=True)
    p = jnp.exp(s - m)
    l = jnp.sum(p, axis=-1, keepdims=True)
    o_ref[0] = (_bdot(p, v_ref[0]) / l).astype(o_ref.dtype)


def _attention(q, k, v, q_start, n_q, n_k, head_dim):
    B = q.shape[0]
    hq = q.shape[2] // head_dim
    group = hq // (k.shape[2] // head_dim)
    tq = next(t for t in (256, 128, 64, 32, 16) if n_q % t == 0 and q_start % t == 0)
    off = q_start // tq
    return pl.pallas_call(
        _attn_kernel,
        out_shape=jax.ShapeDtypeStruct((B, n_q, hq * head_dim), BF16),
        grid=(B, hq, n_q // tq),
        in_specs=[pl.BlockSpec((1, tq, head_dim), lambda b, h, i: (b, i + off, h)),
                  pl.BlockSpec((1, n_k, head_dim), lambda b, h, i: (b, 0, h // group)),
                  pl.BlockSpec((1, n_k, head_dim), lambda b, h, i: (b, 0, h // group))],
        out_specs=pl.BlockSpec((1, tq, head_dim), lambda b, h, i: (b, i, h)),
        compiler_params=_cparams("parallel", "parallel", "parallel"),
        name="attention",
    )(q, k, v)


def _pool_kernel(prev_ref, cur_ref, next_ref, wg_ref, sc_ref, o_ref, *,
                 tt, n_ctx_tiles, n_tiles, gw, windows):
    i = pl.program_id(1)
    is_ctx = i < n_ctx_tiles
    seg_start = jnp.where(is_ctx, 0, n_ctx_tiles)
    seg_tiles = jnp.where(is_ctx, n_ctx_tiles, n_tiles - n_ctx_tiles)
    ti = i - seg_start
    has_prev = ti > 0
    has_next = ti < seg_tiles - 1
    t_seg = seg_tiles * tt
    tpos = ti * tt + lax.broadcasted_iota(jnp.int32, (tt, 1), 0)

    cur = cur_ref[0]
    prev = prev_ref[0]
    nxt = next_ref[0]
    d_cur = (lax.broadcasted_iota(jnp.int32, (tt, tt), 1)
             - lax.broadcasted_iota(jnp.int32, (tt, tt), 0))
    d_halo = (lax.broadcasted_iota(jnp.int32, (tt, POOL_HALO), 1)
              - lax.broadcasted_iota(jnp.int32, (tt, POOL_HALO), 0))
    d_prev = d_halo - POOL_HALO
    d_next = d_halo + tt
    for g, win in enumerate(windows):
        lo_off = -(win // 2)
        hi_off = win - win // 2 - 1
        sl = slice(g * gw, (g + 1) * gw)
        band_c = ((d_cur >= lo_off) & (d_cur <= hi_off)).astype(BF16)
        band_p = ((d_prev >= lo_off) & (d_prev <= hi_off) & has_prev).astype(BF16)
        band_n = ((d_next >= lo_off) & (d_next <= hi_off) & has_next).astype(BF16)
        ug = cur[:, sl]
        tot = _bdot(band_c, ug) + _bdot(band_p, prev[:, sl]) + _bdot(band_n, nxt[:, sl])
        lo = jnp.maximum(tpos + lo_off, 0)
        hi = jnp.minimum(tpos + hi_off + 1, t_seg)
        pooled = tot / (hi - lo).astype(F32) - ug.astype(F32)
        y = _bdot(pooled, wg_ref[g]) * sc_ref[:, sl]
        o_ref[0, :, sl] = y.astype(o_ref.dtype)


def _pool(u, w_group, scale, n_ctx):
    B, T, W = u.shape
    G, gw, _ = w_group.shape
    tt = _pick(n_ctx, (256, 128, 64, 32, 16))
    assert (T - n_ctx) % tt == 0 and tt % POOL_HALO == 0 and max(POOL_WINDOWS) <= POOL_HALO
    n_tiles = T // tt
    hb = tt // POOL_HALO
    last_hb = T // POOL_HALO - 1
    kern = functools.partial(_pool_kernel, tt=tt, n_ctx_tiles=n_ctx // tt, n_tiles=n_tiles,
                             gw=gw, windows=POOL_WINDOWS)
    return pl.pallas_call(
        kern,
        out_shape=jax.ShapeDtypeStruct((B, T, W), BF16),
        grid=(B, n_tiles),
        in_specs=[pl.BlockSpec((1, POOL_HALO, W), lambda b, i: (b, jnp.maximum(i * hb - 1, 0), 0)),
                  pl.BlockSpec((1, tt, W), lambda b, i: (b, i, 0)),
                  pl.BlockSpec((1, POOL_HALO, W), lambda b, i: (b, jnp.minimum((i + 1) * hb, last_hb), 0)),
                  pl.BlockSpec((G, gw, gw), lambda b, i: (0, 0, 0)),
                  pl.BlockSpec((1, W), lambda b, i: (0, 0))],
        out_specs=pl.BlockSpec((1, tt, W), lambda b, i: (b, i, 0)),
        compiler_params=_cparams("parallel", "parallel"),
        name="pool",
    )(u, u, u, w_group, scale.reshape(1, W))


def _rwkv_kernel(r_ref, lw_ref, k_ref, v_ref, kk_ref, b_ref, y_ref, g_ref, *, C, N, QW):
    @pl.when(pl.program_id(1) == 0)
    def _():
        g_ref[...] = jnp.zeros_like(g_ref)

    W = lw_ref.shape[2]
    HQ = QW // N
    SR = HQ * C
    lw = lw_ref[0]
    tri = (lax.broadcasted_iota(jnp.int32, (C, C), 0)
           >= lax.broadcasted_iota(jnp.int32, (C, C), 1)).astype(F32)
    cum = jnp.dot(tri, lw, precision=HIGHEST, preferred_element_type=F32)
    pc = cum[C - 1:C, :]
    p_in = jnp.exp(cum)
    p_inv = jnp.exp(-cum)
    p_hat = jnp.exp(pc - cum)
    k = k_ref[0]
    b = b_ref[0]
    r_t = r_ref[0] * p_in
    k_t = k * p_inv
    b_t = b * p_inv
    a_t = -kk_ref[0] * jnp.exp(cum - lw)
    b_h = b * p_hat
    k_h = k * p_hat
    p_c = jnp.exp(pc)
    v = v_ref[0]

    ri = lax.broadcasted_iota(jnp.int32, (SR, SR), 0)
    ci = lax.broadcasted_iota(jnp.int32, (SR, SR), 1)
    strict = ri > ci
    incl = ri >= ci
    eye = (ri == ci).astype(F32)
    lane_head = lax.broadcasted_iota(jnp.int32, (1, QW), 1) // N
    qi = lax.broadcasted_iota(jnp.int32, (QW, QW), 0)
    qj = lax.broadcasted_iota(jnp.int32, (QW, QW), 1)

    def stack(x):
        return jnp.concatenate([jnp.where(lane_head == h, x, 0.0) for h in range(HQ)], axis=0)

    def unstack(x):
        out = x[0:C]
        for h in range(1, HQ):
            out = out + x[h * C:(h + 1) * C]
        return out

    for q in range(W // QW):
        sl = slice(q * QW, (q + 1) * QW)
        a_s = stack(a_t[:, sl]).astype(BF16)
        r_s = stack(r_t[:, sl])
        b_s = stack(b_t[:, sl]).astype(BF16)
        k_s = stack(k_t[:, sl]).astype(BF16)
        v_s = stack(v[:, sl]).astype(BF16)
        bh_s = stack(b_h[:, sl]).astype(BF16)
        kh_s = stack(k_h[:, sl]).astype(BF16)
        r_sb = r_s.astype(BF16)

        n_ab = jnp.where(strict, _bdot(a_s, b_s, NT), 0.0)
        a_ak = jnp.where(strict, _bdot(a_s, k_s, NT), 0.0)
        a_rb = jnp.where(incl, _bdot(r_sb, b_s, NT), 0.0)
        a_rk = jnp.where(incl, _bdot(r_sb, k_s, NT), 0.0)

        t_inv = eye + n_ab
        pw = n_ab
        span = 2
        while span < C:
            pw = _bdot(pw, pw)
            t_inv = t_inv + _bdot(t_inv, pw)
            span *= 2

        a_hat = _bdot(t_inv, a_s)
        v_hat = _bdot(t_inv, _bdot(a_ak, v_s))
        r_hat = unstack(r_s + _bdot(a_rb, a_hat))
        y_hat = unstack(_bdot(a_rb, v_hat) + _bdot(a_rk, v_s))

        g0 = g_ref[q]
        y_ref[0, :, sl] = _bdot(r_hat, g0) + y_hat
        m_mat = _bdot(bh_s, a_hat, TN) + jnp.where(qi == qj, p_c[:, sl], 0.0)
        z_mat = _bdot(bh_s, v_hat, TN) + _bdot(kh_s, v_s, TN)
        g_ref[q] = _bdot(m_mat, g0) + z_mat


def _rwkv_scan(r, lw, k, v, kk, b, head):
    S, T, W = r.shape
    C = RWKV_CHUNK
    QW = min(W, V7X_MXU)
    kern = functools.partial(_rwkv_kernel, C=C, N=head, QW=QW)
    spec = pl.BlockSpec((1, C, W), lambda s, c: (s, c, 0))
    return pl.pallas_call(
        kern,
        out_shape=jax.ShapeDtypeStruct((S, T, W), F32),
        grid=(S, T // C),
        in_specs=[spec] * 6,
        out_specs=spec,
        scratch_shapes=[pltpu.VMEM((W // QW, QW, QW), F32)],
        compiler_params=_cparams("parallel", "arbitrary"),
        name="rwkv7_chunk",
    )(r, lw, k, v, kk, b)


def _router_kernel(h_ref, w_ref, o_ref, *, n_e):
    logits = jnp.dot(h_ref[...], w_ref[...], precision=HIGHEST, preferred_element_type=F32)
    lane = lax.broadcasted_iota(jnp.int32, logits.shape, 1)
    logits = jnp.where(lane < n_e, logits, -jnp.inf)
    e = jnp.exp(logits - jnp.max(logits, axis=-1, keepdims=True))
    o_ref[...] = e / jnp.sum(e, axis=-1, keepdims=True)


def _router(h, w_router):
    M, D = h.shape
    E = w_router.shape[1]
    wp = jnp.pad(w_router, ((0, 0), (0, V7X_LANES - E)))
    tm = _pick(M, (512, 256, 128, 64, 32, 16, 8))
    aff = pl.pallas_call(
        functools.partial(_router_kernel, n_e=E),
        out_shape=jax.ShapeDtypeStruct((M, V7X_LANES), F32),
        grid=(M // tm,),
        in_specs=[pl.BlockSpec((tm, D), lambda i: (i, 0)),
                  pl.BlockSpec((D, V7X_LANES), lambda i: (0, 0))],
        out_specs=pl.BlockSpec((tm, V7X_LANES), lambda i: (i, 0)),
        compiler_params=_cparams("parallel"),
        name="router",
    )(h, wp)
    return aff[:, :E]


def _excl_prefix(flags, blk):
    n = flags.shape[-1]
    upper = (lax.broadcasted_iota(jnp.int32, (blk, blk), 0)
             < lax.broadcasted_iota(jnp.int32, (blk, blk), 1)).astype(BF16)
    outs = []
    carry = jnp.zeros((flags.shape[0], 1), F32)
    for j in range(n // blk):
        seg = flags[:, j * blk:(j + 1) * blk]
        outs.append(_bdot(seg, upper) + carry)
        carry = carry + jnp.sum(seg, axis=-1, keepdims=True)
    return jnp.concatenate(outs, axis=-1) if len(outs) > 1 else outs[0]


def _topk_kernel(aff_ref, sel_ref, *, cap, blk):
    bits = lax.bitcast_convert_type(aff_ref[0], jnp.int32)
    E = bits.shape[0]

    def body(i, tau):
        cand = tau | jnp.left_shift(jnp.int32(1), 30 - i)
        cnt = jnp.sum((bits >= cand).astype(jnp.int32), axis=-1, keepdims=True)
        return jnp.where(cnt >= cap, cand, tau)

    tau = lax.fori_loop(0, 31, body, jnp.zeros((E, 1), jnp.int32))
    gt = bits > tau
    eq = bits == tau
    need = (cap - jnp.sum(gt.astype(jnp.int32), axis=-1, keepdims=True)).astype(F32)
    eq_rank = _excl_prefix(eq.astype(F32), blk)
    sel = gt | (eq & (eq_rank < need))
    pos = _excl_prefix(sel.astype(F32), blk)
    sel_ref[0] = jnp.where(sel, pos.astype(jnp.int32), -1)


def _topk_slots(aff_t, cap):
    B, E, n = aff_t.shape
    blk = _pick(n, (512, 256, 128))
    return pl.pallas_call(
        functools.partial(_topk_kernel, cap=cap, blk=blk),
        out_shape=jax.ShapeDtypeStruct((B, E, n), jnp.int32),
        grid=(B,),
        in_specs=[pl.BlockSpec((1, E, n), lambda b: (b, 0, 0))],
        out_specs=pl.BlockSpec((1, E, n), lambda b: (b, 0, 0)),
        compiler_params=_cparams("parallel"),
        name="expert_topk",
    )(aff_t)


def _gather_kernel(sel_ref, h_ref, xe_ref, oh_ref, *, cap):
    @pl.when(pl.program_id(2) == 0)
    def _():
        sel = sel_ref[0, 0]
        slot = lax.broadcasted_iota(jnp.int32, (cap, sel.shape[1]), 0)
        oh_ref[...] = (sel == slot).astype(oh_ref.dtype)

    xe_ref[0, 0] = _bdot(oh_ref[...], h_ref[0]).astype(xe_ref.dtype)


def _gather(sel, h, cap):
    B, E, n = sel.shape
    D = h.shape[2]
    td = _pick(D, (512, 256, 128))
    return pl.pallas_call(
        functools.partial(_gather_kernel, cap=cap),
        out_shape=jax.ShapeDtypeStruct((B, E, cap, D), BF16),
        grid=(B, E, D // td),
        in_specs=[pl.BlockSpec((1, 1, 1, n), lambda b, e, j: (b, e, 0, 0)),
                  pl.BlockSpec((1, n, td), lambda b, e, j: (b, 0, j))],
        out_specs=pl.BlockSpec((1, 1, cap, td), lambda b, e, j: (b, e, 0, j)),
        scratch_shapes=[pltpu.VMEM((cap, n), BF16)],
        compiler_params=_cparams("parallel", "parallel", "arbitrary"),
        name="expert_gather",
    )(sel.reshape(B, E, 1, n), h)


def _ffn_kernel(xe_ref, wg_ref, wu_ref, wd_ref, ye_ref):
    xe = xe_ref[0, 0]
    gate = _bdot(xe, wg_ref[0])
    up = _bdot(xe, wu_ref[0])
    hid = gate * jax.nn.sigmoid(gate) * up
    ye_ref[0, 0] = _bdot(hid, wd_ref[0]).astype(ye_ref.dtype)


def _expert_ffn(xe, w_gate, w_up, w_down):
    B, E, cap, D = xe.shape
    FF = w_gate.shape[2]
    return pl.pallas_call(
        _ffn_kernel,
        out_shape=jax.ShapeDtypeStruct((B, E, cap, D), BF16),
        grid=(E, B),
        in_specs=[pl.BlockSpec((1, 1, cap, D), lambda e, b: (b, e, 0, 0)),
                  pl.BlockSpec((1, D, FF), lambda e, b: (e, 0, 0)),
                  pl.BlockSpec((1, D, FF), lambda e, b: (e, 0, 0)),
                  pl.BlockSpec((1, FF, D), lambda e, b: (e, 0, 0))],
        out_specs=pl.BlockSpec((1, 1, cap, D), lambda e, b: (b, e, 0, 0)),
        compiler_params=_cparams("parallel", "parallel"),
        name="expert_ffn",
    )(xe, w_gate, w_up, w_down)


def _scatter_kernel(selc_ref, affc_ref, ye_ref, o_ref, *, cap):
    @pl.when(pl.program_id(2) == 0)
    def _():
        o_ref[...] = jnp.zeros_like(o_ref)

    selc = selc_ref[0, 0]
    onehot = selc == lax.broadcasted_iota(jnp.int32, (selc.shape[0], cap), 1)
    o_ref[0] += affc_ref[0, 0] * _bdot(onehot.astype(BF16), ye_ref[0, 0])


def _scatter(sel, aff_t, ye):
    B, E, n = sel.shape
    cap, D = ye.shape[2:]
    td = _pick(D, (512, 256, 128))
    return pl.pallas_call(
        functools.partial(_scatter_kernel, cap=cap),
        out_shape=jax.ShapeDtypeStruct((B, n, D), F32),
        grid=(B, D // td, E),
        in_specs=[pl.BlockSpec((1, 1, n, 1), lambda b, j, e: (b, e, 0, 0)),
                  pl.BlockSpec((1, 1, n, 1), lambda b, j, e: (b, e, 0, 0)),
                  pl.BlockSpec((1, 1, cap, td), lambda b, j, e: (b, e, 0, j))],
        out_specs=pl.BlockSpec((1, n, td), lambda b, j, e: (b, 0, j)),
        compiler_params=_cparams("parallel", "parallel", "arbitrary"),
        name="expert_scatter",
    )(sel.reshape(B, E, n, 1), aff_t.reshape(B, E, n, 1), ye)


def _expert_choice(h, w_router, w_gate, w_up, w_down):
    B, n, D = h.shape
    E = w_router.shape[1]
    cap = EC_FACTOR * n // E
    aff = _router(h.reshape(B * n, D), w_router).reshape(B, n, E)
    aff_t = jnp.swapaxes(aff, 1, 2)
    sel = _topk_slots(aff_t, cap)
    xe = _gather(sel, h.astype(BF16), cap)
    ye = _expert_ffn(xe, w_gate, w_up, w_down)
    return _scatter(sel, aff_t, ye)


def _rms(x, g):
    return x * lax.rsqrt(jnp.mean(x * x, axis=-1, keepdims=True) + NORM_EPS) * g


def _rope_tables(n_lat, head_dim):
    rows = n_lat // GRID_W
    row = jnp.repeat(jnp.arange(rows), GRID_W).astype(F32)
    col = (jnp.arange(rows * GRID_W) % GRID_W).astype(F32)
    half = head_dim // 2
    inv = ROPE_THETA ** (-jnp.arange(0, half, 2, dtype=F32) / half)
    ar, ac = row[:, None] * inv, col[:, None] * inv
    cos = jnp.concatenate([jnp.cos(ar), jnp.cos(ar), jnp.cos(ac), jnp.cos(ac)], axis=-1)
    sin = jnp.concatenate([-jnp.sin(ar), jnp.sin(ar), -jnp.sin(ac), jnp.sin(ac)], axis=-1)
    return cos, sin


def _rope(x, cos, sin):
    qd = x.shape[-1] // 4
    xs = x.reshape(x.shape[:-1] + (2, 2, qd))
    swapped = xs[..., ::-1, :].reshape(x.shape)
    return x * cos[None, :, None, :] + swapped * sin[None, :, None, :]


def _flip_segments(z, n_ctx):
    return jnp.concatenate([z[:, :n_ctx][:, ::-1], z[:, n_ctx:][:, ::-1]], axis=1)


def _token_shift(u, mu):
    prev = jnp.pad(u[:, :-1], ((0, 0), (1, 0), (0, 0)))
    nxt = jnp.pad(u[:, 1:], ((0, 0), (0, 1), (0, 0)))
    return u + (0.5 * (prev + nxt) - u) * mu


def _rwkv_mixer(u, n_ctx, head, mu, w0, w2, a0, a2, g2, k_k, k_a, r_k, ln_w, ln_b):
    B, T, _ = u.shape
    Wd = k_k.shape[0]
    H = Wd // head
    dl, al = w2.shape[1], a2.shape[1]
    u = jnp.concatenate([_token_shift(u[:, :n_ctx], mu), _token_shift(u[:, n_ctx:], mu)], axis=1)
    o1 = 3 * Wd
    o2 = o1 + N_DIR * dl
    o3 = o2 + N_DIR * al
    r, k, v = u[..., :Wd], u[..., Wd:2 * Wd], u[..., 2 * Wd:o1]
    wd = jnp.tanh(u[..., o1:o2]).reshape(B * T, N_DIR, dl)
    ad = u[..., o2:o3].reshape(B * T, N_DIR, al)
    gd = jax.nn.sigmoid(u[..., o3:]).reshape(B * T, -1)

    lws, ks, bs = [], [], []
    kk = (k * k_k).reshape(B, T, H, head)
    kk = (kk / jnp.maximum(jnp.sqrt(jnp.sum(kk * kk, axis=-1, keepdims=True)), 1e-12)).reshape(B, T, Wd)
    bonus_rk = 0.0
    for z in range(N_DIR):
        w_lin = w0[z] + _mm(wd[:, z].astype(BF16), w2[z].astype(BF16), F32).reshape(B, T, Wd)
        w_log = -jax.nn.softplus(-w_lin) - 0.5
        lws.append(-jnp.exp(w_log))
        a = jax.nn.sigmoid(a0[z] + _mm(ad[:, z].astype(BF16), a2[z].astype(BF16), F32).reshape(B, T, Wd))
        k_dir = k * (1.0 + (a - 1.0) * k_a)
        ks.append(k_dir)
        bs.append(kk * a)
        bonus_rk = bonus_rk + k_dir
    g = _mm(gd.astype(BF16), g2.astype(BF16), F32).reshape(B, T, Wd)

    def both(fwd, bwd):
        return jnp.concatenate([fwd, _flip_segments(bwd, n_ctx)], axis=0)

    y2 = _rwkv_scan(both(r, r), both(lws[0], lws[1]), both(ks[0], ks[1]), both(v, v),
                    both(kk, kk), both(bs[0], bs[1]), head)
    y = (y2[:B] + _flip_segments(y2[B:], n_ctx)).reshape(B, T, H, head)
    mean = jnp.mean(y, axis=-1, keepdims=True)
    var = jnp.mean(jnp.square(y - mean), axis=-1, keepdims=True)
    yn = ((y - mean) * lax.rsqrt(var + GN_EPS)).reshape(B, T, Wd) * ln_w + ln_b
    bonus = jnp.sum((r * bonus_rk).reshape(B, T, H, head) * r_k, axis=-1, keepdims=True)
    bonus = (bonus * v.reshape(B, T, H, head)).reshape(B, T, Wd)
    return (yn + bonus) * g


def kernel(x, c, ctx, c_ctx, w_mod, b_mod, norm_pre, norm_post, w_in, q_norm, k_norm, w_attn_o, w_pool_group, pool_scale, w_pool_o, rwkv_mu, rwkv_w0, rwkv_w2, rwkv_a0, rwkv_a2, rwkv_g2, rwkv_k_k, rwkv_k_a, rwkv_r_k, rwkv_ln_w, rwkv_ln_b, w_rwkv_o, w_out, w_router, w_exp_gate, w_exp_up, w_exp_down):
    B, n_lat, D = x.shape
    n_ctx = ctx.shape[1]
    T = n_ctx + n_lat
    depth = w_mod.shape[0]
    hd = q_norm.shape[1]
    attn_w = w_attn_o.shape[1]
    pool_w = w_pool_o.shape[1]
    n_shift = rwkv_mu.shape[1]
    n_in = w_in.shape[2]
    kv_w = (n_in - attn_w - pool_w - n_shift - N_BRANCH * D) // 2
    head = rwkv_r_k.shape[2]
    col_k = attn_w
    col_v = col_k + kv_w
    col_pool = col_v + kv_w
    col_r = col_pool + pool_w
    col_gate = col_r + n_shift
    hq, hkv = attn_w // hd, kv_w // hd
    scale = hd ** -0.5

    cos, sin = _rope_tables(n_lat, hd)
    s_all = jnp.concatenate([jax.nn.silu(c), jax.nn.silu(c_ctx)[None]], axis=0)
    s_all = jnp.pad(s_all, ((0, (-s_all.shape[0]) % 8), (0, 0)))
    mod = _modulation(s_all, w_mod, b_mod)

    for l in range(depth):
        keep_ctx = l < depth - 1
        m_lat = mod[l, :B].reshape(B, 1, 6, D)
        m_ctx = mod[l, B].reshape(1, 1, 6, D)

        def modnorm(z, g, m, i):
            return _rms(z, g) * (1.0 + m[:, :, i + 1]) + m[:, :, i]

        h = jnp.concatenate([modnorm(ctx, norm_pre[l, 0], m_ctx, 0),
                             modnorm(x, norm_pre[l, 0], m_lat, 0)], axis=1)
        hb = h.astype(BF16).reshape(B * T, D)
        wl = w_in[l]

        def proj(lo, hi, dtype=BF16):
            return _mm(hb, wl[:, lo:hi].astype(BF16), dtype).reshape(B, T, hi - lo)

        q = _rms(proj(0, col_k, F32).reshape(B, T, hq, hd), q_norm[l])
        kx = _rms(proj(col_k, col_v, F32).reshape(B, T, hkv, hd), k_norm[l])
        vx = proj(col_v, col_pool)
        q = jnp.concatenate([q[:, :n_ctx], _rope(q[:, n_ctx:], cos, sin)], axis=1) * scale
        kx = jnp.concatenate([kx[:, :n_ctx], _rope(kx[:, n_ctx:], cos, sin)], axis=1)
        q = q.astype(BF16).reshape(B, T, attn_w)
        kx = kx.astype(BF16).reshape(B, T, kv_w)
        attn = _attention(q, kx, vx, n_ctx, n_lat, T, hd)
        if keep_ctx:
            attn = jnp.concatenate([_attention(q, kx, vx, 0, n_ctx, n_ctx, hd), attn], axis=1)
        pool = _pool(proj(col_pool, col_r), w_pool_group[l].astype(BF16), pool_scale[l], n_ctx)
        n_wide = (n_shift // V7X_LANES) * V7X_LANES
        u_rwkv = jnp.concatenate([proj(col_r, col_r + n_wide), proj(col_r + n_wide, col_gate)], axis=-1)
        rw = _rwkv_mixer(u_rwkv.astype(F32), n_ctx, head, rwkv_mu[l], rwkv_w0[l],
                         rwkv_w2[l], rwkv_a0[l], rwkv_a2[l], rwkv_g2[l], rwkv_k_k[l], rwkv_k_a[l],
                         rwkv_r_k[l], rwkv_ln_w[l], rwkv_ln_b[l])
        start = 0 if keep_ctx else n_ctx
        pool, rw = pool[:, start:], rw[:, start:].astype(BF16)
        Tm = T - start
        gates = jax.nn.sigmoid(proj(col_gate, n_in, F32)[:, start:].reshape(B, Tm, N_BRANCH, D))

        def out_proj(z, w):
            return _mm(z.reshape(B * Tm, -1), w.astype(BF16), F32).reshape(B, Tm, D)

        merged = (gates[:, :, 0] * out_proj(attn, w_attn_o[l]) + gates[:, :, 1] * out_proj(pool, w_pool_o[l])
                  + gates[:, :, 2] * out_proj(rw, w_rwkv_o[l]))
        mix = out_proj(merged.astype(BF16), w_out[l])
        x = x + m_lat[:, :, 2] * _rms(mix[:, Tm - n_lat:], norm_post[l, 0])

        wg, wu, wdn = w_exp_gate[l].astype(BF16), w_exp_up[l].astype(BF16), w_exp_down[l].astype(BF16)
        f_lat = _expert_choice(modnorm(x, norm_pre[l, 1], m_lat, 3), w_router[l], wg, wu, wdn)
        x = x + m_lat[:, :, 5] * _rms(f_lat, norm_post[l, 1])
        if keep_ctx:
            ctx = ctx + m_ctx[:, :, 2] * _rms(mix[:, :n_ctx], norm_post[l, 0])
            f_ctx = _expert_choice(modnorm(ctx, norm_pre[l, 1], m_ctx, 3), w_router[l], wg, wu, wdn)
            ctx = ctx + m_ctx[:, :, 5] * _rms(f_ctx, norm_post[l, 1])
    return x
```

```python
import functools

import jax
import jax.numpy as jnp
from jax import lax
from jax.experimental import pallas as pl
from jax.experimental.pallas import tpu as pltpu

F32 = jnp.float32
BF16 = jnp.bfloat16
HIGHEST = lax.Precision.HIGHEST

GRID_W = 64
NORM_EPS = 1e-6
ROPE_THETA = 10000.0
POOL_WINDOWS = (2, 4, 8, 16)
GN_EPS = 64e-5
EC_FACTOR = 2
N_DIR = 2
N_BRANCH = 3
RWKV_CHUNK = 64
HALO = 16

V7X_LANES = 128
VMEM_LIMIT = 52 * 1024 * 1024

NT = (((1,), (1,)), ((), ()))
TN = (((0,), (0,)), ((), ()))


def _pick(n, cands):
    for c in cands:
        if c <= n and n % c == 0:
            return c
    return n


def _cparams(*sem):
    return pltpu.CompilerParams(dimension_semantics=sem, vmem_limit_bytes=VMEM_LIMIT)


def _bdot(x, y, dn=None):
    x = x.astype(BF16)
    y = y.astype(BF16)
    if dn is None:
        return jnp.dot(x, y, preferred_element_type=F32)
    return lax.dot_general(x, y, dn, preferred_element_type=F32)


def _split_dot(x, y):
    hi = x.astype(BF16)
    lo = x - hi.astype(F32)
    return _bdot(hi, y) + _bdot(lo, y)


def _iota(shape, dim):
    return lax.broadcasted_iota(jnp.int32, shape, dim)


def _row_tile(T):
    for parts in (8, 4, 5, 6, 10, 12, 16, 17, 20, 32, 34):
        if T % parts == 0 and (T // parts) % 16 == 0:
            return T // parts
    return T


def _mod_kernel(s_ref, w_ref, b_ref, o_ref):
    o_ref[0] = jnp.dot(s_ref[...], w_ref[0], precision=HIGHEST,
                       preferred_element_type=F32) + b_ref[0]


def _modulation(s, w_mod, b_mod):
    L, D, N = w_mod.shape
    R = s.shape[0]
    tn = _pick(N, (1024, 512, 256, 128))
    return pl.pallas_call(
        _mod_kernel,
        out_shape=jax.ShapeDtypeStruct((L, R, N), F32),
        grid=(L, N // tn),
        in_specs=[pl.BlockSpec((R, D), lambda l, j: (0, 0)),
                  pl.BlockSpec((1, D, tn), lambda l, j: (l, 0, j)),
                  pl.BlockSpec((1, 1, tn), lambda l, j: (l, 0, j))],
        out_specs=pl.BlockSpec((1, R, tn), lambda l, j: (l, 0, j)),
        compiler_params=_cparams("parallel", "parallel"),
        name="adaln_mod",
    )(s, w_mod, b_mod.reshape(L, 1, N))


def _mod_rows(m_lat_ref, m_ctx_ref, row0, tm, n_lat):
    is_lat = (row0 + _iota((tm, 1), 0)) < n_lat
    return is_lat, m_lat_ref[0], m_ctx_ref[0]


def _modnorm(x, g, is_lat, m_lat, m_ctx, i_shift):
    shift = jnp.where(is_lat, m_lat[i_shift:i_shift + 1], m_ctx[i_shift:i_shift + 1])
    scale = jnp.where(is_lat, m_lat[i_shift + 1:i_shift + 2], m_ctx[i_shift + 1:i_shift + 2])
    y = x * lax.rsqrt(jnp.mean(x * x, axis=-1, keepdims=True) + NORM_EPS) * g
    return y * (1.0 + scale) + shift


def _in_kernel(x_ref, g_ref, ml_ref, mc_ref, w_ref, o_ref, h_ref, *, tm, n_lat):
    @pl.when(pl.program_id(2) == 0)
    def _():
        is_lat, m_lat, m_ctx = _mod_rows(ml_ref, mc_ref, pl.program_id(1) * tm, tm, n_lat)
        h_ref[...] = _modnorm(x_ref[0], g_ref[...], is_lat, m_lat, m_ctx, 0).astype(h_ref.dtype)

    o_ref[0] = _bdot(h_ref[...], w_ref[...]).astype(o_ref.dtype)


def _in_proj(xs, g, mod_l, w, n_lat):
    B, T, D = xs.shape
    N = w.shape[1]
    tm = _row_tile(T)
    tn = _pick(N, (1280, 1024, 896, 768, 640, 512, 384, 256, 128))
    kern = functools.partial(_in_kernel, tm=tm, n_lat=n_lat)
    return pl.pallas_call(
        kern,
        out_shape=jax.ShapeDtypeStruct((B, T, N), BF16),
        grid=(B, T // tm, N // tn),
        in_specs=[pl.BlockSpec((1, tm, D), lambda b, i, j: (b, i, 0)),
                  pl.BlockSpec((1, D), lambda b, i, j: (0, 0)),
                  pl.BlockSpec((1, 6, D), lambda b, i, j: (b, 0, 0)),
                  pl.BlockSpec((1, 6, D), lambda b, i, j: (B, 0, 0)),
                  pl.BlockSpec((D, tn), lambda b, i, j: (0, j))],
        out_specs=pl.BlockSpec((1, tm, tn), lambda b, i, j: (b, i, j)),
        scratch_shapes=[pltpu.VMEM((tm, D), BF16)],
        compiler_params=_cparams("parallel", "parallel", "arbitrary"),
        name="in_proj",
    )(xs, g.reshape(1, D), mod_l, mod_l, w)


def _rot(x, cos, sin):
    qd = x.shape[1] // 4
    first = (_iota((1, x.shape[1]), 1) // qd) % 2 == 0
    swapped = jnp.where(first, pltpu.roll(x, x.shape[1] - qd, 1), pltpu.roll(x, qd, 1))
    return x * cos + swapped * sin


def _head_norm(x, g):
    return x * lax.rsqrt(jnp.mean(x * x, axis=-1, keepdims=True) + NORM_EPS) * g


def _attn_kernel(q_ref, k_ref, v_ref, cq_ref, sq_ref, ck_ref, sk_ref, qn_ref, kn_ref, o_ref,
                 kp_ref, vp_ref, *, hd, group, tq, n_lat, scale):
    i = pl.program_id(2)

    @pl.when(i == 0)
    def _():
        kp_ref[...] = _rot(_head_norm(k_ref[0].astype(F32), kn_ref[...]),
                           ck_ref[...], sk_ref[...]).astype(kp_ref.dtype)
        ones = jnp.ones(v_ref.shape[1:], vp_ref.dtype)
        vp_ref[...] = jnp.concatenate([v_ref[0].astype(vp_ref.dtype), ones], axis=1)

    S = kp_ref.shape[0]
    allowed = (_iota((1, S), 1) >= n_lat) | (i * tq < n_lat)
    for g in range(group):
        q = _head_norm(q_ref[0, :, g * hd:(g + 1) * hd].astype(F32), qn_ref[...])
        q = _rot(q, cq_ref[...], sq_ref[...]) * scale
        s = jnp.where(allowed, _bdot(q, kp_ref[...], NT), -jnp.inf)
        p = jnp.exp(s - jnp.max(s, axis=-1, keepdims=True))
        ov = _bdot(p, vp_ref[...])
        o_ref[0, :, g * hd:(g + 1) * hd] = (ov[:, :hd] / ov[:, hd:hd + 1]).astype(o_ref.dtype)


def _attention(p_qkv, q_col, k_col, v_col, cos, sin, q_norm, k_norm, hd, hq, hkv, n_lat):
    B, T, _ = p_qkv.shape
    group = hq // hkv
    gw = group * hd
    tq = _pick(T - n_lat, (256, 128, 64, 32, 16))
    kern = functools.partial(_attn_kernel, hd=hd, group=group, tq=tq, n_lat=n_lat, scale=hd ** -0.5)
    tab_q = pl.BlockSpec((tq, hd), lambda b, h, i: (i, 0))
    tab_k = pl.BlockSpec((T, hd), lambda b, h, i: (0, 0))
    vec = pl.BlockSpec((1, hd), lambda b, h, i: (0, 0))
    return pl.pallas_call(
        kern,
        out_shape=jax.ShapeDtypeStruct((B, T, hq * hd), BF16),
        grid=(B, hkv, T // tq),
        in_specs=[pl.BlockSpec((1, tq, gw), lambda b, h, i: (b, i, q_col // gw + h)),
                  pl.BlockSpec((1, T, hd), lambda b, h, i: (b, 0, k_col // hd + h)),
                  pl.BlockSpec((1, T, hd), lambda b, h, i: (b, 0, v_col // hd + h)),
                  tab_q, tab_q, tab_k, tab_k, vec, vec],
        out_specs=pl.BlockSpec((1, tq, gw), lambda b, h, i: (b, i, h)),
        scratch_shapes=[pltpu.VMEM((T, hd), BF16), pltpu.VMEM((T, 2 * hd), BF16)],
        compiler_params=_cparams("parallel", "parallel", "arbitrary"),
        name="attention",
    )(p_qkv, p_qkv, p_qkv, cos, sin, cos, sin, q_norm.reshape(1, hd), k_norm.reshape(1, hd))


def _segment_of_tile(i, tt, n_lat_tiles, n_tiles):
    is_lat = i < n_lat_tiles
    ti = jnp.where(is_lat, i, i - n_lat_tiles)
    seg_tiles = jnp.where(is_lat, n_lat_tiles, n_tiles - n_lat_tiles)
    return ti, seg_tiles


def _halo_specs(tt, width, T, col_block):
    hb = tt // HALO
    last = T // HALO - 1
    prev = pl.BlockSpec((1, HALO, width), lambda b, i: (b, jnp.maximum(i * hb - 1, 0), col_block))
    cur = pl.BlockSpec((1, tt, width), lambda b, i: (b, i, col_block))
    nxt = pl.BlockSpec((1, HALO, width), lambda b, i: (b, jnp.minimum((i + 1) * hb, last), col_block))
    return prev, cur, nxt


def _pool_kernel(prev_ref, cur_ref, next_ref, wg_ref, sc_ref, o_ref, *,
                 tt, n_lat_tiles, n_tiles, gw, windows):
    ti, seg_tiles = _segment_of_tile(pl.program_id(1), tt, n_lat_tiles, n_tiles)
    has_prev = ti > 0
    has_next = ti < seg_tiles - 1
    t_seg = seg_tiles * tt
    tpos = ti * tt + _iota((tt, 1), 0)

    cur = cur_ref[0]
    prev = prev_ref[0]
    nxt = next_ref[0]
    d_cur = _iota((tt, tt), 1) - _iota((tt, tt), 0)
    d_halo = _iota((tt, HALO), 1) - _iota((tt, HALO), 0)
    d_prev = d_halo - HALO
    d_next = d_halo + tt
    for g, win in enumerate(windows):
        lo_off = -(win // 2)
        hi_off = win - win // 2 - 1
        sl = slice(g * gw, (g + 1) * gw)
        band_c = ((d_cur >= lo_off) & (d_cur <= hi_off)).astype(BF16)
        band_p = ((d_prev >= lo_off) & (d_prev <= hi_off) & has_prev).astype(BF16)
        band_n = ((d_next >= lo_off) & (d_next <= hi_off) & has_next).astype(BF16)
        ug = cur[:, sl]
        tot = _bdot(band_c, ug) + _bdot(band_p, prev[:, sl]) + _bdot(band_n, nxt[:, sl])
        lo = jnp.maximum(tpos + lo_off, 0)
        hi = jnp.minimum(tpos + hi_off + 1, t_seg)
        pooled = tot / (hi - lo).astype(F32) - ug.astype(F32)
        y = _bdot(pooled, wg_ref[g]) * sc_ref[:, sl]
        o_ref[0, :, sl] = y.astype(o_ref.dtype)


def _pool(p_arr, col_block, w_group, scale, n_lat):
    B, T, _ = p_arr.shape
    G, gw, _ = w_group.shape
    W = G * gw
    tt = _pick(T - n_lat, (256, 128, 64, 32, 16))
    assert n_lat % tt == 0 and tt % HALO == 0 and max(POOL_WINDOWS) <= HALO
    n_tiles = T // tt
    kern = functools.partial(_pool_kernel, tt=tt, n_lat_tiles=n_lat // tt, n_tiles=n_tiles,
                             gw=gw, windows=POOL_WINDOWS)
    prev, cur, nxt = _halo_specs(tt, W, T, col_block)
    return pl.pallas_call(
        kern,
        out_shape=jax.ShapeDtypeStruct((B, T, W), BF16),
        grid=(B, n_tiles),
        in_specs=[prev, cur, nxt,
                  pl.BlockSpec((G, gw, gw), lambda b, i: (0, 0, 0)),
                  pl.BlockSpec((1, W), lambda b, i: (0, 0))],
        out_specs=pl.BlockSpec((1, tt, W), lambda b, i: (b, i, 0)),
        compiler_params=_cparams("parallel", "parallel"),
        name="pool",
    )(p_arr, p_arr, p_arr, w_group, scale.reshape(1, W))


def _head_sum(x, head):
    lanes = x.shape[1]
    blk = min(lanes, V7X_LANES)
    same = (_iota((blk, blk), 0) // head == _iota((blk, blk), 1) // head).astype(BF16)
    parts = [_split_dot(x[:, j:j + blk], same) for j in range(0, lanes, blk)]
    return jnp.concatenate(parts, axis=1) if len(parts) > 1 else parts[0]


def _rwkv_prep_kernel(prev_ref, cur_ref, next_ref, mu_ref, w0_ref, w2_ref, a0_ref, a2_ref, g2_ref,
                      kk_w_ref, ka_ref, rk_ref,
                      r_ref, v_ref, kk_ref, lw_ref, kd_ref, bd_ref, g_ref, bonus_ref, *,
                      tt, n_lat_tiles, n_tiles, W, head, lora):
    ti, seg_tiles = _segment_of_tile(pl.program_id(1), tt, n_lat_tiles, n_tiles)
    u = cur_ref[0].astype(F32)
    row = _iota((tt, 1), 0)
    before = jnp.where(ti > 0, prev_ref[0, HALO - 1:HALO, :].astype(F32), 0.0)
    after = jnp.where(ti < seg_tiles - 1, next_ref[0, 0:1, :].astype(F32), 0.0)
    u_prev = jnp.where(row == 0, before, pltpu.roll(u, 1, 0))
    u_next = jnp.where(row == tt - 1, after, pltpu.roll(u, tt - 1, 0))
    u = u + (0.5 * (u_prev + u_next) - u) * mu_ref[...]

    r, k, v = u[:, :W], u[:, W:2 * W], u[:, 2 * W:3 * W]
    o1 = 3 * W
    o2 = o1 + lora
    o3 = o2 + lora
    w_lin = w0_ref[...] + _bdot(jnp.tanh(u[:, o1:o2]), w2_ref[...])
    w_log = -(jnp.maximum(-w_lin, 0.0) + jnp.log(1.0 + jnp.exp(-jnp.abs(w_lin)))) - 0.5
    lw_ref[0] = -jnp.exp(w_log)
    a = jax.nn.sigmoid(a0_ref[...] + _bdot(u[:, o2:o3], a2_ref[...]))
    g_ref[0] = _bdot(jax.nn.sigmoid(u[:, o3:]), g2_ref[...]).astype(g_ref.dtype)

    kk = k * kk_w_ref[...]
    kk = kk * lax.rsqrt(jnp.maximum(_head_sum(kk * kk, head), 1e-24))
    k_sum = 0.0
    for z in range(N_DIR):
        a_z = a[:, z * W:(z + 1) * W]
        k_z = k * (1.0 + (a_z - 1.0) * ka_ref[...])
        kd_ref[0, :, z * W:(z + 1) * W] = k_z.astype(kd_ref.dtype)
        bd_ref[0, :, z * W:(z + 1) * W] = (kk * a_z).astype(bd_ref.dtype)
        k_sum = k_sum + k_z
    r_ref[0] = r.astype(r_ref.dtype)
    v_ref[0] = v.astype(v_ref.dtype)
    kk_ref[0] = kk.astype(kk_ref.dtype)
    bonus_ref[0] = (_head_sum(r * k_sum * rk_ref[...], head) * v).astype(bonus_ref.dtype)


def _rwkv_prep(p_rwkv, n_lat, W, head, mu, w0, w2, a0, a2, g2, k_k, k_a, r_k):
    B, T, NS = p_rwkv.shape
    lora = N_DIR * w2.shape[1]
    n_gate = NS - 3 * W - 2 * lora
    tt = _pick(T - n_lat, (256, 128, 64, 32, 16))
    n_tiles = T // tt
    def cat(m):
        z = jnp.zeros_like(m[0])
        return jnp.concatenate([jnp.concatenate([m[0], z], axis=1),
                                jnp.concatenate([z, m[1]], axis=1)], axis=0).astype(BF16)
    g2p = jnp.pad(g2, ((0, n_gate - g2.shape[0]), (0, 0))).astype(BF16)
    mup = jnp.pad(mu, (0, NS - mu.shape[0])).reshape(1, NS)
    kern = functools.partial(_rwkv_prep_kernel, tt=tt, n_lat_tiles=n_lat // tt, n_tiles=n_tiles,
                             W=W, head=head, lora=lora)
    prev, cur, nxt = _halo_specs(tt, NS, T, 0)
    full = lambda shp: pl.BlockSpec(shp, lambda b, i: (0,) * len(shp))
    tile = lambda w: pl.BlockSpec((1, tt, w), lambda b, i: (b, i, 0))
    sd = lambda w, dt: jax.ShapeDtypeStruct((B, T, w), dt)
    return pl.pallas_call(
        kern,
        out_shape=(sd(W, BF16), sd(W, BF16), sd(W, BF16), sd(2 * W, F32), sd(2 * W, BF16),
                   sd(2 * W, BF16), sd(W, BF16), sd(W, BF16)),
        grid=(B, n_tiles),
        in_specs=[prev, cur, nxt, full((1, NS)), full((1, 2 * W)), full((lora, 2 * W)),
                  full((1, 2 * W)), full((lora, 2 * W)), full((n_gate, W)),
                  full((1, W)), full((1, W)), full((1, W))],
        out_specs=(tile(W), tile(W), tile(W), tile(2 * W), tile(2 * W), tile(2 * W), tile(W), tile(W)),
        compiler_params=_cparams("parallel", "parallel"),
        name="rwkv_prep",
    )(p_rwkv, p_rwkv, p_rwkv, mup, w0.reshape(1, 2 * W), cat(w2), a0.reshape(1, 2 * W), cat(a2), g2p,
      k_k.reshape(1, W), k_a.reshape(1, W), r_k.reshape(1, W))


def _rwkv_kernel(r_ref, v_ref, kk_ref, lw_ref, k_ref, b_ref, y_ref, g_ref, *, C, N, n_seq):
    @pl.when(pl.program_id(1) == 0)
    def _():
        g_ref[...] = jnp.zeros_like(g_ref)

    W = r_ref.shape[2]
    PW = 2 * N
    sgn = jnp.where(pl.program_id(0) >= n_seq, -1, 1)
    order = (_iota((C, C), 0) - _iota((C, C), 1)) * sgn
    lw = lw_ref[0]
    cum = jnp.dot((order >= 0).astype(F32), lw, precision=HIGHEST, preferred_element_type=F32)
    pc = jnp.sum(lw, axis=0, keepdims=True)
    k = k_ref[0].astype(F32)
    b = b_ref[0].astype(F32)
    p_inv = jnp.exp(-cum)
    p_hat = jnp.exp(pc - cum)
    r_t = r_ref[0].astype(F32) * jnp.exp(cum)
    k_t = k * p_inv
    b_t = b * p_inv
    a_t = -kk_ref[0].astype(F32) * jnp.exp(cum - lw)
    b_h = b * p_hat
    k_h = k * p_hat
    p_c = jnp.exp(pc)
    v = v_ref[0].astype(F32)

    order2 = jnp.concatenate([order, order], axis=1)
    strict2 = order2 > 0
    incl2 = order2 >= 0
    eye = (order == 0).astype(F32)
    first = _iota((1, PW), 1) < N
    lane_lo = _iota((1, 2 * C), 1) < C
    same_head = (_iota((PW, PW), 0) // N) == (_iota((PW, PW), 1) // N)

    for p in range(W // PW):
        sl = slice(p * PW, (p + 1) * PW)
        a_p, r_p, v_p = a_t[:, sl], r_t[:, sl], v[:, sl]
        rhs_bk = jnp.concatenate([b_t[:, sl], k_t[:, sl]], axis=0).astype(BF16)
        lower, ahat, vhat, vmask = [], [], [], []
        for sub in range(2):
            m = first if sub == 0 else jnp.logical_not(first)
            a_m = jnp.where(m, a_p, 0.0)
            v_m = jnp.where(m, v_p, 0.0)
            lhs = jnp.concatenate([a_m, jnp.where(m, r_p, 0.0)], axis=0)
            amat = _bdot(lhs, rhs_bk, NT)
            upper = jnp.where(strict2, amat[:C], 0.0)
            lower.append(jnp.where(incl2, amat[C:], 0.0))
            n_ab = upper[:, :C]
            t_inv = eye + n_ab
            pw = n_ab
            span = 2
            while span < C:
                pw = _bdot(pw, pw)
                t_inv = t_inv + _bdot(t_inv, pw)
                span *= 2
            w1 = _bdot(jnp.where(lane_lo, 0.0, upper), jnp.concatenate([v_m, v_m], axis=0))
            x = _bdot(t_inv, jnp.concatenate([a_m, w1], axis=1))
            ahat.append(x[:, :PW])
            vhat.append(x[:, PW:])
            vmask.append(v_m)

        g0 = g_ref[p]
        u = _bdot(jnp.concatenate(ahat, axis=0), g0) + jnp.concatenate(vhat, axis=0)
        uv = jnp.concatenate([u[:C], vmask[0], u[C:], vmask[1]], axis=0).astype(BF16)
        y_ref[0, 0, :, sl] = _bdot(r_p, g0) + _bdot(jnp.concatenate(lower, axis=1), uv)
        bk_h = jnp.concatenate([b_h[:, sl], k_h[:, sl]], axis=0)
        upd = _bdot(jnp.concatenate([bk_h, bk_h], axis=0), uv, TN)
        decay = jnp.transpose(jnp.broadcast_to(p_c[:, sl], (PW, PW)))
        g_ref[p] = jnp.where(same_head, decay * g0 + upd, 0.0)


def _rwkv_scan(r, v, kk, lw, kd, bd, head, n_lat):
    B, T, W = r.shape
    C = RWKV_CHUNK
    nc = T // C
    nc_lat = n_lat // C

    def chunk(s, c):
        fwd = jnp.where(c < nc - nc_lat, nc_lat + c, c - (nc - nc_lat))
        bwd = jnp.where(c < nc - nc_lat, nc - 1 - c, nc - 1 - c)
        return jnp.where(s >= B, bwd, fwd)

    shared = pl.BlockSpec((1, C, W), lambda s, c: (s % B, chunk(s, c), 0))
    per_dir = pl.BlockSpec((1, C, W), lambda s, c: (s % B, chunk(s, c), s // B))
    kern = functools.partial(_rwkv_kernel, C=C, N=head, n_seq=B)
    return pl.pallas_call(
        kern,
        out_shape=jax.ShapeDtypeStruct((N_DIR, B, T, W), F32),
        grid=(N_DIR * B, nc),
        in_specs=[shared, shared, shared, per_dir, per_dir, per_dir],
        out_specs=pl.BlockSpec((1, 1, C, W), lambda s, c: (s // B, s % B, chunk(s, c), 0)),
        scratch_shapes=[pltpu.VMEM((W // (2 * head), 2 * head, 2 * head), F32)],
        compiler_params=_cparams("parallel", "arbitrary"),
        name="rwkv7_chunk",
    )(r, v, kk, lw, kd, bd)


def _merge_kernel(attn_ref, pool_ref, y_ref, g_ref, bonus_ref, lnw_ref, lnb_ref, gate_ref,
                  wa_ref, wp_ref, wr_ref, o_ref, rw_ref, *, head, tn):
    @pl.when(pl.program_id(2) == 0)
    def _():
        y = y_ref[0, 0] + y_ref[1, 0]
        inv_n = 1.0 / head
        dev = y - _head_sum(y, head) * inv_n
        var = _head_sum(dev * dev, head) * inv_n
        yn = dev * lax.rsqrt(var + GN_EPS) * lnw_ref[...] + lnb_ref[...]
        rw_ref[...] = ((yn + bonus_ref[0].astype(F32)) * g_ref[0].astype(F32)).astype(rw_ref.dtype)

    def gate(z):
        return jax.nn.sigmoid(gate_ref[0, :, z * tn:(z + 1) * tn].astype(F32))

    out = (gate(0) * _bdot(attn_ref[0], wa_ref[...]) + gate(1) * _bdot(pool_ref[0], wp_ref[...])
           + gate(2) * _bdot(rw_ref[...], wr_ref[...]))
    o_ref[0] = out.astype(o_ref.dtype)


def _merge_tile(D):
    return _pick(D, (512, 256, 128))


def _merge(attn, pool, y, g, bonus, ln_w, ln_b, gates, w_a, w_p, w_r, head):
    B, T, _ = attn.shape
    W = g.shape[2]
    D = w_a.shape[1]
    tm = _row_tile(T)
    tn = _merge_tile(D)
    nj = D // tn
    row = lambda w: pl.BlockSpec((1, tm, w), lambda b, i, j: (b, i, 0))
    vec = pl.BlockSpec((1, W), lambda b, i, j: (0, 0))
    wsp = lambda w: pl.BlockSpec((w.shape[0], tn), lambda b, i, j: (0, j))
    return pl.pallas_call(
        functools.partial(_merge_kernel, head=head, tn=tn),
        out_shape=jax.ShapeDtypeStruct((B, T, D), BF16),
        grid=(B, T // tm, nj),
        in_specs=[row(attn.shape[2]), row(pool.shape[2]),
                  pl.BlockSpec((N_DIR, 1, tm, W), lambda b, i, j: (0, b, i, 0)),
                  row(W), row(W), vec, vec,
                  pl.BlockSpec((1, tm, N_BRANCH * tn), lambda b, i, j: (b, i, j)),
                  wsp(w_a), wsp(w_p), wsp(w_r)],
        out_specs=pl.BlockSpec((1, tm, tn), lambda b, i, j: (b, i, j)),
        scratch_shapes=[pltpu.VMEM((tm, W), BF16)],
        compiler_params=_cparams("parallel", "parallel", "arbitrary"),
        name="merge",
    )(attn, pool, y, g, bonus, ln_w.reshape(1, W), ln_b.reshape(1, W), gates, w_a, w_p, w_r)


def _out_kernel(m_ref, w_ref, x_ref, gpost_ref, gpre_ref, ml_ref, mc_ref, wr_ref,
                x_out, h_out, aff_out, *, tm, n_lat, n_e):
    is_lat, m_lat, m_ctx = _mod_rows(ml_ref, mc_ref, pl.program_id(1) * tm, tm, n_lat)
    mix = _bdot(m_ref[0], w_ref[...])
    normed = mix * lax.rsqrt(jnp.mean(mix * mix, axis=-1, keepdims=True) + NORM_EPS) * gpost_ref[...]
    x = x_ref[0] + jnp.where(is_lat, m_lat[2:3], m_ctx[2:3]) * normed
    x_out[0] = x
    h = _modnorm(x, gpre_ref[...], is_lat, m_lat, m_ctx, 3)
    h_out[0] = h.astype(h_out.dtype)
    logits = jnp.dot(h, wr_ref[...], precision=HIGHEST, preferred_element_type=F32)
    logits = jnp.where(_iota(logits.shape, 1) < n_e, logits, -jnp.inf)
    e = jnp.exp(logits - jnp.max(logits, axis=-1, keepdims=True))
    aff_out[0] = e / jnp.sum(e, axis=-1, keepdims=True)


def _out_proj(merged, w_out, xs, g_post, g_pre, mod_l, w_router, n_lat):
    B, T, D = xs.shape
    E = w_router.shape[1]
    tm = _pick(T, (544, 272, 256, 128, 320, 64, 32, 16))
    wr = jnp.pad(w_router, ((0, 0), (0, V7X_LANES - E)))
    row = lambda w: pl.BlockSpec((1, tm, w), lambda b, i: (b, i, 0))
    vec = pl.BlockSpec((1, D), lambda b, i: (0, 0))
    return pl.pallas_call(
        functools.partial(_out_kernel, tm=tm, n_lat=n_lat, n_e=E),
        out_shape=(jax.ShapeDtypeStruct((B, T, D), F32), jax.ShapeDtypeStruct((B, T, D), BF16),
                   jax.ShapeDtypeStruct((B, T, V7X_LANES), F32)),
        grid=(B, T // tm),
        in_specs=[row(D), pl.BlockSpec((D, D), lambda b, i: (0, 0)), row(D), vec, vec,
                  pl.BlockSpec((1, 6, D), lambda b, i: (b, 0, 0)),
                  pl.BlockSpec((1, 6, D), lambda b, i: (B, 0, 0)),
                  pl.BlockSpec((D, V7X_LANES), lambda b, i: (0, 0))],
        out_specs=(row(D), row(D), row(V7X_LANES)),
        compiler_params=_cparams("parallel", "parallel"),
        name="out_proj",
    )(merged, w_out, xs, g_post.reshape(1, D), g_pre.reshape(1, D), mod_l, mod_l, wr)


def _excl_prefix(flags, blk):
    n = flags.shape[-1]
    upper = (_iota((blk, blk), 0) < _iota((blk, blk), 1)).astype(BF16)
    outs = []
    carry = jnp.zeros((flags.shape[0], 1), F32)
    for j in range(n // blk):
        seg = flags[:, j * blk:(j + 1) * blk]
        outs.append(_bdot(seg, upper) + carry)
        carry = carry + jnp.sum(seg, axis=-1, keepdims=True)
    return jnp.concatenate(outs, axis=-1) if len(outs) > 1 else outs[0]


def _topk_kernel(aff_ref, sel_ref, *, cap, blk):
    bits = lax.bitcast_convert_type(aff_ref[0], jnp.int32)
    E = bits.shape[0]

    def body(i, tau):
        cand = tau | jnp.left_shift(jnp.int32(1), 30 - i)
        cnt = jnp.sum((bits >= cand).astype(jnp.int32), axis=-1, keepdims=True)
        return jnp.where(cnt >= cap, cand, tau)

    tau = lax.fori_loop(0, 31, body, jnp.zeros((E, 1), jnp.int32))
    gt = bits > tau
    eq = bits == tau
    need = (cap - jnp.sum(gt.astype(jnp.int32), axis=-1, keepdims=True)).astype(F32)
    eq_rank = _excl_prefix(eq.astype(F32), blk)
    sel = gt | (eq & (eq_rank < need))
    pos = _excl_prefix(sel.astype(F32), blk)
    sel_ref[0] = jnp.where(sel, pos.astype(jnp.int32), -1)


def _topk_slots(aff_t, cap):
    B, E, n = aff_t.shape
    blk = _pick(n, (512, 256, 128))
    return pl.pallas_call(
        functools.partial(_topk_kernel, cap=cap, blk=blk),
        out_shape=jax.ShapeDtypeStruct((B, E, n), jnp.int32),
        grid=(B,),
        in_specs=[pl.BlockSpec((1, E, n), lambda b: (b, 0, 0))],
        out_specs=pl.BlockSpec((1, E, n), lambda b: (b, 0, 0)),
        compiler_params=_cparams("parallel"),
        name="expert_topk",
    )(aff_t)


def _gather_kernel(sel_ref, h_ref, xe_ref, oh_ref, *, cap):
    @pl.when(pl.program_id(2) == 0)
    def _():
        sel = sel_ref[0, 0]
        oh_ref[...] = (sel == _iota((cap, sel.shape[1]), 0)).astype(oh_ref.dtype)

    xe_ref[0, 0] = _bdot(oh_ref[...], h_ref[0]).astype(xe_ref.dtype)


def _gather(sel, h, row_block, cap):
    B, E, n = sel.shape
    D = h.shape[2]
    td = _pick(D, (512, 256, 128))
    return pl.pallas_call(
        functools.partial(_gather_kernel, cap=cap),
        out_shape=jax.ShapeDtypeStruct((B, E, cap, D), BF16),
        grid=(B, E, D // td),
        in_specs=[pl.BlockSpec((1, 1, 1, n), lambda b, e, j: (b, e, 0, 0)),
                  pl.BlockSpec((1, n, td), lambda b, e, j: (b, row_block, j))],
        out_specs=pl.BlockSpec((1, 1, cap, td), lambda b, e, j: (b, e, 0, j)),
        scratch_shapes=[pltpu.VMEM((cap, n), BF16)],
        compiler_params=_cparams("parallel", "parallel", "arbitrary"),
        name="expert_gather",
    )(sel.reshape(B, E, 1, n), h)


def _ffn_kernel(xe_ref, wg_ref, wu_ref, wd_ref, ye_ref):
    xe = xe_ref[0, 0]
    gate = _bdot(xe, wg_ref[0])
    up = _bdot(xe, wu_ref[0])
    hid = gate * jax.nn.sigmoid(gate) * up
    ye_ref[0, 0] = _bdot(hid, wd_ref[0]).astype(ye_ref.dtype)


def _expert_ffn(xe, w_gate, w_up, w_down):
    B, E, cap, D = xe.shape
    FF = w_gate.shape[2]
    return pl.pallas_call(
        _ffn_kernel,
        out_shape=jax.ShapeDtypeStruct((B, E, cap, D), BF16),
        grid=(E, B),
        in_specs=[pl.BlockSpec((1, 1, cap, D), lambda e, b: (b, e, 0, 0)),
                  pl.BlockSpec((1, D, FF), lambda e, b: (e, 0, 0)),
                  pl.BlockSpec((1, D, FF), lambda e, b: (e, 0, 0)),
                  pl.BlockSpec((1, FF, D), lambda e, b: (e, 0, 0))],
        out_specs=pl.BlockSpec((1, 1, cap, D), lambda e, b: (b, e, 0, 0)),
        compiler_params=_cparams("parallel", "parallel"),
        name="expert_ffn",
    )(xe, w_gate, w_up, w_down)


def _scatter_kernel(selc_ref, affc_ref, ye_ref, o_ref, *, cap):
    @pl.when(pl.program_id(2) == 0)
    def _():
        o_ref[...] = jnp.zeros_like(o_ref)

    selc = selc_ref[0, 0]
    onehot = selc == _iota((selc.shape[0], cap), 1)
    o_ref[0] += affc_ref[0, 0] * _bdot(onehot.astype(BF16), ye_ref[0, 0])


def _scatter(sel, aff_t, ye):
    B, E, n = sel.shape
    cap, D = ye.shape[2:]
    td = _pick(D, (512, 256, 128))
    return pl.pallas_call(
        functools.partial(_scatter_kernel, cap=cap),
        out_shape=jax.ShapeDtypeStruct((B, n, D), F32),
        grid=(B, D // td, E),
        in_specs=[pl.BlockSpec((1, 1, n, 1), lambda b, j, e: (b, e, 0, 0)),
                  pl.BlockSpec((1, 1, n, 1), lambda b, j, e: (b, e, 0, 0)),
                  pl.BlockSpec((1, 1, cap, td), lambda b, j, e: (b, e, 0, j))],
        out_specs=pl.BlockSpec((1, n, td), lambda b, j, e: (b, 0, j)),
        compiler_params=_cparams("parallel", "parallel", "arbitrary"),
        name="expert_scatter",
    )(sel.reshape(B, E, n, 1), aff_t.reshape(B, E, n, 1), ye)


def _expert_choice(hb, aff, row_block, n, w_gate, w_up, w_down):
    E = w_gate.shape[0]
    cap = EC_FACTOR * n // E
    aff_t = jnp.swapaxes(aff[:, row_block * n:(row_block + 1) * n, :E], 1, 2)
    sel = _topk_slots(aff_t, cap)
    xe = _gather(sel, hb, row_block, cap)
    ye = _expert_ffn(xe, w_gate, w_up, w_down)
    return _scatter(sel, aff_t, ye)


def _rms(x, g):
    return x * lax.rsqrt(jnp.mean(x * x, axis=-1, keepdims=True) + NORM_EPS) * g


def _rope_tables(n_lat, n_ctx, head_dim):
    rows = n_lat // GRID_W
    row = jnp.repeat(jnp.arange(rows), GRID_W).astype(F32)
    col = (jnp.arange(rows * GRID_W) % GRID_W).astype(F32)
    half = head_dim // 2
    inv = ROPE_THETA ** (-jnp.arange(0, half, 2, dtype=F32) / half)
    ar, ac = row[:, None] * inv, col[:, None] * inv
    cos = jnp.concatenate([jnp.cos(ar), jnp.cos(ar), jnp.cos(ac), jnp.cos(ac)], axis=-1)
    sin = jnp.concatenate([-jnp.sin(ar), jnp.sin(ar), -jnp.sin(ac), jnp.sin(ac)], axis=-1)
    cos = jnp.concatenate([cos, jnp.ones((n_ctx, head_dim), F32)], axis=0)
    sin = jnp.concatenate([sin, jnp.zeros((n_ctx, head_dim), F32)], axis=0)
    return cos, sin


def _pad_cols(w, mult):
    return jnp.pad(w, ((0, 0), (0, (-w.shape[1]) % mult)))


def kernel(x, c, ctx, c_ctx, w_mod, b_mod, norm_pre, norm_post, w_in, q_norm, k_norm, w_attn_o, w_pool_group, pool_scale, w_pool_o, rwkv_mu, rwkv_w0, rwkv_w2, rwkv_a0, rwkv_a2, rwkv_g2, rwkv_k_k, rwkv_k_a, rwkv_r_k, rwkv_ln_w, rwkv_ln_b, w_rwkv_o, w_out, w_router, w_exp_gate, w_exp_up, w_exp_down):
    B, n_lat, D = x.shape
    n_ctx = ctx.shape[1]
    T = n_lat + n_ctx
    depth = w_mod.shape[0]
    hd = q_norm.shape[1]
    attn_w = w_attn_o.shape[1]
    pool_w = w_pool_o.shape[1]
    rwkv_w = w_rwkv_o.shape[1]
    n_shift = rwkv_mu.shape[1]
    n_in = w_in.shape[2]
    kv_w = (n_in - attn_w - pool_w - n_shift - N_BRANCH * D) // 2
    head = rwkv_r_k.shape[2]
    col_k = attn_w
    col_v = col_k + kv_w
    col_pool = col_v + kv_w
    col_r = col_pool + pool_w
    col_gate = col_r + n_shift
    hq, hkv = attn_w // hd, kv_w // hd
    assert n_lat % (hq // hkv * hd) == 0 and pool_w % (hq // hkv * hd) == 0 and n_ctx % RWKV_CHUNK == 0

    cos, sin = _rope_tables(n_lat, n_ctx, hd)
    s_all = jnp.concatenate([jax.nn.silu(c), jax.nn.silu(c_ctx)[None]], axis=0)
    s_all = jnp.pad(s_all, ((0, (-s_all.shape[0]) % 8), (0, 0)))
    mod = _modulation(s_all, w_mod, b_mod)[:, :B + 1].reshape(depth, B + 1, 6, D)
    xs = jnp.concatenate([x, ctx], axis=1)

    for l in range(depth):
        keep_ctx = l < depth - 1
        wl = w_in[l]
        w_a = jnp.concatenate([wl[:, col_pool:col_r], wl[:, :col_pool]], axis=1).astype(BF16)
        w_b = _pad_cols(wl[:, col_r:col_gate], 3 * V7X_LANES).astype(BF16)
        tn_m = _merge_tile(D)
        w_c = (wl[:, col_gate:].reshape(D, N_BRANCH, D // tn_m, tn_m).swapaxes(1, 2)
               .reshape(D, N_BRANCH * D).astype(BF16))
        pre = norm_pre[l, 0]
        p_a = _in_proj(xs, pre, mod[l], w_a, n_lat)
        p_b = _in_proj(xs, pre, mod[l], w_b, n_lat)
        p_c = _in_proj(xs, pre, mod[l], w_c, n_lat)

        attn = _attention(p_a, pool_w, pool_w + attn_w, pool_w + attn_w + kv_w, cos, sin,
                          q_norm[l], k_norm[l], hd, hq, hkv, n_lat)
        pool = _pool(p_a, 0, w_pool_group[l].astype(BF16), pool_scale[l], n_lat)
        r, v, kk, lw, kd, bd, g, bonus = _rwkv_prep(
            p_b, n_lat, rwkv_w, head, rwkv_mu[l], rwkv_w0[l], rwkv_w2[l], rwkv_a0[l], rwkv_a2[l],
            rwkv_g2[l], rwkv_k_k[l], rwkv_k_a[l], rwkv_r_k[l])
        y = _rwkv_scan(r, v, kk, lw, kd, bd, head, n_lat)
        merged = _merge(attn, pool, y, g, bonus, rwkv_ln_w[l], rwkv_ln_b[l], p_c,
                        w_attn_o[l].astype(BF16), w_pool_o[l].astype(BF16), w_rwkv_o[l].astype(BF16), head)
        xs, hb, aff = _out_proj(merged, w_out[l].astype(BF16), xs, norm_post[l, 0], norm_pre[l, 1],
                                mod[l], w_router[l], n_lat)

        wg, wu, wdn = w_exp_gate[l].astype(BF16), w_exp_up[l].astype(BF16), w_exp_down[l].astype(BF16)
        f = _expert_choice(hb, aff, 0, n_lat, wg, wu, wdn)
        m5 = jnp.broadcast_to(mod[l, :B, None, 5], (B, n_lat, D))
        if keep_ctx:
            f = jnp.concatenate([f, _expert_choice(hb, aff, n_lat // n_ctx, n_ctx, wg, wu, wdn)], axis=1)
            m5 = jnp.concatenate([m5, jnp.broadcast_to(mod[l, B, 5], (B, n_ctx, D))], axis=1)
            xs = xs + m5 * _rms(f, norm_post[l, 1])
        else:
            xs = xs[:, :n_lat] + m5 * _rms(f, norm_post[l, 1])
    return xs[:, :n_lat]
```

```python
import functools

import jax
import jax.numpy as jnp
from jax import lax
from jax.experimental import pallas as pl
from jax.experimental.pallas import tpu as pltpu

F32 = jnp.float32
BF16 = jnp.bfloat16
HIGHEST = lax.Precision.HIGHEST

GRID_W = 64
NORM_EPS = 1e-6
ROPE_THETA = 10000.0
POOL_WINDOWS = (2, 4, 8, 16)
GN_EPS = 64e-5
EC_FACTOR = 2
N_DIR = 2
N_BRANCH = 3
RWKV_CHUNK = 64
HALO = 16

V7X_LANES = 128
VMEM_LIMIT = 52 * 1024 * 1024

NT = (((1,), (1,)), ((), ()))
TN = (((0,), (0,)), ((), ()))


def _pick(n, cands):
    for c in cands:
        if c <= n and n % c == 0:
            return c
    return n


def _cparams(*sem):
    return pltpu.CompilerParams(dimension_semantics=sem, vmem_limit_bytes=VMEM_LIMIT)


def _bdot(x, y, dn=None):
    x = x.astype(BF16)
    y = y.astype(BF16)
    if dn is None:
        return jnp.dot(x, y, preferred_element_type=F32)
    return lax.dot_general(x, y, dn, preferred_element_type=F32)


def _split_dot(x, y):
    hi = x.astype(BF16)
    lo = x - hi.astype(F32)
    return _bdot(hi, y) + _bdot(lo, y)


def _iota(shape, dim):
    return lax.broadcasted_iota(jnp.int32, shape, dim)


def _row_tile(T):
    for parts in (8, 4, 5, 6, 10, 12, 16, 17, 20, 32, 34):
        if T % parts == 0 and (T // parts) % 16 == 0:
            return T // parts
    return T


def _mod_kernel(s_ref, w_ref, b_ref, o_ref):
    o_ref[0] = jnp.dot(s_ref[...], w_ref[0], precision=HIGHEST,
                       preferred_element_type=F32) + b_ref[0]


def _modulation(s, w_mod, b_mod):
    L, D, N = w_mod.shape
    R = s.shape[0]
    tn = _pick(N, (1024, 512, 256, 128))
    return pl.pallas_call(
        _mod_kernel,
        out_shape=jax.ShapeDtypeStruct((L, R, N), F32),
        grid=(L, N // tn),
        in_specs=[pl.BlockSpec((R, D), lambda l, j: (0, 0)),
                  pl.BlockSpec((1, D, tn), lambda l, j: (l, 0, j)),
                  pl.BlockSpec((1, 1, tn), lambda l, j: (l, 0, j))],
        out_specs=pl.BlockSpec((1, R, tn), lambda l, j: (l, 0, j)),
        compiler_params=_cparams("parallel", "parallel"),
        name="adaln_mod",
    )(s, w_mod, b_mod.reshape(L, 1, N))


def _mod_rows(m_lat_ref, m_ctx_ref, row0, tm, n_lat):
    is_lat = (row0 + _iota((tm, 1), 0)) < n_lat
    return is_lat, m_lat_ref[0], m_ctx_ref[0]


def _modnorm(x, g, is_lat, m_lat, m_ctx, i_shift):
    shift = jnp.where(is_lat, m_lat[i_shift:i_shift + 1], m_ctx[i_shift:i_shift + 1])
    scale = jnp.where(is_lat, m_lat[i_shift + 1:i_shift + 2], m_ctx[i_shift + 1:i_shift + 2])
    y = x * lax.rsqrt(jnp.mean(x * x, axis=-1, keepdims=True) + NORM_EPS) * g
    return y * (1.0 + scale) + shift


def _in_kernel(x_ref, g_ref, ml_ref, mc_ref, w_ref, o_ref, h_ref, *, tm, n_lat):
    @pl.when(pl.program_id(2) == 0)
    def _():
        is_lat, m_lat, m_ctx = _mod_rows(ml_ref, mc_ref, pl.program_id(1) * tm, tm, n_lat)
        h_ref[...] = _modnorm(x_ref[0], g_ref[...], is_lat, m_lat, m_ctx, 0).astype(h_ref.dtype)

    o_ref[0] = _bdot(h_ref[...], w_ref[...]).astype(o_ref.dtype)


def _in_proj(xs, g, mod_l, w, n_lat):
    B, T, D = xs.shape
    N = w.shape[1]
    tm = _row_tile(T)
    tn = _pick(N, (1280, 1024, 896, 768, 640, 512, 384, 256, 128))
    kern = functools.partial(_in_kernel, tm=tm, n_lat=n_lat)
    return pl.pallas_call(
        kern,
        out_shape=jax.ShapeDtypeStruct((B, T, N), BF16),
        grid=(B, T // tm, N // tn),
        in_specs=[pl.BlockSpec((1, tm, D), lambda b, i, j: (b, i, 0)),
                  pl.BlockSpec((1, D), lambda b, i, j: (0, 0)),
                  pl.BlockSpec((1, 6, D), lambda b, i, j: (b, 0, 0)),
                  pl.BlockSpec((1, 6, D), lambda b, i, j: (B, 0, 0)),
                  pl.BlockSpec((D, tn), lambda b, i, j: (0, j))],
        out_specs=pl.BlockSpec((1, tm, tn), lambda b, i, j: (b, i, j)),
        scratch_shapes=[pltpu.VMEM((tm, D), BF16)],
        compiler_params=_cparams("parallel", "parallel", "arbitrary"),
        name="in_proj",
    )(xs, g.reshape(1, D), mod_l, mod_l, w)


def _rot(x, cos, sin):
    qd = x.shape[1] // 4
    first = (_iota((1, x.shape[1]), 1) // qd) % 2 == 0
    swapped = jnp.where(first, pltpu.roll(x, x.shape[1] - qd, 1), pltpu.roll(x, qd, 1))
    return x * cos + swapped * sin


def _head_norm(x, g):
    return x * lax.rsqrt(jnp.mean(x * x, axis=-1, keepdims=True) + NORM_EPS) * g


def _attn_kernel(q_ref, k_ref, v_ref, cq_ref, sq_ref, ck_ref, sk_ref, qn_ref, kn_ref, o_ref,
                 kp_ref, vp_ref, *, hd, group, tq, n_lat, scale):
    i = pl.program_id(2)

    @pl.when(i == 0)
    def _():
        kp_ref[...] = _rot(_head_norm(k_ref[0].astype(F32), kn_ref[...]),
                           ck_ref[...], sk_ref[...]).astype(kp_ref.dtype)
        ones = jnp.ones(v_ref.shape[1:], vp_ref.dtype)
        vp_ref[...] = jnp.concatenate([v_ref[0].astype(vp_ref.dtype), ones], axis=1)

    S = kp_ref.shape[0]
    allowed = (_iota((1, S), 1) >= n_lat) | (i * tq < n_lat)
    for g in range(group):
        q = _head_norm(q_ref[0, :, g * hd:(g + 1) * hd].astype(F32), qn_ref[...])
        q = _rot(q, cq_ref[...], sq_ref[...]) * scale
        s = jnp.where(allowed, _bdot(q, kp_ref[...], NT), -jnp.inf)
        p = jnp.exp(s - jnp.max(s, axis=-1, keepdims=True))
        ov = _bdot(p, vp_ref[...])
        o_ref[0, :, g * hd:(g + 1) * hd] = (ov[:, :hd] / ov[:, hd:hd + 1]).astype(o_ref.dtype)


def _attention(p_qkv, q_col, k_col, v_col, cos, sin, q_norm, k_norm, hd, hq, hkv, n_lat):
    B, T, _ = p_qkv.shape
    group = hq // hkv
    gw = group * hd
    tq = _pick(T - n_lat, (256, 128, 64, 32, 16))
    kern = functools.partial(_attn_kernel, hd=hd, group=group, tq=tq, n_lat=n_lat, scale=hd ** -0.5)
    tab_q = pl.BlockSpec((tq, hd), lambda b, h, i: (i, 0))
    tab_k = pl.BlockSpec((T, hd), lambda b, h, i: (0, 0))
    vec = pl.BlockSpec((1, hd), lambda b, h, i: (0, 0))
    return pl.pallas_call(
        kern,
        out_shape=jax.ShapeDtypeStruct((B, T, hq * hd), BF16),
        grid=(B, hkv, T // tq),
        in_specs=[pl.BlockSpec((1, tq, gw), lambda b, h, i: (b, i, q_col // gw + h)),
                  pl.BlockSpec((1, T, hd), lambda b, h, i: (b, 0, k_col // hd + h)),
                  pl.BlockSpec((1, T, hd), lambda b, h, i: (b, 0, v_col // hd + h)),
                  tab_q, tab_q, tab_k, tab_k, vec, vec],
        out_specs=pl.BlockSpec((1, tq, gw), lambda b, h, i: (b, i, h)),
        scratch_shapes=[pltpu.VMEM((T, hd), BF16), pltpu.VMEM((T, 2 * hd), BF16)],
        compiler_params=_cparams("parallel", "parallel", "arbitrary"),
        name="attention",
    )(p_qkv, p_qkv, p_qkv, cos, sin, cos, sin, q_norm.reshape(1, hd), k_norm.reshape(1, hd))


def _segment_of_tile(i, tt, n_lat_tiles, n_tiles):
    is_lat = i < n_lat_tiles
    ti = jnp.where(is_lat, i, i - n_lat_tiles)
    seg_tiles = jnp.where(is_lat, n_lat_tiles, n_tiles - n_lat_tiles)
    return ti, seg_tiles


def _halo_specs(tt, width, T, col_block):
    hb = tt // HALO
    last = T // HALO - 1
    prev = pl.BlockSpec((1, HALO, width), lambda b, i: (b, jnp.maximum(i * hb - 1, 0), col_block))
    cur = pl.BlockSpec((1, tt, width), lambda b, i: (b, i, col_block))
    nxt = pl.BlockSpec((1, HALO, width), lambda b, i: (b, jnp.minimum((i + 1) * hb, last), col_block))
    return prev, cur, nxt


def _pool_kernel(prev_ref, cur_ref, next_ref, wg_ref, sc_ref, o_ref, *,
                 tt, n_lat_tiles, n_tiles, gw, windows):
    ti, seg_tiles = _segment_of_tile(pl.program_id(1), tt, n_lat_tiles, n_tiles)
    has_prev = ti > 0
    has_next = ti < seg_tiles - 1
    t_seg = seg_tiles * tt
    tpos = ti * tt + _iota((tt, 1), 0)

    cur = cur_ref[0]
    prev = prev_ref[0]
    nxt = next_ref[0]
    d_cur = _iota((tt, tt), 1) - _iota((tt, tt), 0)
    d_halo = _iota((tt, HALO), 1) - _iota((tt, HALO), 0)
    d_prev = d_halo - HALO
    d_next = d_halo + tt
    for g, win in enumerate(windows):
        lo_off = -(win // 2)
        hi_off = win - win // 2 - 1
        sl = slice(g * gw, (g + 1) * gw)
        band_c = ((d_cur >= lo_off) & (d_cur <= hi_off)).astype(BF16)
        band_p = ((d_prev >= lo_off) & (d_prev <= hi_off) & has_prev).astype(BF16)
        band_n = ((d_next >= lo_off) & (d_next <= hi_off) & has_next).astype(BF16)
        ug = cur[:, sl]
        tot = _bdot(band_c, ug) + _bdot(band_p, prev[:, sl]) + _bdot(band_n, nxt[:, sl])
        lo = jnp.maximum(tpos + lo_off, 0)
        hi = jnp.minimum(tpos + hi_off + 1, t_seg)
        pooled = tot / (hi - lo).astype(F32) - ug.astype(F32)
        y = _bdot(pooled, wg_ref[g]) * sc_ref[:, sl]
        o_ref[0, :, sl] = y.astype(o_ref.dtype)


def _pool(p_arr, col_block, w_group, scale, n_lat):
    B, T, _ = p_arr.shape
    G, gw, _ = w_group.shape
    W = G * gw
    tt = _pick(T - n_lat, (256, 128, 64, 32, 16))
    assert n_lat % tt == 0 and tt % HALO == 0 and max(POOL_WINDOWS) <= HALO
    n_tiles = T // tt
    kern = functools.partial(_pool_kernel, tt=tt, n_lat_tiles=n_lat // tt, n_tiles=n_tiles,
                             gw=gw, windows=POOL_WINDOWS)
    prev, cur, nxt = _halo_specs(tt, W, T, col_block)
    return pl.pallas_call(
        kern,
        out_shape=jax.ShapeDtypeStruct((B, T, W), BF16),
        grid=(B, n_tiles),
        in_specs=[prev, cur, nxt,
                  pl.BlockSpec((G, gw, gw), lambda b, i: (0, 0, 0)),
                  pl.BlockSpec((1, W), lambda b, i: (0, 0))],
        out_specs=pl.BlockSpec((1, tt, W), lambda b, i: (b, i, 0)),
        compiler_params=_cparams("parallel", "parallel"),
        name="pool",
    )(p_arr, p_arr, p_arr, w_group, scale.reshape(1, W))


def _head_sum(x, head):
    lanes = x.shape[1]
    blk = min(lanes, V7X_LANES)
    same = (_iota((blk, blk), 0) // head == _iota((blk, blk), 1) // head).astype(BF16)
    parts = [_split_dot(x[:, j:j + blk], same) for j in range(0, lanes, blk)]
    return jnp.concatenate(parts, axis=1) if len(parts) > 1 else parts[0]


def _rwkv_prep_kernel(prev_ref, cur_ref, next_ref, mu_ref, w0_ref, w2_ref, a0_ref, a2_ref, g2_ref,
                      kk_w_ref, ka_ref, rk_ref,
                      r_ref, v_ref, kk_ref, lw_ref, kd_ref, bd_ref, g_ref, bonus_ref, *,
                      tt, n_lat_tiles, n_tiles, W, head, lora):
    ti, seg_tiles = _segment_of_tile(pl.program_id(1), tt, n_lat_tiles, n_tiles)
    u = cur_ref[0].astype(F32)
    row = _iota((tt, 1), 0)
    before = jnp.where(ti > 0, prev_ref[0, HALO - 1:HALO, :].astype(F32), 0.0)
    after = jnp.where(ti < seg_tiles - 1, next_ref[0, 0:1, :].astype(F32), 0.0)
    u_prev = jnp.where(row == 0, before, pltpu.roll(u, 1, 0))
    u_next = jnp.where(row == tt - 1, after, pltpu.roll(u, tt - 1, 0))
    u = u + (0.5 * (u_prev + u_next) - u) * mu_ref[...]

    r, k, v = u[:, :W], u[:, W:2 * W], u[:, 2 * W:3 * W]
    o1 = 3 * W
    o2 = o1 + lora
    o3 = o2 + lora
    w_lin = w0_ref[...] + _bdot(jnp.tanh(u[:, o1:o2]), w2_ref[...])
    w_log = -(jnp.maximum(-w_lin, 0.0) + jnp.log(1.0 + jnp.exp(-jnp.abs(w_lin)))) - 0.5
    lw_ref[0] = -jnp.exp(w_log)
    a = jax.nn.sigmoid(a0_ref[...] + _bdot(u[:, o2:o3], a2_ref[...]))
    g_ref[0] = _bdot(jax.nn.sigmoid(u[:, o3:]), g2_ref[...]).astype(g_ref.dtype)

    kk = k * kk_w_ref[...]
    kk = kk * lax.rsqrt(jnp.maximum(_head_sum(kk * kk, head), 1e-24))
    k_sum = 0.0
    for z in range(N_DIR):
        a_z = a[:, z * W:(z + 1) * W]
        k_z = k * (1.0 + (a_z - 1.0) * ka_ref[...])
        kd_ref[0, :, z * W:(z + 1) * W] = k_z.astype(kd_ref.dtype)
        bd_ref[0, :, z * W:(z + 1) * W] = (kk * a_z).astype(bd_ref.dtype)
        k_sum = k_sum + k_z
    r_ref[0] = r.astype(r_ref.dtype)
    v_ref[0] = v.astype(v_ref.dtype)
    kk_ref[0] = kk.astype(kk_ref.dtype)
    bonus_ref[0] = (_head_sum(r * k_sum * rk_ref[...], head) * v).astype(bonus_ref.dtype)


def _rwkv_prep(p_rwkv, n_lat, W, head, mu, w0, w2, a0, a2, g2, k_k, k_a, r_k):
    B, T, NS = p_rwkv.shape
    lora = N_DIR * w2.shape[1]
    n_gate = NS - 3 * W - 2 * lora
    tt = _pick(T - n_lat, (256, 128, 64, 32, 16))
    n_tiles = T // tt
    def cat(m):
        z = jnp.zeros_like(m[0])
        return jnp.concatenate([jnp.concatenate([m[0], z], axis=1),
                                jnp.concatenate([z, m[1]], axis=1)], axis=0).astype(BF16)
    g2p = jnp.pad(g2, ((0, n_gate - g2.shape[0]), (0, 0))).astype(BF16)
    mup = jnp.pad(mu, (0, NS - mu.shape[0])).reshape(1, NS)
    kern = functools.partial(_rwkv_prep_kernel, tt=tt, n_lat_tiles=n_lat // tt, n_tiles=n_tiles,
                             W=W, head=head, lora=lora)
    prev, cur, nxt = _halo_specs(tt, NS, T, 0)
    full = lambda shp: pl.BlockSpec(shp, lambda b, i: (0,) * len(shp))
    tile = lambda w: pl.BlockSpec((1, tt, w), lambda b, i: (b, i, 0))
    sd = lambda w, dt: jax.ShapeDtypeStruct((B, T, w), dt)
    return pl.pallas_call(
        kern,
        out_shape=(sd(W, BF16), sd(W, BF16), sd(W, BF16), sd(2 * W, F32), sd(2 * W, BF16),
                   sd(2 * W, BF16), sd(W, BF16), sd(W, BF16)),
        grid=(B, n_tiles),
        in_specs=[prev, cur, nxt, full((1, NS)), full((1, 2 * W)), full((lora, 2 * W)),
                  full((1, 2 * W)), full((lora, 2 * W)), full((n_gate, W)),
                  full((1, W)), full((1, W)), full((1, W))],
        out_specs=(tile(W), tile(W), tile(W), tile(2 * W), tile(2 * W), tile(2 * W), tile(W), tile(W)),
        compiler_params=_cparams("parallel", "parallel"),
        name="rwkv_prep",
    )(p_rwkv, p_rwkv, p_rwkv, mup, w0.reshape(1, 2 * W), cat(w2), a0.reshape(1, 2 * W), cat(a2), g2p,
      k_k.reshape(1, W), k_a.reshape(1, W), r_k.reshape(1, W))


def _rwkv_kernel(r_ref, v_ref, kk_ref, lw_ref, k_ref, b_ref, y_ref, g_ref, *, C, N, n_seq):
    @pl.when(pl.program_id(1) == 0)
    def _():
        g_ref[...] = jnp.zeros_like(g_ref)

    W = r_ref.shape[2]
    PW = 2 * N
    sgn = jnp.where(pl.program_id(0) >= n_seq, -1, 1)
    order = (_iota((C, C), 0) - _iota((C, C), 1)) * sgn
    lw = lw_ref[0]
    cum = jnp.dot((order >= 0).astype(F32), lw, precision=HIGHEST, preferred_element_type=F32)
    pc = jnp.sum(lw, axis=0, keepdims=True)
    k = k_ref[0].astype(F32)
    b = b_ref[0].astype(F32)
    p_inv = jnp.exp(-cum)
    p_hat = jnp.exp(pc - cum)
    r_t = r_ref[0].astype(F32) * jnp.exp(cum)
    k_t = k * p_inv
    b_t = b * p_inv
    a_t = -kk_ref[0].astype(F32) * jnp.exp(cum - lw)
    b_h = b * p_hat
    k_h = k * p_hat
    p_c = jnp.exp(pc)
    v = v_ref[0].astype(F32)

    order2 = jnp.concatenate([order, order], axis=1)
    strict2 = order2 > 0
    incl2 = order2 >= 0
    eye = (order == 0).astype(F32)
    first = _iota((1, PW), 1) < N
    lane_lo = _iota((1, 2 * C), 1) < C
    same_head = (_iota((PW, PW), 0) // N) == (_iota((PW, PW), 1) // N)

    n_pairs = W // PW
    heads = [(p, sub) for p in range(n_pairs) for sub in range(2)]
    psl = [slice(p * PW, (p + 1) * PW) for p in range(n_pairs)]
    rhs_bk = [jnp.concatenate([b_t[:, s], k_t[:, s]], axis=0).astype(BF16) for s in psl]
    a_m, v_m, upper, lower = [], [], [], []
    for p, sub in heads:
        m = first if sub == 0 else jnp.logical_not(first)
        a_m.append(jnp.where(m, a_t[:, psl[p]], 0.0))
        v_m.append(jnp.where(m, v[:, psl[p]], 0.0))
        lhs = jnp.concatenate([a_m[-1], jnp.where(m, r_t[:, psl[p]], 0.0)], axis=0)
        amat = _bdot(lhs, rhs_bk[p], NT)
        upper.append(jnp.where(strict2, amat[:C], 0.0))
        lower.append(jnp.where(incl2, amat[C:], 0.0))
    w1 = [_bdot(jnp.where(lane_lo, 0.0, up), jnp.concatenate([vm, vm], axis=0))
          for up, vm in zip(upper, v_m)]
    pw = [up[:, :C] for up in upper]
    t_inv = [eye + n_ab for n_ab in pw]
    span = 2
    while span < C:
        pw = [_bdot(x, x) for x in pw]
        t_inv = [t + _bdot(t, x) for t, x in zip(t_inv, pw)]
        span *= 2
    x = [_bdot(t, jnp.concatenate([am, w], axis=1)) for t, am, w in zip(t_inv, a_m, w1)]

    g0 = [g_ref[p] for p in range(n_pairs)]
    u = [_bdot(jnp.concatenate([x[2 * p][:, :PW], x[2 * p + 1][:, :PW]], axis=0), g0[p])
         + jnp.concatenate([x[2 * p][:, PW:], x[2 * p + 1][:, PW:]], axis=0) for p in range(n_pairs)]
    uv = [jnp.concatenate([u[p][:C], v_m[2 * p], u[p][C:], v_m[2 * p + 1]], axis=0).astype(BF16)
          for p in range(n_pairs)]
    for p in range(n_pairs):
        sl = psl[p]
        y_ref[0, 0, :, sl] = (_bdot(r_t[:, sl], g0[p])
                              + _bdot(jnp.concatenate([lower[2 * p], lower[2 * p + 1]], axis=1), uv[p]))
    for p in range(n_pairs):
        sl = psl[p]
        bk_h = jnp.concatenate([b_h[:, sl], k_h[:, sl]], axis=0)
        upd = _bdot(jnp.concatenate([bk_h, bk_h], axis=0), uv[p], TN)
        decay = jnp.transpose(jnp.broadcast_to(p_c[:, sl], (PW, PW)))
        g_ref[p] = jnp.where(same_head, decay * g0[p] + upd, 0.0)


def _rwkv_scan(r, v, kk, lw, kd, bd, head, n_lat):
    B, T, W = r.shape
    C = RWKV_CHUNK
    nc = T // C
    nc_lat = n_lat // C

    def chunk(s, c):
        fwd = jnp.where(c < nc - nc_lat, nc_lat + c, c - (nc - nc_lat))
        bwd = jnp.where(c < nc - nc_lat, nc - 1 - c, nc - 1 - c)
        return jnp.where(s >= B, bwd, fwd)

    shared = pl.BlockSpec((1, C, W), lambda s, c: (s % B, chunk(s, c), 0))
    per_dir = pl.BlockSpec((1, C, W), lambda s, c: (s % B, chunk(s, c), s // B))
    kern = functools.partial(_rwkv_kernel, C=C, N=head, n_seq=B)
    return pl.pallas_call(
        kern,
        out_shape=jax.ShapeDtypeStruct((N_DIR, B, T, W), F32),
        grid=(N_DIR * B, nc),
        in_specs=[shared, shared, shared, per_dir, per_dir, per_dir],
        out_specs=pl.BlockSpec((1, 1, C, W), lambda s, c: (s // B, s % B, chunk(s, c), 0)),
        scratch_shapes=[pltpu.VMEM((W // (2 * head), 2 * head, 2 * head), F32)],
        compiler_params=_cparams("parallel", "arbitrary"),
        name="rwkv7_chunk",
    )(r, v, kk, lw, kd, bd)


def _merge_kernel(attn_ref, pool_ref, y_ref, g_ref, bonus_ref, lnw_ref, lnb_ref, gate_ref,
                  wa_ref, wp_ref, wr_ref, o_ref, rw_ref, *, head, tn):
    @pl.when(pl.program_id(2) == 0)
    def _():
        y = y_ref[0, 0] + y_ref[1, 0]
        inv_n = 1.0 / head
        dev = y - _head_sum(y, head) * inv_n
        var = _head_sum(dev * dev, head) * inv_n
        yn = dev * lax.rsqrt(var + GN_EPS) * lnw_ref[...] + lnb_ref[...]
        rw_ref[...] = ((yn + bonus_ref[0].astype(F32)) * g_ref[0].astype(F32)).astype(rw_ref.dtype)

    def gate(z):
        return jax.nn.sigmoid(gate_ref[0, :, z * tn:(z + 1) * tn].astype(F32))

    out = (gate(0) * _bdot(attn_ref[0], wa_ref[...]) + gate(1) * _bdot(pool_ref[0], wp_ref[...])
           + gate(2) * _bdot(rw_ref[...], wr_ref[...]))
    o_ref[0] = out.astype(o_ref.dtype)


def _merge_tile(D):
    return _pick(D, (512, 256, 128))


def _merge(attn, pool, y, g, bonus, ln_w, ln_b, gates, w_a, w_p, w_r, head):
    B, T, _ = attn.shape
    W = g.shape[2]
    D = w_a.shape[1]
    tm = _row_tile(T)
    tn = _merge_tile(D)
    nj = D // tn
    row = lambda w: pl.BlockSpec((1, tm, w), lambda b, i, j: (b, i, 0))
    vec = pl.BlockSpec((1, W), lambda b, i, j: (0, 0))
    wsp = lambda w: pl.BlockSpec((w.shape[0], tn), lambda b, i, j: (0, j))
    return pl.pallas_call(
        functools.partial(_merge_kernel, head=head, tn=tn),
        out_shape=jax.ShapeDtypeStruct((B, T, D), BF16),
        grid=(B, T // tm, nj),
        in_specs=[row(attn.shape[2]), row(pool.shape[2]),
                  pl.BlockSpec((N_DIR, 1, tm, W), lambda b, i, j: (0, b, i, 0)),
                  row(W), row(W), vec, vec,
                  pl.BlockSpec((1, tm, N_BRANCH * tn), lambda b, i, j: (b, i, j)),
                  wsp(w_a), wsp(w_p), wsp(w_r)],
        out_specs=pl.BlockSpec((1, tm, tn), lambda b, i, j: (b, i, j)),
        scratch_shapes=[pltpu.VMEM((tm, W), BF16)],
        compiler_params=_cparams("parallel", "parallel", "arbitrary"),
        name="merge",
    )(attn, pool, y, g, bonus, ln_w.reshape(1, W), ln_b.reshape(1, W), gates, w_a, w_p, w_r)


def _out_kernel(m_ref, w_ref, x_ref, gpost_ref, gpre_ref, ml_ref, mc_ref, wr_ref,
                x_out, h_out, aff_out, *, tm, n_lat, n_e):
    is_lat, m_lat, m_ctx = _mod_rows(ml_ref, mc_ref, pl.program_id(1) * tm, tm, n_lat)
    mix = _bdot(m_ref[0], w_ref[...])
    normed = mix * lax.rsqrt(jnp.mean(mix * mix, axis=-1, keepdims=True) + NORM_EPS) * gpost_ref[...]
    x = x_ref[0] + jnp.where(is_lat, m_lat[2:3], m_ctx[2:3]) * normed
    x_out[0] = x
    h = _modnorm(x, gpre_ref[...], is_lat, m_lat, m_ctx, 3)
    h_out[0] = h.astype(h_out.dtype)
    logits = jnp.dot(h, wr_ref[...], precision=HIGHEST, preferred_element_type=F32)
    logits = jnp.where(_iota(logits.shape, 1) < n_e, logits, -jnp.inf)
    e = jnp.exp(logits - jnp.max(logits, axis=-1, keepdims=True))
    aff_out[0] = e / jnp.sum(e, axis=-1, keepdims=True)


def _out_proj(merged, w_out, xs, g_post, g_pre, mod_l, w_router, n_lat):
    B, T, D = xs.shape
    E = w_router.shape[1]
    tm = _pick(T, (544, 272, 256, 128, 320, 64, 32, 16))
    wr = jnp.pad(w_router, ((0, 0), (0, V7X_LANES - E)))
    row = lambda w: pl.BlockSpec((1, tm, w), lambda b, i: (b, i, 0))
    vec = pl.BlockSpec((1, D), lambda b, i: (0, 0))
    return pl.pallas_call(
        functools.partial(_out_kernel, tm=tm, n_lat=n_lat, n_e=E),
        out_shape=(jax.ShapeDtypeStruct((B, T, D), F32), jax.ShapeDtypeStruct((B, T, D), BF16),
                   jax.ShapeDtypeStruct((B, T, V7X_LANES), F32)),
        grid=(B, T // tm),
        in_specs=[row(D), pl.BlockSpec((D, D), lambda b, i: (0, 0)), row(D), vec, vec,
                  pl.BlockSpec((1, 6, D), lambda b, i: (b, 0, 0)),
                  pl.BlockSpec((1, 6, D), lambda b, i: (B, 0, 0)),
                  pl.BlockSpec((D, V7X_LANES), lambda b, i: (0, 0))],
        out_specs=(row(D), row(D), row(V7X_LANES)),
        compiler_params=_cparams("parallel", "parallel"),
        name="out_proj",
    )(merged, w_out, xs, g_post.reshape(1, D), g_pre.reshape(1, D), mod_l, mod_l, wr)


def _excl_prefix(flags, blk):
    n = flags.shape[-1]
    upper = (_iota((blk, blk), 0) < _iota((blk, blk), 1)).astype(BF16)
    outs = []
    carry = jnp.zeros((flags.shape[0], 1), F32)
    for j in range(n // blk):
        seg = flags[:, j * blk:(j + 1) * blk]
        outs.append(_bdot(seg, upper) + carry)
        carry = carry + jnp.sum(seg, axis=-1, keepdims=True)
    return jnp.concatenate(outs, axis=-1) if len(outs) > 1 else outs[0]


def _topk_kernel(aff_ref, sel_ref, *, cap, blk):
    bits = lax.bitcast_convert_type(aff_ref[0], jnp.int32)
    E = bits.shape[0]

    def body(i, tau):
        cand = tau | jnp.left_shift(jnp.int32(1), 30 - i)
        cnt = jnp.sum((bits >= cand).astype(jnp.int32), axis=-1, keepdims=True)
        return jnp.where(cnt >= cap, cand, tau)

    tau = lax.fori_loop(0, 31, body, jnp.zeros((E, 1), jnp.int32))
    gt = bits > tau
    eq = bits == tau
    need = (cap - jnp.sum(gt.astype(jnp.int32), axis=-1, keepdims=True)).astype(F32)
    eq_rank = _excl_prefix(eq.astype(F32), blk)
    sel = gt | (eq & (eq_rank < need))
    pos = _excl_prefix(sel.astype(F32), blk)
    sel_ref[0] = jnp.where(sel, pos.astype(jnp.int32), -1)


def _topk_slots(aff_t, cap):
    B, E, n = aff_t.shape
    blk = _pick(n, (512, 256, 128))
    return pl.pallas_call(
        functools.partial(_topk_kernel, cap=cap, blk=blk),
        out_shape=jax.ShapeDtypeStruct((B, E, n), jnp.int32),
        grid=(B,),
        in_specs=[pl.BlockSpec((1, E, n), lambda b: (b, 0, 0))],
        out_specs=pl.BlockSpec((1, E, n), lambda b: (b, 0, 0)),
        compiler_params=_cparams("parallel"),
        name="expert_topk",
    )(aff_t)


def _gather_kernel(sel_ref, h_ref, xe_ref, oh_ref, *, cap):
    @pl.when(pl.program_id(2) == 0)
    def _():
        sel = sel_ref[0, 0]
        oh_ref[...] = (sel == _iota((cap, sel.shape[1]), 0)).astype(oh_ref.dtype)

    xe_ref[0, 0] = _bdot(oh_ref[...], h_ref[0]).astype(xe_ref.dtype)


def _gather(sel, h, row_block, cap):
    B, E, n = sel.shape
    D = h.shape[2]
    td = _pick(D, (512, 256, 128))
    return pl.pallas_call(
        functools.partial(_gather_kernel, cap=cap),
        out_shape=jax.ShapeDtypeStruct((B, E, cap, D), BF16),
        grid=(B, E, D // td),
        in_specs=[pl.BlockSpec((1, 1, 1, n), lambda b, e, j: (b, e, 0, 0)),
                  pl.BlockSpec((1, n, td), lambda b, e, j: (b, row_block, j))],
        out_specs=pl.BlockSpec((1, 1, cap, td), lambda b, e, j: (b, e, 0, j)),
        scratch_shapes=[pltpu.VMEM((cap, n), BF16)],
        compiler_params=_cparams("parallel", "parallel", "arbitrary"),
        name="expert_gather",
    )(sel.reshape(B, E, 1, n), h)


def _ffn_kernel(xe_ref, wg_ref, wu_ref, wd_ref, ye_ref):
    xe = xe_ref[0, 0]
    gate = _bdot(xe, wg_ref[0])
    up = _bdot(xe, wu_ref[0])
    hid = gate * jax.nn.sigmoid(gate) * up
    ye_ref[0, 0] = _bdot(hid, wd_ref[0]).astype(ye_ref.dtype)


def _expert_ffn(xe, w_gate, w_up, w_down):
    B, E, cap, D = xe.shape
    FF = w_gate.shape[2]
    return pl.pallas_call(
        _ffn_kernel,
        out_shape=jax.ShapeDtypeStruct((B, E, cap, D), BF16),
        grid=(E, B),
        in_specs=[pl.BlockSpec((1, 1, cap, D), lambda e, b: (b, e, 0, 0)),
                  pl.BlockSpec((1, D, FF), lambda e, b: (e, 0, 0)),
                  pl.BlockSpec((1, D, FF), lambda e, b: (e, 0, 0)),
                  pl.BlockSpec((1, FF, D), lambda e, b: (e, 0, 0))],
        out_specs=pl.BlockSpec((1, 1, cap, D), lambda e, b: (b, e, 0, 0)),
        compiler_params=_cparams("parallel", "parallel"),
        name="expert_ffn",
    )(xe, w_gate, w_up, w_down)


def _scatter_kernel(selc_ref, affc_ref, ye_ref, o_ref, *, cap):
    @pl.when(pl.program_id(2) == 0)
    def _():
        o_ref[...] = jnp.zeros_like(o_ref)

    selc = selc_ref[0, 0]
    onehot = selc == _iota((selc.shape[0], cap), 1)
    o_ref[0] += affc_ref[0, 0] * _bdot(onehot.astype(BF16), ye_ref[0, 0])


def _scatter(sel, aff_t, ye):
    B, E, n = sel.shape
    cap, D = ye.shape[2:]
    td = _pick(D, (512, 256, 128))
    return pl.pallas_call(
        functools.partial(_scatter_kernel, cap=cap),
        out_shape=jax.ShapeDtypeStruct((B, n, D), F32),
        grid=(B, D // td, E),
        in_specs=[pl.BlockSpec((1, 1, n, 1), lambda b, j, e: (b, e, 0, 0)),
                  pl.BlockSpec((1, 1, n, 1), lambda b, j, e: (b, e, 0, 0)),
                  pl.BlockSpec((1, 1, cap, td), lambda b, j, e: (b, e, 0, j))],
        out_specs=pl.BlockSpec((1, n, td), lambda b, j, e: (b, 0, j)),
        compiler_params=_cparams("parallel", "parallel", "arbitrary"),
        name="expert_scatter",
    )(sel.reshape(B, E, n, 1), aff_t.reshape(B, E, n, 1), ye)


def _expert_choice(hb, aff, row_block, n, w_gate, w_up, w_down):
    E = w_gate.shape[0]
    cap = EC_FACTOR * n // E
    aff_t = jnp.swapaxes(aff[:, row_block * n:(row_block + 1) * n, :E], 1, 2)
    sel = _topk_slots(aff_t, cap)
    xe = _gather(sel, hb, row_block, cap)
    ye = _expert_ffn(xe, w_gate, w_up, w_down)
    return _scatter(sel, aff_t, ye)


def _rms(x, g):
    return x * lax.rsqrt(jnp.mean(x * x, axis=-1, keepdims=True) + NORM_EPS) * g


def _rope_tables(n_lat, n_ctx, head_dim):
    rows = n_lat // GRID_W
    row = jnp.repeat(jnp.arange(rows), GRID_W).astype(F32)
    col = (jnp.arange(rows * GRID_W) % GRID_W).astype(F32)
    half = head_dim // 2
    inv = ROPE_THETA ** (-jnp.arange(0, half, 2, dtype=F32) / half)
    ar, ac = row[:, None] * inv, col[:, None] * inv
    cos = jnp.concatenate([jnp.cos(ar), jnp.cos(ar), jnp.cos(ac), jnp.cos(ac)], axis=-1)
    sin = jnp.concatenate([-jnp.sin(ar), jnp.sin(ar), -jnp.sin(ac), jnp.sin(ac)], axis=-1)
    cos = jnp.concatenate([cos, jnp.ones((n_ctx, head_dim), F32)], axis=0)
    sin = jnp.concatenate([sin, jnp.zeros((n_ctx, head_dim), F32)], axis=0)
    return cos, sin


def _pad_cols(w, mult):
    return jnp.pad(w, ((0, 0), (0, (-w.shape[1]) % mult)))


def kernel(x, c, ctx, c_ctx, w_mod, b_mod, norm_pre, norm_post, w_in, q_norm, k_norm, w_attn_o, w_pool_group, pool_scale, w_pool_o, rwkv_mu, rwkv_w0, rwkv_w2, rwkv_a0, rwkv_a2, rwkv_g2, rwkv_k_k, rwkv_k_a, rwkv_r_k, rwkv_ln_w, rwkv_ln_b, w_rwkv_o, w_out, w_router, w_exp_gate, w_exp_up, w_exp_down):
    B, n_lat, D = x.shape
    n_ctx = ctx.shape[1]
    T = n_lat + n_ctx
    depth = w_mod.shape[0]
    hd = q_norm.shape[1]
    attn_w = w_attn_o.shape[1]
    pool_w = w_pool_o.shape[1]
    rwkv_w = w_rwkv_o.shape[1]
    n_shift = rwkv_mu.shape[1]
    n_in = w_in.shape[2]
    kv_w = (n_in - attn_w - pool_w - n_shift - N_BRANCH * D) // 2
    head = rwkv_r_k.shape[2]
    col_k = attn_w
    col_v = col_k + kv_w
    col_pool = col_v + kv_w
    col_r = col_pool + pool_w
    col_gate = col_r + n_shift
    hq, hkv = attn_w // hd, kv_w // hd
    assert n_lat % (hq // hkv * hd) == 0 and pool_w % (hq // hkv * hd) == 0 and n_ctx % RWKV_CHUNK == 0

    cos, sin = _rope_tables(n_lat, n_ctx, hd)
    s_all = jnp.concatenate([jax.nn.silu(c), jax.nn.silu(c_ctx)[None]], axis=0)
    s_all = jnp.pad(s_all, ((0, (-s_all.shape[0]) % 8), (0, 0)))
    mod = _modulation(s_all, w_mod, b_mod)[:, :B + 1].reshape(depth, B + 1, 6, D)
    xs = jnp.concatenate([x, ctx], axis=1)

    for l in range(depth):
        keep_ctx = l < depth - 1
        wl = w_in[l]
        w_a = jnp.concatenate([wl[:, col_pool:col_r], wl[:, :col_pool]], axis=1).astype(BF16)
        w_b = _pad_cols(wl[:, col_r:col_gate], 3 * V7X_LANES).astype(BF16)
        tn_m = _merge_tile(D)
        w_c = (wl[:, col_gate:].reshape(D, N_BRANCH, D // tn_m, tn_m).swapaxes(1, 2)
               .reshape(D, N_BRANCH * D).astype(BF16))
        pre = norm_pre[l, 0]
        p_a = _in_proj(xs, pre, mod[l], w_a, n_lat)
        p_b = _in_proj(xs, pre, mod[l], w_b, n_lat)
        p_c = _in_proj(xs, pre, mod[l], w_c, n_lat)

        attn = _attention(p_a, pool_w, pool_w + attn_w, pool_w + attn_w + kv_w, cos, sin,
                          q_norm[l], k_norm[l], hd, hq, hkv, n_lat)
        pool = _pool(p_a, 0, w_pool_group[l].astype(BF16), pool_scale[l], n_lat)
        r, v, kk, lw, kd, bd, g, bonus = _rwkv_prep(
            p_b, n_lat, rwkv_w, head, rwkv_mu[l], rwkv_w0[l], rwkv_w2[l], rwkv_a0[l], rwkv_a2[l],
            rwkv_g2[l], rwkv_k_k[l], rwkv_k_a[l], rwkv_r_k[l])
        y = _rwkv_scan(r, v, kk, lw, kd, bd, head, n_lat)
        merged = _merge(attn, pool, y, g, bonus, rwkv_ln_w[l], rwkv_ln_b[l], p_c,
                        w_attn_o[l].astype(BF16), w_pool_o[l].astype(BF16), w_rwkv_o[l].astype(BF16), head)
        xs, hb, aff = _out_proj(merged, w_out[l].astype(BF16), xs, norm_post[l, 0], norm_pre[l, 1],
                                mod[l], w_router[l], n_lat)

        wg, wu, wdn = w_exp_gate[l].astype(BF16), w_exp_up[l].astype(BF16), w_exp_down[l].astype(BF16)
        f = _expert_choice(hb, aff, 0, n_lat, wg, wu, wdn)
        m5 = jnp.broadcast_to(mod[l, :B, None, 5], (B, n_lat, D))
        if keep_ctx:
            f = jnp.concatenate([f, _expert_choice(hb, aff, n_lat // n_ctx, n_ctx, wg, wu, wdn)], axis=1)
            m5 = jnp.concatenate([m5, jnp.broadcast_to(mod[l, B, 5], (B, n_ctx, D))], axis=1)
            xs = xs + m5 * _rms(f, norm_post[l, 1])
        else:
            xs = xs[:, :n_lat] + m5 * _rms(f, norm_post[l, 1])
    return xs[:, :n_lat]
```

```python
import functools

import jax
import jax.numpy as jnp
from jax import lax
from jax.experimental import pallas as pl
from jax.experimental.pallas import tpu as pltpu

F32 = jnp.float32
BF16 = jnp.bfloat16
HIGHEST = lax.Precision.HIGHEST

GRID_W = 64
NORM_EPS = 1e-6
ROPE_THETA = 10000.0
POOL_WINDOWS = (2, 4, 8, 16)
GN_EPS = 64e-5
EC_FACTOR = 2
N_DIR = 2
N_BRANCH = 3
RWKV_CHUNK = 64
HALO = 16

LOG2_E = 1.4426950408889634

V7X_LANES = 128
VMEM_LIMIT = 52 * 1024 * 1024

NT = (((1,), (1,)), ((), ()))
TN = (((0,), (0,)), ((), ()))


def _pick(n, cands):
    for c in cands:
        if c <= n and n % c == 0:
            return c
    return n


def _cparams(*sem):
    return pltpu.CompilerParams(dimension_semantics=sem, vmem_limit_bytes=VMEM_LIMIT)


def _bdot(x, y, dn=None):
    x = x.astype(BF16)
    y = y.astype(BF16)
    if dn is None:
        return jnp.dot(x, y, preferred_element_type=F32)
    return lax.dot_general(x, y, dn, preferred_element_type=F32)


def _split_dot(x, y):
    hi = x.astype(BF16)
    lo = x - hi.astype(F32)
    return _bdot(hi, y) + _bdot(lo, y)


def _iota(shape, dim):
    return lax.broadcasted_iota(jnp.int32, shape, dim)


def _row_tile(T, min_parts=8):
    for parts in range(min_parts, T // 16 + 1):
        if T % parts == 0 and (T // parts) % 16 == 0:
            return T // parts
    return T


def _mod_kernel(s_ref, w_ref, b_ref, o_ref):
    o_ref[0] = jnp.dot(s_ref[...], w_ref[0], precision=HIGHEST,
                       preferred_element_type=F32) + b_ref[0]


def _modulation(s, w_mod, b_mod):
    L, D, N = w_mod.shape
    R = s.shape[0]
    tn = _pick(N, (1024, 512, 256, 128))
    return pl.pallas_call(
        _mod_kernel,
        out_shape=jax.ShapeDtypeStruct((L, R, N), F32),
        grid=(L, N // tn),
        in_specs=[pl.BlockSpec((R, D), lambda l, j: (0, 0)),
                  pl.BlockSpec((1, D, tn), lambda l, j: (l, 0, j)),
                  pl.BlockSpec((1, 1, tn), lambda l, j: (l, 0, j))],
        out_specs=pl.BlockSpec((1, R, tn), lambda l, j: (l, 0, j)),
        compiler_params=_cparams("parallel", "parallel"),
        name="adaln_mod",
    )(s, w_mod, b_mod.reshape(L, 1, N))


def _mod_rows(m_lat_ref, m_ctx_ref, row0, tm, n_lat):
    is_lat = (row0 + _iota((tm, 1), 0)) < n_lat
    return is_lat, m_lat_ref[0], m_ctx_ref[0]


def _modnorm(x, g, is_lat, m_lat, m_ctx, i_shift):
    shift = jnp.where(is_lat, m_lat[i_shift:i_shift + 1], m_ctx[i_shift:i_shift + 1])
    scale = jnp.where(is_lat, m_lat[i_shift + 1:i_shift + 2], m_ctx[i_shift + 1:i_shift + 2])
    y = x * lax.rsqrt(jnp.mean(x * x, axis=-1, keepdims=True) + NORM_EPS) * g
    return y * (1.0 + scale) + shift


def _small_row_tile(T, n_ctx):
    return _pick(n_ctx, (256, 128, 64, 32, 16))


def _stream_kernel(*refs, tm, n_lat, has_f, has_ctx_f, emit_h):
    refs = list(refs)
    x_ref = refs.pop(0)
    fl_ref = refs.pop(0) if has_f else None
    fc_ref = refs.pop(0) if has_ctx_f else None
    gpost_ref = refs.pop(0) if has_f else None
    ml_ref, mc_ref = refs.pop(0), refs.pop(0)
    if emit_h:
        gpre_ref, mln_ref, mcn_ref = refs.pop(0), refs.pop(0), refs.pop(0)
    x_out = refs.pop(0) if has_f else None
    h_out = refs.pop(0) if emit_h else None

    is_lat = (pl.program_id(1) * tm + _iota((tm, 1), 0)) < n_lat
    x = x_ref[0]
    if has_f:
        f = fl_ref[0]
        if has_ctx_f:
            f = jnp.where(is_lat, f, fc_ref[0])
        normed = f * lax.rsqrt(jnp.mean(f * f, axis=-1, keepdims=True) + NORM_EPS) * gpost_ref[...]
        x = x + jnp.where(is_lat, ml_ref[0][5:6], mc_ref[0][5:6]) * normed
        x_out[0] = x
    if emit_h:
        h_out[0] = _modnorm(x, gpre_ref[...], is_lat, mln_ref[0], mcn_ref[0], 0).astype(h_out.dtype)


def _stream_update(xs, f_lat, f_ctx, g_post, mod_l, g_pre_next, mod_next, n_lat, n_rows):
    B, T, D = xs.shape
    n_ctx = T - n_lat
    tm = _small_row_tile(T, n_ctx)
    has_f, has_ctx_f, emit_h = f_lat is not None, f_ctx is not None, g_pre_next is not None
    n_lt = n_lat // tm
    row = pl.BlockSpec((1, tm, D), lambda b, i: (b, i, 0))
    vec = pl.BlockSpec((1, D), lambda b, i: (0, 0))
    m_l = pl.BlockSpec((1, 6, D), lambda b, i: (b, 0, 0))
    m_c = pl.BlockSpec((1, 6, D), lambda b, i: (B, 0, 0))
    args, specs = [xs], [row]
    if has_f:
        args.append(f_lat)
        specs.append(pl.BlockSpec((1, tm, D), lambda b, i: (b, jnp.minimum(i, n_lt - 1), 0)))
    if has_ctx_f:
        args.append(f_ctx)
        specs.append(pl.BlockSpec((1, tm, D), lambda b, i: (b, jnp.maximum(i - n_lt, 0), 0)))
    if has_f:
        args.append(g_post.reshape(1, D))
        specs.append(vec)
    args += [mod_l, mod_l]
    specs += [m_l, m_c]
    if emit_h:
        args += [g_pre_next.reshape(1, D), mod_next, mod_next]
        specs += [vec, m_l, m_c]
    out_shape, out_specs = [], []
    if has_f:
        out_shape.append(jax.ShapeDtypeStruct((B, n_rows, D), F32))
        out_specs.append(row)
    if emit_h:
        out_shape.append(jax.ShapeDtypeStruct((B, n_rows, D), BF16))
        out_specs.append(row)
    kern = functools.partial(_stream_kernel, tm=tm, n_lat=n_lat, has_f=has_f, has_ctx_f=has_ctx_f,
                             emit_h=emit_h)
    return pl.pallas_call(
        kern, out_shape=tuple(out_shape), grid=(B, n_rows // tm), in_specs=specs,
        out_specs=tuple(out_specs), compiler_params=_cparams("parallel", "parallel"),
        name="stream_update",
    )(*args)


def _in_kernel(h_ref, w_ref, o_ref):
    o_ref[0] = _bdot(h_ref[0], w_ref[...]).astype(o_ref.dtype)


def _in_proj(h, w):
    B, T, D = h.shape
    N = w.shape[1]
    tm = _row_tile(T, 4)
    tn = _pick(N, (1280, 1024, 896, 768, 640, 512, 384, 256, 128))
    return pl.pallas_call(
        _in_kernel,
        out_shape=jax.ShapeDtypeStruct((B, T, N), BF16),
        grid=(B, T // tm, N // tn),
        in_specs=[pl.BlockSpec((1, tm, D), lambda b, i, j: (b, i, 0)),
                  pl.BlockSpec((D, tn), lambda b, i, j: (0, j))],
        out_specs=pl.BlockSpec((1, tm, tn), lambda b, i, j: (b, i, j)),
        compiler_params=_cparams("parallel", "parallel", "arbitrary"),
        name="in_proj",
    )(h, w)


def _rot(x, cos, sin):
    qd = x.shape[1] // 4
    first = (_iota((1, x.shape[1]), 1) // qd) % 2 == 0
    swapped = jnp.where(first, pltpu.roll(x, x.shape[1] - qd, 1), pltpu.roll(x, qd, 1))
    return x * cos + swapped * sin


def _head_norm(x, g):
    return x * lax.rsqrt(jnp.mean(x * x, axis=-1, keepdims=True) + NORM_EPS) * g


def _attn_kernel(q_ref, k_ref, v_ref, cq_ref, sq_ref, ck_ref, sk_ref, qn_ref, kn_ref, o_ref,
                 kp_ref, vp_ref, *, hd, group, tq, n_lat, scale):
    i = pl.program_id(2)

    @pl.when(i == 0)
    def _():
        kp_ref[...] = _rot(_head_norm(k_ref[0].astype(F32), kn_ref[...]),
                           ck_ref[...], sk_ref[...]).astype(kp_ref.dtype)
        ones = jnp.ones(v_ref.shape[1:], vp_ref.dtype)
        vp_ref[...] = jnp.concatenate([v_ref[0].astype(vp_ref.dtype), ones], axis=1)

    def attend(key0):
        for g in range(group):
            q = _head_norm(q_ref[0, :, g * hd:(g + 1) * hd].astype(F32), qn_ref[...])
            q = _rot(q, cq_ref[...], sq_ref[...]) * scale
            s = _bdot(q, kp_ref[key0:, :], NT)
            p = jnp.exp2(s - jnp.max(s, axis=-1, keepdims=True))
            ov = _bdot(p, vp_ref[key0:, :])
            o_ref[0, :, g * hd:(g + 1) * hd] = (ov[:, :hd] / ov[:, hd:hd + 1]).astype(o_ref.dtype)

    @pl.when(i * tq < n_lat)
    def _():
        attend(0)

    @pl.when(i * tq >= n_lat)
    def _():
        attend(n_lat)


def _attention(p_qkv, q_col, k_col, v_col, cos, sin, q_norm, k_norm, hd, hq, hkv, n_lat):
    B, T, _ = p_qkv.shape
    group = hq // hkv
    gw = group * hd
    tq = _pick(T - n_lat, (256, 128, 64, 32, 16))
    kern = functools.partial(_attn_kernel, hd=hd, group=group, tq=tq, n_lat=n_lat,
                             scale=hd ** -0.5 * LOG2_E)
    tab_q = pl.BlockSpec((tq, hd), lambda b, h, i: (i, 0))
    tab_k = pl.BlockSpec((T, hd), lambda b, h, i: (0, 0))
    vec = pl.BlockSpec((1, hd), lambda b, h, i: (0, 0))
    return pl.pallas_call(
        kern,
        out_shape=jax.ShapeDtypeStruct((B, T, hq * hd), BF16),
        grid=(B, hkv, T // tq),
        in_specs=[pl.BlockSpec((1, tq, gw), lambda b, h, i: (b, i, q_col // gw + h)),
                  pl.BlockSpec((1, T, hd), lambda b, h, i: (b, 0, k_col // hd + h)),
                  pl.BlockSpec((1, T, hd), lambda b, h, i: (b, 0, v_col // hd + h)),
                  tab_q, tab_q, tab_k, tab_k, vec, vec],
        out_specs=pl.BlockSpec((1, tq, gw), lambda b, h, i: (b, i, h)),
        scratch_shapes=[pltpu.VMEM((T, hd), BF16), pltpu.VMEM((T, 2 * hd), BF16)],
        compiler_params=_cparams("parallel", "parallel", "arbitrary"),
        name="attention",
    )(p_qkv, p_qkv, p_qkv, cos, sin, cos, sin, q_norm.reshape(1, hd), k_norm.reshape(1, hd))


def _segment_of_tile(i, tt, n_lat_tiles, n_tiles):
    is_lat = i < n_lat_tiles
    ti = jnp.where(is_lat, i, i - n_lat_tiles)
    seg_tiles = jnp.where(is_lat, n_lat_tiles, n_tiles - n_lat_tiles)
    return ti, seg_tiles


def _halo_specs(tt, width, T, col_block):
    hb = tt // HALO
    last = T // HALO - 1
    prev = pl.BlockSpec((1, HALO, width), lambda b, i: (b, jnp.maximum(i * hb - 1, 0), col_block))
    cur = pl.BlockSpec((1, tt, width), lambda b, i: (b, i, col_block))
    nxt = pl.BlockSpec((1, HALO, width), lambda b, i: (b, jnp.minimum((i + 1) * hb, last), col_block))
    return prev, cur, nxt


def _pool_kernel(prev_ref, cur_ref, next_ref, wg_ref, sc_ref, o_ref, *,
                 tt, n_lat_tiles, n_tiles, gw, windows):
    ti, seg_tiles = _segment_of_tile(pl.program_id(1), tt, n_lat_tiles, n_tiles)
    has_prev = ti > 0
    has_next = ti < seg_tiles - 1
    t_seg = seg_tiles * tt
    tpos = ti * tt + _iota((tt, 1), 0)

    cur = cur_ref[0]
    prev = prev_ref[0]
    nxt = next_ref[0]
    d_cur = _iota((tt, tt), 1) - _iota((tt, tt), 0)
    d_halo = _iota((tt, HALO), 1) - _iota((tt, HALO), 0)
    d_prev = d_halo - HALO
    d_next = d_halo + tt
    for g, win in enumerate(windows):
        lo_off = -(win // 2)
        hi_off = win - win // 2 - 1
        sl = slice(g * gw, (g + 1) * gw)
        band_c = ((d_cur >= lo_off) & (d_cur <= hi_off)).astype(BF16)
        band_p = ((d_prev >= lo_off) & (d_prev <= hi_off) & has_prev).astype(BF16)
        band_n = ((d_next >= lo_off) & (d_next <= hi_off) & has_next).astype(BF16)
        ug = cur[:, sl]
        tot = _bdot(band_c, ug) + _bdot(band_p, prev[:, sl]) + _bdot(band_n, nxt[:, sl])
        lo = jnp.maximum(tpos + lo_off, 0)
        hi = jnp.minimum(tpos + hi_off + 1, t_seg)
        pooled = tot / (hi - lo).astype(F32) - ug.astype(F32)
        y = _bdot(pooled, wg_ref[g]) * sc_ref[:, sl]
        o_ref[0, :, sl] = y.astype(o_ref.dtype)


def _pool(p_arr, col_block, w_group, scale, n_lat):
    B, T, _ = p_arr.shape
    G, gw, _ = w_group.shape
    W = G * gw
    tt = _pick(T - n_lat, (256, 128, 64, 32, 16))
    assert n_lat % tt == 0 and tt % HALO == 0 and max(POOL_WINDOWS) <= HALO
    n_tiles = T // tt
    kern = functools.partial(_pool_kernel, tt=tt, n_lat_tiles=n_lat // tt, n_tiles=n_tiles,
                             gw=gw, windows=POOL_WINDOWS)
    prev, cur, nxt = _halo_specs(tt, W, T, col_block)
    return pl.pallas_call(
        kern,
        out_shape=jax.ShapeDtypeStruct((B, T, W), BF16),
        grid=(B, n_tiles),
        in_specs=[prev, cur, nxt,
                  pl.BlockSpec((G, gw, gw), lambda b, i: (0, 0, 0)),
                  pl.BlockSpec((1, W), lambda b, i: (0, 0))],
        out_specs=pl.BlockSpec((1, tt, W), lambda b, i: (b, i, 0)),
        compiler_params=_cparams("parallel", "parallel"),
        name="pool",
    )(p_arr, p_arr, p_arr, w_group, scale.reshape(1, W))


def _head_sum(x, head):
    lanes = x.shape[1]
    blk = min(lanes, V7X_LANES)
    same = (_iota((blk, blk), 0) // head == _iota((blk, blk), 1) // head).astype(BF16)
    parts = [_split_dot(x[:, j:j + blk], same) for j in range(0, lanes, blk)]
    return jnp.concatenate(parts, axis=1) if len(parts) > 1 else parts[0]


def _rwkv_prep_kernel(prev_ref, cur_ref, next_ref, mu_ref, w0_ref, w2_ref, a0_ref, a2_ref, g2_ref,
                      kk_w_ref, ka_ref, rk_ref,
                      r_ref, v_ref, kk_ref, lw_ref, kd_ref, bd_ref, g_ref, bonus_ref, *,
                      tt, n_lat_tiles, n_tiles, W, head, lora):
    ti, seg_tiles = _segment_of_tile(pl.program_id(1), tt, n_lat_tiles, n_tiles)
    u = cur_ref[0].astype(F32)
    row = _iota((tt, 1), 0)
    before = jnp.where(ti > 0, prev_ref[0, HALO - 1:HALO, :].astype(F32), 0.0)
    after = jnp.where(ti < seg_tiles - 1, next_ref[0, 0:1, :].astype(F32), 0.0)
    u_prev = jnp.where(row == 0, before, pltpu.roll(u, 1, 0))
    u_next = jnp.where(row == tt - 1, after, pltpu.roll(u, tt - 1, 0))
    u = u + (0.5 * (u_prev + u_next) - u) * mu_ref[...]

    r, k, v = u[:, :W], u[:, W:2 * W], u[:, 2 * W:3 * W]
    o1 = 3 * W
    o2 = o1 + lora
    o3 = o2 + lora
    w_lin = w0_ref[...] + _bdot(jnp.tanh(u[:, o1:o2]), w2_ref[...])
    w_log = -(jnp.maximum(-w_lin, 0.0) + jnp.log(1.0 + jnp.exp(-jnp.abs(w_lin)))) - 0.5
    lw_ref[0] = -jnp.exp(w_log)
    a = jax.nn.sigmoid(a0_ref[...] + _bdot(u[:, o2:o3], a2_ref[...]))
    g_ref[0] = _bdot(jax.nn.sigmoid(u[:, o3:]), g2_ref[...]).astype(g_ref.dtype)

    kk = k * kk_w_ref[...]
    kk = kk * lax.rsqrt(jnp.maximum(_head_sum(kk * kk, head), 1e-24))
    k_sum = 0.0
    for z in range(N_DIR):
        a_z = a[:, z * W:(z + 1) * W]
        k_z = k * (1.0 + (a_z - 1.0) * ka_ref[...])
        kd_ref[0, :, z * W:(z + 1) * W] = k_z.astype(kd_ref.dtype)
        bd_ref[0, :, z * W:(z + 1) * W] = (kk * a_z).astype(bd_ref.dtype)
        k_sum = k_sum + k_z
    r_ref[0] = r.astype(r_ref.dtype)
    v_ref[0] = v.astype(v_ref.dtype)
    kk_ref[0] = kk.astype(kk_ref.dtype)
    bonus_ref[0] = (_head_sum(r * k_sum * rk_ref[...], head) * v).astype(bonus_ref.dtype)


def _rwkv_prep(p_rwkv, n_lat, W, head, mu, w0, w2, a0, a2, g2, k_k, k_a, r_k):
    B, T, NS = p_rwkv.shape
    lora = N_DIR * w2.shape[1]
    n_gate = NS - 3 * W - 2 * lora
    tt = _pick(T - n_lat, (256, 128, 64, 32, 16))
    n_tiles = T // tt
    def cat(m):
        z = jnp.zeros_like(m[0])
        return jnp.concatenate([jnp.concatenate([m[0], z], axis=1),
                                jnp.concatenate([z, m[1]], axis=1)], axis=0).astype(BF16)
    g2p = jnp.pad(g2, ((0, n_gate - g2.shape[0]), (0, 0))).astype(BF16)
    mup = jnp.pad(mu, (0, NS - mu.shape[0])).reshape(1, NS)
    kern = functools.partial(_rwkv_prep_kernel, tt=tt, n_lat_tiles=n_lat // tt, n_tiles=n_tiles,
                             W=W, head=head, lora=lora)
    prev, cur, nxt = _halo_specs(tt, NS, T, 0)
    full = lambda shp: pl.BlockSpec(shp, lambda b, i: (0,) * len(shp))
    tile = lambda w: pl.BlockSpec((1, tt, w), lambda b, i: (b, i, 0))
    sd = lambda w, dt: jax.ShapeDtypeStruct((B, T, w), dt)
    return pl.pallas_call(
        kern,
        out_shape=(sd(W, BF16), sd(W, BF16), sd(W, BF16), sd(2 * W, F32), sd(2 * W, BF16),
                   sd(2 * W, BF16), sd(W, BF16), sd(W, BF16)),
        grid=(B, n_tiles),
        in_specs=[prev, cur, nxt, full((1, NS)), full((1, 2 * W)), full((lora, 2 * W)),
                  full((1, 2 * W)), full((lora, 2 * W)), full((n_gate, W)),
                  full((1, W)), full((1, W)), full((1, W))],
        out_specs=(tile(W), tile(W), tile(W), tile(2 * W), tile(2 * W), tile(2 * W), tile(W), tile(W)),
        compiler_params=_cparams("parallel", "parallel"),
        name="rwkv_prep",
    )(p_rwkv, p_rwkv, p_rwkv, mup, w0.reshape(1, 2 * W), cat(w2), a0.reshape(1, 2 * W), cat(a2), g2p,
      k_k.reshape(1, W), k_a.reshape(1, W), r_k.reshape(1, W))


def _rwkv_kernel(r_ref, v_ref, kk_ref, lw_ref, k_ref, b_ref, y_ref, g_ref, *, C, N, n_seq):
    @pl.when(pl.program_id(1) == 0)
    def _():
        g_ref[...] = jnp.zeros_like(g_ref)

    W = r_ref.shape[2]
    PW = 2 * N
    sgn = jnp.where(pl.program_id(0) >= n_seq, -1, 1)
    order = (_iota((C, C), 0) - _iota((C, C), 1)) * sgn
    lw = lw_ref[0]
    cum = jnp.dot((order >= 0).astype(F32), lw, precision=HIGHEST, preferred_element_type=F32)
    pc = jnp.sum(lw, axis=0, keepdims=True)
    k = k_ref[0].astype(F32)
    b = b_ref[0].astype(F32)
    p_inv = jnp.exp(-cum)
    p_hat = jnp.exp(pc - cum)
    r_t = r_ref[0].astype(F32) * jnp.exp(cum)
    k_t = k * p_inv
    b_t = b * p_inv
    a_t = -kk_ref[0].astype(F32) * jnp.exp(cum - lw)
    b_h = b * p_hat
    k_h = k * p_hat
    p_c = jnp.exp(pc)
    v = v_ref[0].astype(F32)

    order2 = jnp.concatenate([order, order], axis=1)
    strict2 = order2 > 0
    incl2 = order2 >= 0
    eye = (order == 0).astype(F32)
    first = _iota((1, PW), 1) < N
    lane_lo = _iota((1, 2 * C), 1) < C
    same_head = (_iota((PW, PW), 0) // N) == (_iota((PW, PW), 1) // N)

    n_pairs = W // PW
    heads = [(p, sub) for p in range(n_pairs) for sub in range(2)]
    psl = [slice(p * PW, (p + 1) * PW) for p in range(n_pairs)]
    rhs_bk = [jnp.concatenate([b_t[:, s], k_t[:, s]], axis=0).astype(BF16) for s in psl]
    a_m, v_m, upper, lower = [], [], [], []
    for p, sub in heads:
        m = first if sub == 0 else jnp.logical_not(first)
        a_m.append(jnp.where(m, a_t[:, psl[p]], 0.0))
        v_m.append(jnp.where(m, v[:, psl[p]], 0.0))
        lhs = jnp.concatenate([a_m[-1], jnp.where(m, r_t[:, psl[p]], 0.0)], axis=0)
        amat = _bdot(lhs, rhs_bk[p], NT)
        upper.append(jnp.where(strict2, amat[:C], 0.0))
        lower.append(jnp.where(incl2, amat[C:], 0.0))
    w1 = [_bdot(jnp.where(lane_lo, 0.0, up), jnp.concatenate([vm, vm], axis=0))
          for up, vm in zip(upper, v_m)]
    pw = [up[:, :C] for up in upper]
    t_inv = [eye + n_ab for n_ab in pw]
    span = 2
    while span < C:
        pw = [_bdot(x, x) for x in pw]
        t_inv = [t + _bdot(t, x) for t, x in zip(t_inv, pw)]
        span *= 2
    x = [_bdot(t, jnp.concatenate([am, w], axis=1)) for t, am, w in zip(t_inv, a_m, w1)]

    g0 = [g_ref[p] for p in range(n_pairs)]
    u = [_bdot(jnp.concatenate([x[2 * p][:, :PW], x[2 * p + 1][:, :PW]], axis=0), g0[p])
         + jnp.concatenate([x[2 * p][:, PW:], x[2 * p + 1][:, PW:]], axis=0) for p in range(n_pairs)]
    uv = [jnp.concatenate([u[p][:C], v_m[2 * p], u[p][C:], v_m[2 * p + 1]], axis=0).astype(BF16)
          for p in range(n_pairs)]
    for p in range(n_pairs):
        sl = psl[p]
        y_ref[0, 0, :, sl] = (_bdot(r_t[:, sl], g0[p])
                              + _bdot(jnp.concatenate([lower[2 * p], lower[2 * p + 1]], axis=1), uv[p]))
    for p in range(n_pairs):
        sl = psl[p]
        bk_h = jnp.concatenate([b_h[:, sl], k_h[:, sl]], axis=0)
        upd = _bdot(jnp.concatenate([bk_h, bk_h], axis=0), uv[p], TN)
        decay = jnp.transpose(jnp.broadcast_to(p_c[:, sl], (PW, PW)))
        g_ref[p] = jnp.where(same_head, decay * g0[p] + upd, 0.0)


def _rwkv_scan(r, v, kk, lw, kd, bd, head, n_lat):
    B, T, W = r.shape
    C = RWKV_CHUNK
    nc = T // C
    nc_lat = n_lat // C

    def chunk(s, c):
        fwd = jnp.where(c < nc - nc_lat, nc_lat + c, c - (nc - nc_lat))
        bwd = jnp.where(c < nc - nc_lat, nc - 1 - c, nc - 1 - c)
        return jnp.where(s >= B, bwd, fwd)

    shared = pl.BlockSpec((1, C, W), lambda s, c: (s % B, chunk(s, c), 0))
    per_dir = pl.BlockSpec((1, C, W), lambda s, c: (s % B, chunk(s, c), s // B))
    kern = functools.partial(_rwkv_kernel, C=C, N=head, n_seq=B)
    return pl.pallas_call(
        kern,
        out_shape=jax.ShapeDtypeStruct((N_DIR, B, T, W), F32),
        grid=(N_DIR * B, nc),
        in_specs=[shared, shared, shared, per_dir, per_dir, per_dir],
        out_specs=pl.BlockSpec((1, 1, C, W), lambda s, c: (s // B, s % B, chunk(s, c), 0)),
        scratch_shapes=[pltpu.VMEM((W // (2 * head), 2 * head, 2 * head), F32)],
        compiler_params=_cparams("parallel", "arbitrary"),
        name="rwkv7_chunk",
    )(r, v, kk, lw, kd, bd)


def _merge_kernel(attn_ref, pool_ref, y_ref, g_ref, bonus_ref, lnw_ref, lnb_ref, gate_ref,
                  wa_ref, wp_ref, wr_ref, o_ref, rw_ref, *, head, tn):
    @pl.when(pl.program_id(2) == 0)
    def _():
        y = y_ref[0, 0] + y_ref[1, 0]
        inv_n = 1.0 / head
        dev = y - _head_sum(y, head) * inv_n
        var = _head_sum(dev * dev, head) * inv_n
        yn = dev * lax.rsqrt(var + GN_EPS) * lnw_ref[...] + lnb_ref[...]
        rw_ref[...] = ((yn + bonus_ref[0].astype(F32)) * g_ref[0].astype(F32)).astype(rw_ref.dtype)

    def gate(z):
        return jax.nn.sigmoid(gate_ref[0, :, z * tn:(z + 1) * tn].astype(F32))

    out = (gate(0) * _bdot(attn_ref[0], wa_ref[...]) + gate(1) * _bdot(pool_ref[0], wp_ref[...])
           + gate(2) * _bdot(rw_ref[...], wr_ref[...]))
    o_ref[0] = out.astype(o_ref.dtype)


def _merge_tile(D):
    return _pick(D, (512, 256, 128))


def _merge(attn, pool, y, g, bonus, ln_w, ln_b, gates, w_a, w_p, w_r, head):
    B, T, _ = attn.shape
    W = g.shape[2]
    D = w_a.shape[1]
    tm = _row_tile(T)
    tn = _merge_tile(D)
    nj = D // tn
    row = lambda w: pl.BlockSpec((1, tm, w), lambda b, i, j: (b, i, 0))
    vec = pl.BlockSpec((1, W), lambda b, i, j: (0, 0))
    wsp = lambda w: pl.BlockSpec((w.shape[0], tn), lambda b, i, j: (0, j))
    return pl.pallas_call(
        functools.partial(_merge_kernel, head=head, tn=tn),
        out_shape=jax.ShapeDtypeStruct((B, T, D), BF16),
        grid=(B, T // tm, nj),
        in_specs=[row(attn.shape[2]), row(pool.shape[2]),
                  pl.BlockSpec((N_DIR, 1, tm, W), lambda b, i, j: (0, b, i, 0)),
                  row(W), row(W), vec, vec,
                  pl.BlockSpec((1, tm, N_BRANCH * tn), lambda b, i, j: (b, i, j)),
                  wsp(w_a), wsp(w_p), wsp(w_r)],
        out_specs=pl.BlockSpec((1, tm, tn), lambda b, i, j: (b, i, j)),
        scratch_shapes=[pltpu.VMEM((tm, W), BF16)],
        compiler_params=_cparams("parallel", "parallel", "arbitrary"),
        name="merge",
    )(attn, pool, y, g, bonus, ln_w.reshape(1, W), ln_b.reshape(1, W), gates, w_a, w_p, w_r)


def _out_kernel(m_ref, w_ref, x_ref, gpost_ref, gpre_ref, ml_ref, mc_ref, wr_ref,
                x_out, h_out, aff_out, *, tm, n_lat, n_e):
    is_lat, m_lat, m_ctx = _mod_rows(ml_ref, mc_ref, pl.program_id(1) * tm, tm, n_lat)
    mix = _bdot(m_ref[0], w_ref[...])
    normed = mix * lax.rsqrt(jnp.mean(mix * mix, axis=-1, keepdims=True) + NORM_EPS) * gpost_ref[...]
    x = x_ref[0] + jnp.where(is_lat, m_lat[2:3], m_ctx[2:3]) * normed
    x_out[0] = x
    h = _modnorm(x, gpre_ref[...], is_lat, m_lat, m_ctx, 3)
    h_out[0] = h.astype(h_out.dtype)
    h_hi = h.astype(BF16)
    part = _bdot(h_hi, wr_ref[...]) + _bdot(h - h_hi.astype(F32), wr_ref[...])
    logits = part[:, :V7X_LANES] + part[:, V7X_LANES:]
    logits = jnp.where(_iota(logits.shape, 1) < n_e, logits, -jnp.inf)
    e = jnp.exp(logits - jnp.max(logits, axis=-1, keepdims=True))
    aff_out[0] = e / jnp.sum(e, axis=-1, keepdims=True)


def _out_proj(merged, w_out, xs, g_post, g_pre, mod_l, w_router, n_lat):
    B, T, D = xs.shape
    E = w_router.shape[1]
    tm = _pick(T, (544, 272, 256, 128, 320, 64, 32, 16))
    wr = jnp.pad(w_router, ((0, 0), (0, V7X_LANES - E)))
    wr_hi = wr.astype(BF16)
    wr = jnp.concatenate([wr_hi, (wr - wr_hi.astype(F32)).astype(BF16)], axis=1)
    row = lambda w: pl.BlockSpec((1, tm, w), lambda b, i: (b, i, 0))
    vec = pl.BlockSpec((1, D), lambda b, i: (0, 0))
    return pl.pallas_call(
        functools.partial(_out_kernel, tm=tm, n_lat=n_lat, n_e=E),
        out_shape=(jax.ShapeDtypeStruct((B, T, D), F32), jax.ShapeDtypeStruct((B, T, D), BF16),
                   jax.ShapeDtypeStruct((B, T, V7X_LANES), F32)),
        grid=(B, T // tm),
        in_specs=[row(D), pl.BlockSpec((D, D), lambda b, i: (0, 0)), row(D), vec, vec,
                  pl.BlockSpec((1, 6, D), lambda b, i: (b, 0, 0)),
                  pl.BlockSpec((1, 6, D), lambda b, i: (B, 0, 0)),
                  pl.BlockSpec((D, 2 * V7X_LANES), lambda b, i: (0, 0))],
        out_specs=(row(D), row(D), row(V7X_LANES)),
        compiler_params=_cparams("parallel", "parallel"),
        name="out_proj",
    )(merged, w_out, xs, g_post.reshape(1, D), g_pre.reshape(1, D), mod_l, mod_l, wr)


def _excl_prefix(flags, blk):
    n = flags.shape[-1]
    upper = (_iota((blk, blk), 0) < _iota((blk, blk), 1)).astype(BF16)
    outs = []
    carry = jnp.zeros((flags.shape[0], 1), F32)
    for j in range(n // blk):
        seg = flags[:, j * blk:(j + 1) * blk]
        outs.append(_bdot(seg, upper) + carry)
        carry = carry + jnp.sum(seg, axis=-1, keepdims=True)
    return jnp.concatenate(outs, axis=-1) if len(outs) > 1 else outs[0]


def _topk_kernel(aff_ref, sel_ref, *, cap, blk):
    bits = lax.bitcast_convert_type(aff_ref[0], jnp.int32)
    E = bits.shape[0]

    def body(i, tau):
        cand = tau | jnp.left_shift(jnp.int32(1), 30 - i)
        cnt = jnp.sum((bits >= cand).astype(jnp.int32), axis=-1, keepdims=True)
        return jnp.where(cnt >= cap, cand, tau)

    tau = lax.fori_loop(0, 31, body, jnp.zeros((E, 1), jnp.int32))
    gt = bits > tau
    eq = bits == tau
    need = (cap - jnp.sum(gt.astype(jnp.int32), axis=-1, keepdims=True)).astype(F32)
    eq_rank = _excl_prefix(eq.astype(F32), blk)
    sel = gt | (eq & (eq_rank < need))
    pos = _excl_prefix(sel.astype(F32), blk)
    sel_ref[0] = jnp.where(sel, pos.astype(jnp.int32), -1)


def _topk_slots(aff_t, cap):
    B, E, n = aff_t.shape
    blk = _pick(n, (512, 256, 128))
    return pl.pallas_call(
        functools.partial(_topk_kernel, cap=cap, blk=blk),
        out_shape=jax.ShapeDtypeStruct((B, E, n), jnp.int32),
        grid=(B,),
        in_specs=[pl.BlockSpec((1, E, n), lambda b: (b, 0, 0))],
        out_specs=pl.BlockSpec((1, E, n), lambda b: (b, 0, 0)),
        compiler_params=_cparams("parallel"),
        name="expert_topk",
    )(aff_t)


def _gather_kernel(sel_ref, h_ref, xe_ref, oh_ref, *, cap):
    @pl.when(pl.program_id(2) == 0)
    def _():
        sel = sel_ref[0, 0]
        oh_ref[...] = (sel == _iota((cap, sel.shape[1]), 0)).astype(oh_ref.dtype)

    xe_ref[0, 0] = _bdot(oh_ref[...], h_ref[0]).astype(xe_ref.dtype)


def _gather(sel, h, row_block, cap):
    B, E, n = sel.shape
    D = h.shape[2]
    td = _pick(D, (512, 256, 128))
    return pl.pallas_call(
        functools.partial(_gather_kernel, cap=cap),
        out_shape=jax.ShapeDtypeStruct((B, E, cap, D), BF16),
        grid=(B, E, D // td),
        in_specs=[pl.BlockSpec((1, 1, 1, n), lambda b, e, j: (b, e, 0, 0)),
                  pl.BlockSpec((1, n, td), lambda b, e, j: (b, row_block, j))],
        out_specs=pl.BlockSpec((1, 1, cap, td), lambda b, e, j: (b, e, 0, j)),
        scratch_shapes=[pltpu.VMEM((cap, n), BF16)],
        compiler_params=_cparams("parallel", "parallel", "arbitrary"),
        name="expert_gather",
    )(sel.reshape(B, E, 1, n), h)


def _ffn_kernel(xe_ref, wg_ref, wu_ref, wd_ref, ye_ref):
    xe = xe_ref[0, 0]
    gate = _bdot(xe, wg_ref[0])
    up = _bdot(xe, wu_ref[0])
    hid = gate * jax.nn.sigmoid(gate) * up
    ye_ref[0, 0] = _bdot(hid, wd_ref[0]).astype(ye_ref.dtype)


def _expert_ffn(xe, w_gate, w_up, w_down):
    B, E, cap, D = xe.shape
    FF = w_gate.shape[2]
    return pl.pallas_call(
        _ffn_kernel,
        out_shape=jax.ShapeDtypeStruct((B, E, cap, D), BF16),
        grid=(E, B),
        in_specs=[pl.BlockSpec((1, 1, cap, D), lambda e, b: (b, e, 0, 0)),
                  pl.BlockSpec((1, D, FF), lambda e, b: (e, 0, 0)),
                  pl.BlockSpec((1, D, FF), lambda e, b: (e, 0, 0)),
                  pl.BlockSpec((1, FF, D), lambda e, b: (e, 0, 0))],
        out_specs=pl.BlockSpec((1, 1, cap, D), lambda e, b: (b, e, 0, 0)),
        compiler_params=_cparams("parallel", "parallel"),
        name="expert_ffn",
    )(xe, w_gate, w_up, w_down)


def _scatter_kernel(selc_ref, affc_ref, ye_ref, o_ref, *, cap):
    @pl.when(pl.program_id(2) == 0)
    def _():
        o_ref[...] = jnp.zeros_like(o_ref)

    selc = selc_ref[0, 0]
    onehot = selc == _iota((selc.shape[0], cap), 1)
    o_ref[0] += affc_ref[0, 0] * _bdot(onehot.astype(BF16), ye_ref[0, 0])


def _scatter(sel, aff_t, ye):
    B, E, n = sel.shape
    cap, D = ye.shape[2:]
    td = _pick(D, (512, 256, 128))
    return pl.pallas_call(
        functools.partial(_scatter_kernel, cap=cap),
        out_shape=jax.ShapeDtypeStruct((B, n, D), F32),
        grid=(B, D // td, E),
        in_specs=[pl.BlockSpec((1, 1, n, 1), lambda b, j, e: (b, e, 0, 0)),
                  pl.BlockSpec((1, 1, n, 1), lambda b, j, e: (b, e, 0, 0)),
                  pl.BlockSpec((1, 1, cap, td), lambda b, j, e: (b, e, 0, j))],
        out_specs=pl.BlockSpec((1, n, td), lambda b, j, e: (b, 0, j)),
        compiler_params=_cparams("parallel", "parallel", "arbitrary"),
        name="expert_scatter",
    )(sel.reshape(B, E, n, 1), aff_t.reshape(B, E, n, 1), ye)


def _expert_choice(hb, aff, row_block, n, w_gate, w_up, w_down):
    E = w_gate.shape[0]
    cap = EC_FACTOR * n // E
    aff_t = jnp.swapaxes(aff[:, row_block * n:(row_block + 1) * n, :E], 1, 2)
    sel = _topk_slots(aff_t, cap)
    xe = _gather(sel, hb, row_block, cap)
    ye = _expert_ffn(xe, w_gate, w_up, w_down)
    return _scatter(sel, aff_t, ye)


def _rms(x, g):
    return x * lax.rsqrt(jnp.mean(x * x, axis=-1, keepdims=True) + NORM_EPS) * g


def _rope_tables(n_lat, n_ctx, head_dim):
    rows = n_lat // GRID_W
    row = jnp.repeat(jnp.arange(rows), GRID_W).astype(F32)
    col = (jnp.arange(rows * GRID_W) % GRID_W).astype(F32)
    half = head_dim // 2
    inv = ROPE_THETA ** (-jnp.arange(0, half, 2, dtype=F32) / half)
    ar, ac = row[:, None] * inv, col[:, None] * inv
    cos = jnp.concatenate([jnp.cos(ar), jnp.cos(ar), jnp.cos(ac), jnp.cos(ac)], axis=-1)
    sin = jnp.concatenate([-jnp.sin(ar), jnp.sin(ar), -jnp.sin(ac), jnp.sin(ac)], axis=-1)
    cos = jnp.concatenate([cos, jnp.ones((n_ctx, head_dim), F32)], axis=0)
    sin = jnp.concatenate([sin, jnp.zeros((n_ctx, head_dim), F32)], axis=0)
    return cos, sin


def _pad_cols(w, mult):
    return jnp.pad(w, ((0, 0), (0, (-w.shape[1]) % mult)))


def kernel(x, c, ctx, c_ctx, w_mod, b_mod, norm_pre, norm_post, w_in, q_norm, k_norm, w_attn_o, w_pool_group, pool_scale, w_pool_o, rwkv_mu, rwkv_w0, rwkv_w2, rwkv_a0, rwkv_a2, rwkv_g2, rwkv_k_k, rwkv_k_a, rwkv_r_k, rwkv_ln_w, rwkv_ln_b, w_rwkv_o, w_out, w_router, w_exp_gate, w_exp_up, w_exp_down):
    B, n_lat, D = x.shape
    n_ctx = ctx.shape[1]
    T = n_lat + n_ctx
    depth = w_mod.shape[0]
    hd = q_norm.shape[1]
    attn_w = w_attn_o.shape[1]
    pool_w = w_pool_o.shape[1]
    rwkv_w = w_rwkv_o.shape[1]
    n_shift = rwkv_mu.shape[1]
    n_in = w_in.shape[2]
    kv_w = (n_in - attn_w - pool_w - n_shift - N_BRANCH * D) // 2
    head = rwkv_r_k.shape[2]
    col_k = attn_w
    col_v = col_k + kv_w
    col_pool = col_v + kv_w
    col_r = col_pool + pool_w
    col_gate = col_r + n_shift
    hq, hkv = attn_w // hd, kv_w // hd
    assert n_lat % (hq // hkv * hd) == 0 and pool_w % (hq // hkv * hd) == 0 and n_ctx % RWKV_CHUNK == 0

    cos, sin = _rope_tables(n_lat, n_ctx, hd)
    s_all = jnp.concatenate([jax.nn.silu(c), jax.nn.silu(c_ctx)[None]], axis=0)
    s_all = jnp.pad(s_all, ((0, (-s_all.shape[0]) % 8), (0, 0)))
    mod = _modulation(s_all, w_mod, b_mod)[:, :B + 1].reshape(depth, B + 1, 6, D)
    xs = jnp.concatenate([x, ctx], axis=1)
    (h,) = _stream_update(xs, None, None, None, mod[0], norm_pre[0, 0], mod[0], n_lat, T)

    for l in range(depth):
        keep_ctx = l < depth - 1
        wl = w_in[l]
        w_a = jnp.concatenate([wl[:, col_pool:col_r], wl[:, :col_pool]], axis=1).astype(BF16)
        w_b = _pad_cols(wl[:, col_r:col_gate], 3 * V7X_LANES).astype(BF16)
        tn_m = _merge_tile(D)
        w_c = (wl[:, col_gate:].reshape(D, N_BRANCH, D // tn_m, tn_m).swapaxes(1, 2)
               .reshape(D, N_BRANCH * D).astype(BF16))
        p_a = _in_proj(h, w_a)
        p_b = _in_proj(h, w_b)
        p_c = _in_proj(h, w_c)

        attn = _attention(p_a, pool_w, pool_w + attn_w, pool_w + attn_w + kv_w, cos, sin,
                          q_norm[l], k_norm[l], hd, hq, hkv, n_lat)
        pool = _pool(p_a, 0, w_pool_group[l].astype(BF16), pool_scale[l], n_lat)
        r, v, kk, lw, kd, bd, g, bonus = _rwkv_prep(
            p_b, n_lat, rwkv_w, head, rwkv_mu[l], rwkv_w0[l], rwkv_w2[l], rwkv_a0[l], rwkv_a2[l],
            rwkv_g2[l], rwkv_k_k[l], rwkv_k_a[l], rwkv_r_k[l])
        y = _rwkv_scan(r, v, kk, lw, kd, bd, head, n_lat)
        merged = _merge(attn, pool, y, g, bonus, rwkv_ln_w[l], rwkv_ln_b[l], p_c,
                        w_attn_o[l].astype(BF16), w_pool_o[l].astype(BF16), w_rwkv_o[l].astype(BF16), head)
        xs, hb, aff = _out_proj(merged, w_out[l].astype(BF16), xs, norm_post[l, 0], norm_pre[l, 1],
                                mod[l], w_router[l], n_lat)

        wg, wu, wdn = w_exp_gate[l].astype(BF16), w_exp_up[l].astype(BF16), w_exp_down[l].astype(BF16)
        f_lat = _expert_choice(hb, aff, 0, n_lat, wg, wu, wdn)
        if keep_ctx:
            f_ctx = _expert_choice(hb, aff, n_lat // n_ctx, n_ctx, wg, wu, wdn)
            xs, h = _stream_update(xs, f_lat, f_ctx, norm_post[l, 1], mod[l], norm_pre[l + 1, 0],
                                   mod[l + 1], n_lat, T)
        else:
            (xs,) = _stream_update(xs, f_lat, None, norm_post[l, 1], mod[l], None, None, n_lat, n_lat)
    return xs
```

```python
import functools

import jax
import jax.numpy as jnp
from jax import lax
from jax.experimental import pallas as pl
from jax.experimental.pallas import tpu as pltpu

F32 = jnp.float32
BF16 = jnp.bfloat16
HIGHEST = lax.Precision.HIGHEST

GRID_W = 64
NORM_EPS = 1e-6
ROPE_THETA = 10000.0
POOL_WINDOWS = (2, 4, 8, 16)
GN_EPS = 64e-5
EC_FACTOR = 2
N_DIR = 2
N_BRANCH = 3
RWKV_CHUNK = 64
HALO = 16

LOG2_E = 1.4426950408889634
TOKEN_RADIX = 64

V7X_LANES = 128
VMEM_LIMIT = 52 * 1024 * 1024

NT = (((1,), (1,)), ((), ()))
TN = (((0,), (0,)), ((), ()))


def _pick(n, cands):
    for c in cands:
        if c <= n and n % c == 0:
            return c
    return n


def _cparams(*sem):
    return pltpu.CompilerParams(dimension_semantics=sem, vmem_limit_bytes=VMEM_LIMIT)


def _bdot(x, y, dn=None):
    x = x.astype(BF16)
    y = y.astype(BF16)
    if dn is None:
        return jnp.dot(x, y, preferred_element_type=F32)
    return lax.dot_general(x, y, dn, preferred_element_type=F32)


def _split_dot(x, y):
    hi = x.astype(BF16)
    lo = x - hi.astype(F32)
    return _bdot(hi, y) + _bdot(lo, y)


def _iota(shape, dim):
    return lax.broadcasted_iota(jnp.int32, shape, dim)


def _row_tile(T, min_parts=8):
    for parts in range(min_parts, T // 16 + 1):
        if T % parts == 0 and (T // parts) % 16 == 0:
            return T // parts
    return T


def _mod_kernel(s_ref, w_ref, b_ref, o_ref):
    o_ref[0] = jnp.dot(s_ref[...], w_ref[0], precision=HIGHEST,
                       preferred_element_type=F32) + b_ref[0]


def _modulation(s, w_mod, b_mod):
    L, D, N = w_mod.shape
    R = s.shape[0]
    tn = _pick(N, (1024, 512, 256, 128))
    return pl.pallas_call(
        _mod_kernel,
        out_shape=jax.ShapeDtypeStruct((L, R, N), F32),
        grid=(L, N // tn),
        in_specs=[pl.BlockSpec((R, D), lambda l, j: (0, 0)),
                  pl.BlockSpec((1, D, tn), lambda l, j: (l, 0, j)),
                  pl.BlockSpec((1, 1, tn), lambda l, j: (l, 0, j))],
        out_specs=pl.BlockSpec((1, R, tn), lambda l, j: (l, 0, j)),
        compiler_params=_cparams("parallel", "parallel"),
        name="adaln_mod",
    )(s, w_mod, b_mod.reshape(L, 1, N))


def _mod_rows(m_lat_ref, m_ctx_ref, row0, tm, n_lat):
    is_lat = (row0 + _iota((tm, 1), 0)) < n_lat
    return is_lat, m_lat_ref[0], m_ctx_ref[0]


def _modnorm(x, g, is_lat, m_lat, m_ctx, i_shift):
    shift = jnp.where(is_lat, m_lat[i_shift:i_shift + 1], m_ctx[i_shift:i_shift + 1])
    scale = jnp.where(is_lat, m_lat[i_shift + 1:i_shift + 2], m_ctx[i_shift + 1:i_shift + 2])
    y = x * lax.rsqrt(jnp.mean(x * x, axis=-1, keepdims=True) + NORM_EPS) * g
    return y * (1.0 + scale) + shift


def _small_row_tile(T, n_ctx):
    return _pick(n_ctx, (256, 128, 64, 32, 16))


def _stream_kernel(*refs, tm, n_lat, has_f, has_ctx_f, emit_h):
    refs = list(refs)
    x_ref = refs.pop(0)
    fl_ref = refs.pop(0) if has_f else None
    fc_ref = refs.pop(0) if has_ctx_f else None
    gpost_ref = refs.pop(0) if has_f else None
    ml_ref, mc_ref = refs.pop(0), refs.pop(0)
    if emit_h:
        gpre_ref, mln_ref, mcn_ref = refs.pop(0), refs.pop(0), refs.pop(0)
    x_out = refs.pop(0) if has_f else None
    h_out = refs.pop(0) if emit_h else None

    is_lat = (pl.program_id(1) * tm + _iota((tm, 1), 0)) < n_lat
    x = x_ref[0]
    if has_f:
        f = fl_ref[0]
        if has_ctx_f:
            f = jnp.where(is_lat, f, fc_ref[0])
        normed = f * lax.rsqrt(jnp.mean(f * f, axis=-1, keepdims=True) + NORM_EPS) * gpost_ref[...]
        x = x + jnp.where(is_lat, ml_ref[0][5:6], mc_ref[0][5:6]) * normed
        x_out[0] = x
    if emit_h:
        h_out[0] = _modnorm(x, gpre_ref[...], is_lat, mln_ref[0], mcn_ref[0], 0).astype(h_out.dtype)


def _stream_update(xs, f_lat, f_ctx, g_post, mod_l, g_pre_next, mod_next, n_lat, n_rows):
    B, T, D = xs.shape
    n_ctx = T - n_lat
    tm = _small_row_tile(T, n_ctx)
    has_f, has_ctx_f, emit_h = f_lat is not None, f_ctx is not None, g_pre_next is not None
    n_lt = n_lat // tm
    row = pl.BlockSpec((1, tm, D), lambda b, i: (b, i, 0))
    vec = pl.BlockSpec((1, D), lambda b, i: (0, 0))
    m_l = pl.BlockSpec((1, 6, D), lambda b, i: (b, 0, 0))
    m_c = pl.BlockSpec((1, 6, D), lambda b, i: (B, 0, 0))
    args, specs = [xs], [row]
    if has_f:
        args.append(f_lat)
        specs.append(pl.BlockSpec((1, tm, D), lambda b, i: (b, jnp.minimum(i, n_lt - 1), 0)))
    if has_ctx_f:
        args.append(f_ctx)
        specs.append(pl.BlockSpec((1, tm, D), lambda b, i: (b, jnp.maximum(i - n_lt, 0), 0)))
    if has_f:
        args.append(g_post.reshape(1, D))
        specs.append(vec)
    args += [mod_l, mod_l]
    specs += [m_l, m_c]
    if emit_h:
        args += [g_pre_next.reshape(1, D), mod_next, mod_next]
        specs += [vec, m_l, m_c]
    out_shape, out_specs = [], []
    if has_f:
        out_shape.append(jax.ShapeDtypeStruct((B, n_rows, D), F32))
        out_specs.append(row)
    if emit_h:
        out_shape.append(jax.ShapeDtypeStruct((B, n_rows, D), BF16))
        out_specs.append(row)
    kern = functools.partial(_stream_kernel, tm=tm, n_lat=n_lat, has_f=has_f, has_ctx_f=has_ctx_f,
                             emit_h=emit_h)
    return pl.pallas_call(
        kern, out_shape=tuple(out_shape), grid=(B, n_rows // tm), in_specs=specs,
        out_specs=tuple(out_specs), compiler_params=_cparams("parallel", "parallel"),
        name="stream_update",
    )(*args)


def _in_kernel(h_ref, w_ref, o_ref):
    o_ref[0] = _bdot(h_ref[0], w_ref[...]).astype(o_ref.dtype)


def _in_proj(h, w):
    B, T, D = h.shape
    N = w.shape[1]
    tm = _row_tile(T, 4)
    tn = _pick(N, (1280, 1024, 896, 768, 640, 512, 384, 256, 128))
    return pl.pallas_call(
        _in_kernel,
        out_shape=jax.ShapeDtypeStruct((B, T, N), BF16),
        grid=(B, T // tm, N // tn),
        in_specs=[pl.BlockSpec((1, tm, D), lambda b, i, j: (b, i, 0)),
                  pl.BlockSpec((D, tn), lambda b, i, j: (0, j))],
        out_specs=pl.BlockSpec((1, tm, tn), lambda b, i, j: (b, i, j)),
        compiler_params=_cparams("parallel", "parallel", "arbitrary"),
        name="in_proj",
    )(h, w)


def _rot(x, cos, sin):
    qd = x.shape[1] // 4
    first = (_iota((1, x.shape[1]), 1) // qd) % 2 == 0
    swapped = jnp.where(first, pltpu.roll(x, x.shape[1] - qd, 1), pltpu.roll(x, qd, 1))
    return x * cos + swapped * sin


def _head_norm(x, g):
    return x * lax.rsqrt(jnp.mean(x * x, axis=-1, keepdims=True) + NORM_EPS) * g


def _attn_kernel(q_ref, k_ref, v_ref, cq_ref, sq_ref, ck_ref, sk_ref, qn_ref, kn_ref, o_ref,
                 kp_ref, vp_ref, *, hd, group, tq, n_lat, scale):
    i = pl.program_id(2)

    @pl.when(i == 0)
    def _():
        kp_ref[...] = _rot(_head_norm(k_ref[0].astype(F32), kn_ref[...]),
                           ck_ref[...], sk_ref[...]).astype(kp_ref.dtype)
        ones = jnp.ones(v_ref.shape[1:], vp_ref.dtype)
        vp_ref[...] = jnp.concatenate([v_ref[0].astype(vp_ref.dtype), ones], axis=1)

    def attend(key0):
        for g in range(group):
            q = _head_norm(q_ref[0, :, g * hd:(g + 1) * hd].astype(F32), qn_ref[...])
            q = _rot(q, cq_ref[...], sq_ref[...]) * scale
            s = _bdot(q, kp_ref[key0:, :], NT)
            p = jnp.exp2(s - jnp.max(s, axis=-1, keepdims=True))
            ov = _bdot(p, vp_ref[key0:, :])
            o_ref[0, :, g * hd:(g + 1) * hd] = (ov[:, :hd] / ov[:, hd:hd + 1]).astype(o_ref.dtype)

    @pl.when(i * tq < n_lat)
    def _():
        attend(0)

    @pl.when(i * tq >= n_lat)
    def _():
        attend(n_lat)


def _attention(p_qkv, q_col, k_col, v_col, cos, sin, q_norm, k_norm, hd, hq, hkv, n_lat):
    B, T, _ = p_qkv.shape
    group = hq // hkv
    gw = group * hd
    tq = _pick(T - n_lat, (256, 128, 64, 32, 16))
    kern = functools.partial(_attn_kernel, hd=hd, group=group, tq=tq, n_lat=n_lat,
                             scale=hd ** -0.5 * LOG2_E)
    tab_q = pl.BlockSpec((tq, hd), lambda b, h, i: (i, 0))
    tab_k = pl.BlockSpec((T, hd), lambda b, h, i: (0, 0))
    vec = pl.BlockSpec((1, hd), lambda b, h, i: (0, 0))
    return pl.pallas_call(
        kern,
        out_shape=jax.ShapeDtypeStruct((B, T, hq * hd), BF16),
        grid=(B, hkv, T // tq),
        in_specs=[pl.BlockSpec((1, tq, gw), lambda b, h, i: (b, i, q_col // gw + h)),
                  pl.BlockSpec((1, T, hd), lambda b, h, i: (b, 0, k_col // hd + h)),
                  pl.BlockSpec((1, T, hd), lambda b, h, i: (b, 0, v_col // hd + h)),
                  tab_q, tab_q, tab_k, tab_k, vec, vec],
        out_specs=pl.BlockSpec((1, tq, gw), lambda b, h, i: (b, i, h)),
        scratch_shapes=[pltpu.VMEM((T, hd), BF16), pltpu.VMEM((T, 2 * hd), BF16)],
        compiler_params=_cparams("parallel", "parallel", "arbitrary"),
        name="attention",
    )(p_qkv, p_qkv, p_qkv, cos, sin, cos, sin, q_norm.reshape(1, hd), k_norm.reshape(1, hd))


def _segment_of_tile(i, tt, n_lat_tiles, n_tiles):
    is_lat = i < n_lat_tiles
    ti = jnp.where(is_lat, i, i - n_lat_tiles)
    seg_tiles = jnp.where(is_lat, n_lat_tiles, n_tiles - n_lat_tiles)
    return ti, seg_tiles


def _halo_specs(tt, width, T, col_block):
    hb = tt // HALO
    last = T // HALO - 1
    prev = pl.BlockSpec((1, HALO, width), lambda b, i: (b, jnp.maximum(i * hb - 1, 0), col_block))
    cur = pl.BlockSpec((1, tt, width), lambda b, i: (b, i, col_block))
    nxt = pl.BlockSpec((1, HALO, width), lambda b, i: (b, jnp.minimum((i + 1) * hb, last), col_block))
    return prev, cur, nxt


def _pool_kernel(prev_ref, cur_ref, next_ref, wg_ref, sc_ref, o_ref, *,
                 tt, n_lat_tiles, n_tiles, gw, windows):
    ti, seg_tiles = _segment_of_tile(pl.program_id(1), tt, n_lat_tiles, n_tiles)
    has_prev = ti > 0
    has_next = ti < seg_tiles - 1
    t_seg = seg_tiles * tt
    tpos = ti * tt + _iota((tt, 1), 0)

    cur = cur_ref[0]
    prev = prev_ref[0]
    nxt = next_ref[0]
    d_cur = _iota((tt, tt), 1) - _iota((tt, tt), 0)
    d_halo = _iota((tt, HALO), 1) - _iota((tt, HALO), 0)
    d_prev = d_halo - HALO
    d_next = d_halo + tt
    for g, win in enumerate(windows):
        lo_off = -(win // 2)
        hi_off = win - win // 2 - 1
        sl = slice(g * gw, (g + 1) * gw)
        band_c = ((d_cur >= lo_off) & (d_cur <= hi_off)).astype(BF16)
        band_p = ((d_prev >= lo_off) & (d_prev <= hi_off) & has_prev).astype(BF16)
        band_n = ((d_next >= lo_off) & (d_next <= hi_off) & has_next).astype(BF16)
        ug = cur[:, sl]
        tot = _bdot(band_c, ug) + _bdot(band_p, prev[:, sl]) + _bdot(band_n, nxt[:, sl])
        lo = jnp.maximum(tpos + lo_off, 0)
        hi = jnp.minimum(tpos + hi_off + 1, t_seg)
        pooled = tot / (hi - lo).astype(F32) - ug.astype(F32)
        y = _bdot(pooled, wg_ref[g]) * sc_ref[:, sl]
        o_ref[0, :, sl] = y.astype(o_ref.dtype)


def _pool(p_arr, col_block, w_group, scale, n_lat):
    B, T, _ = p_arr.shape
    G, gw, _ = w_group.shape
    W = G * gw
    tt = _pick(T - n_lat, (256, 128, 64, 32, 16))
    assert n_lat % tt == 0 and tt % HALO == 0 and max(POOL_WINDOWS) <= HALO
    n_tiles = T // tt
    kern = functools.partial(_pool_kernel, tt=tt, n_lat_tiles=n_lat // tt, n_tiles=n_tiles,
                             gw=gw, windows=POOL_WINDOWS)
    prev, cur, nxt = _halo_specs(tt, W, T, col_block)
    return pl.pallas_call(
        kern,
        out_shape=jax.ShapeDtypeStruct((B, T, W), BF16),
        grid=(B, n_tiles),
        in_specs=[prev, cur, nxt,
                  pl.BlockSpec((G, gw, gw), lambda b, i: (0, 0, 0)),
                  pl.BlockSpec((1, W), lambda b, i: (0, 0))],
        out_specs=pl.BlockSpec((1, tt, W), lambda b, i: (b, i, 0)),
        compiler_params=_cparams("parallel", "parallel"),
        name="pool",
    )(p_arr, p_arr, p_arr, w_group, scale.reshape(1, W))


def _head_sum(x, head):
    lanes = x.shape[1]
    blk = min(lanes, V7X_LANES)
    same = (_iota((blk, blk), 0) // head == _iota((blk, blk), 1) // head).astype(BF16)
    parts = [_split_dot(x[:, j:j + blk], same) for j in range(0, lanes, blk)]
    return jnp.concatenate(parts, axis=1) if len(parts) > 1 else parts[0]


def _rwkv_prep_kernel(prev_ref, cur_ref, next_ref, mu_ref, w0_ref, w2_ref, a0_ref, a2_ref, g2_ref,
                      kk_w_ref, ka_ref, rk_ref,
                      r_ref, v_ref, kk_ref, lw_ref, kd_ref, bd_ref, g_ref, bonus_ref, *,
                      tt, n_lat_tiles, n_tiles, W, head, lora):
    ti, seg_tiles = _segment_of_tile(pl.program_id(1), tt, n_lat_tiles, n_tiles)
    u = cur_ref[0].astype(F32)
    row = _iota((tt, 1), 0)
    before = jnp.where(ti > 0, prev_ref[0, HALO - 1:HALO, :].astype(F32), 0.0)
    after = jnp.where(ti < seg_tiles - 1, next_ref[0, 0:1, :].astype(F32), 0.0)
    u_prev = jnp.where(row == 0, before, pltpu.roll(u, 1, 0))
    u_next = jnp.where(row == tt - 1, after, pltpu.roll(u, tt - 1, 0))
    u = u + (0.5 * (u_prev + u_next) - u) * mu_ref[...]

    r, k, v = u[:, :W], u[:, W:2 * W], u[:, 2 * W:3 * W]
    o1 = 3 * W
    o2 = o1 + lora
    o3 = o2 + lora
    w_lin = w0_ref[...] + _bdot(jnp.tanh(u[:, o1:o2]), w2_ref[...])
    w_log = -(jnp.maximum(-w_lin, 0.0) + jnp.log(1.0 + jnp.exp(-jnp.abs(w_lin)))) - 0.5
    lw_ref[0] = -jnp.exp(w_log)
    a = jax.nn.sigmoid(a0_ref[...] + _bdot(u[:, o2:o3], a2_ref[...]))
    g_ref[0] = _bdot(jax.nn.sigmoid(u[:, o3:]), g2_ref[...]).astype(g_ref.dtype)

    kk = k * kk_w_ref[...]
    kk = kk * lax.rsqrt(jnp.maximum(_head_sum(kk * kk, head), 1e-24))
    k_sum = 0.0
    for z in range(N_DIR):
        a_z = a[:, z * W:(z + 1) * W]
        k_z = k * (1.0 + (a_z - 1.0) * ka_ref[...])
        kd_ref[0, :, z * W:(z + 1) * W] = k_z.astype(kd_ref.dtype)
        bd_ref[0, :, z * W:(z + 1) * W] = (kk * a_z).astype(bd_ref.dtype)
        k_sum = k_sum + k_z
    r_ref[0] = r.astype(r_ref.dtype)
    v_ref[0] = v.astype(v_ref.dtype)
    kk_ref[0] = kk.astype(kk_ref.dtype)
    bonus_ref[0] = (_head_sum(r * k_sum * rk_ref[...], head) * v).astype(bonus_ref.dtype)


def _rwkv_prep(p_rwkv, n_lat, W, head, mu, w0, w2, a0, a2, g2, k_k, k_a, r_k):
    B, T, NS = p_rwkv.shape
    lora = N_DIR * w2.shape[1]
    n_gate = NS - 3 * W - 2 * lora
    tt = _pick(T - n_lat, (256, 128, 64, 32, 16))
    n_tiles = T // tt
    def cat(m):
        z = jnp.zeros_like(m[0])
        return jnp.concatenate([jnp.concatenate([m[0], z], axis=1),
                                jnp.concatenate([z, m[1]], axis=1)], axis=0).astype(BF16)
    g2p = jnp.pad(g2, ((0, n_gate - g2.shape[0]), (0, 0))).astype(BF16)
    mup = jnp.pad(mu, (0, NS - mu.shape[0])).reshape(1, NS)
    kern = functools.partial(_rwkv_prep_kernel, tt=tt, n_lat_tiles=n_lat // tt, n_tiles=n_tiles,
                             W=W, head=head, lora=lora)
    prev, cur, nxt = _halo_specs(tt, NS, T, 0)
    full = lambda shp: pl.BlockSpec(shp, lambda b, i: (0,) * len(shp))
    tile = lambda w: pl.BlockSpec((1, tt, w), lambda b, i: (b, i, 0))
    sd = lambda w, dt: jax.ShapeDtypeStruct((B, T, w), dt)
    return pl.pallas_call(
        kern,
        out_shape=(sd(W, BF16), sd(W, BF16), sd(W, BF16), sd(2 * W, F32), sd(2 * W, BF16),
                   sd(2 * W, BF16), sd(W, BF16), sd(W, BF16)),
        grid=(B, n_tiles),
        in_specs=[prev, cur, nxt, full((1, NS)), full((1, 2 * W)), full((lora, 2 * W)),
                  full((1, 2 * W)), full((lora, 2 * W)), full((n_gate, W)),
                  full((1, W)), full((1, W)), full((1, W))],
        out_specs=(tile(W), tile(W), tile(W), tile(2 * W), tile(2 * W), tile(2 * W), tile(W), tile(W)),
        compiler_params=_cparams("parallel", "parallel"),
        name="rwkv_prep",
    )(p_rwkv, p_rwkv, p_rwkv, mup, w0.reshape(1, 2 * W), cat(w2), a0.reshape(1, 2 * W), cat(a2), g2p,
      k_k.reshape(1, W), k_a.reshape(1, W), r_k.reshape(1, W))


def _rwkv_kernel(r_ref, v_ref, kk_ref, lw_ref, k_ref, b_ref, y_ref, g_ref, *, C, N, n_seq):
    @pl.when(pl.program_id(1) == 0)
    def _():
        g_ref[...] = jnp.zeros_like(g_ref)

    W = r_ref.shape[2]
    PW = 2 * N
    sgn = jnp.where(pl.program_id(0) >= n_seq, -1, 1)
    order = (_iota((C, C), 0) - _iota((C, C), 1)) * sgn
    lw = lw_ref[0]
    cum = jnp.dot((order >= 0).astype(F32), lw, precision=HIGHEST, preferred_element_type=F32)
    pc = jnp.sum(lw, axis=0, keepdims=True)
    k = k_ref[0].astype(F32)
    b = b_ref[0].astype(F32)
    p_inv = jnp.exp(-cum)
    p_hat = jnp.exp(pc - cum)
    r_t = r_ref[0].astype(F32) * jnp.exp(cum)
    k_t = k * p_inv
    b_t = b * p_inv
    a_t = -kk_ref[0].astype(F32) * jnp.exp(cum - lw)
    b_h = b * p_hat
    k_h = k * p_hat
    p_c = jnp.exp(pc)
    v = v_ref[0].astype(F32)

    order2 = jnp.concatenate([order, order], axis=1)
    strict2 = order2 > 0
    incl2 = order2 >= 0
    eye = (order == 0).astype(F32)
    first = _iota((1, PW), 1) < N
    lane_lo = _iota((1, 2 * C), 1) < C
    same_head = (_iota((PW, PW), 0) // N) == (_iota((PW, PW), 1) // N)

    n_pairs = W // PW
    heads = [(p, sub) for p in range(n_pairs) for sub in range(2)]
    psl = [slice(p * PW, (p + 1) * PW) for p in range(n_pairs)]
    rhs_bk = [jnp.concatenate([b_t[:, s], k_t[:, s]], axis=0).astype(BF16) for s in psl]
    a_m, v_m, upper, lower = [], [], [], []
    for p, sub in heads:
        m = first if sub == 0 else jnp.logical_not(first)
        a_m.append(jnp.where(m, a_t[:, psl[p]], 0.0))
        v_m.append(jnp.where(m, v[:, psl[p]], 0.0))
        lhs = jnp.concatenate([a_m[-1], jnp.where(m, r_t[:, psl[p]], 0.0)], axis=0)
        amat = _bdot(lhs, rhs_bk[p], NT)
        upper.append(jnp.where(strict2, amat[:C], 0.0))
        lower.append(jnp.where(incl2, amat[C:], 0.0))
    w1 = [_bdot(jnp.where(lane_lo, 0.0, up), jnp.concatenate([vm, vm], axis=0))
          for up, vm in zip(upper, v_m)]
    pw = [up[:, :C] for up in upper]
    t_inv = [eye + n_ab for n_ab in pw]
    span = 2
    while span < C:
        pw = [_bdot(x, x) for x in pw]
        t_inv = [t + _bdot(t, x) for t, x in zip(t_inv, pw)]
        span *= 2
    x = [_bdot(t, jnp.concatenate([am, w], axis=1)) for t, am, w in zip(t_inv, a_m, w1)]

    g0 = [g_ref[p] for p in range(n_pairs)]
    u = [_bdot(jnp.concatenate([x[2 * p][:, :PW], x[2 * p + 1][:, :PW]], axis=0), g0[p])
         + jnp.concatenate([x[2 * p][:, PW:], x[2 * p + 1][:, PW:]], axis=0) for p in range(n_pairs)]
    uv = [jnp.concatenate([u[p][:C], v_m[2 * p], u[p][C:], v_m[2 * p + 1]], axis=0).astype(BF16)
          for p in range(n_pairs)]
    for p in range(n_pairs):
        sl = psl[p]
        y_ref[0, 0, :, sl] = (_bdot(r_t[:, sl], g0[p])
                              + _bdot(jnp.concatenate([lower[2 * p], lower[2 * p + 1]], axis=1), uv[p]))
    for p in range(n_pairs):
        sl = psl[p]
        bk_h = jnp.concatenate([b_h[:, sl], k_h[:, sl]], axis=0)
        upd = _bdot(jnp.concatenate([bk_h, bk_h], axis=0), uv[p], TN)
        decay = jnp.transpose(jnp.broadcast_to(p_c[:, sl], (PW, PW)))
        g_ref[p] = jnp.where(same_head, decay * g0[p] + upd, 0.0)


def _rwkv_scan(r, v, kk, lw, kd, bd, head, n_lat):
    B, T, W = r.shape
    C = RWKV_CHUNK
    nc = T // C
    nc_lat = n_lat // C

    def chunk(s, c):
        fwd = jnp.where(c < nc - nc_lat, nc_lat + c, c - (nc - nc_lat))
        bwd = jnp.where(c < nc - nc_lat, nc - 1 - c, nc - 1 - c)
        return jnp.where(s >= B, bwd, fwd)

    shared = pl.BlockSpec((1, C, W), lambda s, c: (s % B, chunk(s, c), 0))
    per_dir = pl.BlockSpec((1, C, W), lambda s, c: (s % B, chunk(s, c), s // B))
    kern = functools.partial(_rwkv_kernel, C=C, N=head, n_seq=B)
    return pl.pallas_call(
        kern,
        out_shape=jax.ShapeDtypeStruct((N_DIR, B, T, W), F32),
        grid=(N_DIR * B, nc),
        in_specs=[shared, shared, shared, per_dir, per_dir, per_dir],
        out_specs=pl.BlockSpec((1, 1, C, W), lambda s, c: (s // B, s % B, chunk(s, c), 0)),
        scratch_shapes=[pltpu.VMEM((W // (2 * head), 2 * head, 2 * head), F32)],
        compiler_params=_cparams("parallel", "arbitrary"),
        name="rwkv7_chunk",
    )(r, v, kk, lw, kd, bd)


def _merge_kernel(attn_ref, pool_ref, y_ref, g_ref, bonus_ref, lnw_ref, lnb_ref, gate_ref,
                  wa_ref, wp_ref, wr_ref, o_ref, rw_ref, *, head, tn):
    @pl.when(pl.program_id(2) == 0)
    def _():
        y = y_ref[0, 0] + y_ref[1, 0]
        inv_n = 1.0 / head
        dev = y - _head_sum(y, head) * inv_n
        var = _head_sum(dev * dev, head) * inv_n
        yn = dev * lax.rsqrt(var + GN_EPS) * lnw_ref[...] + lnb_ref[...]
        rw_ref[...] = ((yn + bonus_ref[0].astype(F32)) * g_ref[0].astype(F32)).astype(rw_ref.dtype)

    def gate(z):
        return jax.nn.sigmoid(gate_ref[0, :, z * tn:(z + 1) * tn].astype(F32))

    out = (gate(0) * _bdot(attn_ref[0], wa_ref[...]) + gate(1) * _bdot(pool_ref[0], wp_ref[...])
           + gate(2) * _bdot(rw_ref[...], wr_ref[...]))
    o_ref[0] = out.astype(o_ref.dtype)


def _merge_tile(D):
    return _pick(D, (512, 256, 128))


def _merge(attn, pool, y, g, bonus, ln_w, ln_b, gates, w_a, w_p, w_r, head):
    B, T, _ = attn.shape
    W = g.shape[2]
    D = w_a.shape[1]
    tm = _row_tile(T)
    tn = _merge_tile(D)
    nj = D // tn
    row = lambda w: pl.BlockSpec((1, tm, w), lambda b, i, j: (b, i, 0))
    vec = pl.BlockSpec((1, W), lambda b, i, j: (0, 0))
    wsp = lambda w: pl.BlockSpec((w.shape[0], tn), lambda b, i, j: (0, j))
    return pl.pallas_call(
        functools.partial(_merge_kernel, head=head, tn=tn),
        out_shape=jax.ShapeDtypeStruct((B, T, D), BF16),
        grid=(B, T // tm, nj),
        in_specs=[row(attn.shape[2]), row(pool.shape[2]),
                  pl.BlockSpec((N_DIR, 1, tm, W), lambda b, i, j: (0, b, i, 0)),
                  row(W), row(W), vec, vec,
                  pl.BlockSpec((1, tm, N_BRANCH * tn), lambda b, i, j: (b, i, j)),
                  wsp(w_a), wsp(w_p), wsp(w_r)],
        out_specs=pl.BlockSpec((1, tm, tn), lambda b, i, j: (b, i, j)),
        scratch_shapes=[pltpu.VMEM((tm, W), BF16)],
        compiler_params=_cparams("parallel", "parallel", "arbitrary"),
        name="merge",
    )(attn, pool, y, g, bonus, ln_w.reshape(1, W), ln_b.reshape(1, W), gates, w_a, w_p, w_r)


def _out_kernel(m_ref, w_ref, x_ref, gpost_ref, gpre_ref, ml_ref, mc_ref, wr_ref,
                x_out, h_out, aff_out, *, tm, n_lat, n_e):
    is_lat, m_lat, m_ctx = _mod_rows(ml_ref, mc_ref, pl.program_id(1) * tm, tm, n_lat)
    mix = _bdot(m_ref[0], w_ref[...])
    normed = mix * lax.rsqrt(jnp.mean(mix * mix, axis=-1, keepdims=True) + NORM_EPS) * gpost_ref[...]
    x = x_ref[0] + jnp.where(is_lat, m_lat[2:3], m_ctx[2:3]) * normed
    x_out[0] = x
    h = _modnorm(x, gpre_ref[...], is_lat, m_lat, m_ctx, 3)
    h_out[0] = _pack_halves(h)
    h_hi = h.astype(BF16)
    part = _bdot(h_hi, wr_ref[...]) + _bdot(h - h_hi.astype(F32), wr_ref[...])
    logits = part[:, :V7X_LANES] + part[:, V7X_LANES:]
    logits = jnp.where(_iota(logits.shape, 1) < n_e, logits, -jnp.inf)
    e = jnp.exp(logits - jnp.max(logits, axis=-1, keepdims=True))
    aff_out[0] = e / jnp.sum(e, axis=-1, keepdims=True)


def _out_proj(merged, w_out, xs, g_post, g_pre, mod_l, w_router, n_lat):
    B, T, D = xs.shape
    E = w_router.shape[1]
    tm = _pick(T, (544, 272, 256, 128, 320, 64, 32, 16))
    wr = jnp.pad(w_router, ((0, 0), (0, V7X_LANES - E)))
    wr_hi = wr.astype(BF16)
    wr = jnp.concatenate([wr_hi, (wr - wr_hi.astype(F32)).astype(BF16)], axis=1)
    row = lambda w: pl.BlockSpec((1, tm, w), lambda b, i: (b, i, 0))
    vec = pl.BlockSpec((1, D), lambda b, i: (0, 0))
    return pl.pallas_call(
        functools.partial(_out_kernel, tm=tm, n_lat=n_lat, n_e=E),
        out_shape=(jax.ShapeDtypeStruct((B, T, D), F32), jax.ShapeDtypeStruct((B, T, D // 2), jnp.uint32),
                   jax.ShapeDtypeStruct((B, T, V7X_LANES), F32)),
        grid=(B, T // tm),
        in_specs=[row(D), pl.BlockSpec((D, D), lambda b, i: (0, 0)), row(D), vec, vec,
                  pl.BlockSpec((1, 6, D), lambda b, i: (b, 0, 0)),
                  pl.BlockSpec((1, 6, D), lambda b, i: (B, 0, 0)),
                  pl.BlockSpec((D, 2 * V7X_LANES), lambda b, i: (0, 0))],
        out_specs=(row(D), row(D // 2), row(V7X_LANES)),
        compiler_params=_cparams("parallel", "parallel"),
        name="out_proj",
    )(merged, w_out, xs, g_post.reshape(1, D), g_pre.reshape(1, D), mod_l, mod_l, wr)


def _excl_prefix(flags, blk):
    n = flags.shape[-1]
    upper = (_iota((blk, blk), 0) < _iota((blk, blk), 1)).astype(BF16)
    outs = []
    carry = jnp.zeros((flags.shape[0], 1), F32)
    for j in range(n // blk):
        seg = flags[:, j * blk:(j + 1) * blk]
        outs.append(_bdot(seg, upper) + carry)
        carry = carry + jnp.sum(seg, axis=-1, keepdims=True)
    return jnp.concatenate(outs, axis=-1) if len(outs) > 1 else outs[0]


def _topk_kernel(aff_ref, sel_ref, idx_ref, *, cap, blk):
    bits = lax.bitcast_convert_type(aff_ref[0], jnp.int32)
    E = bits.shape[0]

    def body(i, tau):
        cand = tau | jnp.left_shift(jnp.int32(1), 30 - i)
        cnt = jnp.sum((bits >= cand).astype(jnp.int32), axis=-1, keepdims=True)
        return jnp.where(cnt >= cap, cand, tau)

    tau = lax.fori_loop(0, 31, body, jnp.zeros((E, 1), jnp.int32))
    gt = bits > tau
    eq = bits == tau
    need = (cap - jnp.sum(gt.astype(jnp.int32), axis=-1, keepdims=True)).astype(F32)
    eq_rank = _excl_prefix(eq.astype(F32), blk)
    sel = gt | (eq & (eq_rank < need))
    pos = _excl_prefix(sel.astype(F32), blk)
    slots = jnp.where(sel, pos.astype(jnp.int32), -1)
    sel_ref[0] = slots
    n = slots.shape[1]
    tok = _iota((8, n), 1)
    digit = _iota((8, n), 0)
    digits = jnp.where(digit == 0, tok // TOKEN_RADIX, jnp.where(digit == 1, tok % TOKEN_RADIX, 0))
    for e in range(E):
        onehot = slots[e:e + 1, :] == _iota((cap, n), 0)
        d = _bdot(digits.astype(F32), onehot.astype(F32), NT)
        idx_ref[0, e:e + 1, :] = (d[0:1] * TOKEN_RADIX + d[1:2]).astype(jnp.int32)


def _topk_slots(aff_t, cap):
    B, E, n = aff_t.shape
    assert n <= TOKEN_RADIX * 256
    blk = _pick(n, (512, 256, 128))
    return pl.pallas_call(
        functools.partial(_topk_kernel, cap=cap, blk=blk),
        out_shape=(jax.ShapeDtypeStruct((B, E, n), jnp.int32),
                   jax.ShapeDtypeStruct((B, E, cap), jnp.int32)),
        grid=(B,),
        in_specs=[pl.BlockSpec((1, E, n), lambda b: (b, 0, 0))],
        out_specs=(pl.BlockSpec((1, E, n), lambda b: (b, 0, 0)),
                   pl.BlockSpec((1, E, cap), lambda b: (b, 0, 0))),
        compiler_params=_cparams("parallel"),
        name="expert_topk",
    )(aff_t)


def _gather_kernel(idx_ref, h_ref, xe_ref, *, cap, n_e):
    base = (pl.program_id(0) * n_e + pl.program_id(1)) * cap

    def body(s, carry):
        t = idx_ref[base + s]
        xe_ref[0, 0, pl.ds(s, 1), :] = h_ref[0, pl.ds(t, 1), :]
        return carry

    lax.fori_loop(0, cap, body, 0, unroll=8)


def _gather(idx, hp, row_block, n):
    B, E, cap = idx.shape
    dh = hp.shape[2]
    grid_spec = pltpu.PrefetchScalarGridSpec(
        num_scalar_prefetch=1, grid=(B, E),
        in_specs=[pl.BlockSpec((1, n, dh), lambda b, e, idx_ref: (b, row_block, 0))],
        out_specs=pl.BlockSpec((1, 1, cap, dh), lambda b, e, idx_ref: (b, e, 0, 0)))
    return pl.pallas_call(
        functools.partial(_gather_kernel, cap=cap, n_e=E),
        out_shape=jax.ShapeDtypeStruct((B, E, cap, dh), jnp.uint32),
        grid_spec=grid_spec,
        compiler_params=_cparams("parallel", "arbitrary"),
        name="expert_gather",
    )(idx.reshape(-1), hp)


def _pack_halves(h):
    bits = lax.bitcast_convert_type(h.astype(BF16).astype(F32), jnp.uint32)
    half = h.shape[1] // 2
    return (bits[:, :half] >> 16) | (bits[:, half:] & jnp.uint32(0xFFFF0000))


def _unpack_halves(p):
    lo = lax.bitcast_convert_type(p << 16, F32)
    hi = lax.bitcast_convert_type(p & jnp.uint32(0xFFFF0000), F32)
    return lo, hi


def _ffn_kernel(xe_ref, wg_ref, wu_ref, wd_ref, ye_ref):
    lo, hi = _unpack_halves(xe_ref[0, 0])
    half = lo.shape[1]
    gate = _bdot(lo, wg_ref[0, :half, :]) + _bdot(hi, wg_ref[0, half:, :])
    up = _bdot(lo, wu_ref[0, :half, :]) + _bdot(hi, wu_ref[0, half:, :])
    hid = gate * jax.nn.sigmoid(gate) * up
    ye_ref[0, 0] = _bdot(hid, wd_ref[0]).astype(ye_ref.dtype)


def _expert_ffn(xe, w_gate, w_up, w_down):
    B, E, cap, dh = xe.shape
    D = 2 * dh
    FF = w_gate.shape[2]
    return pl.pallas_call(
        _ffn_kernel,
        out_shape=jax.ShapeDtypeStruct((B, E, cap, D), BF16),
        grid=(E, B),
        in_specs=[pl.BlockSpec((1, 1, cap, dh), lambda e, b: (b, e, 0, 0)),
                  pl.BlockSpec((1, D, FF), lambda e, b: (e, 0, 0)),
                  pl.BlockSpec((1, D, FF), lambda e, b: (e, 0, 0)),
                  pl.BlockSpec((1, FF, D), lambda e, b: (e, 0, 0))],
        out_specs=pl.BlockSpec((1, 1, cap, D), lambda e, b: (b, e, 0, 0)),
        compiler_params=_cparams("parallel", "parallel"),
        name="expert_ffn",
    )(xe, w_gate, w_up, w_down)


def _scatter_kernel(selc_ref, affc_ref, ye_ref, o_ref, *, cap):
    @pl.when(pl.program_id(2) == 0)
    def _():
        o_ref[...] = jnp.zeros_like(o_ref)

    selc = selc_ref[0, 0]
    onehot = selc == _iota((selc.shape[0], cap), 1)
    o_ref[0] += affc_ref[0, 0] * _bdot(onehot.astype(BF16), ye_ref[0, 0])


def _scatter(sel, aff_t, ye):
    B, E, n = sel.shape
    cap, D = ye.shape[2:]
    td = _pick(D, (512, 256, 128))
    return pl.pallas_call(
        functools.partial(_scatter_kernel, cap=cap),
        out_shape=jax.ShapeDtypeStruct((B, n, D), F32),
        grid=(B, D // td, E),
        in_specs=[pl.BlockSpec((1, 1, n, 1), lambda b, j, e: (b, e, 0, 0)),
                  pl.BlockSpec((1, 1, n, 1), lambda b, j, e: (b, e, 0, 0)),
                  pl.BlockSpec((1, 1, cap, td), lambda b, j, e: (b, e, 0, j))],
        out_specs=pl.BlockSpec((1, n, td), lambda b, j, e: (b, 0, j)),
        compiler_params=_cparams("parallel", "parallel", "arbitrary"),
        name="expert_scatter",
    )(sel.reshape(B, E, n, 1), aff_t.reshape(B, E, n, 1), ye)


def _expert_choice(hb, aff, row_block, n, w_gate, w_up, w_down):
    E = w_gate.shape[0]
    cap = EC_FACTOR * n // E
    aff_t = jnp.swapaxes(aff[:, row_block * n:(row_block + 1) * n, :E], 1, 2)
    sel, idx = _topk_slots(aff_t, cap)
    xe = _gather(idx, hb, row_block, n)
    ye = _expert_ffn(xe, w_gate, w_up, w_down)
    return _scatter(sel, aff_t, ye)


def _rms(x, g):
    return x * lax.rsqrt(jnp.mean(x * x, axis=-1, keepdims=True) + NORM_EPS) * g


def _rope_tables(n_lat, n_ctx, head_dim):
    rows = n_lat // GRID_W
    row = jnp.repeat(jnp.arange(rows), GRID_W).astype(F32)
    col = (jnp.arange(rows * GRID_W) % GRID_W).astype(F32)
    half = head_dim // 2
    inv = ROPE_THETA ** (-jnp.arange(0, half, 2, dtype=F32) / half)
    ar, ac = row[:, None] * inv, col[:, None] * inv
    cos = jnp.concatenate([jnp.cos(ar), jnp.cos(ar), jnp.cos(ac), jnp.cos(ac)], axis=-1)
    sin = jnp.concatenate([-jnp.sin(ar), jnp.sin(ar), -jnp.sin(ac), jnp.sin(ac)], axis=-1)
    cos = jnp.concatenate([cos, jnp.ones((n_ctx, head_dim), F32)], axis=0)
    sin = jnp.concatenate([sin, jnp.zeros((n_ctx, head_dim), F32)], axis=0)
    return cos, sin


def _pad_cols(w, mult):
    return jnp.pad(w, ((0, 0), (0, (-w.shape[1]) % mult)))


def kernel(x, c, ctx, c_ctx, w_mod, b_mod, norm_pre, norm_post, w_in, q_norm, k_norm, w_attn_o, w_pool_group, pool_scale, w_pool_o, rwkv_mu, rwkv_w0, rwkv_w2, rwkv_a0, rwkv_a2, rwkv_g2, rwkv_k_k, rwkv_k_a, rwkv_r_k, rwkv_ln_w, rwkv_ln_b, w_rwkv_o, w_out, w_router, w_exp_gate, w_exp_up, w_exp_down):
    B, n_lat, D = x.shape
    n_ctx = ctx.shape[1]
    T = n_lat + n_ctx
    depth = w_mod.shape[0]
    hd = q_norm.shape[1]
    attn_w = w_attn_o.shape[1]
    pool_w = w_pool_o.shape[1]
    rwkv_w = w_rwkv_o.shape[1]
    n_shift = rwkv_mu.shape[1]
    n_in = w_in.shape[2]
    kv_w = (n_in - attn_w - pool_w - n_shift - N_BRANCH * D) // 2
    head = rwkv_r_k.shape[2]
    col_k = attn_w
    col_v = col_k + kv_w
    col_pool = col_v + kv_w
    col_r = col_pool + pool_w
    col_gate = col_r + n_shift
    hq, hkv = attn_w // hd, kv_w // hd
    assert n_lat % (hq // hkv * hd) == 0 and pool_w % (hq // hkv * hd) == 0 and n_ctx % RWKV_CHUNK == 0

    cos, sin = _rope_tables(n_lat, n_ctx, hd)
    s_all = jnp.concatenate([jax.nn.silu(c), jax.nn.silu(c_ctx)[None]], axis=0)
    s_all = jnp.pad(s_all, ((0, (-s_all.shape[0]) % 8), (0, 0)))
    mod = _modulation(s_all, w_mod, b_mod)[:, :B + 1].reshape(depth, B + 1, 6, D)
    xs = jnp.concatenate([x, ctx], axis=1)
    (h,) = _stream_update(xs, None, None, None, mod[0], norm_pre[0, 0], mod[0], n_lat, T)

    for l in range(depth):
        keep_ctx = l < depth - 1
        wl = w_in[l]
        w_a = jnp.concatenate([wl[:, col_pool:col_r], wl[:, :col_pool]], axis=1).astype(BF16)
        w_b = _pad_cols(wl[:, col_r:col_gate], 3 * V7X_LANES).astype(BF16)
        tn_m = _merge_tile(D)
        w_c = (wl[:, col_gate:].reshape(D, N_BRANCH, D // tn_m, tn_m).swapaxes(1, 2)
               .reshape(D, N_BRANCH * D).astype(BF16))
        p_a = _in_proj(h, w_a)
        p_b = _in_proj(h, w_b)
        p_c = _in_proj(h, w_c)

        attn = _attention(p_a, pool_w, pool_w + attn_w, pool_w + attn_w + kv_w, cos, sin,
                          q_norm[l], k_norm[l], hd, hq, hkv, n_lat)
        pool = _pool(p_a, 0, w_pool_group[l].astype(BF16), pool_scale[l], n_lat)
        r, v, kk, lw, kd, bd, g, bonus = _rwkv_prep(
            p_b, n_lat, rwkv_w, head, rwkv_mu[l], rwkv_w0[l], rwkv_w2[l], rwkv_a0[l], rwkv_a2[l],
            rwkv_g2[l], rwkv_k_k[l], rwkv_k_a[l], rwkv_r_k[l])
        y = _rwkv_scan(r, v, kk, lw, kd, bd, head, n_lat)
        merged = _merge(attn, pool, y, g, bonus, rwkv_ln_w[l], rwkv_ln_b[l], p_c,
                        w_attn_o[l].astype(BF16), w_pool_o[l].astype(BF16), w_rwkv_o[l].astype(BF16), head)
        xs, hb, aff = _out_proj(merged, w_out[l].astype(BF16), xs, norm_post[l, 0], norm_pre[l, 1],
                                mod[l], w_router[l], n_lat)

        wg, wu, wdn = w_exp_gate[l].astype(BF16), w_exp_up[l].astype(BF16), w_exp_down[l].astype(BF16)
        f_lat = _expert_choice(hb, aff, 0, n_lat, wg, wu, wdn)
        if keep_ctx:
            f_ctx = _expert_choice(hb, aff, n_lat // n_ctx, n_ctx, wg, wu, wdn)
            xs, h = _stream_update(xs, f_lat, f_ctx, norm_post[l, 1], mod[l], norm_pre[l + 1, 0],
                                   mod[l + 1], n_lat, T)
        else:
            (xs,) = _stream_update(xs, f_lat, None, norm_post[l, 1], mod[l], None, None, n_lat, n_lat)
    return xs
```

```python
import functools

import jax
import jax.numpy as jnp
from jax import lax
from jax.experimental import pallas as pl
from jax.experimental.pallas import tpu as pltpu

F32 = jnp.float32
BF16 = jnp.bfloat16
HIGHEST = lax.Precision.HIGHEST

GRID_W = 64
NORM_EPS = 1e-6
ROPE_THETA = 10000.0
POOL_WINDOWS = (2, 4, 8, 16)
GN_EPS = 64e-5
EC_FACTOR = 2
N_DIR = 2
N_BRANCH = 3
RWKV_CHUNK = 64
HALO = 16

LOG2_E = 1.4426950408889634
TOKEN_RADIX = 64

V7X_LANES = 128
VMEM_LIMIT = 52 * 1024 * 1024

NT = (((1,), (1,)), ((), ()))
TN = (((0,), (0,)), ((), ()))


def _pick(n, cands):
    for c in cands:
        if c <= n and n % c == 0:
            return c
    return n


def _cparams(*sem):
    return pltpu.CompilerParams(dimension_semantics=sem, vmem_limit_bytes=VMEM_LIMIT)


def _bdot(x, y, dn=None):
    x = x.astype(BF16)
    y = y.astype(BF16)
    if dn is None:
        return jnp.dot(x, y, preferred_element_type=F32)
    return lax.dot_general(x, y, dn, preferred_element_type=F32)


def _split_dot(x, y):
    hi = x.astype(BF16)
    lo = x - hi.astype(F32)
    return _bdot(hi, y) + _bdot(lo, y)


def _iota(shape, dim):
    return lax.broadcasted_iota(jnp.int32, shape, dim)


def _row_tile(T, min_parts=8):
    for parts in range(min_parts, T // 16 + 1):
        if T % parts == 0 and (T // parts) % 16 == 0:
            return T // parts
    return T


def _mod_kernel(s_ref, w_ref, b_ref, o_ref):
    o_ref[0] = jnp.dot(s_ref[...], w_ref[0], precision=HIGHEST,
                       preferred_element_type=F32) + b_ref[0]


def _modulation(s, w_mod, b_mod):
    L, D, N = w_mod.shape
    R = s.shape[0]
    tn = _pick(N, (1024, 512, 256, 128))
    return pl.pallas_call(
        _mod_kernel,
        out_shape=jax.ShapeDtypeStruct((L, R, N), F32),
        grid=(L, N // tn),
        in_specs=[pl.BlockSpec((R, D), lambda l, j: (0, 0)),
                  pl.BlockSpec((1, D, tn), lambda l, j: (l, 0, j)),
                  pl.BlockSpec((1, 1, tn), lambda l, j: (l, 0, j))],
        out_specs=pl.BlockSpec((1, R, tn), lambda l, j: (l, 0, j)),
        compiler_params=_cparams("parallel", "parallel"),
        name="adaln_mod",
    )(s, w_mod, b_mod.reshape(L, 1, N))


def _mod_rows(m_lat_ref, m_ctx_ref, row0, tm, n_lat):
    is_lat = (row0 + _iota((tm, 1), 0)) < n_lat
    return is_lat, m_lat_ref[0], m_ctx_ref[0]


def _modnorm(x, g, is_lat, m_lat, m_ctx, i_shift):
    shift = jnp.where(is_lat, m_lat[i_shift:i_shift + 1], m_ctx[i_shift:i_shift + 1])
    scale = jnp.where(is_lat, m_lat[i_shift + 1:i_shift + 2], m_ctx[i_shift + 1:i_shift + 2])
    y = x * lax.rsqrt(jnp.mean(x * x, axis=-1, keepdims=True) + NORM_EPS) * g
    return y * (1.0 + scale) + shift


def _small_row_tile(T, n_ctx):
    return _pick(n_ctx, (256, 128, 64, 32, 16))


def _stream_kernel(*refs, tm, n_lat, has_f, has_ctx_f, emit_h):
    refs = list(refs)
    x_ref = refs.pop(0)
    fl_ref = refs.pop(0) if has_f else None
    fc_ref = refs.pop(0) if has_ctx_f else None
    gpost_ref = refs.pop(0) if has_f else None
    ml_ref, mc_ref = refs.pop(0), refs.pop(0)
    if emit_h:
        gpre_ref, mln_ref, mcn_ref = refs.pop(0), refs.pop(0), refs.pop(0)
    x_out = refs.pop(0) if has_f else None
    h_out = refs.pop(0) if emit_h else None

    is_lat = (pl.program_id(1) * tm + _iota((tm, 1), 0)) < n_lat
    x = x_ref[0]
    if has_f:
        f = fl_ref[0]
        if has_ctx_f:
            f = jnp.where(is_lat, f, fc_ref[0])
        normed = f * lax.rsqrt(jnp.mean(f * f, axis=-1, keepdims=True) + NORM_EPS) * gpost_ref[...]
        x = x + jnp.where(is_lat, ml_ref[0][5:6], mc_ref[0][5:6]) * normed
        x_out[0] = x
    if emit_h:
        h_out[0] = _modnorm(x, gpre_ref[...], is_lat, mln_ref[0], mcn_ref[0], 0).astype(h_out.dtype)


def _stream_update(xs, f_lat, f_ctx, g_post, mod_l, g_pre_next, mod_next, n_lat, n_rows):
    B, T, D = xs.shape
    n_ctx = T - n_lat
    tm = _small_row_tile(T, n_ctx)
    has_f, has_ctx_f, emit_h = f_lat is not None, f_ctx is not None, g_pre_next is not None
    n_lt = n_lat // tm
    row = pl.BlockSpec((1, tm, D), lambda b, i: (b, i, 0))
    vec = pl.BlockSpec((1, D), lambda b, i: (0, 0))
    m_l = pl.BlockSpec((1, 6, D), lambda b, i: (b, 0, 0))
    m_c = pl.BlockSpec((1, 6, D), lambda b, i: (B, 0, 0))
    args, specs = [xs], [row]
    if has_f:
        args.append(f_lat)
        specs.append(pl.BlockSpec((1, tm, D), lambda b, i: (b, jnp.minimum(i, n_lt - 1), 0)))
    if has_ctx_f:
        args.append(f_ctx)
        specs.append(pl.BlockSpec((1, tm, D), lambda b, i: (b, jnp.maximum(i - n_lt, 0), 0)))
    if has_f:
        args.append(g_post.reshape(1, D))
        specs.append(vec)
    args += [mod_l, mod_l]
    specs += [m_l, m_c]
    if emit_h:
        args += [g_pre_next.reshape(1, D), mod_next, mod_next]
        specs += [vec, m_l, m_c]
    out_shape, out_specs = [], []
    if has_f:
        out_shape.append(jax.ShapeDtypeStruct((B, n_rows, D), F32))
        out_specs.append(row)
    if emit_h:
        out_shape.append(jax.ShapeDtypeStruct((B, n_rows, D), BF16))
        out_specs.append(row)
    kern = functools.partial(_stream_kernel, tm=tm, n_lat=n_lat, has_f=has_f, has_ctx_f=has_ctx_f,
                             emit_h=emit_h)
    return pl.pallas_call(
        kern, out_shape=tuple(out_shape), grid=(B, n_rows // tm), in_specs=specs,
        out_specs=tuple(out_specs), compiler_params=_cparams("parallel", "parallel"),
        name="stream_update",
    )(*args)


def _in_kernel(h_ref, w_ref, o_ref):
    o_ref[0] = _bdot(h_ref[0], w_ref[...]).astype(o_ref.dtype)


def _in_proj(h, w):
    B, T, D = h.shape
    N = w.shape[1]
    tm = _row_tile(T, 4)
    tn = _pick(N, (1280, 1024, 896, 768, 640, 512, 384, 256, 128))
    return pl.pallas_call(
        _in_kernel,
        out_shape=jax.ShapeDtypeStruct((B, T, N), BF16),
        grid=(B, T // tm, N // tn),
        in_specs=[pl.BlockSpec((1, tm, D), lambda b, i, j: (b, i, 0)),
                  pl.BlockSpec((D, tn), lambda b, i, j: (0, j))],
        out_specs=pl.BlockSpec((1, tm, tn), lambda b, i, j: (b, i, j)),
        compiler_params=_cparams("parallel", "parallel", "arbitrary"),
        name="in_proj",
    )(h, w)


def _rot(x, cos, sin):
    qd = x.shape[1] // 4
    first = (_iota((1, x.shape[1]), 1) // qd) % 2 == 0
    swapped = jnp.where(first, pltpu.roll(x, x.shape[1] - qd, 1), pltpu.roll(x, qd, 1))
    return x * cos + swapped * sin


def _head_norm(x, g):
    return x * lax.rsqrt(jnp.mean(x * x, axis=-1, keepdims=True) + NORM_EPS) * g


def _attn_kernel(q_ref, k_ref, v_ref, cq_ref, sq_ref, ck_ref, sk_ref, qn_ref, kn_ref, o_ref,
                 kp_ref, vp_ref, *, hd, group, tq, n_lat, scale):
    i = pl.program_id(2)

    @pl.when(i == 0)
    def _():
        kp_ref[...] = _rot(_head_norm(k_ref[0].astype(F32), kn_ref[...]),
                           ck_ref[...], sk_ref[...]).astype(kp_ref.dtype)
        ones = jnp.ones(v_ref.shape[1:], vp_ref.dtype)
        vp_ref[...] = jnp.concatenate([v_ref[0].astype(vp_ref.dtype), ones], axis=1)

    def attend(key0):
        for g in range(group):
            q = _head_norm(q_ref[0, :, g * hd:(g + 1) * hd].astype(F32), qn_ref[...])
            q = _rot(q, cq_ref[...], sq_ref[...]) * scale
            s = _bdot(q, kp_ref[key0:, :], NT)
            p = jnp.exp2(s - jnp.max(s, axis=-1, keepdims=True))
            ov = _bdot(p, vp_ref[key0:, :])
            o_ref[0, :, g * hd:(g + 1) * hd] = (ov[:, :hd] / ov[:, hd:hd + 1]).astype(o_ref.dtype)

    @pl.when(i * tq < n_lat)
    def _():
        attend(0)

    @pl.when(i * tq >= n_lat)
    def _():
        attend(n_lat)


def _attention(p_qkv, q_col, k_col, v_col, cos, sin, q_norm, k_norm, hd, hq, hkv, n_lat):
    B, T, _ = p_qkv.shape
    group = hq // hkv
    gw = group * hd
    tq = _pick(T - n_lat, (256, 128, 64, 32, 16))
    kern = functools.partial(_attn_kernel, hd=hd, group=group, tq=tq, n_lat=n_lat,
                             scale=hd ** -0.5 * LOG2_E)
    tab_q = pl.BlockSpec((tq, hd), lambda b, h, i: (i, 0))
    tab_k = pl.BlockSpec((T, hd), lambda b, h, i: (0, 0))
    vec = pl.BlockSpec((1, hd), lambda b, h, i: (0, 0))
    return pl.pallas_call(
        kern,
        out_shape=jax.ShapeDtypeStruct((B, T, hq * hd), BF16),
        grid=(B, hkv, T // tq),
        in_specs=[pl.BlockSpec((1, tq, gw), lambda b, h, i: (b, i, q_col // gw + h)),
                  pl.BlockSpec((1, T, hd), lambda b, h, i: (b, 0, k_col // hd + h)),
                  pl.BlockSpec((1, T, hd), lambda b, h, i: (b, 0, v_col // hd + h)),
                  tab_q, tab_q, tab_k, tab_k, vec, vec],
        out_specs=pl.BlockSpec((1, tq, gw), lambda b, h, i: (b, i, h)),
        scratch_shapes=[pltpu.VMEM((T, hd), BF16), pltpu.VMEM((T, 2 * hd), BF16)],
        compiler_params=_cparams("parallel", "parallel", "arbitrary"),
        name="attention",
    )(p_qkv, p_qkv, p_qkv, cos, sin, cos, sin, q_norm.reshape(1, hd), k_norm.reshape(1, hd))


def _segment_of_tile(i, tt, n_lat_tiles, n_tiles):
    is_lat = i < n_lat_tiles
    ti = jnp.where(is_lat, i, i - n_lat_tiles)
    seg_tiles = jnp.where(is_lat, n_lat_tiles, n_tiles - n_lat_tiles)
    return ti, seg_tiles


def _halo_specs(tt, width, T, col_block):
    hb = tt // HALO
    last = T // HALO - 1
    prev = pl.BlockSpec((1, HALO, width), lambda b, i: (b, jnp.maximum(i * hb - 1, 0), col_block))
    cur = pl.BlockSpec((1, tt, width), lambda b, i: (b, i, col_block))
    nxt = pl.BlockSpec((1, HALO, width), lambda b, i: (b, jnp.minimum((i + 1) * hb, last), col_block))
    return prev, cur, nxt


def _pool_kernel(prev_ref, cur_ref, next_ref, wg_ref, sc_ref, o_ref, *,
                 tt, n_lat_tiles, n_tiles, gw, windows):
    ti, seg_tiles = _segment_of_tile(pl.program_id(1), tt, n_lat_tiles, n_tiles)
    has_prev = ti > 0
    has_next = ti < seg_tiles - 1
    t_seg = seg_tiles * tt
    tpos = ti * tt + _iota((tt, 1), 0)

    cur = cur_ref[0]
    prev = prev_ref[0]
    nxt = next_ref[0]
    d_cur = _iota((tt, tt), 1) - _iota((tt, tt), 0)
    d_halo = _iota((tt, HALO), 1) - _iota((tt, HALO), 0)
    d_prev = d_halo - HALO
    d_next = d_halo + tt
    for g, win in enumerate(windows):
        lo_off = -(win // 2)
        hi_off = win - win // 2 - 1
        sl = slice(g * gw, (g + 1) * gw)
        band_c = ((d_cur >= lo_off) & (d_cur <= hi_off)).astype(BF16)
        band_p = ((d_prev >= lo_off) & (d_prev <= hi_off) & has_prev).astype(BF16)
        band_n = ((d_next >= lo_off) & (d_next <= hi_off) & has_next).astype(BF16)
        ug = cur[:, sl]
        tot = _bdot(band_c, ug) + _bdot(band_p, prev[:, sl]) + _bdot(band_n, nxt[:, sl])
        lo = jnp.maximum(tpos + lo_off, 0)
        hi = jnp.minimum(tpos + hi_off + 1, t_seg)
        pooled = tot / (hi - lo).astype(F32) - ug.astype(F32)
        y = _bdot(pooled, wg_ref[g]) * sc_ref[:, sl]
        o_ref[0, :, sl] = y.astype(o_ref.dtype)


def _pool(p_arr, col_block, w_group, scale, n_lat):
    B, T, _ = p_arr.shape
    G, gw, _ = w_group.shape
    W = G * gw
    tt = _pick(T - n_lat, (256, 128, 64, 32, 16))
    assert n_lat % tt == 0 and tt % HALO == 0 and max(POOL_WINDOWS) <= HALO
    n_tiles = T // tt
    kern = functools.partial(_pool_kernel, tt=tt, n_lat_tiles=n_lat // tt, n_tiles=n_tiles,
                             gw=gw, windows=POOL_WINDOWS)
    prev, cur, nxt = _halo_specs(tt, W, T, col_block)
    return pl.pallas_call(
        kern,
        out_shape=jax.ShapeDtypeStruct((B, T, W), BF16),
        grid=(B, n_tiles),
        in_specs=[prev, cur, nxt,
                  pl.BlockSpec((G, gw, gw), lambda b, i: (0, 0, 0)),
                  pl.BlockSpec((1, W), lambda b, i: (0, 0))],
        out_specs=pl.BlockSpec((1, tt, W), lambda b, i: (b, i, 0)),
        compiler_params=_cparams("parallel", "parallel"),
        name="pool",
    )(p_arr, p_arr, p_arr, w_group, scale.reshape(1, W))


def _head_sum(x, head):
    lanes = x.shape[1]
    blk = min(lanes, V7X_LANES)
    same = (_iota((blk, blk), 0) // head == _iota((blk, blk), 1) // head).astype(BF16)
    parts = [_split_dot(x[:, j:j + blk], same) for j in range(0, lanes, blk)]
    return jnp.concatenate(parts, axis=1) if len(parts) > 1 else parts[0]


def _rwkv_prep_kernel(prev_ref, cur_ref, next_ref, mu_ref, w0_ref, w2_ref, a0_ref, a2_ref, g2_ref,
                      kk_w_ref, ka_ref, rk_ref,
                      r_ref, v_ref, kk_ref, lw_ref, cum_ref, kd_ref, bd_ref, g_ref, bonus_ref, *,
                      tt, n_lat_tiles, n_tiles, W, head, lora, chunk):
    ti, seg_tiles = _segment_of_tile(pl.program_id(1), tt, n_lat_tiles, n_tiles)
    u = cur_ref[0].astype(F32)
    row = _iota((tt, 1), 0)
    before = jnp.where(ti > 0, prev_ref[0, HALO - 1:HALO, :].astype(F32), 0.0)
    after = jnp.where(ti < seg_tiles - 1, next_ref[0, 0:1, :].astype(F32), 0.0)
    u_prev = jnp.where(row == 0, before, pltpu.roll(u, 1, 0))
    u_next = jnp.where(row == tt - 1, after, pltpu.roll(u, tt - 1, 0))
    u = u + (0.5 * (u_prev + u_next) - u) * mu_ref[...]

    r, k, v = u[:, :W], u[:, W:2 * W], u[:, 2 * W:3 * W]
    o1 = 3 * W
    o2 = o1 + lora
    o3 = o2 + lora
    w_lin = w0_ref[...] + _bdot(jnp.tanh(u[:, o1:o2]), w2_ref[...])
    w_log = -(jnp.maximum(-w_lin, 0.0) + jnp.log(1.0 + jnp.exp(-jnp.abs(w_lin)))) - 0.5
    lw = -jnp.exp(w_log)
    lw_ref[0] = lw
    ri, ci = _iota((tt, tt), 0), _iota((tt, tt), 1)
    same_chunk = ri // chunk == ci // chunk
    p1 = lw.astype(BF16)
    p2 = (lw - p1.astype(F32)).astype(BF16)
    p3 = (lw - p1.astype(F32) - p2.astype(F32)).astype(BF16)
    for z, tri in enumerate((same_chunk & (ci <= ri), same_chunk & (ci >= ri))):
        sl = slice(z * W, (z + 1) * W)
        cum_ref[0, :, sl] = _bdot(tri, p1[:, sl]) + _bdot(tri, p2[:, sl]) + _bdot(tri, p3[:, sl])
    a = jax.nn.sigmoid(a0_ref[...] + _bdot(u[:, o2:o3], a2_ref[...]))
    g_ref[0] = _bdot(jax.nn.sigmoid(u[:, o3:]), g2_ref[...]).astype(g_ref.dtype)

    kk = k * kk_w_ref[...]
    kk = kk * lax.rsqrt(jnp.maximum(_head_sum(kk * kk, head), 1e-24))
    k_sum = 0.0
    for z in range(N_DIR):
        a_z = a[:, z * W:(z + 1) * W]
        k_z = k * (1.0 + (a_z - 1.0) * ka_ref[...])
        kd_ref[0, :, z * W:(z + 1) * W] = k_z.astype(kd_ref.dtype)
        bd_ref[0, :, z * W:(z + 1) * W] = (kk * a_z).astype(bd_ref.dtype)
        k_sum = k_sum + k_z
    r_ref[0] = r.astype(r_ref.dtype)
    v_ref[0] = v.astype(v_ref.dtype)
    kk_ref[0] = kk.astype(kk_ref.dtype)
    bonus_ref[0] = (_head_sum(r * k_sum * rk_ref[...], head) * v).astype(bonus_ref.dtype)


def _rwkv_prep(p_rwkv, n_lat, W, head, mu, w0, w2, a0, a2, g2, k_k, k_a, r_k):
    B, T, NS = p_rwkv.shape
    lora = N_DIR * w2.shape[1]
    n_gate = NS - 3 * W - 2 * lora
    tt = _pick(T - n_lat, (256, 128, 64, 32, 16))
    n_tiles = T // tt
    def cat(m):
        z = jnp.zeros_like(m[0])
        return jnp.concatenate([jnp.concatenate([m[0], z], axis=1),
                                jnp.concatenate([z, m[1]], axis=1)], axis=0).astype(BF16)
    g2p = jnp.pad(g2, ((0, n_gate - g2.shape[0]), (0, 0))).astype(BF16)
    mup = jnp.pad(mu, (0, NS - mu.shape[0])).reshape(1, NS)
    kern = functools.partial(_rwkv_prep_kernel, tt=tt, n_lat_tiles=n_lat // tt, n_tiles=n_tiles,
                             W=W, head=head, lora=lora, chunk=RWKV_CHUNK)
    assert tt % RWKV_CHUNK == 0
    prev, cur, nxt = _halo_specs(tt, NS, T, 0)
    full = lambda shp: pl.BlockSpec(shp, lambda b, i: (0,) * len(shp))
    tile = lambda w: pl.BlockSpec((1, tt, w), lambda b, i: (b, i, 0))
    sd = lambda w, dt: jax.ShapeDtypeStruct((B, T, w), dt)
    return pl.pallas_call(
        kern,
        out_shape=(sd(W, BF16), sd(W, BF16), sd(W, BF16), sd(2 * W, F32), sd(2 * W, F32), sd(2 * W, BF16),
                   sd(2 * W, BF16), sd(W, BF16), sd(W, BF16)),
        grid=(B, n_tiles),
        in_specs=[prev, cur, nxt, full((1, NS)), full((1, 2 * W)), full((lora, 2 * W)),
                  full((1, 2 * W)), full((lora, 2 * W)), full((n_gate, W)),
                  full((1, W)), full((1, W)), full((1, W))],
        out_specs=(tile(W), tile(W), tile(W), tile(2 * W), tile(2 * W), tile(2 * W), tile(2 * W), tile(W),
                   tile(W)),
        compiler_params=_cparams("parallel", "parallel"),
        name="rwkv_prep",
    )(p_rwkv, p_rwkv, p_rwkv, mup, w0.reshape(1, 2 * W), cat(w2), a0.reshape(1, 2 * W), cat(a2), g2p,
      k_k.reshape(1, W), k_a.reshape(1, W), r_k.reshape(1, W))


def _rwkv_kernel(r_ref, v_ref, kk_ref, lw_ref, cum_ref, k_ref, b_ref, y_ref, g_ref, *, C, N, n_seq):
    @pl.when(pl.program_id(1) == 0)
    def _():
        g_ref[...] = jnp.zeros_like(g_ref)

    W = r_ref.shape[2]
    PW = 2 * N
    backward = pl.program_id(0) >= n_seq
    sgn = jnp.where(backward, -1, 1)
    order = (_iota((C, C), 0) - _iota((C, C), 1)) * sgn
    lw = lw_ref[0]
    cum = cum_ref[0]
    pc = jnp.where(backward, cum[0:1, :], cum[C - 1:C, :])
    k = k_ref[0].astype(F32)
    b = b_ref[0].astype(F32)
    p_inv = jnp.exp(-cum)
    p_hat = jnp.exp(pc - cum)
    r_t = r_ref[0].astype(F32) * jnp.exp(cum)
    k_t = k * p_inv
    b_t = b * p_inv
    a_t = -kk_ref[0].astype(F32) * jnp.exp(cum - lw)
    b_h = b * p_hat
    k_h = k * p_hat
    p_c = jnp.exp(pc)
    v = v_ref[0].astype(F32)

    order2 = jnp.concatenate([order, order], axis=1)
    strict2 = order2 > 0
    incl2 = order2 >= 0
    eye = (order == 0).astype(F32)
    first = _iota((1, PW), 1) < N
    lane_lo = _iota((1, 2 * C), 1) < C
    same_head = (_iota((PW, PW), 0) // N) == (_iota((PW, PW), 1) // N)

    n_pairs = W // PW
    heads = [(p, sub) for p in range(n_pairs) for sub in range(2)]
    psl = [slice(p * PW, (p + 1) * PW) for p in range(n_pairs)]
    masks = (first, jnp.logical_not(first))
    v_m, upper, lower = [], [], []
    for p in range(n_pairs):
        sl = psl[p]
        rhs_bk = jnp.concatenate([b_t[:, sl], k_t[:, sl]], axis=0)
        lhs = jnp.concatenate([jnp.where(m, x[:, sl], 0.0) for m in masks for x in (a_t, r_t)], axis=0)
        amat = _bdot(lhs, rhs_bk, NT)
        for sub in range(2):
            upper.append(jnp.where(strict2, amat[2 * sub * C:(2 * sub + 1) * C], 0.0))
            lower.append(jnp.where(incl2, amat[(2 * sub + 1) * C:(2 * sub + 2) * C], 0.0))
            v_m.append(jnp.where(masks[sub], v[:, sl], 0.0))
    zero = jnp.zeros((C, PW), F32)
    pw, x = [], []
    for p in range(n_pairs):
        up0, up1 = upper[2 * p], upper[2 * p + 1]
        a_ak = jnp.where(first, pltpu.roll(up0, C, 1), up1)
        w = _bdot(a_ak, jnp.concatenate([v_m[2 * p], v_m[2 * p + 1]], axis=0))
        w = pltpu.roll(w, N, 1)
        a_p = a_t[:, psl[p]]
        x += [jnp.where(first, a_p, w), jnp.where(first, w, a_p)]
        pw.append(jnp.where(first, up0, pltpu.roll(up1, C, 1)))
    span = 1
    while span < C:
        span *= 2
        for p in range(n_pairs):
            x0, x1 = x[2 * p], x[2 * p + 1]
            rhs = [jnp.concatenate([x0, zero], axis=1), jnp.concatenate([zero, x1], axis=1)]
            if span < C:
                rhs = [jnp.concatenate([jnp.where(m, pw[p], 0.0), xr], axis=1) for m, xr in zip(masks, rhs)]
            res = _bdot(pw[p], jnp.concatenate(rhs, axis=0))
            if span < C:
                pw[p] = res[:, :PW]
                res = res[:, PW:]
            x[2 * p], x[2 * p + 1] = x0 + res[:, :PW], x1 + res[:, PW:]

    g0 = [g_ref[p] for p in range(n_pairs)]
    uv, rg = [], []
    for p in range(n_pairs):
        x0, x1 = x[2 * p], x[2 * p + 1]
        ahat = [jnp.where(masks[0], x0, 0.0), jnp.where(masks[1], x1, 0.0)]
        vhat = [jnp.where(masks[0], pltpu.roll(x0, N, 1), 0.0), jnp.where(masks[1], pltpu.roll(x1, N, 1), 0.0)]
        res = _bdot(jnp.concatenate([r_t[:, psl[p]]] + ahat, axis=0), g0[p])
        rg.append(res[:C])
        u0 = res[C:2 * C] + vhat[0]
        u1 = res[2 * C:] + vhat[1]
        uv.append(jnp.concatenate([u0, v_m[2 * p], u1, v_m[2 * p + 1]], axis=0).astype(BF16))
    for p in range(n_pairs):
        y_ref[0, 0, :, psl[p]] = rg[p] + _bdot(jnp.concatenate([lower[2 * p], lower[2 * p + 1]], axis=1),
                                               uv[p])
    for p in range(n_pairs):
        sl = psl[p]
        bk_h = jnp.concatenate([b_h[:, sl], k_h[:, sl]], axis=0)
        upd = _bdot(jnp.concatenate([bk_h, bk_h], axis=0), uv[p], TN)
        decay = jnp.transpose(jnp.broadcast_to(p_c[:, sl], (PW, PW)))
        g_ref[p] = jnp.where(same_head, decay * g0[p] + upd, 0.0)


def _rwkv_scan(r, v, kk, lw, cum, kd, bd, head, n_lat):
    B, T, W = r.shape
    C = RWKV_CHUNK
    assert C == head
    nc = T // C
    nc_lat = n_lat // C

    def chunk(s, c):
        fwd = jnp.where(c < nc - nc_lat, nc_lat + c, c - (nc - nc_lat))
        bwd = jnp.where(c < nc - nc_lat, nc - 1 - c, nc - 1 - c)
        return jnp.where(s >= B, bwd, fwd)

    shared = pl.BlockSpec((1, C, W), lambda s, c: (s % B, chunk(s, c), 0))
    per_dir = pl.BlockSpec((1, C, W), lambda s, c: (s % B, chunk(s, c), s // B))
    kern = functools.partial(_rwkv_kernel, C=C, N=head, n_seq=B)
    return pl.pallas_call(
        kern,
        out_shape=jax.ShapeDtypeStruct((N_DIR, B, T, W), F32),
        grid=(N_DIR * B, nc),
        in_specs=[shared, shared, shared, per_dir, per_dir, per_dir, per_dir],
        out_specs=pl.BlockSpec((1, 1, C, W), lambda s, c: (s // B, s % B, chunk(s, c), 0)),
        scratch_shapes=[pltpu.VMEM((W // (2 * head), 2 * head, 2 * head), F32)],
        compiler_params=_cparams("parallel", "arbitrary"),
        name="rwkv7_chunk",
    )(r, v, kk, lw, cum, kd, bd)


def _merge_kernel(attn_ref, pool_ref, y_ref, g_ref, bonus_ref, lnw_ref, lnb_ref, gate_ref,
                  wa_ref, wp_ref, wr_ref, o_ref, rw_ref, *, head, tn):
    @pl.when(pl.program_id(2) == 0)
    def _():
        y = y_ref[0, 0] + y_ref[1, 0]
        inv_n = 1.0 / head
        dev = y - _head_sum(y, head) * inv_n
        var = _head_sum(dev * dev, head) * inv_n
        yn = dev * lax.rsqrt(var + GN_EPS) * lnw_ref[...] + lnb_ref[...]
        rw_ref[...] = ((yn + bonus_ref[0].astype(F32)) * g_ref[0].astype(F32)).astype(rw_ref.dtype)

    def gate(z):
        return jax.nn.sigmoid(gate_ref[0, :, z * tn:(z + 1) * tn].astype(F32))

    out = (gate(0) * _bdot(attn_ref[0], wa_ref[...]) + gate(1) * _bdot(pool_ref[0], wp_ref[...])
           + gate(2) * _bdot(rw_ref[...], wr_ref[...]))
    o_ref[0] = out.astype(o_ref.dtype)


def _merge_tile(D):
    return _pick(D, (512, 256, 128))


def _merge(attn, pool, y, g, bonus, ln_w, ln_b, gates, w_a, w_p, w_r, head):
    B, T, _ = attn.shape
    W = g.shape[2]
    D = w_a.shape[1]
    tm = _row_tile(T)
    tn = _merge_tile(D)
    nj = D // tn
    row = lambda w: pl.BlockSpec((1, tm, w), lambda b, i, j: (b, i, 0))
    vec = pl.BlockSpec((1, W), lambda b, i, j: (0, 0))
    wsp = lambda w: pl.BlockSpec((w.shape[0], tn), lambda b, i, j: (0, j))
    return pl.pallas_call(
        functools.partial(_merge_kernel, head=head, tn=tn),
        out_shape=jax.ShapeDtypeStruct((B, T, D), BF16),
        grid=(B, T // tm, nj),
        in_specs=[row(attn.shape[2]), row(pool.shape[2]),
                  pl.BlockSpec((N_DIR, 1, tm, W), lambda b, i, j: (0, b, i, 0)),
                  row(W), row(W), vec, vec,
                  pl.BlockSpec((1, tm, N_BRANCH * tn), lambda b, i, j: (b, i, j)),
                  wsp(w_a), wsp(w_p), wsp(w_r)],
        out_specs=pl.BlockSpec((1, tm, tn), lambda b, i, j: (b, i, j)),
        scratch_shapes=[pltpu.VMEM((tm, W), BF16)],
        compiler_params=_cparams("parallel", "parallel", "arbitrary"),
        name="merge",
    )(attn, pool, y, g, bonus, ln_w.reshape(1, W), ln_b.reshape(1, W), gates, w_a, w_p, w_r)


def _out_kernel(m_ref, w_ref, x_ref, gpost_ref, gpre_ref, ml_ref, mc_ref, wr_ref,
                x_out, h_out, aff_out, *, tm, n_lat, n_e):
    is_lat, m_lat, m_ctx = _mod_rows(ml_ref, mc_ref, pl.program_id(1) * tm, tm, n_lat)
    mix = _bdot(m_ref[0], w_ref[...])
    normed = mix * lax.rsqrt(jnp.mean(mix * mix, axis=-1, keepdims=True) + NORM_EPS) * gpost_ref[...]
    x = x_ref[0] + jnp.where(is_lat, m_lat[2:3], m_ctx[2:3]) * normed
    x_out[0] = x
    h = _modnorm(x, gpre_ref[...], is_lat, m_lat, m_ctx, 3)
    h_out[0] = _pack_halves(h)
    h_hi = h.astype(BF16)
    part = _bdot(h_hi, wr_ref[...]) + _bdot(h - h_hi.astype(F32), wr_ref[...])
    logits = part[:, :V7X_LANES] + part[:, V7X_LANES:]
    logits = jnp.where(_iota(logits.shape, 1) < n_e, logits, -jnp.inf)
    e = jnp.exp(logits - jnp.max(logits, axis=-1, keepdims=True))
    aff_out[0] = e / jnp.sum(e, axis=-1, keepdims=True)


def _out_proj(merged, w_out, xs, g_post, g_pre, mod_l, w_router, n_lat):
    B, T, D = xs.shape
    E = w_router.shape[1]
    tm = _pick(T, (544, 272, 256, 128, 320, 64, 32, 16))
    wr = jnp.pad(w_router, ((0, 0), (0, V7X_LANES - E)))
    wr_hi = wr.astype(BF16)
    wr = jnp.concatenate([wr_hi, (wr - wr_hi.astype(F32)).astype(BF16)], axis=1)
    row = lambda w: pl.BlockSpec((1, tm, w), lambda b, i: (b, i, 0))
    vec = pl.BlockSpec((1, D), lambda b, i: (0, 0))
    return pl.pallas_call(
        functools.partial(_out_kernel, tm=tm, n_lat=n_lat, n_e=E),
        out_shape=(jax.ShapeDtypeStruct((B, T, D), F32), jax.ShapeDtypeStruct((B, T, D // 2), jnp.uint32),
                   jax.ShapeDtypeStruct((B, T, V7X_LANES), F32)),
        grid=(B, T // tm),
        in_specs=[row(D), pl.BlockSpec((D, D), lambda b, i: (0, 0)), row(D), vec, vec,
                  pl.BlockSpec((1, 6, D), lambda b, i: (b, 0, 0)),
                  pl.BlockSpec((1, 6, D), lambda b, i: (B, 0, 0)),
                  pl.BlockSpec((D, 2 * V7X_LANES), lambda b, i: (0, 0))],
        out_specs=(row(D), row(D // 2), row(V7X_LANES)),
        compiler_params=_cparams("parallel", "parallel"),
        name="out_proj",
    )(merged, w_out, xs, g_post.reshape(1, D), g_pre.reshape(1, D), mod_l, mod_l, wr)


def _excl_prefix(flags, blk):
    n = flags.shape[-1]
    upper = (_iota((blk, blk), 0) < _iota((blk, blk), 1)).astype(BF16)
    outs = []
    carry = jnp.zeros((flags.shape[0], 1), F32)
    for j in range(n // blk):
        seg = flags[:, j * blk:(j + 1) * blk]
        outs.append(_bdot(seg, upper) + carry)
        carry = carry + jnp.sum(seg, axis=-1, keepdims=True)
    return jnp.concatenate(outs, axis=-1) if len(outs) > 1 else outs[0]


def _topk_kernel(aff_ref, sel_ref, idx_ref, *, cap, blk):
    bits = lax.bitcast_convert_type(aff_ref[0], jnp.int32)
    E = bits.shape[0]

    def body(i, tau):
        cand = tau | jnp.left_shift(jnp.int32(1), 30 - i)
        cnt = jnp.sum((bits >= cand).astype(jnp.int32), axis=-1, keepdims=True)
        return jnp.where(cnt >= cap, cand, tau)

    tau = lax.fori_loop(0, 31, body, jnp.zeros((E, 1), jnp.int32))
    gt = bits > tau
    eq = bits == tau
    need = (cap - jnp.sum(gt.astype(jnp.int32), axis=-1, keepdims=True)).astype(F32)
    eq_rank = _excl_prefix(eq.astype(F32), blk)
    sel = gt | (eq & (eq_rank < need))
    pos = _excl_prefix(sel.astype(F32), blk)
    slots = jnp.where(sel, pos.astype(jnp.int32), -1)
    sel_ref[0] = slots
    n = slots.shape[1]
    tok = _iota((8, n), 1)
    digit = _iota((8, n), 0)
    digits = jnp.where(digit == 0, tok // TOKEN_RADIX, jnp.where(digit == 1, tok % TOKEN_RADIX, 0))
    for e in range(E):
        onehot = slots[e:e + 1, :] == _iota((cap, n), 0)
        d = _bdot(digits.astype(F32), onehot.astype(F32), NT)
        idx_ref[0, e:e + 1, :] = (d[0:1] * TOKEN_RADIX + d[1:2]).astype(jnp.int32)


def _topk_slots(aff_t, cap):
    B, E, n = aff_t.shape
    assert n <= TOKEN_RADIX * 256
    blk = _pick(n, (512, 256, 128))
    return pl.pallas_call(
        functools.partial(_topk_kernel, cap=cap, blk=blk),
        out_shape=(jax.ShapeDtypeStruct((B, E, n), jnp.int32),
                   jax.ShapeDtypeStruct((B, E, cap), jnp.int32)),
        grid=(B,),
        in_specs=[pl.BlockSpec((1, E, n), lambda b: (b, 0, 0))],
        out_specs=(pl.BlockSpec((1, E, n), lambda b: (b, 0, 0)),
                   pl.BlockSpec((1, E, cap), lambda b: (b, 0, 0))),
        compiler_params=_cparams("parallel"),
        name="expert_topk",
    )(aff_t)


def _gather_kernel(idx_ref, h_ref, xe_ref, *, cap, n_e):
    base = (pl.program_id(0) * n_e + pl.program_id(1)) * cap

    def body(s, carry):
        t = idx_ref[base + s]
        xe_ref[0, 0, pl.ds(s, 1), :] = h_ref[0, pl.ds(t, 1), :]
        return carry

    lax.fori_loop(0, cap, body, 0, unroll=8)


def _gather(idx, hp, row_block, n):
    B, E, cap = idx.shape
    dh = hp.shape[2]
    grid_spec = pltpu.PrefetchScalarGridSpec(
        num_scalar_prefetch=1, grid=(B, E),
        in_specs=[pl.BlockSpec((1, n, dh), lambda b, e, idx_ref: (b, row_block, 0))],
        out_specs=pl.BlockSpec((1, 1, cap, dh), lambda b, e, idx_ref: (b, e, 0, 0)))
    return pl.pallas_call(
        functools.partial(_gather_kernel, cap=cap, n_e=E),
        out_shape=jax.ShapeDtypeStruct((B, E, cap, dh), jnp.uint32),
        grid_spec=grid_spec,
        compiler_params=_cparams("parallel", "arbitrary"),
        name="expert_gather",
    )(idx.reshape(-1), hp)


def _pack_halves(h):
    bits = lax.bitcast_convert_type(h.astype(BF16).astype(F32), jnp.uint32)
    half = h.shape[1] // 2
    return (bits[:, :half] >> 16) | (bits[:, half:] & jnp.uint32(0xFFFF0000))


def _unpack_halves(p):
    lo = lax.bitcast_convert_type(p << 16, F32)
    hi = lax.bitcast_convert_type(p & jnp.uint32(0xFFFF0000), F32)
    return lo, hi


def _ffn_kernel(xe_ref, wg_ref, wu_ref, wd_ref, ye_ref):
    lo, hi = _unpack_halves(xe_ref[0, 0])
    half = lo.shape[1]
    gate = _bdot(lo, wg_ref[0, :half, :]) + _bdot(hi, wg_ref[0, half:, :])
    up = _bdot(lo, wu_ref[0, :half, :]) + _bdot(hi, wu_ref[0, half:, :])
    hid = gate * jax.nn.sigmoid(gate) * up
    ye_ref[0, 0] = _bdot(hid, wd_ref[0]).astype(ye_ref.dtype)


def _expert_ffn(xe, w_gate, w_up, w_down):
    B, E, cap, dh = xe.shape
    D = 2 * dh
    FF = w_gate.shape[2]
    return pl.pallas_call(
        _ffn_kernel,
        out_shape=jax.ShapeDtypeStruct((B, E, cap, D), BF16),
        grid=(E, B),
        in_specs=[pl.BlockSpec((1, 1, cap, dh), lambda e, b: (b, e, 0, 0)),
                  pl.BlockSpec((1, D, FF), lambda e, b: (e, 0, 0)),
                  pl.BlockSpec((1, D, FF), lambda e, b: (e, 0, 0)),
                  pl.BlockSpec((1, FF, D), lambda e, b: (e, 0, 0))],
        out_specs=pl.BlockSpec((1, 1, cap, D), lambda e, b: (b, e, 0, 0)),
        compiler_params=_cparams("parallel", "parallel"),
        name="expert_ffn",
    )(xe, w_gate, w_up, w_down)


def _scatter_kernel(selc_ref, affc_ref, ye_ref, o_ref, *, cap):
    @pl.when(pl.program_id(2) == 0)
    def _():
        o_ref[...] = jnp.zeros_like(o_ref)

    selc = selc_ref[0, 0]
    onehot = selc == _iota((selc.shape[0], cap), 1)
    o_ref[0] += affc_ref[0, 0] * _bdot(onehot.astype(BF16), ye_ref[0, 0])


def _scatter(sel, aff_t, ye):
    B, E, n = sel.shape
    cap, D = ye.shape[2:]
    td = _pick(D, (512, 256, 128))
    return pl.pallas_call(
        functools.partial(_scatter_kernel, cap=cap),
        out_shape=jax.ShapeDtypeStruct((B, n, D), F32),
        grid=(B, D // td, E),
        in_specs=[pl.BlockSpec((1, 1, n, 1), lambda b, j, e: (b, e, 0, 0)),
                  pl.BlockSpec((1, 1, n, 1), lambda b, j, e: (b, e, 0, 0)),
                  pl.BlockSpec((1, 1, cap, td), lambda b, j, e: (b, e, 0, j))],
        out_specs=pl.BlockSpec((1, n, td), lambda b, j, e: (b, 0, j)),
        compiler_params=_cparams("parallel", "parallel", "arbitrary"),
        name="expert_scatter",
    )(sel.reshape(B, E, n, 1), aff_t.reshape(B, E, n, 1), ye)


def _expert_choice(hb, aff, row_block, n, w_gate, w_up, w_down):
    E = w_gate.shape[0]
    cap = EC_FACTOR * n // E
    aff_t = jnp.swapaxes(aff[:, row_block * n:(row_block + 1) * n, :E], 1, 2)
    sel, idx = _topk_slots(aff_t, cap)
    xe = _gather(idx, hb, row_block, n)
    ye = _expert_ffn(xe, w_gate, w_up, w_down)
    return _scatter(sel, aff_t, ye)


def _rms(x, g):
    return x * lax.rsqrt(jnp.mean(x * x, axis=-1, keepdims=True) + NORM_EPS) * g


def _rope_tables(n_lat, n_ctx, head_dim):
    rows = n_lat // GRID_W
    row = jnp.repeat(jnp.arange(rows), GRID_W).astype(F32)
    col = (jnp.arange(rows * GRID_W) % GRID_W).astype(F32)
    half = head_dim // 2
    inv = ROPE_THETA ** (-jnp.arange(0, half, 2, dtype=F32) / half)
    ar, ac = row[:, None] * inv, col[:, None] * inv
    cos = jnp.concatenate([jnp.cos(ar), jnp.cos(ar), jnp.cos(ac), jnp.cos(ac)], axis=-1)
    sin = jnp.concatenate([-jnp.sin(ar), jnp.sin(ar), -jnp.sin(ac), jnp.sin(ac)], axis=-1)
    cos = jnp.concatenate([cos, jnp.ones((n_ctx, head_dim), F32)], axis=0)
    sin = jnp.concatenate([sin, jnp.zeros((n_ctx, head_dim), F32)], axis=0)
    return cos, sin


def _pad_cols(w, mult):
    return jnp.pad(w, ((0, 0), (0, (-w.shape[1]) % mult)))


def kernel(x, c, ctx, c_ctx, w_mod, b_mod, norm_pre, norm_post, w_in, q_norm, k_norm, w_attn_o, w_pool_group, pool_scale, w_pool_o, rwkv_mu, rwkv_w0, rwkv_w2, rwkv_a0, rwkv_a2, rwkv_g2, rwkv_k_k, rwkv_k_a, rwkv_r_k, rwkv_ln_w, rwkv_ln_b, w_rwkv_o, w_out, w_router, w_exp_gate, w_exp_up, w_exp_down):
    B, n_lat, D = x.shape
    n_ctx = ctx.shape[1]
    T = n_lat + n_ctx
    depth = w_mod.shape[0]
    hd = q_norm.shape[1]
    attn_w = w_attn_o.shape[1]
    pool_w = w_pool_o.shape[1]
    rwkv_w = w_rwkv_o.shape[1]
    n_shift = rwkv_mu.shape[1]
    n_in = w_in.shape[2]
    kv_w = (n_in - attn_w - pool_w - n_shift - N_BRANCH * D) // 2
    head = rwkv_r_k.shape[2]
    col_k = attn_w
    col_v = col_k + kv_w
    col_pool = col_v + kv_w
    col_r = col_pool + pool_w
    col_gate = col_r + n_shift
    hq, hkv = attn_w // hd, kv_w // hd
    assert n_lat % (hq // hkv * hd) == 0 and pool_w % (hq // hkv * hd) == 0 and n_ctx % RWKV_CHUNK == 0

    cos, sin = _rope_tables(n_lat, n_ctx, hd)
    s_all = jnp.concatenate([jax.nn.silu(c), jax.nn.silu(c_ctx)[None]], axis=0)
    s_all = jnp.pad(s_all, ((0, (-s_all.shape[0]) % 8), (0, 0)))
    mod = _modulation(s_all, w_mod, b_mod)[:, :B + 1].reshape(depth, B + 1, 6, D)
    xs = jnp.concatenate([x, ctx], axis=1)
    (h,) = _stream_update(xs, None, None, None, mod[0], norm_pre[0, 0], mod[0], n_lat, T)

    for l in range(depth):
        keep_ctx = l < depth - 1
        wl = w_in[l]
        w_a = jnp.concatenate([wl[:, col_pool:col_r], wl[:, :col_pool]], axis=1).astype(BF16)
        w_b = _pad_cols(wl[:, col_r:col_gate], 3 * V7X_LANES).astype(BF16)
        tn_m = _merge_tile(D)
        w_c = (wl[:, col_gate:].reshape(D, N_BRANCH, D // tn_m, tn_m).swapaxes(1, 2)
               .reshape(D, N_BRANCH * D).astype(BF16))
        p_a = _in_proj(h, w_a)
        p_b = _in_proj(h, w_b)
        p_c = _in_proj(h, w_c)

        attn = _attention(p_a, pool_w, pool_w + attn_w, pool_w + attn_w + kv_w, cos, sin,
                          q_norm[l], k_norm[l], hd, hq, hkv, n_lat)
        pool = _pool(p_a, 0, w_pool_group[l].astype(BF16), pool_scale[l], n_lat)
        r, v, kk, lw, cum, kd, bd, g, bonus = _rwkv_prep(
            p_b, n_lat, rwkv_w, head, rwkv_mu[l], rwkv_w0[l], rwkv_w2[l], rwkv_a0[l], rwkv_a2[l],
            rwkv_g2[l], rwkv_k_k[l], rwkv_k_a[l], rwkv_r_k[l])
        y = _rwkv_scan(r, v, kk, lw, cum, kd, bd, head, n_lat)
        merged = _merge(attn, pool, y, g, bonus, rwkv_ln_w[l], rwkv_ln_b[l], p_c,
                        w_attn_o[l].astype(BF16), w_pool_o[l].astype(BF16), w_rwkv_o[l].astype(BF16), head)
        xs, hb, aff = _out_proj(merged, w_out[l].astype(BF16), xs, norm_post[l, 0], norm_pre[l, 1],
                                mod[l], w_router[l], n_lat)

        wg, wu, wdn = w_exp_gate[l].astype(BF16), w_exp_up[l].astype(BF16), w_exp_down[l].astype(BF16)
        f_lat = _expert_choice(hb, aff, 0, n_lat, wg, wu, wdn)
        if keep_ctx:
            f_ctx = _expert_choice(hb, aff, n_lat // n_ctx, n_ctx, wg, wu, wdn)
            xs, h = _stream_update(xs, f_lat, f_ctx, norm_post[l, 1], mod[l], norm_pre[l + 1, 0],
                                   mod[l + 1], n_lat, T)
        else:
            (xs,) = _stream_update(xs, f_lat, None, norm_post[l, 1], mod[l], None, None, n_lat, n_lat)
    return xs
```

```python
import functools

import jax
import jax.numpy as jnp
from jax import lax
from jax.experimental import pallas as pl
from jax.experimental.pallas import tpu as pltpu

F32 = jnp.float32
BF16 = jnp.bfloat16
HIGHEST = lax.Precision.HIGHEST

GRID_W = 64
NORM_EPS = 1e-6
ROPE_THETA = 10000.0
POOL_WINDOWS = (2, 4, 8, 16)
GN_EPS = 64e-5
EC_FACTOR = 2
N_DIR = 2
N_BRANCH = 3
RWKV_CHUNK = 64
HALO = 16

LOG2_E = 1.4426950408889634
ATTN_SHIFT_LIMIT = 60.0
TOKEN_RADIX = 64

V7X_LANES = 128
VMEM_LIMIT = 52 * 1024 * 1024

NT = (((1,), (1,)), ((), ()))
TN = (((0,), (0,)), ((), ()))


def _pick(n, cands):
    for c in cands:
        if c <= n and n % c == 0:
            return c
    return n


def _cparams(*sem):
    return pltpu.CompilerParams(dimension_semantics=sem, vmem_limit_bytes=VMEM_LIMIT)


def _bdot(x, y, dn=None):
    x = x.astype(BF16)
    y = y.astype(BF16)
    if dn is None:
        return jnp.dot(x, y, preferred_element_type=F32)
    return lax.dot_general(x, y, dn, preferred_element_type=F32)


def _split_dot(x, y):
    hi = x.astype(BF16)
    lo = x - hi.astype(F32)
    return _bdot(hi, y) + _bdot(lo, y)


def _iota(shape, dim):
    return lax.broadcasted_iota(jnp.int32, shape, dim)


def _row_tile(T, min_parts=8):
    for parts in range(min_parts, T // 16 + 1):
        if T % parts == 0 and (T // parts) % 16 == 0:
            return T // parts
    return T


def _mod_kernel(s_ref, w_ref, b_ref, o_ref):
    o_ref[0] = jnp.dot(s_ref[...], w_ref[0], precision=HIGHEST,
                       preferred_element_type=F32) + b_ref[0]


def _modulation(s, w_mod, b_mod):
    L, D, N = w_mod.shape
    R = s.shape[0]
    tn = _pick(N, (1024, 512, 256, 128))
    return pl.pallas_call(
        _mod_kernel,
        out_shape=jax.ShapeDtypeStruct((L, R, N), F32),
        grid=(L, N // tn),
        in_specs=[pl.BlockSpec((R, D), lambda l, j: (0, 0)),
                  pl.BlockSpec((1, D, tn), lambda l, j: (l, 0, j)),
                  pl.BlockSpec((1, 1, tn), lambda l, j: (l, 0, j))],
        out_specs=pl.BlockSpec((1, R, tn), lambda l, j: (l, 0, j)),
        compiler_params=_cparams("parallel", "parallel"),
        name="adaln_mod",
    )(s, w_mod, b_mod.reshape(L, 1, N))


def _mod_rows(m_lat_ref, m_ctx_ref, row0, tm, n_lat):
    is_lat = (row0 + _iota((tm, 1), 0)) < n_lat
    return is_lat, m_lat_ref[0], m_ctx_ref[0]


def _modnorm(x, g, is_lat, m_lat, m_ctx, i_shift):
    shift = jnp.where(is_lat, m_lat[i_shift:i_shift + 1], m_ctx[i_shift:i_shift + 1])
    scale = jnp.where(is_lat, m_lat[i_shift + 1:i_shift + 2], m_ctx[i_shift + 1:i_shift + 2])
    y = x * lax.rsqrt(jnp.mean(x * x, axis=-1, keepdims=True) + NORM_EPS) * g
    return y * (1.0 + scale) + shift


def _small_row_tile(T, n_ctx):
    return _pick(n_ctx, (256, 128, 64, 32, 16))


def _stream_kernel(*refs, tm, n_lat, has_f, has_ctx_f, emit_h):
    refs = list(refs)
    x_ref = refs.pop(0)
    fl_ref = refs.pop(0) if has_f else None
    fc_ref = refs.pop(0) if has_ctx_f else None
    gpost_ref = refs.pop(0) if has_f else None
    ml_ref, mc_ref = refs.pop(0), refs.pop(0)
    if emit_h:
        gpre_ref, mln_ref, mcn_ref = refs.pop(0), refs.pop(0), refs.pop(0)
    x_out = refs.pop(0) if has_f else None
    h_out = refs.pop(0) if emit_h else None

    is_lat = (pl.program_id(1) * tm + _iota((tm, 1), 0)) < n_lat
    x = x_ref[0]
    if has_f:
        f = fl_ref[0]
        if has_ctx_f:
            f = jnp.where(is_lat, f, fc_ref[0])
        normed = f * lax.rsqrt(jnp.mean(f * f, axis=-1, keepdims=True) + NORM_EPS) * gpost_ref[...]
        x = x + jnp.where(is_lat, ml_ref[0][5:6], mc_ref[0][5:6]) * normed
        x_out[0] = x
    if emit_h:
        h_out[0] = _modnorm(x, gpre_ref[...], is_lat, mln_ref[0], mcn_ref[0], 0).astype(h_out.dtype)


def _stream_update(xs, f_lat, f_ctx, g_post, mod_l, g_pre_next, mod_next, n_lat, n_rows):
    B, T, D = xs.shape
    n_ctx = T - n_lat
    tm = _small_row_tile(T, n_ctx)
    has_f, has_ctx_f, emit_h = f_lat is not None, f_ctx is not None, g_pre_next is not None
    n_lt = n_lat // tm
    row = pl.BlockSpec((1, tm, D), lambda b, i: (b, i, 0))
    vec = pl.BlockSpec((1, D), lambda b, i: (0, 0))
    m_l = pl.BlockSpec((1, 6, D), lambda b, i: (b, 0, 0))
    m_c = pl.BlockSpec((1, 6, D), lambda b, i: (B, 0, 0))
    args, specs = [xs], [row]
    if has_f:
        args.append(f_lat)
        specs.append(pl.BlockSpec((1, tm, D), lambda b, i: (b, jnp.minimum(i, n_lt - 1), 0)))
    if has_ctx_f:
        args.append(f_ctx)
        specs.append(pl.BlockSpec((1, tm, D), lambda b, i: (b, jnp.maximum(i - n_lt, 0), 0)))
    if has_f:
        args.append(g_post.reshape(1, D))
        specs.append(vec)
    args += [mod_l, mod_l]
    specs += [m_l, m_c]
    if emit_h:
        args += [g_pre_next.reshape(1, D), mod_next, mod_next]
        specs += [vec, m_l, m_c]
    out_shape, out_specs = [], []
    if has_f:
        out_shape.append(jax.ShapeDtypeStruct((B, n_rows, D), F32))
        out_specs.append(row)
    if emit_h:
        out_shape.append(jax.ShapeDtypeStruct((B, n_rows, D), BF16))
        out_specs.append(row)
    kern = functools.partial(_stream_kernel, tm=tm, n_lat=n_lat, has_f=has_f, has_ctx_f=has_ctx_f,
                             emit_h=emit_h)
    return pl.pallas_call(
        kern, out_shape=tuple(out_shape), grid=(B, n_rows // tm), in_specs=specs,
        out_specs=tuple(out_specs), compiler_params=_cparams("parallel", "parallel"),
        name="stream_update",
    )(*args)


def _in_kernel(h_ref, w_ref, o_ref):
    o_ref[0] = _bdot(h_ref[0], w_ref[...]).astype(o_ref.dtype)


def _in_proj(h, w):
    B, T, D = h.shape
    N = w.shape[1]
    tm = _row_tile(T, 4)
    tn = _pick(N, (1280, 1024, 896, 768, 640, 512, 384, 256, 128))
    return pl.pallas_call(
        _in_kernel,
        out_shape=jax.ShapeDtypeStruct((B, T, N), BF16),
        grid=(B, T // tm, N // tn),
        in_specs=[pl.BlockSpec((1, tm, D), lambda b, i, j: (b, i, 0)),
                  pl.BlockSpec((D, tn), lambda b, i, j: (0, j))],
        out_specs=pl.BlockSpec((1, tm, tn), lambda b, i, j: (b, i, j)),
        compiler_params=_cparams("parallel", "parallel", "arbitrary"),
        name="in_proj",
    )(h, w)


def _rot(x, cos, sin):
    qd = x.shape[1] // 4
    first = (_iota((1, x.shape[1]), 1) // qd) % 2 == 0
    swapped = jnp.where(first, pltpu.roll(x, x.shape[1] - qd, 1), pltpu.roll(x, qd, 1))
    return x * cos + swapped * sin


def _head_norm(x, g):
    return x * lax.rsqrt(jnp.mean(x * x, axis=-1, keepdims=True) + NORM_EPS) * g


def _attn_kernel(bound_ref, q_ref, k_ref, v_ref, cq_ref, sq_ref, ck_ref, sk_ref, qn_ref, kn_ref, o_ref,
                 kp_ref, vp_ref, *, hd, group, tq, n_lat, scale):
    i = pl.program_id(2)
    unit = jnp.where(_iota((1, hd), 1) == 0, 1.0, 0.0)

    @pl.when(i == 0)
    def _():
        k = _rot(_head_norm(k_ref[0].astype(F32), kn_ref[...]), ck_ref[...], sk_ref[...])
        kp_ref[...] = jnp.concatenate([k, jnp.broadcast_to(unit, k.shape)], axis=1).astype(kp_ref.dtype)
        ones = jnp.ones(v_ref.shape[1:], vp_ref.dtype)
        vp_ref[...] = jnp.concatenate([v_ref[0].astype(vp_ref.dtype), ones], axis=1)

    bound = bound_ref[0]

    def attend(key0, shift_in_matmul):
        for g in range(group):
            q = _head_norm(q_ref[0, :, g * hd:(g + 1) * hd].astype(F32), qn_ref[...])
            q = _rot(q, cq_ref[...], sq_ref[...]) * scale
            if shift_in_matmul:
                q_aug = jnp.concatenate([q, jnp.broadcast_to(unit * (-bound), q.shape)], axis=1)
                p = jnp.exp2(_bdot(q_aug, kp_ref[key0:, :], NT))
            else:
                s = _bdot(q, kp_ref[key0:, :hd], NT)
                p = jnp.exp2(s - jnp.max(s, axis=-1, keepdims=True))
            ov = _bdot(p, vp_ref[key0:, :])
            o_ref[0, :, g * hd:(g + 1) * hd] = (ov[:, :hd] / ov[:, hd:hd + 1]).astype(o_ref.dtype)

    is_lat = i * tq < n_lat
    fast = bound < ATTN_SHIFT_LIMIT
    for lat_tile, key0 in ((True, 0), (False, n_lat)):
        for use_fast in (True, False):
            cond = jnp.logical_and(is_lat if lat_tile else jnp.logical_not(is_lat),
                                   fast if use_fast else jnp.logical_not(fast))
            pl.when(cond)(functools.partial(attend, key0, use_fast))


def _attention(p_qkv, q_col, k_col, v_col, cos, sin, q_norm, k_norm, hd, hq, hkv, n_lat):
    B, T, _ = p_qkv.shape
    group = hq // hkv
    gw = group * hd
    tq = _pick(T - n_lat, (256, 128, 64, 32, 16))
    kern = functools.partial(_attn_kernel, hd=hd, group=group, tq=tq, n_lat=n_lat,
                             scale=hd ** -0.5 * LOG2_E)
    tab_q = pl.BlockSpec((tq, hd), lambda b, h, i, s: (i, 0))
    tab_k = pl.BlockSpec((T, hd), lambda b, h, i, s: (0, 0))
    vec = pl.BlockSpec((1, hd), lambda b, h, i, s: (0, 0))
    bound = (1.02 * hd * hd ** -0.5 * LOG2_E) * jnp.max(jnp.abs(q_norm)) * jnp.max(jnp.abs(k_norm))
    grid_spec = pltpu.PrefetchScalarGridSpec(
        num_scalar_prefetch=1, grid=(B, hkv, T // tq),
        in_specs=[pl.BlockSpec((1, tq, gw), lambda b, h, i, s: (b, i, q_col // gw + h)),
                  pl.BlockSpec((1, T, hd), lambda b, h, i, s: (b, 0, k_col // hd + h)),
                  pl.BlockSpec((1, T, hd), lambda b, h, i, s: (b, 0, v_col // hd + h)),
                  tab_q, tab_q, tab_k, tab_k, vec, vec],
        out_specs=pl.BlockSpec((1, tq, gw), lambda b, h, i, s: (b, i, h)),
        scratch_shapes=[pltpu.VMEM((T, 2 * hd), BF16), pltpu.VMEM((T, 2 * hd), BF16)])
    return pl.pallas_call(
        kern,
        out_shape=jax.ShapeDtypeStruct((B, T, hq * hd), BF16),
        grid_spec=grid_spec,
        compiler_params=_cparams("parallel", "parallel", "arbitrary"),
        name="attention",
    )(bound.reshape(1).astype(F32), p_qkv, p_qkv, p_qkv, cos, sin, cos, sin,
      q_norm.reshape(1, hd), k_norm.reshape(1, hd))


def _segment_of_tile(i, tt, n_lat_tiles, n_tiles):
    is_lat = i < n_lat_tiles
    ti = jnp.where(is_lat, i, i - n_lat_tiles)
    seg_tiles = jnp.where(is_lat, n_lat_tiles, n_tiles - n_lat_tiles)
    return ti, seg_tiles


def _halo_specs(tt, width, T, col_block):
    hb = tt // HALO
    last = T // HALO - 1
    prev = pl.BlockSpec((1, HALO, width), lambda b, i: (b, jnp.maximum(i * hb - 1, 0), col_block))
    cur = pl.BlockSpec((1, tt, width), lambda b, i: (b, i, col_block))
    nxt = pl.BlockSpec((1, HALO, width), lambda b, i: (b, jnp.minimum((i + 1) * hb, last), col_block))
    return prev, cur, nxt


def _pool_kernel(prev_ref, cur_ref, next_ref, wg_ref, sc_ref, o_ref, *,
                 tt, n_lat_tiles, n_tiles, gw, windows):
    ti, seg_tiles = _segment_of_tile(pl.program_id(1), tt, n_lat_tiles, n_tiles)
    has_prev = ti > 0
    has_next = ti < seg_tiles - 1
    t_seg = seg_tiles * tt
    tpos = ti * tt + _iota((tt, 1), 0)

    cur = cur_ref[0]
    prev = prev_ref[0]
    nxt = next_ref[0]
    d_cur = _iota((tt, tt), 1) - _iota((tt, tt), 0)
    d_halo = _iota((tt, HALO), 1) - _iota((tt, HALO), 0)
    d_prev = d_halo - HALO
    d_next = d_halo + tt
    for g, win in enumerate(windows):
        lo_off = -(win // 2)
        hi_off = win - win // 2 - 1
        sl = slice(g * gw, (g + 1) * gw)
        band_c = ((d_cur >= lo_off) & (d_cur <= hi_off)).astype(BF16)
        band_p = ((d_prev >= lo_off) & (d_prev <= hi_off) & has_prev).astype(BF16)
        band_n = ((d_next >= lo_off) & (d_next <= hi_off) & has_next).astype(BF16)
        ug = cur[:, sl]
        tot = _bdot(band_c, ug) + _bdot(band_p, prev[:, sl]) + _bdot(band_n, nxt[:, sl])
        lo = jnp.maximum(tpos + lo_off, 0)
        hi = jnp.minimum(tpos + hi_off + 1, t_seg)
        pooled = tot / (hi - lo).astype(F32) - ug.astype(F32)
        y = _bdot(pooled, wg_ref[g]) * sc_ref[:, sl]
        o_ref[0, :, sl] = y.astype(o_ref.dtype)


def _pool(p_arr, col_block, w_group, scale, n_lat):
    B, T, _ = p_arr.shape
    G, gw, _ = w_group.shape
    W = G * gw
    tt = _pick(T - n_lat, (256, 128, 64, 32, 16))
    assert n_lat % tt == 0 and tt % HALO == 0 and max(POOL_WINDOWS) <= HALO
    n_tiles = T // tt
    kern = functools.partial(_pool_kernel, tt=tt, n_lat_tiles=n_lat // tt, n_tiles=n_tiles,
                             gw=gw, windows=POOL_WINDOWS)
    prev, cur, nxt = _halo_specs(tt, W, T, col_block)
    return pl.pallas_call(
        kern,
        out_shape=jax.ShapeDtypeStruct((B, T, W), BF16),
        grid=(B, n_tiles),
        in_specs=[prev, cur, nxt,
                  pl.BlockSpec((G, gw, gw), lambda b, i: (0, 0, 0)),
                  pl.BlockSpec((1, W), lambda b, i: (0, 0))],
        out_specs=pl.BlockSpec((1, tt, W), lambda b, i: (b, i, 0)),
        compiler_params=_cparams("parallel", "parallel"),
        name="pool",
    )(p_arr, p_arr, p_arr, w_group, scale.reshape(1, W))


def _head_sum(x, head):
    lanes = x.shape[1]
    blk = min(lanes, V7X_LANES)
    same = (_iota((blk, blk), 0) // head == _iota((blk, blk), 1) // head).astype(BF16)
    parts = [_split_dot(x[:, j:j + blk], same) for j in range(0, lanes, blk)]
    return jnp.concatenate(parts, axis=1) if len(parts) > 1 else parts[0]


def _rwkv_prep_kernel(prev_ref, cur_ref, next_ref, mu_ref, w0_ref, w2_ref, a0_ref, a2_ref, g2_ref,
                      kk_w_ref, ka_ref, rk_ref,
                      r_ref, v_ref, kk_ref, lw_ref, cum_ref, kd_ref, bd_ref, g_ref, bonus_ref, *,
                      tt, n_lat_tiles, n_tiles, W, head, lora, chunk):
    ti, seg_tiles = _segment_of_tile(pl.program_id(1), tt, n_lat_tiles, n_tiles)
    u = cur_ref[0].astype(F32)
    row = _iota((tt, 1), 0)
    before = jnp.where(ti > 0, prev_ref[0, HALO - 1:HALO, :].astype(F32), 0.0)
    after = jnp.where(ti < seg_tiles - 1, next_ref[0, 0:1, :].astype(F32), 0.0)
    u_prev = jnp.where(row == 0, before, pltpu.roll(u, 1, 0))
    u_next = jnp.where(row == tt - 1, after, pltpu.roll(u, tt - 1, 0))
    u = u + (0.5 * (u_prev + u_next) - u) * mu_ref[...]

    r, k, v = u[:, :W], u[:, W:2 * W], u[:, 2 * W:3 * W]
    o1 = 3 * W
    o2 = o1 + lora
    o3 = o2 + lora
    w_lin = w0_ref[...] + _bdot(jnp.tanh(u[:, o1:o2]), w2_ref[...])
    w_log = -(jnp.maximum(-w_lin, 0.0) + jnp.log(1.0 + jnp.exp(-jnp.abs(w_lin)))) - 0.5
    lw = -jnp.exp(w_log)
    lw_ref[0] = lw
    ri, ci = _iota((tt, tt), 0), _iota((tt, tt), 1)
    same_chunk = ri // chunk == ci // chunk
    p1 = lw.astype(BF16)
    p2 = (lw - p1.astype(F32)).astype(BF16)
    p3 = (lw - p1.astype(F32) - p2.astype(F32)).astype(BF16)
    for z, tri in enumerate((same_chunk & (ci <= ri), same_chunk & (ci >= ri))):
        sl = slice(z * W, (z + 1) * W)
        cum_ref[0, :, sl] = _bdot(tri, p1[:, sl]) + _bdot(tri, p2[:, sl]) + _bdot(tri, p3[:, sl])
    a = jax.nn.sigmoid(a0_ref[...] + _bdot(u[:, o2:o3], a2_ref[...]))
    g_ref[0] = _bdot(jax.nn.sigmoid(u[:, o3:]), g2_ref[...]).astype(g_ref.dtype)

    kk = k * kk_w_ref[...]
    kk = kk * lax.rsqrt(jnp.maximum(_head_sum(kk * kk, head), 1e-24))
    k_sum = 0.0
    for z in range(N_DIR):
        a_z = a[:, z * W:(z + 1) * W]
        k_z = k * (1.0 + (a_z - 1.0) * ka_ref[...])
        kd_ref[0, :, z * W:(z + 1) * W] = k_z.astype(kd_ref.dtype)
        bd_ref[0, :, z * W:(z + 1) * W] = (kk * a_z).astype(bd_ref.dtype)
        k_sum = k_sum + k_z
    r_ref[0] = r.astype(r_ref.dtype)
    v_ref[0] = v.astype(v_ref.dtype)
    kk_ref[0] = kk.astype(kk_ref.dtype)
    bonus_ref[0] = (_head_sum(r * k_sum * rk_ref[...], head) * v).astype(bonus_ref.dtype)


def _rwkv_prep(p_rwkv, n_lat, W, head, mu, w0, w2, a0, a2, g2, k_k, k_a, r_k):
    B, T, NS = p_rwkv.shape
    lora = N_DIR * w2.shape[1]
    n_gate = NS - 3 * W - 2 * lora
    tt = _pick(T - n_lat, (256, 128, 64, 32, 16))
    n_tiles = T // tt
    def cat(m):
        z = jnp.zeros_like(m[0])
        return jnp.concatenate([jnp.concatenate([m[0], z], axis=1),
                                jnp.concatenate([z, m[1]], axis=1)], axis=0).astype(BF16)
    g2p = jnp.pad(g2, ((0, n_gate - g2.shape[0]), (0, 0))).astype(BF16)
    mup = jnp.pad(mu, (0, NS - mu.shape[0])).reshape(1, NS)
    kern = functools.partial(_rwkv_prep_kernel, tt=tt, n_lat_tiles=n_lat // tt, n_tiles=n_tiles,
                             W=W, head=head, lora=lora, chunk=RWKV_CHUNK)
    assert tt % RWKV_CHUNK == 0
    prev, cur, nxt = _halo_specs(tt, NS, T, 0)
    full = lambda shp: pl.BlockSpec(shp, lambda b, i: (0,) * len(shp))
    tile = lambda w: pl.BlockSpec((1, tt, w), lambda b, i: (b, i, 0))
    sd = lambda w, dt: jax.ShapeDtypeStruct((B, T, w), dt)
    return pl.pallas_call(
        kern,
        out_shape=(sd(W, BF16), sd(W, BF16), sd(W, BF16), sd(2 * W, F32), sd(2 * W, F32), sd(2 * W, BF16),
                   sd(2 * W, BF16), sd(W, BF16), sd(W, BF16)),
        grid=(B, n_tiles),
        in_specs=[prev, cur, nxt, full((1, NS)), full((1, 2 * W)), full((lora, 2 * W)),
                  full((1, 2 * W)), full((lora, 2 * W)), full((n_gate, W)),
                  full((1, W)), full((1, W)), full((1, W))],
        out_specs=(tile(W), tile(W), tile(W), tile(2 * W), tile(2 * W), tile(2 * W), tile(2 * W), tile(W),
                   tile(W)),
        compiler_params=_cparams("parallel", "parallel"),
        name="rwkv_prep",
    )(p_rwkv, p_rwkv, p_rwkv, mup, w0.reshape(1, 2 * W), cat(w2), a0.reshape(1, 2 * W), cat(a2), g2p,
      k_k.reshape(1, W), k_a.reshape(1, W), r_k.reshape(1, W))


def _rwkv_kernel(r_ref, v_ref, kk_ref, lw_ref, cum_ref, k_ref, b_ref, y_ref, g_ref, *, C, N, n_seq):
    @pl.when(pl.program_id(1) == 0)
    def _():
        g_ref[...] = jnp.zeros_like(g_ref)

    W = r_ref.shape[2]
    PW = 2 * N
    backward = pl.program_id(0) >= n_seq
    sgn = jnp.where(backward, -1, 1)
    order = (_iota((C, C), 0) - _iota((C, C), 1)) * sgn
    lw = lw_ref[0]
    cum = cum_ref[0]
    pc = jnp.where(backward, cum[0:1, :], cum[C - 1:C, :])
    k = k_ref[0].astype(F32)
    b = b_ref[0].astype(F32)
    p_inv = jnp.exp(-cum)
    p_hat = jnp.exp(pc - cum)
    r_t = r_ref[0].astype(F32) * jnp.exp(cum)
    k_t = k * p_inv
    b_t = b * p_inv
    a_t = -kk_ref[0].astype(F32) * jnp.exp(cum - lw)
    b_h = b * p_hat
    k_h = k * p_hat
    p_c = jnp.exp(pc)
    v = v_ref[0].astype(F32)

    order2 = jnp.concatenate([order, order], axis=1)
    strict2 = order2 > 0
    incl2 = order2 >= 0
    eye = (order == 0).astype(F32)
    first = _iota((1, PW), 1) < N
    lane_lo = _iota((1, 2 * C), 1) < C
    same_head = (_iota((PW, PW), 0) // N) == (_iota((PW, PW), 1) // N)

    n_pairs = W // PW
    heads = [(p, sub) for p in range(n_pairs) for sub in range(2)]
    psl = [slice(p * PW, (p + 1) * PW) for p in range(n_pairs)]
    masks = (first, jnp.logical_not(first))
    v_m, upper, lower = [], [], []
    for p in range(n_pairs):
        sl = psl[p]
        rhs_bk = jnp.concatenate([b_t[:, sl], k_t[:, sl]], axis=0)
        lhs = jnp.concatenate([jnp.where(m, x[:, sl], 0.0) for m in masks for x in (a_t, r_t)], axis=0)
        amat = _bdot(lhs, rhs_bk, NT)
        for sub in range(2):
            upper.append(jnp.where(strict2, amat[2 * sub * C:(2 * sub + 1) * C], 0.0))
            lower.append(jnp.where(incl2, amat[(2 * sub + 1) * C:(2 * sub + 2) * C], 0.0))
            v_m.append(jnp.where(masks[sub], v[:, sl], 0.0))
    zero = jnp.zeros((C, PW), F32)
    pw, x = [], []
    for p in range(n_pairs):
        up0, up1 = upper[2 * p], upper[2 * p + 1]
        a_ak = jnp.where(first, pltpu.roll(up0, C, 1), up1)
        w = _bdot(a_ak, jnp.concatenate([v_m[2 * p], v_m[2 * p + 1]], axis=0))
        w = pltpu.roll(w, N, 1)
        a_p = a_t[:, psl[p]]
        x += [jnp.where(first, a_p, w), jnp.where(first, w, a_p)]
        pw.append(jnp.where(first, up0, pltpu.roll(up1, C, 1)))
    span = 1
    while span < C:
        span *= 2
        for p in range(n_pairs):
            x0, x1 = x[2 * p], x[2 * p + 1]
            rhs = [jnp.concatenate([x0, zero], axis=1), jnp.concatenate([zero, x1], axis=1)]
            if span < C:
                rhs = [jnp.concatenate([jnp.where(m, pw[p], 0.0), xr], axis=1) for m, xr in zip(masks, rhs)]
            res = _bdot(pw[p], jnp.concatenate(rhs, axis=0))
            if span < C:
                pw[p] = res[:, :PW]
                res = res[:, PW:]
            x[2 * p], x[2 * p + 1] = x0 + res[:, :PW], x1 + res[:, PW:]

    g0 = [g_ref[p] for p in range(n_pairs)]
    uv, rg = [], []
    for p in range(n_pairs):
        x0, x1 = x[2 * p], x[2 * p + 1]
        ahat = [jnp.where(masks[0], x0, 0.0), jnp.where(masks[1], x1, 0.0)]
        vhat = [jnp.where(masks[0], pltpu.roll(x0, N, 1), 0.0), jnp.where(masks[1], pltpu.roll(x1, N, 1), 0.0)]
        res = _bdot(jnp.concatenate([r_t[:, psl[p]]] + ahat, axis=0), g0[p])
        rg.append(res[:C])
        u0 = res[C:2 * C] + vhat[0]
        u1 = res[2 * C:] + vhat[1]
        uv.append(jnp.concatenate([u0, v_m[2 * p], u1, v_m[2 * p + 1]], axis=0).astype(BF16))
    for p in range(n_pairs):
        y_ref[0, 0, :, psl[p]] = rg[p] + _bdot(jnp.concatenate([lower[2 * p], lower[2 * p + 1]], axis=1),
                                               uv[p])
    for p in range(n_pairs):
        sl = psl[p]
        bk_h = jnp.concatenate([b_h[:, sl], k_h[:, sl]], axis=0)
        upd = _bdot(jnp.concatenate([bk_h, bk_h], axis=0), uv[p], TN)
        decay = jnp.transpose(jnp.broadcast_to(p_c[:, sl], (PW, PW)))
        g_ref[p] = jnp.where(same_head, decay * g0[p] + upd, 0.0)


def _rwkv_scan(r, v, kk, lw, cum, kd, bd, head, n_lat):
    B, T, W = r.shape
    C = RWKV_CHUNK
    assert C == head
    nc = T // C
    nc_lat = n_lat // C

    def chunk(s, c):
        fwd = jnp.where(c < nc - nc_lat, nc_lat + c, c - (nc - nc_lat))
        bwd = jnp.where(c < nc - nc_lat, nc - 1 - c, nc - 1 - c)
        return jnp.where(s >= B, bwd, fwd)

    shared = pl.BlockSpec((1, C, W), lambda s, c: (s % B, chunk(s, c), 0))
    per_dir = pl.BlockSpec((1, C, W), lambda s, c: (s % B, chunk(s, c), s // B))
    kern = functools.partial(_rwkv_kernel, C=C, N=head, n_seq=B)
    return pl.pallas_call(
        kern,
        out_shape=jax.ShapeDtypeStruct((N_DIR, B, T, W), F32),
        grid=(N_DIR * B, nc),
        in_specs=[shared, shared, shared, per_dir, per_dir, per_dir, per_dir],
        out_specs=pl.BlockSpec((1, 1, C, W), lambda s, c: (s // B, s % B, chunk(s, c), 0)),
        scratch_shapes=[pltpu.VMEM((W // (2 * head), 2 * head, 2 * head), F32)],
        compiler_params=_cparams("parallel", "arbitrary"),
        name="rwkv7_chunk",
    )(r, v, kk, lw, cum, kd, bd)


def _merge_kernel(attn_ref, pool_ref, y_ref, g_ref, bonus_ref, lnw_ref, lnb_ref, gate_ref,
                  wa_ref, wp_ref, wr_ref, o_ref, rw_ref, *, head, tn):
    @pl.when(pl.program_id(2) == 0)
    def _():
        y = y_ref[0, 0] + y_ref[1, 0]
        inv_n = 1.0 / head
        dev = y - _head_sum(y, head) * inv_n
        var = _head_sum(dev * dev, head) * inv_n
        yn = dev * lax.rsqrt(var + GN_EPS) * lnw_ref[...] + lnb_ref[...]
        rw_ref[...] = ((yn + bonus_ref[0].astype(F32)) * g_ref[0].astype(F32)).astype(rw_ref.dtype)

    def gate(z):
        return jax.nn.sigmoid(gate_ref[0, :, z * tn:(z + 1) * tn].astype(F32))

    out = (gate(0) * _bdot(attn_ref[0], wa_ref[...]) + gate(1) * _bdot(pool_ref[0], wp_ref[...])
           + gate(2) * _bdot(rw_ref[...], wr_ref[...]))
    o_ref[0] = out.astype(o_ref.dtype)


def _merge_tile(D):
    return _pick(D, (512, 256, 128))


def _merge(attn, pool, y, g, bonus, ln_w, ln_b, gates, w_a, w_p, w_r, head):
    B, T, _ = attn.shape
    W = g.shape[2]
    D = w_a.shape[1]
    tm = _row_tile(T)
    tn = _merge_tile(D)
    nj = D // tn
    row = lambda w: pl.BlockSpec((1, tm, w), lambda b, i, j: (b, i, 0))
    vec = pl.BlockSpec((1, W), lambda b, i, j: (0, 0))
    wsp = lambda w: pl.BlockSpec((w.shape[0], tn), lambda b, i, j: (0, j))
    return pl.pallas_call(
        functools.partial(_merge_kernel, head=head, tn=tn),
        out_shape=jax.ShapeDtypeStruct((B, T, D), BF16),
        grid=(B, T // tm, nj),
        in_specs=[row(attn.shape[2]), row(pool.shape[2]),
                  pl.BlockSpec((N_DIR, 1, tm, W), lambda b, i, j: (0, b, i, 0)),
                  row(W), row(W), vec, vec,
                  pl.BlockSpec((1, tm, N_BRANCH * tn), lambda b, i, j: (b, i, j)),
                  wsp(w_a), wsp(w_p), wsp(w_r)],
        out_specs=pl.BlockSpec((1, tm, tn), lambda b, i, j: (b, i, j)),
        scratch_shapes=[pltpu.VMEM((tm, W), BF16)],
        compiler_params=_cparams("parallel", "parallel", "arbitrary"),
        name="merge",
    )(attn, pool, y, g, bonus, ln_w.reshape(1, W), ln_b.reshape(1, W), gates, w_a, w_p, w_r)


def _out_kernel(m_ref, w_ref, x_ref, gpost_ref, gpre_ref, ml_ref, mc_ref, wr_ref,
                x_out, h_out, aff_out, *, tm, n_lat, n_e):
    is_lat, m_lat, m_ctx = _mod_rows(ml_ref, mc_ref, pl.program_id(1) * tm, tm, n_lat)
    mix = _bdot(m_ref[0], w_ref[...])
    normed = mix * lax.rsqrt(jnp.mean(mix * mix, axis=-1, keepdims=True) + NORM_EPS) * gpost_ref[...]
    x = x_ref[0] + jnp.where(is_lat, m_lat[2:3], m_ctx[2:3]) * normed
    x_out[0] = x
    h = _modnorm(x, gpre_ref[...], is_lat, m_lat, m_ctx, 3)
    h_out[0] = _pack_halves(h)
    h_hi = h.astype(BF16)
    part = _bdot(h_hi, wr_ref[...]) + _bdot(h - h_hi.astype(F32), wr_ref[...])
    logits = part[:, :V7X_LANES] + part[:, V7X_LANES:]
    logits = jnp.where(_iota(logits.shape, 1) < n_e, logits, -jnp.inf)
    e = jnp.exp(logits - jnp.max(logits, axis=-1, keepdims=True))
    aff_out[0] = e / jnp.sum(e, axis=-1, keepdims=True)


def _out_proj(merged, w_out, xs, g_post, g_pre, mod_l, w_router, n_lat):
    B, T, D = xs.shape
    E = w_router.shape[1]
    tm = _pick(T, (544, 272, 256, 128, 320, 64, 32, 16))
    wr = jnp.pad(w_router, ((0, 0), (0, V7X_LANES - E)))
    wr_hi = wr.astype(BF16)
    wr = jnp.concatenate([wr_hi, (wr - wr_hi.astype(F32)).astype(BF16)], axis=1)
    row = lambda w: pl.BlockSpec((1, tm, w), lambda b, i: (b, i, 0))
    vec = pl.BlockSpec((1, D), lambda b, i: (0, 0))
    return pl.pallas_call(
        functools.partial(_out_kernel, tm=tm, n_lat=n_lat, n_e=E),
        out_shape=(jax.ShapeDtypeStruct((B, T, D), F32), jax.ShapeDtypeStruct((B, T, D // 2), jnp.uint32),
                   jax.ShapeDtypeStruct((B, T, V7X_LANES), F32)),
        grid=(B, T // tm),
        in_specs=[row(D), pl.BlockSpec((D, D), lambda b, i: (0, 0)), row(D), vec, vec,
                  pl.BlockSpec((1, 6, D), lambda b, i: (b, 0, 0)),
                  pl.BlockSpec((1, 6, D), lambda b, i: (B, 0, 0)),
                  pl.BlockSpec((D, 2 * V7X_LANES), lambda b, i: (0, 0))],
        out_specs=(row(D), row(D // 2), row(V7X_LANES)),
        compiler_params=_cparams("parallel", "parallel"),
        name="out_proj",
    )(merged, w_out, xs, g_post.reshape(1, D), g_pre.reshape(1, D), mod_l, mod_l, wr)


def _excl_prefix(flags, blk):
    n = flags.shape[-1]
    upper = (_iota((blk, blk), 0) < _iota((blk, blk), 1)).astype(BF16)
    outs = []
    carry = jnp.zeros((flags.shape[0], 1), F32)
    for j in range(n // blk):
        seg = flags[:, j * blk:(j + 1) * blk]
        outs.append(_bdot(seg, upper) + carry)
        carry = carry + jnp.sum(seg, axis=-1, keepdims=True)
    return jnp.concatenate(outs, axis=-1) if len(outs) > 1 else outs[0]


def _topk_kernel(aff_ref, sel_ref, idx_ref, *, cap, blk):
    bits = lax.bitcast_convert_type(aff_ref[0], jnp.int32)
    E = bits.shape[0]

    def body(i, tau):
        cand = tau | jnp.left_shift(jnp.int32(1), 30 - i)
        cnt = jnp.sum((bits >= cand).astype(jnp.int32), axis=-1, keepdims=True)
        return jnp.where(cnt >= cap, cand, tau)

    tau = lax.fori_loop(0, 31, body, jnp.zeros((E, 1), jnp.int32))
    gt = bits > tau
    eq = bits == tau
    need = (cap - jnp.sum(gt.astype(jnp.int32), axis=-1, keepdims=True)).astype(F32)
    eq_rank = _excl_prefix(eq.astype(F32), blk)
    sel = gt | (eq & (eq_rank < need))
    pos = _excl_prefix(sel.astype(F32), blk)
    slots = jnp.where(sel, pos.astype(jnp.int32), -1)
    sel_ref[0] = slots
    n = slots.shape[1]
    tok = _iota((8, n), 1)
    digit = _iota((8, n), 0)
    digits = jnp.where(digit == 0, tok // TOKEN_RADIX, jnp.where(digit == 1, tok % TOKEN_RADIX, 0))
    for e in range(E):
        onehot = slots[e:e + 1, :] == _iota((cap, n), 0)
        d = _bdot(digits.astype(F32), onehot.astype(F32), NT)
        idx_ref[0, e:e + 1, :] = (d[0:1] * TOKEN_RADIX + d[1:2]).astype(jnp.int32)


def _topk_slots(aff_t, cap):
    B, E, n = aff_t.shape
    assert n <= TOKEN_RADIX * 256
    blk = _pick(n, (512, 256, 128))
    return pl.pallas_call(
        functools.partial(_topk_kernel, cap=cap, blk=blk),
        out_shape=(jax.ShapeDtypeStruct((B, E, n), jnp.int32),
                   jax.ShapeDtypeStruct((B, E, cap), jnp.int32)),
        grid=(B,),
        in_specs=[pl.BlockSpec((1, E, n), lambda b: (b, 0, 0))],
        out_specs=(pl.BlockSpec((1, E, n), lambda b: (b, 0, 0)),
                   pl.BlockSpec((1, E, cap), lambda b: (b, 0, 0))),
        compiler_params=_cparams("parallel"),
        name="expert_topk",
    )(aff_t)


def _gather_kernel(idx_ref, h_ref, xe_ref, *, cap, n_e):
    base = (pl.program_id(0) * n_e + pl.program_id(1)) * cap

    def body(s, carry):
        t = idx_ref[base + s]
        xe_ref[0, 0, pl.ds(s, 1), :] = h_ref[0, pl.ds(t, 1), :]
        return carry

    lax.fori_loop(0, cap, body, 0, unroll=8)


def _gather(idx, hp, row_block, n):
    B, E, cap = idx.shape
    dh = hp.shape[2]
    grid_spec = pltpu.PrefetchScalarGridSpec(
        num_scalar_prefetch=1, grid=(B, E),
        in_specs=[pl.BlockSpec((1, n, dh), lambda b, e, idx_ref: (b, row_block, 0))],
        out_specs=pl.BlockSpec((1, 1, cap, dh), lambda b, e, idx_ref: (b, e, 0, 0)))
    return pl.pallas_call(
        functools.partial(_gather_kernel, cap=cap, n_e=E),
        out_shape=jax.ShapeDtypeStruct((B, E, cap, dh), jnp.uint32),
        grid_spec=grid_spec,
        compiler_params=_cparams("parallel", "arbitrary"),
        name="expert_gather",
    )(idx.reshape(-1), hp)


def _pack_halves(h):
    bits = lax.bitcast_convert_type(h.astype(BF16).astype(F32), jnp.uint32)
    half = h.shape[1] // 2
    return (bits[:, :half] >> 16) | (bits[:, half:] & jnp.uint32(0xFFFF0000))


def _unpack_halves(p):
    lo = lax.bitcast_convert_type(p << 16, F32)
    hi = lax.bitcast_convert_type(p & jnp.uint32(0xFFFF0000), F32)
    return lo, hi


def _ffn_kernel(xe_ref, wg_ref, wu_ref, wd_ref, ye_ref):
    lo, hi = _unpack_halves(xe_ref[0, 0])
    half = lo.shape[1]
    gate = _bdot(lo, wg_ref[0, :half, :]) + _bdot(hi, wg_ref[0, half:, :])
    up = _bdot(lo, wu_ref[0, :half, :]) + _bdot(hi, wu_ref[0, half:, :])
    hid = gate * jax.nn.sigmoid(gate) * up
    ye_ref[0, 0] = _bdot(hid, wd_ref[0]).astype(ye_ref.dtype)


def _expert_ffn(xe, w_gate, w_up, w_down):
    B, E, cap, dh = xe.shape
    D = 2 * dh
    FF = w_gate.shape[2]
    return pl.pallas_call(
        _ffn_kernel,
        out_shape=jax.ShapeDtypeStruct((B, E, cap, D), BF16),
        grid=(E, B),
        in_specs=[pl.BlockSpec((1, 1, cap, dh), lambda e, b: (b, e, 0, 0)),
                  pl.BlockSpec((1, D, FF), lambda e, b: (e, 0, 0)),
                  pl.BlockSpec((1, D, FF), lambda e, b: (e, 0, 0)),
                  pl.BlockSpec((1, FF, D), lambda e, b: (e, 0, 0))],
        out_specs=pl.BlockSpec((1, 1, cap, D), lambda e, b: (b, e, 0, 0)),
        compiler_params=_cparams("parallel", "parallel"),
        name="expert_ffn",
    )(xe, w_gate, w_up, w_down)


def _scatter_kernel(selt_ref, aff_ref, ye_ref, o_ref, *, cap, n_e):
    sel_all = selt_ref[0]
    aff_all = aff_ref[0]
    slot = _iota((1, cap), 1)
    acc = None
    for e in range(n_e):
        onehot = sel_all[:, e:e + 1] == slot
        term = aff_all[:, e:e + 1] * _bdot(onehot, ye_ref[0, e])
        acc = term if acc is None else acc + term
    o_ref[0] = acc


def _scatter(sel, aff, row_block, ye):
    B, E, n = sel.shape
    cap, D = ye.shape[2:]
    td = _pick(D, (256, 128))
    return pl.pallas_call(
        functools.partial(_scatter_kernel, cap=cap, n_e=E),
        out_shape=jax.ShapeDtypeStruct((B, n, D), F32),
        grid=(B, D // td),
        in_specs=[pl.BlockSpec((1, n, E), lambda b, j: (b, 0, 0)),
                  pl.BlockSpec((1, n, aff.shape[2]), lambda b, j: (b, row_block, 0)),
                  pl.BlockSpec((1, E, cap, td), lambda b, j: (b, 0, 0, j))],
        out_specs=pl.BlockSpec((1, n, td), lambda b, j: (b, 0, j)),
        compiler_params=_cparams("parallel", "parallel"),
        name="expert_scatter",
    )(jnp.swapaxes(sel, 1, 2), aff, ye)


def _expert_choice(hb, aff, row_block, n, w_gate, w_up, w_down):
    E = w_gate.shape[0]
    cap = EC_FACTOR * n // E
    aff_t = jnp.swapaxes(aff[:, row_block * n:(row_block + 1) * n, :E], 1, 2)
    sel, idx = _topk_slots(aff_t, cap)
    xe = _gather(idx, hb, row_block, n)
    ye = _expert_ffn(xe, w_gate, w_up, w_down)
    return _scatter(sel, aff, row_block, ye)


def _rms(x, g):
    return x * lax.rsqrt(jnp.mean(x * x, axis=-1, keepdims=True) + NORM_EPS) * g


def _rope_tables(n_lat, n_ctx, head_dim):
    rows = n_lat // GRID_W
    row = jnp.repeat(jnp.arange(rows), GRID_W).astype(F32)
    col = (jnp.arange(rows * GRID_W) % GRID_W).astype(F32)
    half = head_dim // 2
    inv = ROPE_THETA ** (-jnp.arange(0, half, 2, dtype=F32) / half)
    ar, ac = row[:, None] * inv, col[:, None] * inv
    cos = jnp.concatenate([jnp.cos(ar), jnp.cos(ar), jnp.cos(ac), jnp.cos(ac)], axis=-1)
    sin = jnp.concatenate([-jnp.sin(ar), jnp.sin(ar), -jnp.sin(ac), jnp.sin(ac)], axis=-1)
    cos = jnp.concatenate([cos, jnp.ones((n_ctx, head_dim), F32)], axis=0)
    sin = jnp.concatenate([sin, jnp.zeros((n_ctx, head_dim), F32)], axis=0)
    return cos, sin


def _pad_cols(w, mult):
    return jnp.pad(w, ((0, 0), (0, (-w.shape[1]) % mult)))


def kernel(x, c, ctx, c_ctx, w_mod, b_mod, norm_pre, norm_post, w_in, q_norm, k_norm, w_attn_o, w_pool_group, pool_scale, w_pool_o, rwkv_mu, rwkv_w0, rwkv_w2, rwkv_a0, rwkv_a2, rwkv_g2, rwkv_k_k, rwkv_k_a, rwkv_r_k, rwkv_ln_w, rwkv_ln_b, w_rwkv_o, w_out, w_router, w_exp_gate, w_exp_up, w_exp_down):
    B, n_lat, D = x.shape
    n_ctx = ctx.shape[1]
    T = n_lat + n_ctx
    depth = w_mod.shape[0]
    hd = q_norm.shape[1]
    attn_w = w_attn_o.shape[1]
    pool_w = w_pool_o.shape[1]
    rwkv_w = w_rwkv_o.shape[1]
    n_shift = rwkv_mu.shape[1]
    n_in = w_in.shape[2]
    kv_w = (n_in - attn_w - pool_w - n_shift - N_BRANCH * D) // 2
    head = rwkv_r_k.shape[2]
    col_k = attn_w
    col_v = col_k + kv_w
    col_pool = col_v + kv_w
    col_r = col_pool + pool_w
    col_gate = col_r + n_shift
    hq, hkv = attn_w // hd, kv_w // hd
    assert n_lat % (hq // hkv * hd) == 0 and pool_w % (hq // hkv * hd) == 0 and n_ctx % RWKV_CHUNK == 0

    cos, sin = _rope_tables(n_lat, n_ctx, hd)
    s_all = jnp.concatenate([jax.nn.silu(c), jax.nn.silu(c_ctx)[None]], axis=0)
    s_all = jnp.pad(s_all, ((0, (-s_all.shape[0]) % 8), (0, 0)))
    mod = _modulation(s_all, w_mod, b_mod)[:, :B + 1].reshape(depth, B + 1, 6, D)
    xs = jnp.concatenate([x, ctx], axis=1)
    (h,) = _stream_update(xs, None, None, None, mod[0], norm_pre[0, 0], mod[0], n_lat, T)

    for l in range(depth):
        keep_ctx = l < depth - 1
        wl = w_in[l]
        w_a = jnp.concatenate([wl[:, col_pool:col_r], wl[:, :col_pool]], axis=1).astype(BF16)
        w_b = _pad_cols(wl[:, col_r:col_gate], 3 * V7X_LANES).astype(BF16)
        tn_m = _merge_tile(D)
        w_c = (wl[:, col_gate:].reshape(D, N_BRANCH, D // tn_m, tn_m).swapaxes(1, 2)
               .reshape(D, N_BRANCH * D).astype(BF16))
        p_a = _in_proj(h, w_a)
        p_b = _in_proj(h, w_b)
        p_c = _in_proj(h, w_c)

        attn = _attention(p_a, pool_w, pool_w + attn_w, pool_w + attn_w + kv_w, cos, sin,
                          q_norm[l], k_norm[l], hd, hq, hkv, n_lat)
        pool = _pool(p_a, 0, w_pool_group[l].astype(BF16), pool_scale[l], n_lat)
        r, v, kk, lw, cum, kd, bd, g, bonus = _rwkv_prep(
            p_b, n_lat, rwkv_w, head, rwkv_mu[l], rwkv_w0[l], rwkv_w2[l], rwkv_a0[l], rwkv_a2[l],
            rwkv_g2[l], rwkv_k_k[l], rwkv_k_a[l], rwkv_r_k[l])
        y = _rwkv_scan(r, v, kk, lw, cum, kd, bd, head, n_lat)
        merged = _merge(attn, pool, y, g, bonus, rwkv_ln_w[l], rwkv_ln_b[l], p_c,
                        w_attn_o[l].astype(BF16), w_pool_o[l].astype(BF16), w_rwkv_o[l].astype(BF16), head)
        xs, hb, aff = _out_proj(merged, w_out[l].astype(BF16), xs, norm_post[l, 0], norm_pre[l, 1],
                                mod[l], w_router[l], n_lat)

        wg, wu, wdn = w_exp_gate[l].astype(BF16), w_exp_up[l].astype(BF16), w_exp_down[l].astype(BF16)
        f_lat = _expert_choice(hb, aff, 0, n_lat, wg, wu, wdn)
        if keep_ctx:
            f_ctx = _expert_choice(hb, aff, n_lat // n_ctx, n_ctx, wg, wu, wdn)
            xs, h = _stream_update(xs, f_lat, f_ctx, norm_post[l, 1], mod[l], norm_pre[l + 1, 0],
                                   mod[l + 1], n_lat, T)
        else:
            (xs,) = _stream_update(xs, f_lat, None, norm_post[l, 1], mod[l], None, None, n_lat, n_lat)
    return xs
```

```python
import functools

import jax
import jax.numpy as jnp
from jax import lax
from jax.experimental import pallas as pl
from jax.experimental.pallas import tpu as pltpu

F32 = jnp.float32
BF16 = jnp.bfloat16
HIGHEST = lax.Precision.HIGHEST

GRID_W = 64
NORM_EPS = 1e-6
ROPE_THETA = 10000.0
POOL_WINDOWS = (2, 4, 8, 16)
GN_EPS = 64e-5
EC_FACTOR = 2
N_DIR = 2
N_BRANCH = 3
RWKV_CHUNK = 64
HALO = 16

LOG2_E = 1.4426950408889634
ATTN_SHIFT_LIMIT = 60.0
FFN_SLABS = 2
TOKEN_RADIX = 64

V7X_LANES = 128
VMEM_LIMIT = 52 * 1024 * 1024

NT = (((1,), (1,)), ((), ()))
TN = (((0,), (0,)), ((), ()))


def _pick(n, cands):
    for c in cands:
        if c <= n and n % c == 0:
            return c
    return n


def _cparams(*sem):
    return pltpu.CompilerParams(dimension_semantics=sem, vmem_limit_bytes=VMEM_LIMIT)


def _bdot(x, y, dn=None):
    x = x.astype(BF16)
    y = y.astype(BF16)
    if dn is None:
        return jnp.dot(x, y, preferred_element_type=F32)
    return lax.dot_general(x, y, dn, preferred_element_type=F32)


def _split_dot(x, y):
    hi = x.astype(BF16)
    lo = x - hi.astype(F32)
    return _bdot(hi, y) + _bdot(lo, y)


def _iota(shape, dim):
    return lax.broadcasted_iota(jnp.int32, shape, dim)


def _row_tile(T, min_parts=8):
    for parts in range(min_parts, T // 16 + 1):
        if T % parts == 0 and (T // parts) % 16 == 0:
            return T // parts
    return T


def _mod_kernel(s_ref, w_ref, b_ref, o_ref):
    o_ref[0] = jnp.dot(s_ref[...], w_ref[0], precision=HIGHEST,
                       preferred_element_type=F32) + b_ref[0]


def _modulation(s, w_mod, b_mod):
    L, D, N = w_mod.shape
    R = s.shape[0]
    tn = _pick(N, (1024, 512, 256, 128))
    return pl.pallas_call(
        _mod_kernel,
        out_shape=jax.ShapeDtypeStruct((L, R, N), F32),
        grid=(L, N // tn),
        in_specs=[pl.BlockSpec((R, D), lambda l, j: (0, 0)),
                  pl.BlockSpec((1, D, tn), lambda l, j: (l, 0, j)),
                  pl.BlockSpec((1, 1, tn), lambda l, j: (l, 0, j))],
        out_specs=pl.BlockSpec((1, R, tn), lambda l, j: (l, 0, j)),
        compiler_params=_cparams("parallel", "parallel"),
        name="adaln_mod",
    )(s, w_mod, b_mod.reshape(L, 1, N))


def _mod_rows(m_lat_ref, m_ctx_ref, row0, tm, n_lat):
    is_lat = (row0 + _iota((tm, 1), 0)) < n_lat
    return is_lat, m_lat_ref[0], m_ctx_ref[0]


def _modnorm(x, g, is_lat, m_lat, m_ctx, i_shift):
    shift = jnp.where(is_lat, m_lat[i_shift:i_shift + 1], m_ctx[i_shift:i_shift + 1])
    scale = jnp.where(is_lat, m_lat[i_shift + 1:i_shift + 2], m_ctx[i_shift + 1:i_shift + 2])
    y = x * lax.rsqrt(jnp.mean(x * x, axis=-1, keepdims=True) + NORM_EPS) * g
    return y * (1.0 + scale) + shift


def _small_row_tile(T, n_ctx):
    return _pick(n_ctx, (256, 128, 64, 32, 16))


def _stream_kernel(*refs, tm, n_lat, has_f, has_ctx_f, emit_h):
    refs = list(refs)
    x_ref = refs.pop(0)
    fl_ref = refs.pop(0) if has_f else None
    fc_ref = refs.pop(0) if has_ctx_f else None
    gpost_ref = refs.pop(0) if has_f else None
    ml_ref, mc_ref = refs.pop(0), refs.pop(0)
    if emit_h:
        gpre_ref, mln_ref, mcn_ref = refs.pop(0), refs.pop(0), refs.pop(0)
    x_out = refs.pop(0) if has_f else None
    h_out = refs.pop(0) if emit_h else None

    is_lat = (pl.program_id(1) * tm + _iota((tm, 1), 0)) < n_lat
    x = x_ref[0]
    if has_f:
        f = fl_ref[0]
        if has_ctx_f:
            f = jnp.where(is_lat, f, fc_ref[0])
        normed = f * lax.rsqrt(jnp.mean(f * f, axis=-1, keepdims=True) + NORM_EPS) * gpost_ref[...]
        x = x + jnp.where(is_lat, ml_ref[0][5:6], mc_ref[0][5:6]) * normed
        x_out[0] = x
    if emit_h:
        h_out[0] = _modnorm(x, gpre_ref[...], is_lat, mln_ref[0], mcn_ref[0], 0).astype(h_out.dtype)


def _stream_update(xs, f_lat, f_ctx, g_post, mod_l, g_pre_next, mod_next, n_lat, n_rows):
    B, T, D = xs.shape
    n_ctx = T - n_lat
    tm = _small_row_tile(T, n_ctx)
    has_f, has_ctx_f, emit_h = f_lat is not None, f_ctx is not None, g_pre_next is not None
    n_lt = n_lat // tm
    row = pl.BlockSpec((1, tm, D), lambda b, i: (b, i, 0))
    vec = pl.BlockSpec((1, D), lambda b, i: (0, 0))
    m_l = pl.BlockSpec((1, 6, D), lambda b, i: (b, 0, 0))
    m_c = pl.BlockSpec((1, 6, D), lambda b, i: (B, 0, 0))
    args, specs = [xs], [row]
    if has_f:
        args.append(f_lat)
        specs.append(pl.BlockSpec((1, tm, D), lambda b, i: (b, jnp.minimum(i, n_lt - 1), 0)))
    if has_ctx_f:
        args.append(f_ctx)
        specs.append(pl.BlockSpec((1, tm, D), lambda b, i: (b, jnp.maximum(i - n_lt, 0), 0)))
    if has_f:
        args.append(g_post.reshape(1, D))
        specs.append(vec)
    args += [mod_l, mod_l]
    specs += [m_l, m_c]
    if emit_h:
        args += [g_pre_next.reshape(1, D), mod_next, mod_next]
        specs += [vec, m_l, m_c]
    out_shape, out_specs = [], []
    if has_f:
        out_shape.append(jax.ShapeDtypeStruct((B, n_rows, D), F32))
        out_specs.append(row)
    if emit_h:
        out_shape.append(jax.ShapeDtypeStruct((B, n_rows, D), BF16))
        out_specs.append(row)
    kern = functools.partial(_stream_kernel, tm=tm, n_lat=n_lat, has_f=has_f, has_ctx_f=has_ctx_f,
                             emit_h=emit_h)
    return pl.pallas_call(
        kern, out_shape=tuple(out_shape), grid=(B, n_rows // tm), in_specs=specs,
        out_specs=tuple(out_specs), compiler_params=_cparams("parallel", "parallel"),
        name="stream_update",
    )(*args)


def _in_kernel(h_ref, w_ref, o_ref):
    o_ref[0] = _bdot(h_ref[0], w_ref[...]).astype(o_ref.dtype)


def _in_proj(h, w):
    B, T, D = h.shape
    N = w.shape[1]
    tm = _row_tile(T, 4)
    tn = _pick(N, (1280, 1024, 896, 768, 640, 512, 384, 256, 128))
    return pl.pallas_call(
        _in_kernel,
        out_shape=jax.ShapeDtypeStruct((B, T, N), BF16),
        grid=(B, T // tm, N // tn),
        in_specs=[pl.BlockSpec((1, tm, D), lambda b, i, j: (b, i, 0)),
                  pl.BlockSpec((D, tn), lambda b, i, j: (0, j))],
        out_specs=pl.BlockSpec((1, tm, tn), lambda b, i, j: (b, i, j)),
        compiler_params=_cparams("parallel", "parallel", "arbitrary"),
        name="in_proj",
    )(h, w)


def _rot(x, cos, sin):
    qd = x.shape[1] // 4
    first = (_iota((1, x.shape[1]), 1) // qd) % 2 == 0
    swapped = jnp.where(first, pltpu.roll(x, x.shape[1] - qd, 1), pltpu.roll(x, qd, 1))
    return x * cos + swapped * sin


def _head_norm(x, g):
    return x * lax.rsqrt(jnp.mean(x * x, axis=-1, keepdims=True) + NORM_EPS) * g


def _attn_kernel(bound_ref, q_ref, k_ref, v_ref, cq_ref, sq_ref, ck_ref, sk_ref, qn_ref, kn_ref, o_ref,
                 kp_ref, vp_ref, *, hd, group, tq, n_lat, scale):
    i = pl.program_id(2)
    unit = jnp.where(_iota((1, hd), 1) == 0, 1.0, 0.0)

    @pl.when(i == 0)
    def _():
        k = _rot(_head_norm(k_ref[0].astype(F32), kn_ref[...]), ck_ref[...], sk_ref[...])
        kp_ref[...] = jnp.concatenate([k, jnp.broadcast_to(unit, k.shape)], axis=1).astype(kp_ref.dtype)
        ones = jnp.ones(v_ref.shape[1:], vp_ref.dtype)
        vp_ref[...] = jnp.concatenate([v_ref[0].astype(vp_ref.dtype), ones], axis=1)

    bound = bound_ref[0]

    def attend(key0, shift_in_matmul):
        for g in range(group):
            q = _head_norm(q_ref[0, :, g * hd:(g + 1) * hd].astype(F32), qn_ref[...])
            q = _rot(q, cq_ref[...], sq_ref[...]) * scale
            if shift_in_matmul:
                q_aug = jnp.concatenate([q, jnp.broadcast_to(unit * (-bound), q.shape)], axis=1)
                p = jnp.exp2(_bdot(q_aug, kp_ref[key0:, :], NT))
            else:
                s = _bdot(q, kp_ref[key0:, :hd], NT)
                p = jnp.exp2(s - jnp.max(s, axis=-1, keepdims=True))
            ov = _bdot(p, vp_ref[key0:, :])
            o_ref[0, :, g * hd:(g + 1) * hd] = (ov[:, :hd] / ov[:, hd:hd + 1]).astype(o_ref.dtype)

    is_lat = i * tq < n_lat
    fast = bound < ATTN_SHIFT_LIMIT
    for lat_tile, key0 in ((True, 0), (False, n_lat)):
        for use_fast in (True, False):
            cond = jnp.logical_and(is_lat if lat_tile else jnp.logical_not(is_lat),
                                   fast if use_fast else jnp.logical_not(fast))
            pl.when(cond)(functools.partial(attend, key0, use_fast))


def _attention(p_qkv, q_col, k_col, v_col, cos, sin, q_norm, k_norm, hd, hq, hkv, n_lat):
    B, T, _ = p_qkv.shape
    group = hq // hkv
    gw = group * hd
    tq = _pick(T - n_lat, (256, 128, 64, 32, 16))
    kern = functools.partial(_attn_kernel, hd=hd, group=group, tq=tq, n_lat=n_lat,
                             scale=hd ** -0.5 * LOG2_E)
    tab_q = pl.BlockSpec((tq, hd), lambda b, h, i, s: (i, 0))
    tab_k = pl.BlockSpec((T, hd), lambda b, h, i, s: (0, 0))
    vec = pl.BlockSpec((1, hd), lambda b, h, i, s: (0, 0))
    bound = (1.02 * hd * hd ** -0.5 * LOG2_E) * jnp.max(jnp.abs(q_norm)) * jnp.max(jnp.abs(k_norm))
    grid_spec = pltpu.PrefetchScalarGridSpec(
        num_scalar_prefetch=1, grid=(B, hkv, T // tq),
        in_specs=[pl.BlockSpec((1, tq, gw), lambda b, h, i, s: (b, i, q_col // gw + h)),
                  pl.BlockSpec((1, T, hd), lambda b, h, i, s: (b, 0, k_col // hd + h)),
                  pl.BlockSpec((1, T, hd), lambda b, h, i, s: (b, 0, v_col // hd + h)),
                  tab_q, tab_q, tab_k, tab_k, vec, vec],
        out_specs=pl.BlockSpec((1, tq, gw), lambda b, h, i, s: (b, i, h)),
        scratch_shapes=[pltpu.VMEM((T, 2 * hd), BF16), pltpu.VMEM((T, 2 * hd), BF16)])
    return pl.pallas_call(
        kern,
        out_shape=jax.ShapeDtypeStruct((B, T, hq * hd), BF16),
        grid_spec=grid_spec,
        compiler_params=_cparams("parallel", "parallel", "arbitrary"),
        name="attention",
    )(bound.reshape(1).astype(F32), p_qkv, p_qkv, p_qkv, cos, sin, cos, sin,
      q_norm.reshape(1, hd), k_norm.reshape(1, hd))


def _segment_of_tile(i, tt, n_lat_tiles, n_tiles):
    is_lat = i < n_lat_tiles
    ti = jnp.where(is_lat, i, i - n_lat_tiles)
    seg_tiles = jnp.where(is_lat, n_lat_tiles, n_tiles - n_lat_tiles)
    return ti, seg_tiles


def _halo_specs(tt, width, T, col_block):
    hb = tt // HALO
    last = T // HALO - 1
    prev = pl.BlockSpec((1, HALO, width), lambda b, i: (b, jnp.maximum(i * hb - 1, 0), col_block))
    cur = pl.BlockSpec((1, tt, width), lambda b, i: (b, i, col_block))
    nxt = pl.BlockSpec((1, HALO, width), lambda b, i: (b, jnp.minimum((i + 1) * hb, last), col_block))
    return prev, cur, nxt


def _pool_kernel(prev_ref, cur_ref, next_ref, wg_ref, sc_ref, o_ref, *,
                 tt, n_lat_tiles, n_tiles, gw, windows):
    ti, seg_tiles = _segment_of_tile(pl.program_id(1), tt, n_lat_tiles, n_tiles)
    has_prev = ti > 0
    has_next = ti < seg_tiles - 1
    t_seg = seg_tiles * tt
    tpos = ti * tt + _iota((tt, 1), 0)

    cur = cur_ref[0]
    prev = prev_ref[0]
    nxt = next_ref[0]
    d_cur = _iota((tt, tt), 1) - _iota((tt, tt), 0)
    d_halo = _iota((tt, HALO), 1) - _iota((tt, HALO), 0)
    d_prev = d_halo - HALO
    d_next = d_halo + tt
    for g, win in enumerate(windows):
        lo_off = -(win // 2)
        hi_off = win - win // 2 - 1
        sl = slice(g * gw, (g + 1) * gw)
        band_c = ((d_cur >= lo_off) & (d_cur <= hi_off)).astype(BF16)
        band_p = ((d_prev >= lo_off) & (d_prev <= hi_off) & has_prev).astype(BF16)
        band_n = ((d_next >= lo_off) & (d_next <= hi_off) & has_next).astype(BF16)
        ug = cur[:, sl]
        tot = _bdot(band_c, ug) + _bdot(band_p, prev[:, sl]) + _bdot(band_n, nxt[:, sl])
        lo = jnp.maximum(tpos + lo_off, 0)
        hi = jnp.minimum(tpos + hi_off + 1, t_seg)
        pooled = tot / (hi - lo).astype(F32) - ug.astype(F32)
        y = _bdot(pooled, wg_ref[g]) * sc_ref[:, sl]
        o_ref[0, :, sl] = y.astype(o_ref.dtype)


def _pool(p_arr, col_block, w_group, scale, n_lat):
    B, T, _ = p_arr.shape
    G, gw, _ = w_group.shape
    W = G * gw
    tt = _pick(T - n_lat, (256, 128, 64, 32, 16))
    assert n_lat % tt == 0 and tt % HALO == 0 and max(POOL_WINDOWS) <= HALO
    n_tiles = T // tt
    kern = functools.partial(_pool_kernel, tt=tt, n_lat_tiles=n_lat // tt, n_tiles=n_tiles,
                             gw=gw, windows=POOL_WINDOWS)
    prev, cur, nxt = _halo_specs(tt, W, T, col_block)
    return pl.pallas_call(
        kern,
        out_shape=jax.ShapeDtypeStruct((B, T, W), BF16),
        grid=(B, n_tiles),
        in_specs=[prev, cur, nxt,
                  pl.BlockSpec((G, gw, gw), lambda b, i: (0, 0, 0)),
                  pl.BlockSpec((1, W), lambda b, i: (0, 0))],
        out_specs=pl.BlockSpec((1, tt, W), lambda b, i: (b, i, 0)),
        compiler_params=_cparams("parallel", "parallel"),
        name="pool",
    )(p_arr, p_arr, p_arr, w_group, scale.reshape(1, W))


def _head_sum(x, head):
    lanes = x.shape[1]
    blk = min(lanes, V7X_LANES)
    same = (_iota((blk, blk), 0) // head == _iota((blk, blk), 1) // head).astype(BF16)
    parts = [_split_dot(x[:, j:j + blk], same) for j in range(0, lanes, blk)]
    return jnp.concatenate(parts, axis=1) if len(parts) > 1 else parts[0]


def _rwkv_prep_kernel(prev_ref, cur_ref, next_ref, mu_ref, w0_ref, w2_ref, a0_ref, a2_ref, g2_ref,
                      kk_w_ref, ka_ref, rk_ref,
                      r_ref, v_ref, kk_ref, lw_ref, cum_ref, kd_ref, bd_ref, g_ref, bonus_ref, *,
                      tt, n_lat_tiles, n_tiles, W, head, lora, chunk):
    ti, seg_tiles = _segment_of_tile(pl.program_id(1), tt, n_lat_tiles, n_tiles)
    u = cur_ref[0].astype(F32)
    row = _iota((tt, 1), 0)
    before = jnp.where(ti > 0, prev_ref[0, HALO - 1:HALO, :].astype(F32), 0.0)
    after = jnp.where(ti < seg_tiles - 1, next_ref[0, 0:1, :].astype(F32), 0.0)
    u_prev = jnp.where(row == 0, before, pltpu.roll(u, 1, 0))
    u_next = jnp.where(row == tt - 1, after, pltpu.roll(u, tt - 1, 0))
    u = u + (0.5 * (u_prev + u_next) - u) * mu_ref[...]

    r, k, v = u[:, :W], u[:, W:2 * W], u[:, 2 * W:3 * W]
    o1 = 3 * W
    o2 = o1 + lora
    o3 = o2 + lora
    w_lin = w0_ref[...] + _bdot(jnp.tanh(u[:, o1:o2]), w2_ref[...])
    w_log = -(jnp.maximum(-w_lin, 0.0) + jnp.log(1.0 + jnp.exp(-jnp.abs(w_lin)))) - 0.5
    lw = -jnp.exp(w_log)
    lw_ref[0] = lw
    ri, ci = _iota((tt, tt), 0), _iota((tt, tt), 1)
    same_chunk = ri // chunk == ci // chunk
    p1 = lw.astype(BF16)
    p2 = (lw - p1.astype(F32)).astype(BF16)
    p3 = (lw - p1.astype(F32) - p2.astype(F32)).astype(BF16)
    for z, tri in enumerate((same_chunk & (ci <= ri), same_chunk & (ci >= ri))):
        sl = slice(z * W, (z + 1) * W)
        cum_ref[0, :, sl] = _bdot(tri, p1[:, sl]) + _bdot(tri, p2[:, sl]) + _bdot(tri, p3[:, sl])
    a = jax.nn.sigmoid(a0_ref[...] + _bdot(u[:, o2:o3], a2_ref[...]))
    g_ref[0] = _bdot(jax.nn.sigmoid(u[:, o3:]), g2_ref[...]).astype(g_ref.dtype)

    kk = k * kk_w_ref[...]
    kk = kk * lax.rsqrt(jnp.maximum(_head_sum(kk * kk, head), 1e-24))
    k_sum = 0.0
    for z in range(N_DIR):
        a_z = a[:, z * W:(z + 1) * W]
        k_z = k * (1.0 + (a_z - 1.0) * ka_ref[...])
        kd_ref[0, :, z * W:(z + 1) * W] = k_z.astype(kd_ref.dtype)
        bd_ref[0, :, z * W:(z + 1) * W] = (kk * a_z).astype(bd_ref.dtype)
        k_sum = k_sum + k_z
    r_ref[0] = r.astype(r_ref.dtype)
    v_ref[0] = v.astype(v_ref.dtype)
    kk_ref[0] = kk.astype(kk_ref.dtype)
    bonus_ref[0] = (_head_sum(r * k_sum * rk_ref[...], head) * v).astype(bonus_ref.dtype)


def _rwkv_prep(p_rwkv, n_lat, W, head, mu, w0, w2, a0, a2, g2, k_k, k_a, r_k):
    B, T, NS = p_rwkv.shape
    lora = N_DIR * w2.shape[1]
    n_gate = NS - 3 * W - 2 * lora
    tt = _pick(T - n_lat, (256, 128, 64, 32, 16))
    n_tiles = T // tt
    def cat(m):
        z = jnp.zeros_like(m[0])
        return jnp.concatenate([jnp.concatenate([m[0], z], axis=1),
                                jnp.concatenate([z, m[1]], axis=1)], axis=0).astype(BF16)
    g2p = jnp.pad(g2, ((0, n_gate - g2.shape[0]), (0, 0))).astype(BF16)
    mup = jnp.pad(mu, (0, NS - mu.shape[0])).reshape(1, NS)
    kern = functools.partial(_rwkv_prep_kernel, tt=tt, n_lat_tiles=n_lat // tt, n_tiles=n_tiles,
                             W=W, head=head, lora=lora, chunk=RWKV_CHUNK)
    assert tt % RWKV_CHUNK == 0
    prev, cur, nxt = _halo_specs(tt, NS, T, 0)
    full = lambda shp: pl.BlockSpec(shp, lambda b, i: (0,) * len(shp))
    tile = lambda w: pl.BlockSpec((1, tt, w), lambda b, i: (b, i, 0))
    sd = lambda w, dt: jax.ShapeDtypeStruct((B, T, w), dt)
    return pl.pallas_call(
        kern,
        out_shape=(sd(W, BF16), sd(W, BF16), sd(W, BF16), sd(2 * W, F32), sd(2 * W, F32), sd(2 * W, BF16),
                   sd(2 * W, BF16), sd(W, BF16), sd(W, BF16)),
        grid=(B, n_tiles),
        in_specs=[prev, cur, nxt, full((1, NS)), full((1, 2 * W)), full((lora, 2 * W)),
                  full((1, 2 * W)), full((lora, 2 * W)), full((n_gate, W)),
                  full((1, W)), full((1, W)), full((1, W))],
        out_specs=(tile(W), tile(W), tile(W), tile(2 * W), tile(2 * W), tile(2 * W), tile(2 * W), tile(W),
                   tile(W)),
        compiler_params=_cparams("parallel", "parallel"),
        name="rwkv_prep",
    )(p_rwkv, p_rwkv, p_rwkv, mup, w0.reshape(1, 2 * W), cat(w2), a0.reshape(1, 2 * W), cat(a2), g2p,
      k_k.reshape(1, W), k_a.reshape(1, W), r_k.reshape(1, W))


def _rwkv_kernel(r_ref, v_ref, kk_ref, lw_ref, cum_ref, k_ref, b_ref, y_ref, g_ref, *, C, N, n_seq):
    @pl.when(pl.program_id(1) == 0)
    def _():
        g_ref[...] = jnp.zeros_like(g_ref)

    W = r_ref.shape[2]
    PW = 2 * N
    backward = pl.program_id(0) >= n_seq
    sgn = jnp.where(backward, -1, 1)
    order = (_iota((C, C), 0) - _iota((C, C), 1)) * sgn
    lw = lw_ref[0]
    cum = cum_ref[0]
    pc = jnp.where(backward, cum[0:1, :], cum[C - 1:C, :])
    k = k_ref[0].astype(F32)
    b = b_ref[0].astype(F32)
    p_inv = jnp.exp(-cum)
    p_hat = jnp.exp(pc - cum)
    r_t = r_ref[0].astype(F32) * jnp.exp(cum)
    k_t = k * p_inv
    b_t = b * p_inv
    a_t = -kk_ref[0].astype(F32) * jnp.exp(cum - lw)
    b_h = b * p_hat
    k_h = k * p_hat
    p_c = jnp.exp(pc)
    v = v_ref[0].astype(F32)

    order2 = jnp.concatenate([order, order], axis=1)
    strict2 = order2 > 0
    incl2 = order2 >= 0
    eye = (order == 0).astype(F32)
    first = _iota((1, PW), 1) < N
    lane_lo = _iota((1, 2 * C), 1) < C
    same_head = (_iota((PW, PW), 0) // N) == (_iota((PW, PW), 1) // N)

    n_pairs = W // PW
    heads = [(p, sub) for p in range(n_pairs) for sub in range(2)]
    psl = [slice(p * PW, (p + 1) * PW) for p in range(n_pairs)]
    masks = (first, jnp.logical_not(first))
    v_m, upper, lower = [], [], []
    for p in range(n_pairs):
        sl = psl[p]
        rhs_bk = jnp.concatenate([b_t[:, sl], k_t[:, sl]], axis=0)
        lhs = jnp.concatenate([jnp.where(m, x[:, sl], 0.0) for m in masks for x in (a_t, r_t)], axis=0)
        amat = _bdot(lhs, rhs_bk, NT)
        for sub in range(2):
            upper.append(jnp.where(strict2, amat[2 * sub * C:(2 * sub + 1) * C], 0.0))
            lower.append(jnp.where(incl2, amat[(2 * sub + 1) * C:(2 * sub + 2) * C], 0.0))
            v_m.append(jnp.where(masks[sub], v[:, sl], 0.0))
    zero = jnp.zeros((C, PW), F32)
    pw, x = [], []
    for p in range(n_pairs):
        up0, up1 = upper[2 * p], upper[2 * p + 1]
        a_ak = jnp.where(first, pltpu.roll(up0, C, 1), up1)
        w = _bdot(a_ak, jnp.concatenate([v_m[2 * p], v_m[2 * p + 1]], axis=0))
        w = pltpu.roll(w, N, 1)
        a_p = a_t[:, psl[p]]
        x += [jnp.where(first, a_p, w), jnp.where(first, w, a_p)]
        pw.append(jnp.where(first, up0, pltpu.roll(up1, C, 1)))
    span = 1
    while span < C:
        span *= 2
        for p in range(n_pairs):
            x0, x1 = x[2 * p], x[2 * p + 1]
            rhs = [jnp.concatenate([x0, zero], axis=1), jnp.concatenate([zero, x1], axis=1)]
            if span < C:
                rhs = [jnp.concatenate([jnp.where(m, pw[p], 0.0), xr], axis=1) for m, xr in zip(masks, rhs)]
            res = _bdot(pw[p], jnp.concatenate(rhs, axis=0))
            if span < C:
                pw[p] = res[:, :PW]
                res = res[:, PW:]
            x[2 * p], x[2 * p + 1] = x0 + res[:, :PW], x1 + res[:, PW:]

    g0 = [g_ref[p] for p in range(n_pairs)]
    uv, rg = [], []
    for p in range(n_pairs):
        x0, x1 = x[2 * p], x[2 * p + 1]
        ahat = [jnp.where(masks[0], x0, 0.0), jnp.where(masks[1], x1, 0.0)]
        vhat = [jnp.where(masks[0], pltpu.roll(x0, N, 1), 0.0), jnp.where(masks[1], pltpu.roll(x1, N, 1), 0.0)]
        res = _bdot(jnp.concatenate([r_t[:, psl[p]]] + ahat, axis=0), g0[p])
        rg.append(res[:C])
        u0 = res[C:2 * C] + vhat[0]
        u1 = res[2 * C:] + vhat[1]
        uv.append(jnp.concatenate([u0, v_m[2 * p], u1, v_m[2 * p + 1]], axis=0).astype(BF16))
    for p in range(n_pairs):
        y_ref[0, 0, :, psl[p]] = rg[p] + _bdot(jnp.concatenate([lower[2 * p], lower[2 * p + 1]], axis=1),
                                               uv[p])
    for p in range(n_pairs):
        sl = psl[p]
        bk_h = jnp.concatenate([b_h[:, sl], k_h[:, sl]], axis=0)
        upd = _bdot(jnp.concatenate([bk_h, bk_h], axis=0), uv[p], TN)
        decay = jnp.transpose(jnp.broadcast_to(p_c[:, sl], (PW, PW)))
        g_ref[p] = jnp.where(same_head, decay * g0[p] + upd, 0.0)


def _rwkv_scan(r, v, kk, lw, cum, kd, bd, head, n_lat):
    B, T, W = r.shape
    C = RWKV_CHUNK
    assert C == head
    nc = T // C
    nc_lat = n_lat // C

    def chunk(s, c):
        fwd = jnp.where(c < nc - nc_lat, nc_lat + c, c - (nc - nc_lat))
        bwd = jnp.where(c < nc - nc_lat, nc - 1 - c, nc - 1 - c)
        return jnp.where(s >= B, bwd, fwd)

    shared = pl.BlockSpec((1, C, W), lambda s, c: (s % B, chunk(s, c), 0))
    per_dir = pl.BlockSpec((1, C, W), lambda s, c: (s % B, chunk(s, c), s // B))
    kern = functools.partial(_rwkv_kernel, C=C, N=head, n_seq=B)
    return pl.pallas_call(
        kern,
        out_shape=jax.ShapeDtypeStruct((N_DIR, B, T, W), F32),
        grid=(N_DIR * B, nc),
        in_specs=[shared, shared, shared, per_dir, per_dir, per_dir, per_dir],
        out_specs=pl.BlockSpec((1, 1, C, W), lambda s, c: (s // B, s % B, chunk(s, c), 0)),
        scratch_shapes=[pltpu.VMEM((W // (2 * head), 2 * head, 2 * head), F32)],
        compiler_params=_cparams("parallel", "arbitrary"),
        name="rwkv7_chunk",
    )(r, v, kk, lw, cum, kd, bd)


def _sigmoid(x):
    return 0.5 * jnp.tanh(0.5 * x) + 0.5


def _merge_kernel(attn_ref, pool_ref, y_ref, g_ref, bonus_ref, lnw_ref, lnb_ref, gate_ref,
                  wa_ref, wp_ref, wr_ref, o_ref, *, head, D):
    y = y_ref[0, 0] + y_ref[1, 0]
    inv_n = 1.0 / head
    dev = y - _head_sum(y, head) * inv_n
    var = _head_sum(dev * dev, head) * inv_n
    yn = dev * lax.rsqrt(var + GN_EPS) * lnw_ref[...] + lnb_ref[...]
    rw = (yn + bonus_ref[0].astype(F32)) * g_ref[0].astype(F32)

    def gate(z):
        return _sigmoid(gate_ref[0, :, z * D:(z + 1) * D].astype(F32))

    out = gate(0) * _bdot(attn_ref[0], wa_ref[...])
    out = out + gate(1) * _bdot(pool_ref[0], wp_ref[...])
    out = out + gate(2) * _bdot(rw, wr_ref[...])
    o_ref[0] = out.astype(o_ref.dtype)


def _merge(attn, pool, y, g, bonus, ln_w, ln_b, gates, w_a, w_p, w_r, head):
    B, T, _ = attn.shape
    W = g.shape[2]
    D = w_a.shape[1]
    tm = _row_tile(T, 16)
    row = lambda w: pl.BlockSpec((1, tm, w), lambda b, i: (b, i, 0))
    vec = pl.BlockSpec((1, W), lambda b, i: (0, 0))
    wsp = lambda w: pl.BlockSpec(w.shape, lambda b, i: (0, 0), pipeline_mode=pl.Buffered(1))
    return pl.pallas_call(
        functools.partial(_merge_kernel, head=head, D=D),
        out_shape=jax.ShapeDtypeStruct((B, T, D), BF16),
        grid=(B, T // tm),
        in_specs=[row(attn.shape[2]), row(pool.shape[2]),
                  pl.BlockSpec((N_DIR, 1, tm, W), lambda b, i: (0, b, i, 0)),
                  row(W), row(W), vec, vec, row(N_BRANCH * D),
                  wsp(w_a), wsp(w_p), wsp(w_r)],
        out_specs=row(D),
        compiler_params=_cparams("parallel", "parallel"),
        name="merge",
    )(attn, pool, y, g, bonus, ln_w.reshape(1, W), ln_b.reshape(1, W), gates, w_a, w_p, w_r)


def _out_kernel(m_ref, w_ref, x_ref, gpost_ref, gpre_ref, ml_ref, mc_ref, wr_ref,
                x_out, h_out, aff_out, *, tm, n_lat, n_e):
    is_lat, m_lat, m_ctx = _mod_rows(ml_ref, mc_ref, pl.program_id(1) * tm, tm, n_lat)
    mix = _bdot(m_ref[0], w_ref[...])
    normed = mix * lax.rsqrt(jnp.mean(mix * mix, axis=-1, keepdims=True) + NORM_EPS) * gpost_ref[...]
    x = x_ref[0] + jnp.where(is_lat, m_lat[2:3], m_ctx[2:3]) * normed
    x_out[0] = x
    h = _modnorm(x, gpre_ref[...], is_lat, m_lat, m_ctx, 3)
    h_out[0] = _pack_halves(h)
    h_hi = h.astype(BF16)
    part = _bdot(h_hi, wr_ref[...]) + _bdot(h - h_hi.astype(F32), wr_ref[...])
    logits = part[:, :V7X_LANES] + part[:, V7X_LANES:]
    logits = jnp.where(_iota(logits.shape, 1) < n_e, logits, -jnp.inf)
    e = jnp.exp(logits - jnp.max(logits, axis=-1, keepdims=True))
    aff_out[0] = e / jnp.sum(e, axis=-1, keepdims=True)


def _out_proj(merged, w_out, xs, g_post, g_pre, mod_l, w_router, n_lat):
    B, T, D = xs.shape
    E = w_router.shape[1]
    tm = _pick(T, (544, 272, 256, 128, 320, 64, 32, 16))
    wr = jnp.pad(w_router, ((0, 0), (0, V7X_LANES - E)))
    wr_hi = wr.astype(BF16)
    wr = jnp.concatenate([wr_hi, (wr - wr_hi.astype(F32)).astype(BF16)], axis=1)
    row = lambda w: pl.BlockSpec((1, tm, w), lambda b, i: (b, i, 0))
    vec = pl.BlockSpec((1, D), lambda b, i: (0, 0))
    return pl.pallas_call(
        functools.partial(_out_kernel, tm=tm, n_lat=n_lat, n_e=E),
        out_shape=(jax.ShapeDtypeStruct((B, T, D), F32), jax.ShapeDtypeStruct((B, T, D // 2), jnp.uint32),
                   jax.ShapeDtypeStruct((B, T, V7X_LANES), F32)),
        grid=(B, T // tm),
        in_specs=[row(D), pl.BlockSpec((D, D), lambda b, i: (0, 0)), row(D), vec, vec,
                  pl.BlockSpec((1, 6, D), lambda b, i: (b, 0, 0)),
                  pl.BlockSpec((1, 6, D), lambda b, i: (B, 0, 0)),
                  pl.BlockSpec((D, 2 * V7X_LANES), lambda b, i: (0, 0))],
        out_specs=(row(D), row(D // 2), row(V7X_LANES)),
        compiler_params=_cparams("parallel", "parallel"),
        name="out_proj",
    )(merged, w_out, xs, g_post.reshape(1, D), g_pre.reshape(1, D), mod_l, mod_l, wr)


def _excl_prefix(flags, blk):
    n = flags.shape[-1]
    upper = (_iota((blk, blk), 0) < _iota((blk, blk), 1)).astype(BF16)
    outs = []
    carry = jnp.zeros((flags.shape[0], 1), F32)
    for j in range(n // blk):
        seg = flags[:, j * blk:(j + 1) * blk]
        outs.append(_bdot(seg, upper) + carry)
        carry = carry + jnp.sum(seg, axis=-1, keepdims=True)
    return jnp.concatenate(outs, axis=-1) if len(outs) > 1 else outs[0]


def _topk_kernel(aff_ref, sel_ref, idx_ref, *, cap, blk):
    bits = lax.bitcast_convert_type(aff_ref[0], jnp.int32)
    E = bits.shape[0]

    def body(i, tau):
        cand = tau | jnp.left_shift(jnp.int32(1), 30 - i)
        cnt = jnp.sum((bits >= cand).astype(jnp.int32), axis=-1, keepdims=True)
        return jnp.where(cnt >= cap, cand, tau)

    tau = lax.fori_loop(0, 31, body, jnp.zeros((E, 1), jnp.int32))
    gt = bits > tau
    eq = bits == tau
    need = (cap - jnp.sum(gt.astype(jnp.int32), axis=-1, keepdims=True)).astype(F32)
    eq_rank = _excl_prefix(eq.astype(F32), blk)
    sel = gt | (eq & (eq_rank < need))
    pos = _excl_prefix(sel.astype(F32), blk)
    slots = jnp.where(sel, pos.astype(jnp.int32), -1)
    sel_ref[0] = slots
    n = slots.shape[1]
    tok = _iota((8, n), 1)
    digit = _iota((8, n), 0)
    digits = jnp.where(digit == 0, tok // TOKEN_RADIX, jnp.where(digit == 1, tok % TOKEN_RADIX, 0))
    for e in range(E):
        onehot = slots[e:e + 1, :] == _iota((cap, n), 0)
        d = _bdot(digits.astype(F32), onehot.astype(F32), NT)
        idx_ref[0, e:e + 1, :] = (d[0:1] * TOKEN_RADIX + d[1:2]).astype(jnp.int32)


def _topk_slots(aff_t, cap):
    B, E, n = aff_t.shape
    assert n <= TOKEN_RADIX * 256
    blk = _pick(n, (512, 256, 128))
    return pl.pallas_call(
        functools.partial(_topk_kernel, cap=cap, blk=blk),
        out_shape=(jax.ShapeDtypeStruct((B, E, n), jnp.int32),
                   jax.ShapeDtypeStruct((B, E, cap), jnp.int32)),
        grid=(B,),
        in_specs=[pl.BlockSpec((1, E, n), lambda b: (b, 0, 0))],
        out_specs=(pl.BlockSpec((1, E, n), lambda b: (b, 0, 0)),
                   pl.BlockSpec((1, E, cap), lambda b: (b, 0, 0))),
        compiler_params=_cparams("parallel"),
        name="expert_topk",
    )(aff_t)


def _gather_kernel(idx_ref, h_ref, xe_ref, *, cap, n_e):
    base = (pl.program_id(0) * n_e + pl.program_id(1)) * cap

    def body(s, carry):
        t = idx_ref[base + s]
        xe_ref[0, 0, pl.ds(s, 1), :] = h_ref[0, pl.ds(t, 1), :]
        return carry

    lax.fori_loop(0, cap, body, 0, unroll=8)


def _gather(idx, hp, row_block, n):
    B, E, cap = idx.shape
    dh = hp.shape[2]
    grid_spec = pltpu.PrefetchScalarGridSpec(
        num_scalar_prefetch=1, grid=(B, E),
        in_specs=[pl.BlockSpec((1, n, dh), lambda b, e, idx_ref: (b, row_block, 0))],
        out_specs=pl.BlockSpec((1, 1, cap, dh), lambda b, e, idx_ref: (b, e, 0, 0)))
    return pl.pallas_call(
        functools.partial(_gather_kernel, cap=cap, n_e=E),
        out_shape=jax.ShapeDtypeStruct((B, E, cap, dh), jnp.uint32),
        grid_spec=grid_spec,
        compiler_params=_cparams("parallel", "arbitrary"),
        name="expert_gather",
    )(idx.reshape(-1), hp)


def _pack_halves(h):
    bits = lax.bitcast_convert_type(h.astype(BF16).astype(F32), jnp.uint32)
    half = h.shape[1] // 2
    return (bits[:, :half] >> 16) | (bits[:, half:] & jnp.uint32(0xFFFF0000))


def _unpack_halves(p):
    lo = lax.bitcast_convert_type(p << 16, F32)
    hi = lax.bitcast_convert_type(p & jnp.uint32(0xFFFF0000), F32)
    return lo, hi


def _ffn_kernel(*refs, n_sets):
    xe_refs = refs[:n_sets]
    wg_ref, wu_ref, wd_ref = refs[n_sets:n_sets + 3]
    ye_refs = refs[n_sets + 3:2 * n_sets + 3]
    wgb_ref, wub_ref, wdb_ref = refs[2 * n_sets + 3:]

    @pl.when(pl.program_id(2) == 0)
    def _():
        wgb_ref[...] = wg_ref[0, 0].astype(wgb_ref.dtype)
        wub_ref[...] = wu_ref[0, 0].astype(wub_ref.dtype)
        wdb_ref[...] = wd_ref[0, 0].astype(wdb_ref.dtype)

    for xe_ref, ye_ref in zip(xe_refs, ye_refs):
        lo, hi = _unpack_halves(xe_ref[0, 0])
        half = lo.shape[1]
        gate = _bdot(lo, wgb_ref[:half, :]) + _bdot(hi, wgb_ref[half:, :])
        up = _bdot(lo, wub_ref[:half, :]) + _bdot(hi, wub_ref[half:, :])
        hid = gate * _sigmoid(gate) * up
        ye_ref[0, 0, 0] = _bdot(hid, wdb_ref[...]).astype(ye_ref.dtype)


def _expert_ffn(xes, w_gate, w_up, w_down, layer):
    B, E, _, dh = xes[0].shape
    D = 2 * dh
    FF = w_gate.shape[3]
    fs = FF // FFN_SLABS
    n_sets = len(xes)
    return pl.pallas_call(
        functools.partial(_ffn_kernel, n_sets=n_sets),
        out_shape=tuple(jax.ShapeDtypeStruct((FFN_SLABS, B, E, xe.shape[2], D), BF16) for xe in xes),
        grid=(E, FFN_SLABS, B),
        in_specs=[pl.BlockSpec((1, 1, xe.shape[2], dh), lambda e, s, b: (b, e, 0, 0)) for xe in xes]
        + [pl.BlockSpec((1, 1, D, fs), lambda e, s, b: (layer, e, 0, s)),
           pl.BlockSpec((1, 1, D, fs), lambda e, s, b: (layer, e, 0, s)),
           pl.BlockSpec((1, 1, fs, D), lambda e, s, b: (layer, e, s, 0))],
        out_specs=tuple(pl.BlockSpec((1, 1, 1, xe.shape[2], D), lambda e, s, b: (s, b, e, 0, 0))
                        for xe in xes),
        scratch_shapes=[pltpu.VMEM((D, fs), BF16), pltpu.VMEM((D, fs), BF16), pltpu.VMEM((fs, D), BF16)],
        compiler_params=_cparams("parallel", "parallel", "arbitrary"),
        name="expert_ffn",
    )(*xes, w_gate, w_up, w_down)


def _scatter_kernel(selt_ref, aff_ref, ye_ref, o_ref, *, cap, n_e):
    sel_all = selt_ref[0]
    aff_all = aff_ref[0]
    slot = _iota((1, cap), 1)
    acc = None
    for e in range(n_e):
        onehot = sel_all[:, e:e + 1] == slot
        ye = ye_ref[0, 0, e].astype(F32)
        for s in range(1, ye_ref.shape[0]):
            ye = ye + ye_ref[s, 0, e].astype(F32)
        term = aff_all[:, e:e + 1] * _bdot(onehot, ye)
        acc = term if acc is None else acc + term
    o_ref[0] = acc


def _scatter(sel, aff, row_block, ye):
    B, E, n = sel.shape
    n_slab, cap, D = ye.shape[0], ye.shape[3], ye.shape[4]
    td = _pick(D, (256, 128))
    return pl.pallas_call(
        functools.partial(_scatter_kernel, cap=cap, n_e=E),
        out_shape=jax.ShapeDtypeStruct((B, n, D), F32),
        grid=(B, D // td),
        in_specs=[pl.BlockSpec((1, n, E), lambda b, j: (b, 0, 0)),
                  pl.BlockSpec((1, n, aff.shape[2]), lambda b, j: (b, row_block, 0)),
                  pl.BlockSpec((n_slab, 1, E, cap, td), lambda b, j: (0, b, 0, 0, j))],
        out_specs=pl.BlockSpec((1, n, td), lambda b, j: (b, 0, j)),
        compiler_params=_cparams("parallel", "parallel"),
        name="expert_scatter",
    )(jnp.swapaxes(sel, 1, 2), aff, ye)


def _expert_choice(hb, aff, sets, w_gate, w_up, w_down, layer):
    E = w_gate.shape[1]
    sels, xes = [], []
    for row_block, n in sets:
        aff_t = jnp.swapaxes(aff[:, row_block * n:(row_block + 1) * n, :E], 1, 2)
        sel, idx = _topk_slots(aff_t, EC_FACTOR * n // E)
        sels.append(sel)
        xes.append(_gather(idx, hb, row_block, n))
    yes = _expert_ffn(xes, w_gate, w_up, w_down, layer)
    return [_scatter(sel, aff, row_block, ye) for sel, (row_block, n), ye in zip(sels, sets, yes)]


def _rms(x, g):
    return x * lax.rsqrt(jnp.mean(x * x, axis=-1, keepdims=True) + NORM_EPS) * g


def _rope_tables(n_lat, n_ctx, head_dim):
    rows = n_lat // GRID_W
    row = jnp.repeat(jnp.arange(rows), GRID_W).astype(F32)
    col = (jnp.arange(rows * GRID_W) % GRID_W).astype(F32)
    half = head_dim // 2
    inv = ROPE_THETA ** (-jnp.arange(0, half, 2, dtype=F32) / half)
    ar, ac = row[:, None] * inv, col[:, None] * inv
    cos = jnp.concatenate([jnp.cos(ar), jnp.cos(ar), jnp.cos(ac), jnp.cos(ac)], axis=-1)
    sin = jnp.concatenate([-jnp.sin(ar), jnp.sin(ar), -jnp.sin(ac), jnp.sin(ac)], axis=-1)
    cos = jnp.concatenate([cos, jnp.ones((n_ctx, head_dim), F32)], axis=0)
    sin = jnp.concatenate([sin, jnp.zeros((n_ctx, head_dim), F32)], axis=0)
    return cos, sin


def _pad_cols(w, mult):
    return jnp.pad(w, ((0, 0), (0, (-w.shape[1]) % mult)))


def kernel(x, c, ctx, c_ctx, w_mod, b_mod, norm_pre, norm_post, w_in, q_norm, k_norm, w_attn_o, w_pool_group, pool_scale, w_pool_o, rwkv_mu, rwkv_w0, rwkv_w2, rwkv_a0, rwkv_a2, rwkv_g2, rwkv_k_k, rwkv_k_a, rwkv_r_k, rwkv_ln_w, rwkv_ln_b, w_rwkv_o, w_out, w_router, w_exp_gate, w_exp_up, w_exp_down):
    B, n_lat, D = x.shape
    n_ctx = ctx.shape[1]
    T = n_lat + n_ctx
    depth = w_mod.shape[0]
    hd = q_norm.shape[1]
    attn_w = w_attn_o.shape[1]
    pool_w = w_pool_o.shape[1]
    rwkv_w = w_rwkv_o.shape[1]
    n_shift = rwkv_mu.shape[1]
    n_in = w_in.shape[2]
    kv_w = (n_in - attn_w - pool_w - n_shift - N_BRANCH * D) // 2
    head = rwkv_r_k.shape[2]
    col_k = attn_w
    col_v = col_k + kv_w
    col_pool = col_v + kv_w
    col_r = col_pool + pool_w
    col_gate = col_r + n_shift
    hq, hkv = attn_w // hd, kv_w // hd
    assert n_lat % (hq // hkv * hd) == 0 and pool_w % (hq // hkv * hd) == 0 and n_ctx % RWKV_CHUNK == 0

    cos, sin = _rope_tables(n_lat, n_ctx, hd)
    s_all = jnp.concatenate([jax.nn.silu(c), jax.nn.silu(c_ctx)[None]], axis=0)
    s_all = jnp.pad(s_all, ((0, (-s_all.shape[0]) % 8), (0, 0)))
    mod = _modulation(s_all, w_mod, b_mod)[:, :B + 1].reshape(depth, B + 1, 6, D)
    xs = jnp.concatenate([x, ctx], axis=1)
    (h,) = _stream_update(xs, None, None, None, mod[0], norm_pre[0, 0], mod[0], n_lat, T)

    for l in range(depth):
        keep_ctx = l < depth - 1
        wl = w_in[l]
        w_a = jnp.concatenate([wl[:, col_pool:col_r], wl[:, :col_pool]], axis=1).astype(BF16)
        w_b = _pad_cols(wl[:, col_r:col_gate], 3 * V7X_LANES).astype(BF16)
        w_c = wl[:, col_gate:].astype(BF16)
        p_a = _in_proj(h, w_a)
        p_b = _in_proj(h, w_b)
        p_c = _in_proj(h, w_c)

        attn = _attention(p_a, pool_w, pool_w + attn_w, pool_w + attn_w + kv_w, cos, sin,
                          q_norm[l], k_norm[l], hd, hq, hkv, n_lat)
        pool = _pool(p_a, 0, w_pool_group[l].astype(BF16), pool_scale[l], n_lat)
        r, v, kk, lw, cum, kd, bd, g, bonus = _rwkv_prep(
            p_b, n_lat, rwkv_w, head, rwkv_mu[l], rwkv_w0[l], rwkv_w2[l], rwkv_a0[l], rwkv_a2[l],
            rwkv_g2[l], rwkv_k_k[l], rwkv_k_a[l], rwkv_r_k[l])
        y = _rwkv_scan(r, v, kk, lw, cum, kd, bd, head, n_lat)
        merged = _merge(attn, pool, y, g, bonus, rwkv_ln_w[l], rwkv_ln_b[l], p_c,
                        w_attn_o[l].astype(BF16), w_pool_o[l].astype(BF16), w_rwkv_o[l].astype(BF16), head)
        xs, hb, aff = _out_proj(merged, w_out[l].astype(BF16), xs, norm_post[l, 0], norm_pre[l, 1],
                                mod[l], w_router[l], n_lat)

        sets = [(0, n_lat)] + ([(n_lat // n_ctx, n_ctx)] if keep_ctx else [])
        mixed = _expert_choice(hb, aff, sets, w_exp_gate, w_exp_up, w_exp_down, l)
        f_lat = mixed[0]
        if keep_ctx:
            f_ctx = mixed[1]
            xs, h = _stream_update(xs, f_lat, f_ctx, norm_post[l, 1], mod[l], norm_pre[l + 1, 0],
                                   mod[l + 1], n_lat, T)
        else:
            (xs,) = _stream_update(xs, f_lat, None, norm_post[l, 1], mod[l], None, None, n_lat, n_lat)
    return xs
```

```python
import functools

import jax
import jax.numpy as jnp
from jax import lax
from jax.experimental import pallas as pl
from jax.experimental.pallas import tpu as pltpu

F32 = jnp.float32
BF16 = jnp.bfloat16
HIGHEST = lax.Precision.HIGHEST

GRID_W = 64
NORM_EPS = 1e-6
ROPE_THETA = 10000.0
POOL_WINDOWS = (2, 4, 8, 16)
GN_EPS = 64e-5
EC_FACTOR = 2
N_DIR = 2
N_BRANCH = 3
RWKV_CHUNK = 64
HALO = 16

LOG2_E = 1.4426950408889634
ATTN_SHIFT_LIMIT = 60.0
FFN_SLABS = 2
TOKEN_RADIX = 64

V7X_LANES = 128
VMEM_LIMIT = 52 * 1024 * 1024

NT = (((1,), (1,)), ((), ()))
TN = (((0,), (0,)), ((), ()))


def _pick(n, cands):
    for c in cands:
        if c <= n and n % c == 0:
            return c
    return n


def _cparams(*sem):
    return pltpu.CompilerParams(dimension_semantics=sem, vmem_limit_bytes=VMEM_LIMIT)


def _bdot(x, y, dn=None):
    x = x.astype(BF16)
    y = y.astype(BF16)
    if dn is None:
        return jnp.dot(x, y, preferred_element_type=F32)
    return lax.dot_general(x, y, dn, preferred_element_type=F32)


def _split_dot(x, y):
    hi = x.astype(BF16)
    lo = x - hi.astype(F32)
    return _bdot(hi, y) + _bdot(lo, y)


def _iota(shape, dim):
    return lax.broadcasted_iota(jnp.int32, shape, dim)


def _row_tile(T, min_parts=8):
    for parts in range(min_parts, T // 16 + 1):
        if T % parts == 0 and (T // parts) % 16 == 0:
            return T // parts
    return T


def _mod_kernel(s_ref, w_ref, b_ref, o_ref):
    o_ref[0] = jnp.dot(s_ref[...], w_ref[0], precision=HIGHEST,
                       preferred_element_type=F32) + b_ref[0]


def _modulation(s, w_mod, b_mod):
    L, D, N = w_mod.shape
    R = s.shape[0]
    tn = _pick(N, (1024, 512, 256, 128))
    return pl.pallas_call(
        _mod_kernel,
        out_shape=jax.ShapeDtypeStruct((L, R, N), F32),
        grid=(L, N // tn),
        in_specs=[pl.BlockSpec((R, D), lambda l, j: (0, 0)),
                  pl.BlockSpec((1, D, tn), lambda l, j: (l, 0, j)),
                  pl.BlockSpec((1, 1, tn), lambda l, j: (l, 0, j))],
        out_specs=pl.BlockSpec((1, R, tn), lambda l, j: (l, 0, j)),
        compiler_params=_cparams("parallel", "parallel"),
        name="adaln_mod",
    )(s, w_mod, b_mod.reshape(L, 1, N))


def _mod_rows(m_lat_ref, m_ctx_ref, row0, tm, n_lat):
    is_lat = (row0 + _iota((tm, 1), 0)) < n_lat
    return is_lat, m_lat_ref[0], m_ctx_ref[0]


def _modnorm(x, g, is_lat, m_lat, m_ctx, i_shift):
    shift = jnp.where(is_lat, m_lat[i_shift:i_shift + 1], m_ctx[i_shift:i_shift + 1])
    scale = jnp.where(is_lat, m_lat[i_shift + 1:i_shift + 2], m_ctx[i_shift + 1:i_shift + 2])
    y = x * lax.rsqrt(jnp.mean(x * x, axis=-1, keepdims=True) + NORM_EPS) * g
    return y * (1.0 + scale) + shift


def _small_row_tile(T, n_ctx):
    return _pick(n_ctx, (256, 128, 64, 32, 16))


def _stream_kernel(*refs, tm, n_lat, has_f, has_ctx_f, emit_h):
    refs = list(refs)
    x_ref = refs.pop(0)
    fl_ref = refs.pop(0) if has_f else None
    fc_ref = refs.pop(0) if has_ctx_f else None
    gpost_ref = refs.pop(0) if has_f else None
    ml_ref, mc_ref = refs.pop(0), refs.pop(0)
    if emit_h:
        gpre_ref, mln_ref, mcn_ref = refs.pop(0), refs.pop(0), refs.pop(0)
    x_out = refs.pop(0) if has_f else None
    h_out = refs.pop(0) if emit_h else None

    is_lat = (pl.program_id(1) * tm + _iota((tm, 1), 0)) < n_lat
    x = x_ref[0]
    if has_f:
        f = fl_ref[0]
        if has_ctx_f:
            f = jnp.where(is_lat, f, fc_ref[0])
        normed = f * lax.rsqrt(jnp.mean(f * f, axis=-1, keepdims=True) + NORM_EPS) * gpost_ref[...]
        x = x + jnp.where(is_lat, ml_ref[0][5:6], mc_ref[0][5:6]) * normed
        x_out[0] = x
    if emit_h:
        h_out[0] = _modnorm(x, gpre_ref[...], is_lat, mln_ref[0], mcn_ref[0], 0).astype(h_out.dtype)


def _stream_update(xs, f_lat, f_ctx, g_post, mod_l, g_pre_next, mod_next, n_lat, n_rows):
    B, T, D = xs.shape
    n_ctx = T - n_lat
    tm = _small_row_tile(T, n_ctx)
    has_f, has_ctx_f, emit_h = f_lat is not None, f_ctx is not None, g_pre_next is not None
    n_lt = n_lat // tm
    row = pl.BlockSpec((1, tm, D), lambda b, i: (b, i, 0))
    vec = pl.BlockSpec((1, D), lambda b, i: (0, 0))
    m_l = pl.BlockSpec((1, 6, D), lambda b, i: (b, 0, 0))
    m_c = pl.BlockSpec((1, 6, D), lambda b, i: (B, 0, 0))
    args, specs = [xs], [row]
    if has_f:
        args.append(f_lat)
        specs.append(pl.BlockSpec((1, tm, D), lambda b, i: (b, jnp.minimum(i, n_lt - 1), 0)))
    if has_ctx_f:
        args.append(f_ctx)
        specs.append(pl.BlockSpec((1, tm, D), lambda b, i: (b, jnp.maximum(i - n_lt, 0), 0)))
    if has_f:
        args.append(g_post.reshape(1, D))
        specs.append(vec)
    args += [mod_l, mod_l]
    specs += [m_l, m_c]
    if emit_h:
        args += [g_pre_next.reshape(1, D), mod_next, mod_next]
        specs += [vec, m_l, m_c]
    out_shape, out_specs = [], []
    if has_f:
        out_shape.append(jax.ShapeDtypeStruct((B, n_rows, D), F32))
        out_specs.append(row)
    if emit_h:
        out_shape.append(jax.ShapeDtypeStruct((B, n_rows, D), BF16))
        out_specs.append(row)
    kern = functools.partial(_stream_kernel, tm=tm, n_lat=n_lat, has_f=has_f, has_ctx_f=has_ctx_f,
                             emit_h=emit_h)
    return pl.pallas_call(
        kern, out_shape=tuple(out_shape), grid=(B, n_rows // tm), in_specs=specs,
        out_specs=tuple(out_specs), compiler_params=_cparams("parallel", "parallel"),
        name="stream_update",
    )(*args)


def _in_kernel(h_ref, w_ref, o_ref):
    o_ref[0] = _bdot(h_ref[0], w_ref[...]).astype(o_ref.dtype)


def _in_proj(h, w):
    B, T, D = h.shape
    N = w.shape[1]
    tm = _row_tile(T, 4)
    tn = _pick(N, (1280, 1024, 896, 768, 640, 512, 384, 256, 128))
    return pl.pallas_call(
        _in_kernel,
        out_shape=jax.ShapeDtypeStruct((B, T, N), BF16),
        grid=(B, T // tm, N // tn),
        in_specs=[pl.BlockSpec((1, tm, D), lambda b, i, j: (b, i, 0)),
                  pl.BlockSpec((D, tn), lambda b, i, j: (0, j))],
        out_specs=pl.BlockSpec((1, tm, tn), lambda b, i, j: (b, i, j)),
        compiler_params=_cparams("parallel", "parallel", "arbitrary"),
        name="in_proj",
    )(h, w)


def _rot(x, cos, sin):
    qd = x.shape[1] // 4
    first = (_iota((1, x.shape[1]), 1) // qd) % 2 == 0
    swapped = jnp.where(first, pltpu.roll(x, x.shape[1] - qd, 1), pltpu.roll(x, qd, 1))
    return x * cos + swapped * sin


def _head_norm(x, g):
    return x * lax.rsqrt(jnp.mean(x * x, axis=-1, keepdims=True) + NORM_EPS) * g


def _attn_kernel(bound_ref, q_ref, k_ref, v_ref, cq_ref, sq_ref, ck_ref, sk_ref, qn_ref, kn_ref, o_ref,
                 kp_ref, vp_ref, *, hd, group, tq, n_lat, scale):
    i = pl.program_id(2)
    unit = jnp.where(_iota((1, hd), 1) == 0, 1.0, 0.0)

    @pl.when(i == 0)
    def _():
        k = _rot(_head_norm(k_ref[0].astype(F32), kn_ref[...]), ck_ref[...], sk_ref[...])
        kp_ref[...] = jnp.concatenate([k, jnp.broadcast_to(unit, k.shape)], axis=1).astype(kp_ref.dtype)
        ones = jnp.ones(v_ref.shape[1:], vp_ref.dtype)
        vp_ref[...] = jnp.concatenate([v_ref[0].astype(vp_ref.dtype), ones], axis=1)

    bound = bound_ref[0]

    def attend(key0, shift_in_matmul):
        for g in range(group):
            q = _head_norm(q_ref[0, :, g * hd:(g + 1) * hd].astype(F32), qn_ref[...])
            q = _rot(q, cq_ref[...], sq_ref[...]) * scale
            if shift_in_matmul:
                q_aug = jnp.concatenate([q, jnp.broadcast_to(unit * (-bound), q.shape)], axis=1)
                p = jnp.exp2(_bdot(q_aug, kp_ref[key0:, :], NT))
            else:
                s = _bdot(q, kp_ref[key0:, :hd], NT)
                p = jnp.exp2(s - jnp.max(s, axis=-1, keepdims=True))
            ov = _bdot(p, vp_ref[key0:, :])
            o_ref[0, :, g * hd:(g + 1) * hd] = (ov[:, :hd] / ov[:, hd:hd + 1]).astype(o_ref.dtype)

    is_lat = i * tq < n_lat
    fast = bound < ATTN_SHIFT_LIMIT
    for lat_tile, key0 in ((True, 0), (False, n_lat)):
        for use_fast in (True, False):
            cond = jnp.logical_and(is_lat if lat_tile else jnp.logical_not(is_lat),
                                   fast if use_fast else jnp.logical_not(fast))
            pl.when(cond)(functools.partial(attend, key0, use_fast))


def _attention(p_qkv, q_col, k_col, v_col, cos, sin, q_norm, k_norm, hd, hq, hkv, n_lat):
    B, T, _ = p_qkv.shape
    group = hq // hkv
    gw = group * hd
    tq = _pick(T - n_lat, (256, 128, 64, 32, 16))
    kern = functools.partial(_attn_kernel, hd=hd, group=group, tq=tq, n_lat=n_lat,
                             scale=hd ** -0.5 * LOG2_E)
    tab_q = pl.BlockSpec((tq, hd), lambda b, h, i, s: (i, 0))
    tab_k = pl.BlockSpec((T, hd), lambda b, h, i, s: (0, 0))
    vec = pl.BlockSpec((1, hd), lambda b, h, i, s: (0, 0))
    bound = (1.02 * hd * hd ** -0.5 * LOG2_E) * jnp.max(jnp.abs(q_norm)) * jnp.max(jnp.abs(k_norm))
    grid_spec = pltpu.PrefetchScalarGridSpec(
        num_scalar_prefetch=1, grid=(B, hkv, T // tq),
        in_specs=[pl.BlockSpec((1, tq, gw), lambda b, h, i, s: (b, i, q_col // gw + h)),
                  pl.BlockSpec((1, T, hd), lambda b, h, i, s: (b, 0, k_col // hd + h)),
                  pl.BlockSpec((1, T, hd), lambda b, h, i, s: (b, 0, v_col // hd + h)),
                  tab_q, tab_q, tab_k, tab_k, vec, vec],
        out_specs=pl.BlockSpec((1, tq, gw), lambda b, h, i, s: (b, i, h)),
        scratch_shapes=[pltpu.VMEM((T, 2 * hd), BF16), pltpu.VMEM((T, 2 * hd), BF16)])
    return pl.pallas_call(
        kern,
        out_shape=jax.ShapeDtypeStruct((B, T, hq * hd), BF16),
        grid_spec=grid_spec,
        compiler_params=_cparams("parallel", "parallel", "arbitrary"),
        name="attention",
    )(bound.reshape(1).astype(F32), p_qkv, p_qkv, p_qkv, cos, sin, cos, sin,
      q_norm.reshape(1, hd), k_norm.reshape(1, hd))


def _segment_of_tile(i, tt, n_lat_tiles, n_tiles):
    is_lat = i < n_lat_tiles
    ti = jnp.where(is_lat, i, i - n_lat_tiles)
    seg_tiles = jnp.where(is_lat, n_lat_tiles, n_tiles - n_lat_tiles)
    return ti, seg_tiles


def _halo_specs(tt, width, T, col_block):
    hb = tt // HALO
    last = T // HALO - 1
    prev = pl.BlockSpec((1, HALO, width), lambda b, i: (b, jnp.maximum(i * hb - 1, 0), col_block))
    cur = pl.BlockSpec((1, tt, width), lambda b, i: (b, i, col_block))
    nxt = pl.BlockSpec((1, HALO, width), lambda b, i: (b, jnp.minimum((i + 1) * hb, last), col_block))
    return prev, cur, nxt


def _pool_kernel(prev_ref, cur_ref, next_ref, wg_ref, sc_ref, o_ref, *,
                 tt, n_lat_tiles, n_tiles, gw, windows):
    ti, seg_tiles = _segment_of_tile(pl.program_id(1), tt, n_lat_tiles, n_tiles)
    has_prev = ti > 0
    has_next = ti < seg_tiles - 1
    t_seg = seg_tiles * tt
    tpos = ti * tt + _iota((tt, 1), 0)

    cur = cur_ref[0]
    prev = prev_ref[0]
    nxt = next_ref[0]
    d_cur = _iota((tt, tt), 1) - _iota((tt, tt), 0)
    d_halo = _iota((tt, HALO), 1) - _iota((tt, HALO), 0)
    d_prev = d_halo - HALO
    d_next = d_halo + tt
    for g, win in enumerate(windows):
        lo_off = -(win // 2)
        hi_off = win - win // 2 - 1
        sl = slice(g * gw, (g + 1) * gw)
        band_c = ((d_cur >= lo_off) & (d_cur <= hi_off)).astype(BF16)
        band_p = ((d_prev >= lo_off) & (d_prev <= hi_off) & has_prev).astype(BF16)
        band_n = ((d_next >= lo_off) & (d_next <= hi_off) & has_next).astype(BF16)
        ug = cur[:, sl]
        tot = _bdot(band_c, ug) + _bdot(band_p, prev[:, sl]) + _bdot(band_n, nxt[:, sl])
        lo = jnp.maximum(tpos + lo_off, 0)
        hi = jnp.minimum(tpos + hi_off + 1, t_seg)
        pooled = tot / (hi - lo).astype(F32) - ug.astype(F32)
        y = _bdot(pooled, wg_ref[g]) * sc_ref[:, sl]
        o_ref[0, :, sl] = y.astype(o_ref.dtype)


def _pool(p_arr, col_block, w_group, scale, n_lat):
    B, T, _ = p_arr.shape
    G, gw, _ = w_group.shape
    W = G * gw
    tt = _pick(T - n_lat, (256, 128, 64, 32, 16))
    assert n_lat % tt == 0 and tt % HALO == 0 and max(POOL_WINDOWS) <= HALO
    n_tiles = T // tt
    kern = functools.partial(_pool_kernel, tt=tt, n_lat_tiles=n_lat // tt, n_tiles=n_tiles,
                             gw=gw, windows=POOL_WINDOWS)
    prev, cur, nxt = _halo_specs(tt, W, T, col_block)
    return pl.pallas_call(
        kern,
        out_shape=jax.ShapeDtypeStruct((B, T, W), BF16),
        grid=(B, n_tiles),
        in_specs=[prev, cur, nxt,
                  pl.BlockSpec((G, gw, gw), lambda b, i: (0, 0, 0)),
                  pl.BlockSpec((1, W), lambda b, i: (0, 0))],
        out_specs=pl.BlockSpec((1, tt, W), lambda b, i: (b, i, 0)),
        compiler_params=_cparams("parallel", "parallel"),
        name="pool",
    )(p_arr, p_arr, p_arr, w_group, scale.reshape(1, W))


def _head_sum(x, head):
    lanes = x.shape[1]
    blk = min(lanes, V7X_LANES)
    same = (_iota((blk, blk), 0) // head == _iota((blk, blk), 1) // head).astype(BF16)
    parts = [_split_dot(x[:, j:j + blk], same) for j in range(0, lanes, blk)]
    return jnp.concatenate(parts, axis=1) if len(parts) > 1 else parts[0]


def _rwkv_prep_kernel(prev_ref, cur_ref, next_ref, mu_ref, w0_ref, w2_ref, a0_ref, a2_ref, g2_ref,
                      kk_w_ref, ka_ref, rk_ref,
                      r_ref, v_ref, kk_ref, lw_ref, cum_ref, kd_ref, bd_ref, g_ref, bonus_ref, *,
                      tt, n_lat_tiles, n_tiles, W, head, lora, chunk):
    ti, seg_tiles = _segment_of_tile(pl.program_id(1), tt, n_lat_tiles, n_tiles)
    u = cur_ref[0].astype(F32)
    row = _iota((tt, 1), 0)
    before = jnp.where(ti > 0, prev_ref[0, HALO - 1:HALO, :].astype(F32), 0.0)
    after = jnp.where(ti < seg_tiles - 1, next_ref[0, 0:1, :].astype(F32), 0.0)
    u_prev = jnp.where(row == 0, before, pltpu.roll(u, 1, 0))
    u_next = jnp.where(row == tt - 1, after, pltpu.roll(u, tt - 1, 0))
    u = u + (0.5 * (u_prev + u_next) - u) * mu_ref[...]

    r, k, v = u[:, :W], u[:, W:2 * W], u[:, 2 * W:3 * W]
    o1 = 3 * W
    o2 = o1 + lora
    o3 = o2 + lora
    w_lin = w0_ref[...] + _bdot(jnp.tanh(u[:, o1:o2]), w2_ref[...])
    w_log = -(jnp.maximum(-w_lin, 0.0) + jnp.log(1.0 + jnp.exp(-jnp.abs(w_lin)))) - 0.5
    lw = -jnp.exp(w_log)
    lw_ref[0] = lw
    ri, ci = _iota((tt, tt), 0), _iota((tt, tt), 1)
    same_chunk = ri // chunk == ci // chunk
    p1 = lw.astype(BF16)
    p2 = (lw - p1.astype(F32)).astype(BF16)
    p3 = (lw - p1.astype(F32) - p2.astype(F32)).astype(BF16)
    for z, tri in enumerate((same_chunk & (ci <= ri), same_chunk & (ci >= ri))):
        sl = slice(z * W, (z + 1) * W)
        cum_ref[0, :, sl] = _bdot(tri, p1[:, sl]) + _bdot(tri, p2[:, sl]) + _bdot(tri, p3[:, sl])
    a = jax.nn.sigmoid(a0_ref[...] + _bdot(u[:, o2:o3], a2_ref[...]))
    g_ref[0] = _bdot(jax.nn.sigmoid(u[:, o3:]), g2_ref[...]).astype(g_ref.dtype)

    kk = k * kk_w_ref[...]
    kk = kk * lax.rsqrt(jnp.maximum(_head_sum(kk * kk, head), 1e-24))
    k_sum = 0.0
    for z in range(N_DIR):
        a_z = a[:, z * W:(z + 1) * W]
        k_z = k * (1.0 + (a_z - 1.0) * ka_ref[...])
        kd_ref[0, :, z * W:(z + 1) * W] = k_z.astype(kd_ref.dtype)
        bd_ref[0, :, z * W:(z + 1) * W] = (kk * a_z).astype(bd_ref.dtype)
        k_sum = k_sum + k_z
    r_ref[0] = r.astype(r_ref.dtype)
    v_ref[0] = v.astype(v_ref.dtype)
    kk_ref[0] = kk.astype(kk_ref.dtype)
    bonus_ref[0] = (_head_sum(r * k_sum * rk_ref[...], head) * v).astype(bonus_ref.dtype)


def _rwkv_prep(p_rwkv, n_lat, W, head, mu, w0, w2, a0, a2, g2, k_k, k_a, r_k):
    B, T, NS = p_rwkv.shape
    lora = N_DIR * w2.shape[1]
    n_gate = NS - 3 * W - 2 * lora
    tt = _pick(T - n_lat, (256, 128, 64, 32, 16))
    n_tiles = T // tt
    def cat(m):
        z = jnp.zeros_like(m[0])
        return jnp.concatenate([jnp.concatenate([m[0], z], axis=1),
                                jnp.concatenate([z, m[1]], axis=1)], axis=0).astype(BF16)
    g2p = jnp.pad(g2, ((0, n_gate - g2.shape[0]), (0, 0))).astype(BF16)
    mup = jnp.pad(mu, (0, NS - mu.shape[0])).reshape(1, NS)
    kern = functools.partial(_rwkv_prep_kernel, tt=tt, n_lat_tiles=n_lat // tt, n_tiles=n_tiles,
                             W=W, head=head, lora=lora, chunk=RWKV_CHUNK)
    assert tt % RWKV_CHUNK == 0
    prev, cur, nxt = _halo_specs(tt, NS, T, 0)
    full = lambda shp: pl.BlockSpec(shp, lambda b, i: (0,) * len(shp))
    tile = lambda w: pl.BlockSpec((1, tt, w), lambda b, i: (b, i, 0))
    sd = lambda w, dt: jax.ShapeDtypeStruct((B, T, w), dt)
    return pl.pallas_call(
        kern,
        out_shape=(sd(W, BF16), sd(W, BF16), sd(W, BF16), sd(2 * W, F32), sd(2 * W, F32), sd(2 * W, BF16),
                   sd(2 * W, BF16), sd(W, BF16), sd(W, BF16)),
        grid=(B, n_tiles),
        in_specs=[prev, cur, nxt, full((1, NS)), full((1, 2 * W)), full((lora, 2 * W)),
                  full((1, 2 * W)), full((lora, 2 * W)), full((n_gate, W)),
                  full((1, W)), full((1, W)), full((1, W))],
        out_specs=(tile(W), tile(W), tile(W), tile(2 * W), tile(2 * W), tile(2 * W), tile(2 * W), tile(W),
                   tile(W)),
        compiler_params=_cparams("parallel", "parallel"),
        name="rwkv_prep",
    )(p_rwkv, p_rwkv, p_rwkv, mup, w0.reshape(1, 2 * W), cat(w2), a0.reshape(1, 2 * W), cat(a2), g2p,
      k_k.reshape(1, W), k_a.reshape(1, W), r_k.reshape(1, W))


def _rwkv_kernel(rf_ref, vf_ref, kkf_ref, rb_ref, vb_ref, kkb_ref, lwf_ref, cumf_ref, kf_ref, bf_ref,
                 lwb_ref, cumb_ref, kb_ref, bb_ref, yf_ref, yb_ref, g_ref, *, C, N):
    @pl.when(pl.program_id(1) == 0)
    def _():
        g_ref[...] = jnp.zeros_like(g_ref)

    W = rf_ref.shape[2]
    PW = 2 * N
    n_pairs = W // PW
    psl = [slice(p * PW, (p + 1) * PW) for p in range(n_pairs)]
    first = _iota((1, PW), 1) < N
    masks = (first, jnp.logical_not(first))
    same_head = (_iota((PW, PW), 0) // N) == (_iota((PW, PW), 1) // N)
    zero = jnp.zeros((C, PW), F32)
    diff = _iota((C, C), 0) - _iota((C, C), 1)

    dirs = []
    for d, (r_ref, v_ref, kk_ref, lw_ref, cum_ref, k_ref, b_ref) in enumerate((
            (rf_ref, vf_ref, kkf_ref, lwf_ref, cumf_ref, kf_ref, bf_ref),
            (rb_ref, vb_ref, kkb_ref, lwb_ref, cumb_ref, kb_ref, bb_ref))):
        order = diff if d == 0 else -diff
        order2 = jnp.concatenate([order, order], axis=1)
        cum = cum_ref[0]
        pc = cum[C - 1:C, :] if d == 0 else cum[0:1, :]
        k = k_ref[0].astype(F32)
        b = b_ref[0].astype(F32)
        p_inv = jnp.exp(-cum)
        p_hat = jnp.exp(pc - cum)
        dirs.append(dict(
            strict2=order2 > 0, incl2=order2 >= 0,
            r_t=r_ref[0].astype(F32) * jnp.exp(cum), k_t=k * p_inv, b_t=b * p_inv,
            a_t=-kk_ref[0].astype(F32) * jnp.exp(cum - lw_ref[0]),
            b_h=b * p_hat, k_h=k * p_hat, p_c=jnp.exp(pc), v=v_ref[0].astype(F32)))

    items = [(d, p) for p in range(n_pairs) for d in range(N_DIR)]
    v_m, upper, lower = [], [], []
    for d, p in items:
        t, sl = dirs[d], psl[p]
        rhs_bk = jnp.concatenate([t["b_t"][:, sl], t["k_t"][:, sl]], axis=0)
        lhs = jnp.concatenate([jnp.where(m, x[:, sl], 0.0) for m in masks for x in (t["a_t"], t["r_t"])],
                              axis=0)
        amat = _bdot(lhs, rhs_bk, NT)
        for sub in range(2):
            upper.append(jnp.where(t["strict2"], amat[2 * sub * C:(2 * sub + 1) * C], 0.0))
            lower.append(jnp.where(t["incl2"], amat[(2 * sub + 1) * C:(2 * sub + 2) * C], 0.0))
            v_m.append(jnp.where(masks[sub], t["v"][:, sl], 0.0))
    pw, x = [], []
    for i, (d, p) in enumerate(items):
        up0, up1 = upper[2 * i], upper[2 * i + 1]
        a_ak = jnp.where(first, pltpu.roll(up0, C, 1), up1)
        w = _bdot(a_ak, jnp.concatenate([v_m[2 * i], v_m[2 * i + 1]], axis=0))
        w = pltpu.roll(w, N, 1)
        a_p = dirs[d]["a_t"][:, psl[p]]
        x += [jnp.where(first, a_p, w), jnp.where(first, w, a_p)]
        pw.append(jnp.where(first, up0, pltpu.roll(up1, C, 1)))
    span = 1
    while span < C:
        span *= 2
        for i in range(len(items)):
            x0, x1 = x[2 * i], x[2 * i + 1]
            rhs = [jnp.concatenate([x0, zero], axis=1), jnp.concatenate([zero, x1], axis=1)]
            if span < C:
                rhs = [jnp.concatenate([jnp.where(m, pw[i], 0.0), xr], axis=1) for m, xr in zip(masks, rhs)]
            res = _bdot(pw[i], jnp.concatenate(rhs, axis=0))
            if span < C:
                pw[i] = res[:, :PW]
                res = res[:, PW:]
            x[2 * i], x[2 * i + 1] = x0 + res[:, :PW], x1 + res[:, PW:]

    g0 = [g_ref[d * n_pairs + p] for d, p in items]
    uv, rg = [], []
    for i, (d, p) in enumerate(items):
        x0, x1 = x[2 * i], x[2 * i + 1]
        ahat = [jnp.where(masks[0], x0, 0.0), jnp.where(masks[1], x1, 0.0)]
        vhat = [jnp.where(masks[0], pltpu.roll(x0, N, 1), 0.0), jnp.where(masks[1], pltpu.roll(x1, N, 1), 0.0)]
        res = _bdot(jnp.concatenate([dirs[d]["r_t"][:, psl[p]]] + ahat, axis=0), g0[i])
        rg.append(res[:C])
        u0 = res[C:2 * C] + vhat[0]
        u1 = res[2 * C:] + vhat[1]
        uv.append(jnp.concatenate([u0, v_m[2 * i], u1, v_m[2 * i + 1]], axis=0).astype(BF16))
    for i, (d, p) in enumerate(items):
        y_ref = yf_ref if d == 0 else yb_ref
        y_ref[0, :, psl[p]] = rg[i] + _bdot(jnp.concatenate([lower[2 * i], lower[2 * i + 1]], axis=1), uv[i])
    for i, (d, p) in enumerate(items):
        t, sl = dirs[d], psl[p]
        bk_h = jnp.concatenate([t["b_h"][:, sl], t["k_h"][:, sl]], axis=0)
        upd = _bdot(jnp.concatenate([bk_h, bk_h], axis=0), uv[i], TN)
        decay = jnp.transpose(jnp.broadcast_to(t["p_c"][:, sl], (PW, PW)))
        g_ref[d * n_pairs + p] = jnp.where(same_head, decay * g0[i] + upd, 0.0)


def _rwkv_scan(r, v, kk, lw, cum, kd, bd, head, n_lat):
    B, T, W = r.shape
    C = RWKV_CHUNK
    assert C == head
    nc = T // C
    nc_ctx = nc - n_lat // C

    def fwd(c):
        return jnp.where(c < nc_ctx, nc - nc_ctx + c, c - nc_ctx)

    def bwd(c):
        return nc - 1 - c

    spec = lambda chunk, col: pl.BlockSpec((1, C, W), lambda b, c: (b, chunk(c), col))
    kern = functools.partial(_rwkv_kernel, C=C, N=head)
    out = jax.ShapeDtypeStruct((B, T, W), F32)
    return pl.pallas_call(
        kern,
        out_shape=(out, out),
        grid=(B, nc),
        in_specs=[spec(fwd, 0)] * 3 + [spec(bwd, 0)] * 3 + [spec(fwd, 0)] * 4 + [spec(bwd, 1)] * 4,
        out_specs=(spec(fwd, 0), spec(bwd, 0)),
        scratch_shapes=[pltpu.VMEM((N_DIR * W // (2 * head), 2 * head, 2 * head), F32)],
        compiler_params=_cparams("parallel", "arbitrary"),
        name="rwkv7_chunk",
    )(r, v, kk, r, v, kk, lw, cum, kd, bd, lw, cum, kd, bd)


def _sigmoid(x):
    return 0.5 * jnp.tanh(0.5 * x) + 0.5


def _merge_kernel(attn_ref, pool_ref, yf_ref, yb_ref, g_ref, bonus_ref, lnw_ref, lnb_ref, gate_ref,
                  wa_ref, wp_ref, wr_ref, o_ref, *, head, D):
    y = yf_ref[0] + yb_ref[0]
    inv_n = 1.0 / head
    dev = y - _head_sum(y, head) * inv_n
    var = _head_sum(dev * dev, head) * inv_n
    yn = dev * lax.rsqrt(var + GN_EPS) * lnw_ref[...] + lnb_ref[...]
    rw = (yn + bonus_ref[0].astype(F32)) * g_ref[0].astype(F32)

    def gate(z):
        return _sigmoid(gate_ref[0, :, z * D:(z + 1) * D].astype(F32))

    out = gate(0) * _bdot(attn_ref[0], wa_ref[...])
    out = out + gate(1) * _bdot(pool_ref[0], wp_ref[...])
    out = out + gate(2) * _bdot(rw, wr_ref[...])
    o_ref[0] = out.astype(o_ref.dtype)


def _merge(attn, pool, y_fwd, y_bwd, g, bonus, ln_w, ln_b, gates, w_a, w_p, w_r, head):
    B, T, _ = attn.shape
    W = g.shape[2]
    D = w_a.shape[1]
    tm = _row_tile(T, 16)
    row = lambda w: pl.BlockSpec((1, tm, w), lambda b, i: (b, i, 0))
    vec = pl.BlockSpec((1, W), lambda b, i: (0, 0))
    wsp = lambda w: pl.BlockSpec(w.shape, lambda b, i: (0, 0), pipeline_mode=pl.Buffered(1))
    return pl.pallas_call(
        functools.partial(_merge_kernel, head=head, D=D),
        out_shape=jax.ShapeDtypeStruct((B, T, D), BF16),
        grid=(B, T // tm),
        in_specs=[row(attn.shape[2]), row(pool.shape[2]), row(W), row(W),
                  row(W), row(W), vec, vec, row(N_BRANCH * D),
                  wsp(w_a), wsp(w_p), wsp(w_r)],
        out_specs=row(D),
        compiler_params=_cparams("parallel", "parallel"),
        name="merge",
    )(attn, pool, y_fwd, y_bwd, g, bonus, ln_w.reshape(1, W), ln_b.reshape(1, W), gates, w_a, w_p, w_r)


def _out_kernel(m_ref, w_ref, x_ref, gpost_ref, gpre_ref, ml_ref, mc_ref, wr_ref,
                x_out, h_out, aff_out, *, tm, n_lat, n_e):
    is_lat, m_lat, m_ctx = _mod_rows(ml_ref, mc_ref, pl.program_id(1) * tm, tm, n_lat)
    mix = _bdot(m_ref[0], w_ref[...])
    normed = mix * lax.rsqrt(jnp.mean(mix * mix, axis=-1, keepdims=True) + NORM_EPS) * gpost_ref[...]
    x = x_ref[0] + jnp.where(is_lat, m_lat[2:3], m_ctx[2:3]) * normed
    x_out[0] = x
    h = _modnorm(x, gpre_ref[...], is_lat, m_lat, m_ctx, 3)
    h_out[0] = _pack_halves(h)
    h_hi = h.astype(BF16)
    part = _bdot(h_hi, wr_ref[...]) + _bdot(h - h_hi.astype(F32), wr_ref[...])
    logits = part[:, :V7X_LANES] + part[:, V7X_LANES:]
    logits = jnp.where(_iota(logits.shape, 1) < n_e, logits, -jnp.inf)
    e = jnp.exp(logits - jnp.max(logits, axis=-1, keepdims=True))
    aff_out[0] = e / jnp.sum(e, axis=-1, keepdims=True)


def _out_proj(merged, w_out, xs, g_post, g_pre, mod_l, w_router, n_lat):
    B, T, D = xs.shape
    E = w_router.shape[1]
    tm = _pick(T, (544, 272, 256, 128, 320, 64, 32, 16))
    wr = jnp.pad(w_router, ((0, 0), (0, V7X_LANES - E)))
    wr_hi = wr.astype(BF16)
    wr = jnp.concatenate([wr_hi, (wr - wr_hi.astype(F32)).astype(BF16)], axis=1)
    row = lambda w: pl.BlockSpec((1, tm, w), lambda b, i: (b, i, 0))
    vec = pl.BlockSpec((1, D), lambda b, i: (0, 0))
    return pl.pallas_call(
        functools.partial(_out_kernel, tm=tm, n_lat=n_lat, n_e=E),
        out_shape=(jax.ShapeDtypeStruct((B, T, D), F32), jax.ShapeDtypeStruct((B, T, D // 2), jnp.uint32),
                   jax.ShapeDtypeStruct((B, T, V7X_LANES), F32)),
        grid=(B, T // tm),
        in_specs=[row(D), pl.BlockSpec((D, D), lambda b, i: (0, 0)), row(D), vec, vec,
                  pl.BlockSpec((1, 6, D), lambda b, i: (b, 0, 0)),
                  pl.BlockSpec((1, 6, D), lambda b, i: (B, 0, 0)),
                  pl.BlockSpec((D, 2 * V7X_LANES), lambda b, i: (0, 0))],
        out_specs=(row(D), row(D // 2), row(V7X_LANES)),
        compiler_params=_cparams("parallel", "parallel"),
        name="out_proj",
    )(merged, w_out, xs, g_post.reshape(1, D), g_pre.reshape(1, D), mod_l, mod_l, wr)


def _excl_prefix(flags, blk):
    n = flags.shape[-1]
    upper = (_iota((blk, blk), 0) < _iota((blk, blk), 1)).astype(BF16)
    outs = []
    carry = jnp.zeros((flags.shape[0], 1), F32)
    for j in range(n // blk):
        seg = flags[:, j * blk:(j + 1) * blk]
        outs.append(_bdot(seg, upper) + carry)
        carry = carry + jnp.sum(seg, axis=-1, keepdims=True)
    return jnp.concatenate(outs, axis=-1) if len(outs) > 1 else outs[0]


def _topk_kernel(aff_ref, sel_ref, idx_ref, *, cap, blk):
    bits = lax.bitcast_convert_type(aff_ref[0], jnp.int32)
    E = bits.shape[0]

    def body(i, tau):
        cand = tau | jnp.left_shift(jnp.int32(1), 30 - i)
        cnt = jnp.sum((bits >= cand).astype(jnp.int32), axis=-1, keepdims=True)
        return jnp.where(cnt >= cap, cand, tau)

    tau = lax.fori_loop(0, 31, body, jnp.zeros((E, 1), jnp.int32))
    gt = bits > tau
    eq = bits == tau
    need = (cap - jnp.sum(gt.astype(jnp.int32), axis=-1, keepdims=True)).astype(F32)
    eq_rank = _excl_prefix(eq.astype(F32), blk)
    sel = gt | (eq & (eq_rank < need))
    pos = _excl_prefix(sel.astype(F32), blk)
    slots = jnp.where(sel, pos.astype(jnp.int32), -1)
    sel_ref[0] = slots
    n = slots.shape[1]
    tok = _iota((8, n), 1)
    digit = _iota((8, n), 0)
    digits = jnp.where(digit == 0, tok // TOKEN_RADIX, jnp.where(digit == 1, tok % TOKEN_RADIX, 0))
    for e in range(E):
        onehot = slots[e:e + 1, :] == _iota((cap, n), 0)
        d = _bdot(digits.astype(F32), onehot.astype(F32), NT)
        idx_ref[0, e:e + 1, :] = (d[0:1] * TOKEN_RADIX + d[1:2]).astype(jnp.int32)


def _topk_slots(aff_t, cap):
    B, E, n = aff_t.shape
    assert n <= TOKEN_RADIX * 256
    blk = _pick(n, (512, 256, 128))
    return pl.pallas_call(
        functools.partial(_topk_kernel, cap=cap, blk=blk),
        out_shape=(jax.ShapeDtypeStruct((B, E, n), jnp.int32),
                   jax.ShapeDtypeStruct((B, E, cap), jnp.int32)),
        grid=(B,),
        in_specs=[pl.BlockSpec((1, E, n), lambda b: (b, 0, 0))],
        out_specs=(pl.BlockSpec((1, E, n), lambda b: (b, 0, 0)),
                   pl.BlockSpec((1, E, cap), lambda b: (b, 0, 0))),
        compiler_params=_cparams("parallel"),
        name="expert_topk",
    )(aff_t)


def _gather_kernel(idx_ref, h_ref, xe_ref, *, cap, n_e):
    base = (pl.program_id(0) * n_e + pl.program_id(1)) * cap

    def body(s, carry):
        t = idx_ref[base + s]
        xe_ref[0, 0, pl.ds(s, 1), :] = h_ref[0, pl.ds(t, 1), :]
        return carry

    lax.fori_loop(0, cap, body, 0, unroll=8)


def _gather(idx, hp, row_block, n):
    B, E, cap = idx.shape
    dh = hp.shape[2]
    grid_spec = pltpu.PrefetchScalarGridSpec(
        num_scalar_prefetch=1, grid=(B, E),
        in_specs=[pl.BlockSpec((1, n, dh), lambda b, e, idx_ref: (b, row_block, 0))],
        out_specs=pl.BlockSpec((1, 1, cap, dh), lambda b, e, idx_ref: (b, e, 0, 0)))
    return pl.pallas_call(
        functools.partial(_gather_kernel, cap=cap, n_e=E),
        out_shape=jax.ShapeDtypeStruct((B, E, cap, dh), jnp.uint32),
        grid_spec=grid_spec,
        compiler_params=_cparams("parallel", "arbitrary"),
        name="expert_gather",
    )(idx.reshape(-1), hp)


def _pack_halves(h):
    bits = lax.bitcast_convert_type(h.astype(BF16).astype(F32), jnp.uint32)
    half = h.shape[1] // 2
    return (bits[:, :half] >> 16) | (bits[:, half:] & jnp.uint32(0xFFFF0000))


def _unpack_halves(p):
    lo = lax.bitcast_convert_type(p << 16, F32)
    hi = lax.bitcast_convert_type(p & jnp.uint32(0xFFFF0000), F32)
    return lo, hi


def _ffn_kernel(*refs, n_sets):
    xe_refs = refs[:n_sets]
    wg_ref, wu_ref, wd_ref = refs[n_sets:n_sets + 3]
    ye_refs = refs[n_sets + 3:2 * n_sets + 3]
    wgb_ref, wub_ref, wdb_ref = refs[2 * n_sets + 3:]

    @pl.when(pl.program_id(2) == 0)
    def _():
        wgb_ref[...] = wg_ref[0, 0].astype(wgb_ref.dtype)
        wub_ref[...] = wu_ref[0, 0].astype(wub_ref.dtype)
        wdb_ref[...] = wd_ref[0, 0].astype(wdb_ref.dtype)

    for xe_ref, ye_ref in zip(xe_refs, ye_refs):
        lo, hi = _unpack_halves(xe_ref[0, 0])
        half = lo.shape[1]
        gate = _bdot(lo, wgb_ref[:half, :]) + _bdot(hi, wgb_ref[half:, :])
        up = _bdot(lo, wub_ref[:half, :]) + _bdot(hi, wub_ref[half:, :])
        hid = gate * _sigmoid(gate) * up
        ye_ref[0, 0, 0] = _bdot(hid, wdb_ref[...]).astype(ye_ref.dtype)


def _expert_ffn(xes, w_gate, w_up, w_down, layer):
    B, E, _, dh = xes[0].shape
    D = 2 * dh
    FF = w_gate.shape[3]
    fs = FF // FFN_SLABS
    n_sets = len(xes)
    return pl.pallas_call(
        functools.partial(_ffn_kernel, n_sets=n_sets),
        out_shape=tuple(jax.ShapeDtypeStruct((FFN_SLABS, B, E, xe.shape[2], D), BF16) for xe in xes),
        grid=(E, FFN_SLABS, B),
        in_specs=[pl.BlockSpec((1, 1, xe.shape[2], dh), lambda e, s, b: (b, e, 0, 0)) for xe in xes]
        + [pl.BlockSpec((1, 1, D, fs), lambda e, s, b: (layer, e, 0, s)),
           pl.BlockSpec((1, 1, D, fs), lambda e, s, b: (layer, e, 0, s)),
           pl.BlockSpec((1, 1, fs, D), lambda e, s, b: (layer, e, s, 0))],
        out_specs=tuple(pl.BlockSpec((1, 1, 1, xe.shape[2], D), lambda e, s, b: (s, b, e, 0, 0))
                        for xe in xes),
        scratch_shapes=[pltpu.VMEM((D, fs), BF16), pltpu.VMEM((D, fs), BF16), pltpu.VMEM((fs, D), BF16)],
        compiler_params=_cparams("parallel", "parallel", "arbitrary"),
        name="expert_ffn",
    )(*xes, w_gate, w_up, w_down)


def _scatter_kernel(selt_ref, aff_ref, ye_ref, o_ref, *, cap, n_e):
    sel_all = selt_ref[0]
    aff_all = aff_ref[0]
    slot = _iota((1, cap), 1)
    acc = None
    for e in range(n_e):
        onehot = sel_all[:, e:e + 1] == slot
        ye = ye_ref[0, 0, e].astype(F32)
        for s in range(1, ye_ref.shape[0]):
            ye = ye + ye_ref[s, 0, e].astype(F32)
        term = aff_all[:, e:e + 1] * _bdot(onehot, ye)
        acc = term if acc is None else acc + term
    o_ref[0] = acc


def _scatter(sel, aff, row_block, ye):
    B, E, n = sel.shape
    n_slab, cap, D = ye.shape[0], ye.shape[3], ye.shape[4]
    td = _pick(D, (256, 128))
    return pl.pallas_call(
        functools.partial(_scatter_kernel, cap=cap, n_e=E),
        out_shape=jax.ShapeDtypeStruct((B, n, D), F32),
        grid=(B, D // td),
        in_specs=[pl.BlockSpec((1, n, E), lambda b, j: (b, 0, 0)),
                  pl.BlockSpec((1, n, aff.shape[2]), lambda b, j: (b, row_block, 0)),
                  pl.BlockSpec((n_slab, 1, E, cap, td), lambda b, j: (0, b, 0, 0, j))],
        out_specs=pl.BlockSpec((1, n, td), lambda b, j: (b, 0, j)),
        compiler_params=_cparams("parallel", "parallel"),
        name="expert_scatter",
    )(jnp.swapaxes(sel, 1, 2), aff, ye)


def _expert_choice(hb, aff, sets, w_gate, w_up, w_down, layer):
    E = w_gate.shape[1]
    sels, xes = [], []
    for row_block, n in sets:
        aff_t = jnp.swapaxes(aff[:, row_block * n:(row_block + 1) * n, :E], 1, 2)
        sel, idx = _topk_slots(aff_t, EC_FACTOR * n // E)
        sels.append(sel)
        xes.append(_gather(idx, hb, row_block, n))
    yes = _expert_ffn(xes, w_gate, w_up, w_down, layer)
    return [_scatter(sel, aff, row_block, ye) for sel, (row_block, n), ye in zip(sels, sets, yes)]


def _rms(x, g):
    return x * lax.rsqrt(jnp.mean(x * x, axis=-1, keepdims=True) + NORM_EPS) * g


def _rope_tables(n_lat, n_ctx, head_dim):
    rows = n_lat // GRID_W
    row = jnp.repeat(jnp.arange(rows), GRID_W).astype(F32)
    col = (jnp.arange(rows * GRID_W) % GRID_W).astype(F32)
    half = head_dim // 2
    inv = ROPE_THETA ** (-jnp.arange(0, half, 2, dtype=F32) / half)
    ar, ac = row[:, None] * inv, col[:, None] * inv
    cos = jnp.concatenate([jnp.cos(ar), jnp.cos(ar), jnp.cos(ac), jnp.cos(ac)], axis=-1)
    sin = jnp.concatenate([-jnp.sin(ar), jnp.sin(ar), -jnp.sin(ac), jnp.sin(ac)], axis=-1)
    cos = jnp.concatenate([cos, jnp.ones((n_ctx, head_dim), F32)], axis=0)
    sin = jnp.concatenate([sin, jnp.zeros((n_ctx, head_dim), F32)], axis=0)
    return cos, sin


def _pad_cols(w, mult):
    return jnp.pad(w, ((0, 0), (0, (-w.shape[1]) % mult)))


def kernel(x, c, ctx, c_ctx, w_mod, b_mod, norm_pre, norm_post, w_in, q_norm, k_norm, w_attn_o, w_pool_group, pool_scale, w_pool_o, rwkv_mu, rwkv_w0, rwkv_w2, rwkv_a0, rwkv_a2, rwkv_g2, rwkv_k_k, rwkv_k_a, rwkv_r_k, rwkv_ln_w, rwkv_ln_b, w_rwkv_o, w_out, w_router, w_exp_gate, w_exp_up, w_exp_down):
    B, n_lat, D = x.shape
    n_ctx = ctx.shape[1]
    T = n_lat + n_ctx
    depth = w_mod.shape[0]
    hd = q_norm.shape[1]
    attn_w = w_attn_o.shape[1]
    pool_w = w_pool_o.shape[1]
    rwkv_w = w_rwkv_o.shape[1]
    n_shift = rwkv_mu.shape[1]
    n_in = w_in.shape[2]
    kv_w = (n_in - attn_w - pool_w - n_shift - N_BRANCH * D) // 2
    head = rwkv_r_k.shape[2]
    col_k = attn_w
    col_v = col_k + kv_w
    col_pool = col_v + kv_w
    col_r = col_pool + pool_w
    col_gate = col_r + n_shift
    hq, hkv = attn_w // hd, kv_w // hd
    assert n_lat % (hq // hkv * hd) == 0 and pool_w % (hq // hkv * hd) == 0 and n_ctx % RWKV_CHUNK == 0

    cos, sin = _rope_tables(n_lat, n_ctx, hd)
    s_all = jnp.concatenate([jax.nn.silu(c), jax.nn.silu(c_ctx)[None]], axis=0)
    s_all = jnp.pad(s_all, ((0, (-s_all.shape[0]) % 8), (0, 0)))
    mod = _modulation(s_all, w_mod, b_mod)[:, :B + 1].reshape(depth, B + 1, 6, D)
    xs = jnp.concatenate([x, ctx], axis=1)
    (h,) = _stream_update(xs, None, None, None, mod[0], norm_pre[0, 0], mod[0], n_lat, T)

    for l in range(depth):
        keep_ctx = l < depth - 1
        wl = w_in[l]
        w_a = jnp.concatenate([wl[:, col_pool:col_r], wl[:, :col_pool]], axis=1).astype(BF16)
        w_b = _pad_cols(wl[:, col_r:col_gate], 3 * V7X_LANES).astype(BF16)
        w_c = wl[:, col_gate:].astype(BF16)
        p_a = _in_proj(h, w_a)
        p_b = _in_proj(h, w_b)
        p_c = _in_proj(h, w_c)

        attn = _attention(p_a, pool_w, pool_w + attn_w, pool_w + attn_w + kv_w, cos, sin,
                          q_norm[l], k_norm[l], hd, hq, hkv, n_lat)
        pool = _pool(p_a, 0, w_pool_group[l].astype(BF16), pool_scale[l], n_lat)
        r, v, kk, lw, cum, kd, bd, g, bonus = _rwkv_prep(
            p_b, n_lat, rwkv_w, head, rwkv_mu[l], rwkv_w0[l], rwkv_w2[l], rwkv_a0[l], rwkv_a2[l],
            rwkv_g2[l], rwkv_k_k[l], rwkv_k_a[l], rwkv_r_k[l])
        y_fwd, y_bwd = _rwkv_scan(r, v, kk, lw, cum, kd, bd, head, n_lat)
        merged = _merge(attn, pool, y_fwd, y_bwd, g, bonus, rwkv_ln_w[l], rwkv_ln_b[l], p_c,
                        w_attn_o[l].astype(BF16), w_pool_o[l].astype(BF16), w_rwkv_o[l].astype(BF16), head)
        xs, hb, aff = _out_proj(merged, w_out[l].astype(BF16), xs, norm_post[l, 0], norm_pre[l, 1],
                                mod[l], w_router[l], n_lat)

        sets = [(0, n_lat)] + ([(n_lat // n_ctx, n_ctx)] if keep_ctx else [])
        mixed = _expert_choice(hb, aff, sets, w_exp_gate, w_exp_up, w_exp_down, l)
        f_lat = mixed[0]
        if keep_ctx:
            f_ctx = mixed[1]
            xs, h = _stream_update(xs, f_lat, f_ctx, norm_post[l, 1], mod[l], norm_pre[l + 1, 0],
                                   mod[l + 1], n_lat, T)
        else:
            (xs,) = _stream_update(xs, f_lat, None, norm_post[l, 1], mod[l], None, None, n_lat, n_lat)
    return xs
```

```python
import functools

import jax
import jax.numpy as jnp
from jax import lax
from jax.experimental import pallas as pl
from jax.experimental.pallas import tpu as pltpu

F32 = jnp.float32
BF16 = jnp.bfloat16
HIGHEST = lax.Precision.HIGHEST

GRID_W = 64
NORM_EPS = 1e-6
ROPE_THETA = 10000.0
POOL_WINDOWS = (2, 4, 8, 16)
GN_EPS = 64e-5
EC_FACTOR = 2
N_DIR = 2
N_BRANCH = 3
RWKV_CHUNK = 64
HALO = 16

LOG2_E = 1.4426950408889634
ATTN_SHIFT_LIMIT = 60.0
FFN_SLABS = 2
TOKEN_RADIX = 64

V7X_LANES = 128
VMEM_LIMIT = 52 * 1024 * 1024

NT = (((1,), (1,)), ((), ()))
TN = (((0,), (0,)), ((), ()))


def _pick(n, cands):
    for c in cands:
        if c <= n and n % c == 0:
            return c
    return n


def _cparams(*sem):
    return pltpu.CompilerParams(dimension_semantics=sem, vmem_limit_bytes=VMEM_LIMIT)


def _bdot(x, y, dn=None):
    x = x.astype(BF16)
    y = y.astype(BF16)
    if dn is None:
        return jnp.dot(x, y, preferred_element_type=F32)
    return lax.dot_general(x, y, dn, preferred_element_type=F32)


def _split_dot(x, y):
    hi = x.astype(BF16)
    lo = x - hi.astype(F32)
    return _bdot(hi, y) + _bdot(lo, y)


def _iota(shape, dim):
    return lax.broadcasted_iota(jnp.int32, shape, dim)


def _row_tile(T, min_parts=8):
    for parts in range(min_parts, T // 16 + 1):
        if T % parts == 0 and (T // parts) % 16 == 0:
            return T // parts
    return T


def _pieces(x):
    p1 = x.astype(BF16)
    r1 = x - p1.astype(F32)
    p2 = r1.astype(BF16)
    p3 = (r1 - p2.astype(F32)).astype(BF16)
    return p1, p2, p3


def _mod_kernel(s_ref, w_ref, b_ref, o_ref):
    s1, s2, s3 = (p.astype(F32) for p in _pieces(s_ref[...]))
    w1, w2, w3 = _pieces(w_ref[0])
    R = s1.shape[0]
    a = _bdot(jnp.concatenate([s1, s2, s3], axis=0), w1)
    b = _bdot(jnp.concatenate([s1, s2], axis=0), w2)
    c = _bdot(s1, w3)
    o_ref[0] = (a[:R] + a[R:2 * R] + a[2 * R:]) + (b[:R] + b[R:]) + c + b_ref[0]


def _modulation(s, w_mod, b_mod):
    L, D, N = w_mod.shape
    R = s.shape[0]
    tn = _pick(N, (1024, 512, 256, 128))
    return pl.pallas_call(
        _mod_kernel,
        out_shape=jax.ShapeDtypeStruct((L, R, N), F32),
        grid=(L, N // tn),
        in_specs=[pl.BlockSpec((R, D), lambda l, j: (0, 0)),
                  pl.BlockSpec((1, D, tn), lambda l, j: (l, 0, j)),
                  pl.BlockSpec((1, 1, tn), lambda l, j: (l, 0, j))],
        out_specs=pl.BlockSpec((1, R, tn), lambda l, j: (l, 0, j)),
        compiler_params=_cparams("parallel", "parallel"),
        name="adaln_mod",
    )(s, w_mod, b_mod.reshape(L, 1, N))


def _mod_rows(m_lat_ref, m_ctx_ref, row0, tm, n_lat):
    is_lat = (row0 + _iota((tm, 1), 0)) < n_lat
    return is_lat, m_lat_ref[0], m_ctx_ref[0]


def _modnorm(x, g, is_lat, m_lat, m_ctx, i_shift):
    shift = jnp.where(is_lat, m_lat[i_shift:i_shift + 1], m_ctx[i_shift:i_shift + 1])
    scale = jnp.where(is_lat, m_lat[i_shift + 1:i_shift + 2], m_ctx[i_shift + 1:i_shift + 2])
    y = x * lax.rsqrt(jnp.mean(x * x, axis=-1, keepdims=True) + NORM_EPS) * g
    return y * (1.0 + scale) + shift


def _small_row_tile(T, n_ctx):
    return _pick(n_ctx, (256, 128, 64, 32, 16))


def _stream_kernel(*refs, tm, n_lat, has_f, has_ctx_f, emit_h):
    refs = list(refs)
    x_ref = refs.pop(0)
    fl_ref = refs.pop(0) if has_f else None
    fc_ref = refs.pop(0) if has_ctx_f else None
    gpost_ref = refs.pop(0) if has_f else None
    ml_ref, mc_ref = refs.pop(0), refs.pop(0)
    if emit_h:
        gpre_ref, mln_ref, mcn_ref = refs.pop(0), refs.pop(0), refs.pop(0)
    x_out = refs.pop(0) if has_f else None
    h_out = refs.pop(0) if emit_h else None

    is_lat = (pl.program_id(1) * tm + _iota((tm, 1), 0)) < n_lat
    x = x_ref[0]
    if has_f:
        f = fl_ref[0]
        if has_ctx_f:
            f = jnp.where(is_lat, f, fc_ref[0])
        normed = f * lax.rsqrt(jnp.mean(f * f, axis=-1, keepdims=True) + NORM_EPS) * gpost_ref[...]
        x = x + jnp.where(is_lat, ml_ref[0][5:6], mc_ref[0][5:6]) * normed
        x_out[0] = x
    if emit_h:
        h_out[0] = _modnorm(x, gpre_ref[...], is_lat, mln_ref[0], mcn_ref[0], 0).astype(h_out.dtype)


def _stream_update(xs, f_lat, f_ctx, g_post, mod_l, g_pre_next, mod_next, n_lat, n_rows):
    B, T, D = xs.shape
    n_ctx = T - n_lat
    tm = _small_row_tile(T, n_ctx)
    has_f, has_ctx_f, emit_h = f_lat is not None, f_ctx is not None, g_pre_next is not None
    n_lt = n_lat // tm
    row = pl.BlockSpec((1, tm, D), lambda b, i: (b, i, 0))
    vec = pl.BlockSpec((1, D), lambda b, i: (0, 0))
    m_l = pl.BlockSpec((1, 6, D), lambda b, i: (b, 0, 0))
    m_c = pl.BlockSpec((1, 6, D), lambda b, i: (B, 0, 0))
    args, specs = [xs], [row]
    if has_f:
        args.append(f_lat)
        specs.append(pl.BlockSpec((1, tm, D), lambda b, i: (b, jnp.minimum(i, n_lt - 1), 0)))
    if has_ctx_f:
        args.append(f_ctx)
        specs.append(pl.BlockSpec((1, tm, D), lambda b, i: (b, jnp.maximum(i - n_lt, 0), 0)))
    if has_f:
        args.append(g_post.reshape(1, D))
        specs.append(vec)
    args += [mod_l, mod_l]
    specs += [m_l, m_c]
    if emit_h:
        args += [g_pre_next.reshape(1, D), mod_next, mod_next]
        specs += [vec, m_l, m_c]
    out_shape, out_specs = [], []
    if has_f:
        out_shape.append(jax.ShapeDtypeStruct((B, n_rows, D), F32))
        out_specs.append(row)
    if emit_h:
        out_shape.append(jax.ShapeDtypeStruct((B, n_rows, D), BF16))
        out_specs.append(row)
    kern = functools.partial(_stream_kernel, tm=tm, n_lat=n_lat, has_f=has_f, has_ctx_f=has_ctx_f,
                             emit_h=emit_h)
    return pl.pallas_call(
        kern, out_shape=tuple(out_shape), grid=(B, n_rows // tm), in_specs=specs,
        out_specs=tuple(out_specs), compiler_params=_cparams("parallel", "parallel"),
        name="stream_update",
    )(*args)


def _in_kernel(h_ref, w_ref, o_ref):
    o_ref[0] = _bdot(h_ref[0], w_ref[...]).astype(o_ref.dtype)


def _in_proj(h, w):
    B, T, D = h.shape
    N = w.shape[1]
    tm = _row_tile(T, 4)
    tn = _pick(N, (1280, 1024, 896, 768, 640, 512, 384, 256, 128))
    return pl.pallas_call(
        _in_kernel,
        out_shape=jax.ShapeDtypeStruct((B, T, N), BF16),
        grid=(B, T // tm, N // tn),
        in_specs=[pl.BlockSpec((1, tm, D), lambda b, i, j: (b, i, 0)),
                  pl.BlockSpec((D, tn), lambda b, i, j: (0, j))],
        out_specs=pl.BlockSpec((1, tm, tn), lambda b, i, j: (b, i, j)),
        compiler_params=_cparams("parallel", "parallel", "arbitrary"),
        name="in_proj",
    )(h, w)


def _rot(x, cos, sin):
    qd = x.shape[1] // 4
    first = (_iota((1, x.shape[1]), 1) // qd) % 2 == 0
    swapped = jnp.where(first, pltpu.roll(x, x.shape[1] - qd, 1), pltpu.roll(x, qd, 1))
    return x * cos + swapped * sin


def _head_norm(x, g):
    return x * lax.rsqrt(jnp.mean(x * x, axis=-1, keepdims=True) + NORM_EPS) * g


def _attn_kernel(bound_ref, q_ref, k_ref, v_ref, cq_ref, sq_ref, ck_ref, sk_ref, qn_ref, kn_ref, *rest,
                 hd, group, scale):
    o_ref, kp_ref, vp_ref = rest[-3:]
    unit = jnp.where(_iota((1, hd), 1) == 0, 1.0, 0.0)

    @pl.when(pl.program_id(2) == 0)
    def _():
        k = _rot(_head_norm(k_ref[0].astype(F32), kn_ref[...]), ck_ref[...], sk_ref[...])
        kp_ref[...] = jnp.concatenate([k, jnp.broadcast_to(unit, k.shape)], axis=1).astype(kp_ref.dtype)
        ones = jnp.ones(v_ref.shape[1:], vp_ref.dtype)
        vp_ref[...] = jnp.concatenate([v_ref[0].astype(vp_ref.dtype), ones], axis=1)

    bound = bound_ref[0]

    def attend(shift_in_matmul):
        for g in range(group):
            q = _head_norm(q_ref[0, :, g * hd:(g + 1) * hd].astype(F32), qn_ref[...])
            q = _rot(q, cq_ref[...], sq_ref[...]) * scale
            if shift_in_matmul:
                q_aug = jnp.concatenate([q, jnp.broadcast_to(unit * (-bound), q.shape)], axis=1)
                p = jnp.exp2(_bdot(q_aug, kp_ref[...], NT))
            else:
                s = _bdot(q, kp_ref[:, :hd], NT)
                p = jnp.exp2(s - jnp.max(s, axis=-1, keepdims=True))
            ov = _bdot(p, vp_ref[...])
            o_ref[0, :, g * hd:(g + 1) * hd] = (ov[:, :hd] / ov[:, hd:hd + 1]).astype(o_ref.dtype)

    fast = bound < ATTN_SHIFT_LIMIT
    pl.when(fast)(functools.partial(attend, True))
    pl.when(jnp.logical_not(fast))(functools.partial(attend, False))


def _attention(p_qkv, q_col, k_col, v_col, cos, sin, q_norm, k_norm, hd, hq, hkv, q_rows, key_rows, out=None):
    B, T, _ = p_qkv.shape
    group = hq // hkv
    gw = group * hd
    (q0, nq), (k0, nk) = q_rows, key_rows
    tq = next(t for t in (512, 256, 128, 64, 32, 16) if nq % t == 0 and q0 % t == 0)
    assert k0 % nk == 0
    kern = functools.partial(_attn_kernel, hd=hd, group=group, scale=hd ** -0.5 * LOG2_E)
    qb, kb = q0 // tq, k0 // nk
    tab_q = pl.BlockSpec((tq, hd), lambda b, h, i, s: (i + qb, 0))
    tab_k = pl.BlockSpec((nk, hd), lambda b, h, i, s: (kb, 0))
    vec = pl.BlockSpec((1, hd), lambda b, h, i, s: (0, 0))
    bound = (1.02 * hd * hd ** -0.5 * LOG2_E) * jnp.max(jnp.abs(q_norm)) * jnp.max(jnp.abs(k_norm))
    in_specs = [pl.BlockSpec((1, tq, gw), lambda b, h, i, s: (b, i + qb, q_col // gw + h)),
                pl.BlockSpec((1, nk, hd), lambda b, h, i, s: (b, kb, k_col // hd + h)),
                pl.BlockSpec((1, nk, hd), lambda b, h, i, s: (b, kb, v_col // hd + h)),
                tab_q, tab_q, tab_k, tab_k, vec, vec]
    args = [bound.reshape(1).astype(F32), p_qkv, p_qkv, p_qkv, cos, sin, cos, sin,
            q_norm.reshape(1, hd), k_norm.reshape(1, hd)]
    aliases = {}
    if out is not None:
        in_specs.append(pl.BlockSpec(memory_space=pl.ANY))
        args.append(out)
        aliases = {len(args) - 1: 0}
    grid_spec = pltpu.PrefetchScalarGridSpec(
        num_scalar_prefetch=1, grid=(B, hkv, nq // tq), in_specs=in_specs,
        out_specs=pl.BlockSpec((1, tq, gw), lambda b, h, i, s: (b, i + qb, h)),
        scratch_shapes=[pltpu.VMEM((nk, 2 * hd), BF16), pltpu.VMEM((nk, 2 * hd), BF16)])
    return pl.pallas_call(
        kern,
        out_shape=jax.ShapeDtypeStruct((B, T, hq * hd), BF16),
        grid_spec=grid_spec,
        input_output_aliases=aliases,
        compiler_params=_cparams("parallel", "parallel", "arbitrary"),
        name="attention",
    )(*args)


def _segment_of_tile(i, tt, n_lat_tiles, n_tiles):
    is_lat = i < n_lat_tiles
    ti = jnp.where(is_lat, i, i - n_lat_tiles)
    seg_tiles = jnp.where(is_lat, n_lat_tiles, n_tiles - n_lat_tiles)
    return ti, seg_tiles


def _halo_specs(tt, width, T, col_block):
    hb = tt // HALO
    last = T // HALO - 1
    prev = pl.BlockSpec((1, HALO, width), lambda b, i: (b, jnp.maximum(i * hb - 1, 0), col_block))
    cur = pl.BlockSpec((1, tt, width), lambda b, i: (b, i, col_block))
    nxt = pl.BlockSpec((1, HALO, width), lambda b, i: (b, jnp.minimum((i + 1) * hb, last), col_block))
    return prev, cur, nxt


def _pool_kernel(prev_ref, cur_ref, next_ref, wg_ref, sc_ref, o_ref, *,
                 tt, n_lat_tiles, n_tiles, gw, windows):
    ti, seg_tiles = _segment_of_tile(pl.program_id(1), tt, n_lat_tiles, n_tiles)
    has_prev = ti > 0
    has_next = ti < seg_tiles - 1
    t_seg = seg_tiles * tt
    tpos = ti * tt + _iota((tt, 1), 0)

    cur = cur_ref[0]
    prev = prev_ref[0]
    nxt = next_ref[0]
    d_cur = _iota((tt, tt), 1) - _iota((tt, tt), 0)
    d_halo = _iota((tt, HALO), 1) - _iota((tt, HALO), 0)
    d_prev = d_halo - HALO
    d_next = d_halo + tt
    for g, win in enumerate(windows):
        lo_off = -(win // 2)
        hi_off = win - win // 2 - 1
        sl = slice(g * gw, (g + 1) * gw)
        band_c = ((d_cur >= lo_off) & (d_cur <= hi_off)).astype(BF16)
        band_p = ((d_prev >= lo_off) & (d_prev <= hi_off) & has_prev).astype(BF16)
        band_n = ((d_next >= lo_off) & (d_next <= hi_off) & has_next).astype(BF16)
        ug = cur[:, sl]
        tot = _bdot(band_c, ug) + _bdot(band_p, prev[:, sl]) + _bdot(band_n, nxt[:, sl])
        lo = jnp.maximum(tpos + lo_off, 0)
        hi = jnp.minimum(tpos + hi_off + 1, t_seg)
        pooled = tot / (hi - lo).astype(F32) - ug.astype(F32)
        y = _bdot(pooled, wg_ref[g]) * sc_ref[:, sl]
        o_ref[0, :, sl] = y.astype(o_ref.dtype)


def _pool(p_arr, col_block, w_group, scale, n_lat):
    B, T, _ = p_arr.shape
    G, gw, _ = w_group.shape
    W = G * gw
    tt = _pick(T - n_lat, (256, 128, 64, 32, 16))
    assert n_lat % tt == 0 and tt % HALO == 0 and max(POOL_WINDOWS) <= HALO
    n_tiles = T // tt
    kern = functools.partial(_pool_kernel, tt=tt, n_lat_tiles=n_lat // tt, n_tiles=n_tiles,
                             gw=gw, windows=POOL_WINDOWS)
    prev, cur, nxt = _halo_specs(tt, W, T, col_block)
    return pl.pallas_call(
        kern,
        out_shape=jax.ShapeDtypeStruct((B, T, W), BF16),
        grid=(B, n_tiles),
        in_specs=[prev, cur, nxt,
                  pl.BlockSpec((G, gw, gw), lambda b, i: (0, 0, 0)),
                  pl.BlockSpec((1, W), lambda b, i: (0, 0))],
        out_specs=pl.BlockSpec((1, tt, W), lambda b, i: (b, i, 0)),
        compiler_params=_cparams("parallel", "parallel"),
        name="pool",
    )(p_arr, p_arr, p_arr, w_group, scale.reshape(1, W))


def _head_sum(x, head):
    lanes = x.shape[1]
    blk = min(lanes, V7X_LANES)
    same = (_iota((blk, blk), 0) // head == _iota((blk, blk), 1) // head).astype(BF16)
    parts = [_split_dot(x[:, j:j + blk], same) for j in range(0, lanes, blk)]
    return jnp.concatenate(parts, axis=1) if len(parts) > 1 else parts[0]


def _rwkv_prep_kernel(prev_ref, cur_ref, next_ref, mu_ref, w0_ref, w2_ref, a0_ref, a2_ref, g2_ref,
                      kk_w_ref, ka_ref, rk_ref,
                      r_ref, v_ref, kk_ref, lw_ref, cum_ref, kd_ref, bd_ref, g_ref, bonus_ref, *,
                      tt, n_lat_tiles, n_tiles, W, head, lora, chunk):
    ti, seg_tiles = _segment_of_tile(pl.program_id(1), tt, n_lat_tiles, n_tiles)
    u = cur_ref[0].astype(F32)
    row = _iota((tt, 1), 0)
    before = jnp.where(ti > 0, prev_ref[0, HALO - 1:HALO, :].astype(F32), 0.0)
    after = jnp.where(ti < seg_tiles - 1, next_ref[0, 0:1, :].astype(F32), 0.0)
    u_prev = jnp.where(row == 0, before, pltpu.roll(u, 1, 0))
    u_next = jnp.where(row == tt - 1, after, pltpu.roll(u, tt - 1, 0))
    u = u + (0.5 * (u_prev + u_next) - u) * mu_ref[...]

    r, k, v = u[:, :W], u[:, W:2 * W], u[:, 2 * W:3 * W]
    o1 = 3 * W
    o2 = o1 + lora
    o3 = o2 + lora
    w_lin = w0_ref[...] + _bdot(jnp.tanh(u[:, o1:o2]), w2_ref[...])
    w_log = -(jnp.maximum(-w_lin, 0.0) + jnp.log(1.0 + jnp.exp(-jnp.abs(w_lin)))) - 0.5
    lw = -jnp.exp(w_log)
    lw_ref[0] = lw
    ri, ci = _iota((tt, tt), 0), _iota((tt, tt), 1)
    same_chunk = ri // chunk == ci // chunk
    p1 = lw.astype(BF16)
    p2 = (lw - p1.astype(F32)).astype(BF16)
    p3 = (lw - p1.astype(F32) - p2.astype(F32)).astype(BF16)
    for z, tri in enumerate((same_chunk & (ci <= ri), same_chunk & (ci >= ri))):
        sl = slice(z * W, (z + 1) * W)
        cum_ref[0, :, sl] = _bdot(tri, p1[:, sl]) + _bdot(tri, p2[:, sl]) + _bdot(tri, p3[:, sl])
    a = jax.nn.sigmoid(a0_ref[...] + _bdot(u[:, o2:o3], a2_ref[...]))
    g_ref[0] = _bdot(jax.nn.sigmoid(u[:, o3:]), g2_ref[...]).astype(g_ref.dtype)

    kk = k * kk_w_ref[...]
    kk = kk * lax.rsqrt(jnp.maximum(_head_sum(kk * kk, head), 1e-24))
    k_sum = 0.0
    for z in range(N_DIR):
        a_z = a[:, z * W:(z + 1) * W]
        k_z = k * (1.0 + (a_z - 1.0) * ka_ref[...])
        kd_ref[0, :, z * W:(z + 1) * W] = k_z.astype(kd_ref.dtype)
        bd_ref[0, :, z * W:(z + 1) * W] = (kk * a_z).astype(bd_ref.dtype)
        k_sum = k_sum + k_z
    r_ref[0] = r.astype(r_ref.dtype)
    v_ref[0] = v.astype(v_ref.dtype)
    kk_ref[0] = kk.astype(kk_ref.dtype)
    bonus_ref[0] = (_head_sum(r * k_sum * rk_ref[...], head) * v).astype(bonus_ref.dtype)


def _rwkv_prep(p_rwkv, n_lat, W, head, mu, w0, w2, a0, a2, g2, k_k, k_a, r_k):
    B, T, NS = p_rwkv.shape
    lora = N_DIR * w2.shape[1]
    n_gate = NS - 3 * W - 2 * lora
    tt = _pick(T - n_lat, (256, 128, 64, 32, 16))
    n_tiles = T // tt
    def cat(m):
        z = jnp.zeros_like(m[0])
        return jnp.concatenate([jnp.concatenate([m[0], z], axis=1),
                                jnp.concatenate([z, m[1]], axis=1)], axis=0).astype(BF16)
    g2p = jnp.pad(g2, ((0, n_gate - g2.shape[0]), (0, 0))).astype(BF16)
    mup = jnp.pad(mu, (0, NS - mu.shape[0])).reshape(1, NS)
    kern = functools.partial(_rwkv_prep_kernel, tt=tt, n_lat_tiles=n_lat // tt, n_tiles=n_tiles,
                             W=W, head=head, lora=lora, chunk=RWKV_CHUNK)
    assert tt % RWKV_CHUNK == 0
    prev, cur, nxt = _halo_specs(tt, NS, T, 0)
    full = lambda shp: pl.BlockSpec(shp, lambda b, i: (0,) * len(shp))
    tile = lambda w: pl.BlockSpec((1, tt, w), lambda b, i: (b, i, 0))
    sd = lambda w, dt: jax.ShapeDtypeStruct((B, T, w), dt)
    return pl.pallas_call(
        kern,
        out_shape=(sd(W, BF16), sd(W, BF16), sd(W, BF16), sd(2 * W, F32), sd(2 * W, F32), sd(2 * W, BF16),
                   sd(2 * W, BF16), sd(W, BF16), sd(W, BF16)),
        grid=(B, n_tiles),
        in_specs=[prev, cur, nxt, full((1, NS)), full((1, 2 * W)), full((lora, 2 * W)),
                  full((1, 2 * W)), full((lora, 2 * W)), full((n_gate, W)),
                  full((1, W)), full((1, W)), full((1, W))],
        out_specs=(tile(W), tile(W), tile(W), tile(2 * W), tile(2 * W), tile(2 * W), tile(2 * W), tile(W),
                   tile(W)),
        compiler_params=_cparams("parallel", "parallel"),
        name="rwkv_prep",
    )(p_rwkv, p_rwkv, p_rwkv, mup, w0.reshape(1, 2 * W), cat(w2), a0.reshape(1, 2 * W), cat(a2), g2p,
      k_k.reshape(1, W), k_a.reshape(1, W), r_k.reshape(1, W))


def _rwkv_kernel(rf_ref, vf_ref, kkf_ref, rb_ref, vb_ref, kkb_ref, lwf_ref, cumf_ref, kf_ref, bf_ref,
                 lwb_ref, cumb_ref, kb_ref, bb_ref, yf_ref, yb_ref, g_ref, *, C, N):
    @pl.when(pl.program_id(1) == 0)
    def _():
        g_ref[...] = jnp.zeros_like(g_ref)

    W = rf_ref.shape[2]
    PW = 2 * N
    n_pairs = W // PW
    psl = [slice(p * PW, (p + 1) * PW) for p in range(n_pairs)]
    first = _iota((1, PW), 1) < N
    masks = (first, jnp.logical_not(first))
    same_head = (_iota((PW, PW), 0) // N) == (_iota((PW, PW), 1) // N)
    zero = jnp.zeros((C, PW), F32)
    diff = _iota((C, C), 0) - _iota((C, C), 1)

    dirs = []
    for d, (r_ref, v_ref, kk_ref, lw_ref, cum_ref, k_ref, b_ref) in enumerate((
            (rf_ref, vf_ref, kkf_ref, lwf_ref, cumf_ref, kf_ref, bf_ref),
            (rb_ref, vb_ref, kkb_ref, lwb_ref, cumb_ref, kb_ref, bb_ref))):
        order = diff if d == 0 else -diff
        order2 = jnp.concatenate([order, order], axis=1)
        cum = cum_ref[0]
        pc = cum[C - 1:C, :] if d == 0 else cum[0:1, :]
        k = k_ref[0].astype(F32)
        b = b_ref[0].astype(F32)
        p_inv = jnp.exp(-cum)
        p_hat = jnp.exp(pc - cum)
        dirs.append(dict(
            strict2=order2 > 0, incl2=order2 >= 0,
            r_t=r_ref[0].astype(F32) * jnp.exp(cum), k_t=k * p_inv, b_t=b * p_inv,
            a_t=-kk_ref[0].astype(F32) * jnp.exp(cum - lw_ref[0]),
            b_h=b * p_hat, k_h=k * p_hat, p_c=jnp.exp(pc), v=v_ref[0].astype(F32)))

    items = [(d, p) for p in range(n_pairs) for d in range(N_DIR)]
    v_m, upper, lower = [], [], []
    for d, p in items:
        t, sl = dirs[d], psl[p]
        rhs_bk = jnp.concatenate([t["b_t"][:, sl], t["k_t"][:, sl]], axis=0)
        lhs = jnp.concatenate([jnp.where(m, x[:, sl], 0.0) for m in masks for x in (t["a_t"], t["r_t"])],
                              axis=0)
        amat = _bdot(lhs, rhs_bk, NT)
        for sub in range(2):
            upper.append(jnp.where(t["strict2"], amat[2 * sub * C:(2 * sub + 1) * C], 0.0))
            lower.append(jnp.where(t["incl2"], amat[(2 * sub + 1) * C:(2 * sub + 2) * C], 0.0))
            v_m.append(jnp.where(masks[sub], t["v"][:, sl], 0.0))
    pw, x = [], []
    for i, (d, p) in enumerate(items):
        up0, up1 = upper[2 * i], upper[2 * i + 1]
        a_ak = jnp.where(first, pltpu.roll(up0, C, 1), up1)
        w = _bdot(a_ak, jnp.concatenate([v_m[2 * i], v_m[2 * i + 1]], axis=0))
        w = pltpu.roll(w, N, 1)
        a_p = dirs[d]["a_t"][:, psl[p]]
        x += [jnp.where(first, a_p, w), jnp.where(first, w, a_p)]
        pw.append(jnp.where(first, up0, pltpu.roll(up1, C, 1)))
    span = 1
    while span < C:
        span *= 2
        for i in range(len(items)):
            x0, x1 = x[2 * i], x[2 * i + 1]
            rhs = [jnp.concatenate([x0, zero], axis=1), jnp.concatenate([zero, x1], axis=1)]
            if span < C:
                rhs = [jnp.concatenate([jnp.where(m, pw[i], 0.0), xr], axis=1) for m, xr in zip(masks, rhs)]
            res = _bdot(pw[i], jnp.concatenate(rhs, axis=0))
            if span < C:
                pw[i] = res[:, :PW]
                res = res[:, PW:]
            x[2 * i], x[2 * i + 1] = x0 + res[:, :PW], x1 + res[:, PW:]

    g0 = [g_ref[d * n_pairs + p] for d, p in items]
    uv, rg = [], []
    for i, (d, p) in enumerate(items):
        x0, x1 = x[2 * i], x[2 * i + 1]
        ahat = [jnp.where(masks[0], x0, 0.0), jnp.where(masks[1], x1, 0.0)]
        vhat = [jnp.where(masks[0], pltpu.roll(x0, N, 1), 0.0), jnp.where(masks[1], pltpu.roll(x1, N, 1), 0.0)]
        res = _bdot(jnp.concatenate([dirs[d]["r_t"][:, psl[p]]] + ahat, axis=0), g0[i])
        rg.append(res[:C])
        u0 = res[C:2 * C] + vhat[0]
        u1 = res[2 * C:] + vhat[1]
        uv.append(jnp.concatenate([u0, v_m[2 * i], u1, v_m[2 * i + 1]], axis=0).astype(BF16))
    for i, (d, p) in enumerate(items):
        y_ref = yf_ref if d == 0 else yb_ref
        y_ref[0, :, psl[p]] = rg[i] + _bdot(jnp.concatenate([lower[2 * i], lower[2 * i + 1]], axis=1), uv[i])
    for i, (d, p) in enumerate(items):
        t, sl = dirs[d], psl[p]
        bk_h = jnp.concatenate([t["b_h"][:, sl], t["k_h"][:, sl]], axis=0)
        upd = _bdot(jnp.concatenate([bk_h, bk_h], axis=0), uv[i], TN)
        decay = jnp.transpose(jnp.broadcast_to(t["p_c"][:, sl], (PW, PW)))
        g_ref[d * n_pairs + p] = jnp.where(same_head, decay * g0[i] + upd, 0.0)


def _rwkv_scan(r, v, kk, lw, cum, kd, bd, head, n_lat):
    B, T, W = r.shape
    C = RWKV_CHUNK
    assert C == head
    nc = T // C
    nc_ctx = nc - n_lat // C

    def fwd(c):
        return jnp.where(c < nc_ctx, nc - nc_ctx + c, c - nc_ctx)

    def bwd(c):
        return nc - 1 - c

    spec = lambda chunk, col: pl.BlockSpec((1, C, W), lambda b, c: (b, chunk(c), col))
    kern = functools.partial(_rwkv_kernel, C=C, N=head)
    out = jax.ShapeDtypeStruct((B, T, W), F32)
    return pl.pallas_call(
        kern,
        out_shape=(out, out),
        grid=(B, nc),
        in_specs=[spec(fwd, 0)] * 3 + [spec(bwd, 0)] * 3 + [spec(fwd, 0)] * 4 + [spec(bwd, 1)] * 4,
        out_specs=(spec(fwd, 0), spec(bwd, 0)),
        scratch_shapes=[pltpu.VMEM((N_DIR * W // (2 * head), 2 * head, 2 * head), F32)],
        compiler_params=_cparams("parallel", "arbitrary"),
        name="rwkv7_chunk",
    )(r, v, kk, r, v, kk, lw, cum, kd, bd, lw, cum, kd, bd)


def _sigmoid(x):
    return 0.5 * jnp.tanh(0.5 * x) + 0.5


def _merge_kernel(attn_ref, pool_ref, yf_ref, yb_ref, g_ref, bonus_ref, lnw_ref, lnb_ref, gate_ref,
                  wa_ref, wp_ref, wr_ref, o_ref, *, head, D):
    y = yf_ref[0] + yb_ref[0]
    inv_n = 1.0 / head
    dev = y - _head_sum(y, head) * inv_n
    var = _head_sum(dev * dev, head) * inv_n
    yn = dev * lax.rsqrt(var + GN_EPS) * lnw_ref[...] + lnb_ref[...]
    rw = (yn + bonus_ref[0].astype(F32)) * g_ref[0].astype(F32)

    def gate(z):
        return _sigmoid(gate_ref[0, :, z * D:(z + 1) * D].astype(F32))

    out = gate(0) * _bdot(attn_ref[0], wa_ref[...])
    out = out + gate(1) * _bdot(pool_ref[0], wp_ref[...])
    out = out + gate(2) * _bdot(rw, wr_ref[...])
    o_ref[0] = out.astype(o_ref.dtype)


def _merge(attn, pool, y_fwd, y_bwd, g, bonus, ln_w, ln_b, gates, w_a, w_p, w_r, head):
    B, T, _ = attn.shape
    W = g.shape[2]
    D = w_a.shape[1]
    tm = _row_tile(T, 16)
    row = lambda w: pl.BlockSpec((1, tm, w), lambda b, i: (b, i, 0))
    vec = pl.BlockSpec((1, W), lambda b, i: (0, 0))
    wsp = lambda w: pl.BlockSpec(w.shape, lambda b, i: (0, 0), pipeline_mode=pl.Buffered(1))
    return pl.pallas_call(
        functools.partial(_merge_kernel, head=head, D=D),
        out_shape=jax.ShapeDtypeStruct((B, T, D), BF16),
        grid=(B, T // tm),
        in_specs=[row(attn.shape[2]), row(pool.shape[2]), row(W), row(W),
                  row(W), row(W), vec, vec, row(N_BRANCH * D),
                  wsp(w_a), wsp(w_p), wsp(w_r)],
        out_specs=row(D),
        compiler_params=_cparams("parallel", "parallel"),
        name="merge",
    )(attn, pool, y_fwd, y_bwd, g, bonus, ln_w.reshape(1, W), ln_b.reshape(1, W), gates, w_a, w_p, w_r)


def _out_kernel(m_ref, w_ref, x_ref, gpost_ref, gpre_ref, ml_ref, mc_ref, wr_ref,
                x_out, h_out, aff_out, *, tm, n_lat, n_e):
    th = tm // 2
    for half in range(2):
        rows = slice(half * th, (half + 1) * th)
        is_lat, m_lat, m_ctx = _mod_rows(ml_ref, mc_ref, pl.program_id(1) * tm + half * th, th, n_lat)
        mix = _bdot(m_ref[0, rows, :], w_ref[...])
        normed = mix * lax.rsqrt(jnp.mean(mix * mix, axis=-1, keepdims=True) + NORM_EPS) * gpost_ref[...]
        x = x_ref[0, rows, :] + jnp.where(is_lat, m_lat[2:3], m_ctx[2:3]) * normed
        x_out[0, rows, :] = x
        h = _modnorm(x, gpre_ref[...], is_lat, m_lat, m_ctx, 3)
        h_out[0, rows, :] = _pack_halves(h)
        h_hi = h.astype(BF16)
        part = _bdot(h_hi, wr_ref[...]) + _bdot(h - h_hi.astype(F32), wr_ref[...])
        logits = part[:, :V7X_LANES] + part[:, V7X_LANES:]
        logits = jnp.where(_iota(logits.shape, 1) < n_e, logits, -jnp.inf)
        e = jnp.exp(logits - jnp.max(logits, axis=-1, keepdims=True))
        aff_out[0, rows, :] = e / jnp.sum(e, axis=-1, keepdims=True)


def _out_proj(merged, w_out, xs, g_post, g_pre, mod_l, w_router, n_lat):
    B, T, D = xs.shape
    E = w_router.shape[1]
    tm = _pick(T, (544, 272, 256, 128, 320, 64, 32, 16))
    wr = jnp.pad(w_router, ((0, 0), (0, V7X_LANES - E)))
    wr_hi = wr.astype(BF16)
    wr = jnp.concatenate([wr_hi, (wr - wr_hi.astype(F32)).astype(BF16)], axis=1)
    row = lambda w: pl.BlockSpec((1, tm, w), lambda b, i: (b, i, 0))
    vec = pl.BlockSpec((1, D), lambda b, i: (0, 0))
    return pl.pallas_call(
        functools.partial(_out_kernel, tm=tm, n_lat=n_lat, n_e=E),
        out_shape=(jax.ShapeDtypeStruct((B, T, D), F32), jax.ShapeDtypeStruct((B, T, D // 2), jnp.uint32),
                   jax.ShapeDtypeStruct((B, T, V7X_LANES), F32)),
        grid=(B, T // tm),
        in_specs=[row(D), pl.BlockSpec((D, D), lambda b, i: (0, 0)), row(D), vec, vec,
                  pl.BlockSpec((1, 6, D), lambda b, i: (b, 0, 0)),
                  pl.BlockSpec((1, 6, D), lambda b, i: (B, 0, 0)),
                  pl.BlockSpec((D, 2 * V7X_LANES), lambda b, i: (0, 0))],
        out_specs=(row(D), row(D // 2), row(V7X_LANES)),
        compiler_params=_cparams("parallel", "parallel"),
        name="out_proj",
    )(merged, w_out, xs, g_post.reshape(1, D), g_pre.reshape(1, D), mod_l, mod_l, wr)


def _excl_prefix(flags, blk):
    n = flags.shape[-1]
    upper = (_iota((blk, blk), 0) < _iota((blk, blk), 1)).astype(BF16)
    outs = []
    carry = jnp.zeros((flags.shape[0], 1), F32)
    for j in range(n // blk):
        seg = flags[:, j * blk:(j + 1) * blk]
        outs.append(_bdot(seg, upper) + carry)
        carry = carry + jnp.sum(seg, axis=-1, keepdims=True)
    return jnp.concatenate(outs, axis=-1) if len(outs) > 1 else outs[0]


def _topk_kernel(aff_ref, sel_ref, idx_ref, *, cap, blk):
    bits = lax.bitcast_convert_type(aff_ref[0], jnp.int32)
    E = bits.shape[0]

    def body(i, tau):
        cand = tau | jnp.left_shift(jnp.int32(1), 30 - i)
        cnt = jnp.sum((bits >= cand).astype(jnp.int32), axis=-1, keepdims=True)
        return jnp.where(cnt >= cap, cand, tau)

    tau = lax.fori_loop(0, 31, body, jnp.zeros((E, 1), jnp.int32))
    gt = bits > tau
    eq = bits == tau
    need = (cap - jnp.sum(gt.astype(jnp.int32), axis=-1, keepdims=True)).astype(F32)
    eq_rank = _excl_prefix(eq.astype(F32), blk)
    sel = gt | (eq & (eq_rank < need))
    pos = _excl_prefix(sel.astype(F32), blk)
    slots = jnp.where(sel, pos.astype(jnp.int32), -1)
    sel_ref[0] = slots
    n = slots.shape[1]
    tok = _iota((8, n), 1)
    digit = _iota((8, n), 0)
    digits = jnp.where(digit == 0, tok // TOKEN_RADIX, jnp.where(digit == 1, tok % TOKEN_RADIX, 0))
    for e in range(E):
        onehot = slots[e:e + 1, :] == _iota((cap, n), 0)
        d = _bdot(digits.astype(F32), onehot.astype(F32), NT)
        idx_ref[0, e:e + 1, :] = (d[0:1] * TOKEN_RADIX + d[1:2]).astype(jnp.int32)


def _topk_slots(aff_t, cap):
    B, E, n = aff_t.shape
    assert n <= TOKEN_RADIX * 256
    blk = _pick(n, (512, 256, 128))
    return pl.pallas_call(
        functools.partial(_topk_kernel, cap=cap, blk=blk),
        out_shape=(jax.ShapeDtypeStruct((B, E, n), jnp.int32),
                   jax.ShapeDtypeStruct((B, E, cap), jnp.int32)),
        grid=(B,),
        in_specs=[pl.BlockSpec((1, E, n), lambda b: (b, 0, 0))],
        out_specs=(pl.BlockSpec((1, E, n), lambda b: (b, 0, 0)),
                   pl.BlockSpec((1, E, cap), lambda b: (b, 0, 0))),
        compiler_params=_cparams("parallel"),
        name="expert_topk",
    )(aff_t)


def _gather_kernel(idx_ref, h_ref, xe_ref, *, cap, n_e):
    base = (pl.program_id(0) * n_e + pl.program_id(1)) * cap

    def body(s, carry):
        t = idx_ref[base + s]
        xe_ref[0, 0, pl.ds(s, 1), :] = h_ref[0, pl.ds(t, 1), :]
        return carry

    lax.fori_loop(0, cap, body, 0, unroll=8)


def _gather(idx, hp, row_block, n):
    B, E, cap = idx.shape
    dh = hp.shape[2]
    grid_spec = pltpu.PrefetchScalarGridSpec(
        num_scalar_prefetch=1, grid=(B, E),
        in_specs=[pl.BlockSpec((1, n, dh), lambda b, e, idx_ref: (b, row_block, 0))],
        out_specs=pl.BlockSpec((1, 1, cap, dh), lambda b, e, idx_ref: (b, e, 0, 0)))
    return pl.pallas_call(
        functools.partial(_gather_kernel, cap=cap, n_e=E),
        out_shape=jax.ShapeDtypeStruct((B, E, cap, dh), jnp.uint32),
        grid_spec=grid_spec,
        compiler_params=_cparams("parallel", "arbitrary"),
        name="expert_gather",
    )(idx.reshape(-1), hp)


def _pack_halves(h):
    bits = lax.bitcast_convert_type(h.astype(BF16).astype(F32), jnp.uint32)
    half = h.shape[1] // 2
    return (bits[:, :half] >> 16) | (bits[:, half:] & jnp.uint32(0xFFFF0000))


def _unpack_halves(p):
    lo = lax.bitcast_convert_type(p << 16, F32)
    hi = lax.bitcast_convert_type(p & jnp.uint32(0xFFFF0000), F32)
    return lo, hi


def _ffn_kernel(*refs, n_sets):
    xe_refs = refs[:n_sets]
    wg_ref, wu_ref, wd_ref = refs[n_sets:n_sets + 3]
    ye_refs = refs[n_sets + 3:2 * n_sets + 3]
    wgb_ref, wub_ref, wdb_ref = refs[2 * n_sets + 3:]

    @pl.when(pl.program_id(2) == 0)
    def _():
        wgb_ref[...] = wg_ref[0, 0].astype(wgb_ref.dtype)
        wub_ref[...] = wu_ref[0, 0].astype(wub_ref.dtype)
        wdb_ref[...] = wd_ref[0, 0].astype(wdb_ref.dtype)

    for xe_ref, ye_ref in zip(xe_refs, ye_refs):
        lo, hi = _unpack_halves(xe_ref[0, 0])
        half = lo.shape[1]
        gate = _bdot(lo, wgb_ref[:half, :]) + _bdot(hi, wgb_ref[half:, :])
        up = _bdot(lo, wub_ref[:half, :]) + _bdot(hi, wub_ref[half:, :])
        hid = gate * _sigmoid(gate) * up
        ye_ref[0, 0, 0] = _bdot(hid, wdb_ref[...]).astype(ye_ref.dtype)


def _expert_ffn(xes, w_gate, w_up, w_down, layer):
    B, E, _, dh = xes[0].shape
    D = 2 * dh
    FF = w_gate.shape[3]
    fs = FF // FFN_SLABS
    n_sets = len(xes)
    return pl.pallas_call(
        functools.partial(_ffn_kernel, n_sets=n_sets),
        out_shape=tuple(jax.ShapeDtypeStruct((FFN_SLABS, B, E, xe.shape[2], D), BF16) for xe in xes),
        grid=(E, FFN_SLABS, B),
        in_specs=[pl.BlockSpec((1, 1, xe.shape[2], dh), lambda e, s, b: (b, e, 0, 0)) for xe in xes]
        + [pl.BlockSpec((1, 1, D, fs), lambda e, s, b: (layer, e, 0, s)),
           pl.BlockSpec((1, 1, D, fs), lambda e, s, b: (layer, e, 0, s)),
           pl.BlockSpec((1, 1, fs, D), lambda e, s, b: (layer, e, s, 0))],
        out_specs=tuple(pl.BlockSpec((1, 1, 1, xe.shape[2], D), lambda e, s, b: (s, b, e, 0, 0))
                        for xe in xes),
        scratch_shapes=[pltpu.VMEM((D, fs), BF16), pltpu.VMEM((D, fs), BF16), pltpu.VMEM((fs, D), BF16)],
        compiler_params=_cparams("parallel", "parallel", "arbitrary"),
        name="expert_ffn",
    )(*xes, w_gate, w_up, w_down)


def _scatter_kernel(selt_ref, aff_ref, ye_ref, o_ref, *, cap, n_e):
    sel_all = selt_ref[0]
    aff_all = aff_ref[0]
    slot = _iota((1, cap), 1)
    acc = None
    for e in range(n_e):
        onehot = sel_all[:, e:e + 1] == slot
        ye = ye_ref[0, 0, e].astype(F32)
        for s in range(1, ye_ref.shape[0]):
            ye = ye + ye_ref[s, 0, e].astype(F32)
        term = aff_all[:, e:e + 1] * _bdot(onehot, ye)
        acc = term if acc is None else acc + term
    o_ref[0] = acc


def _scatter(sel, aff, row_block, ye):
    B, E, n = sel.shape
    n_slab, cap, D = ye.shape[0], ye.shape[3], ye.shape[4]
    td = _pick(D, (256, 128))
    return pl.pallas_call(
        functools.partial(_scatter_kernel, cap=cap, n_e=E),
        out_shape=jax.ShapeDtypeStruct((B, n, D), F32),
        grid=(B, D // td),
        in_specs=[pl.BlockSpec((1, n, E), lambda b, j: (b, 0, 0)),
                  pl.BlockSpec((1, n, aff.shape[2]), lambda b, j: (b, row_block, 0)),
                  pl.BlockSpec((n_slab, 1, E, cap, td), lambda b, j: (0, b, 0, 0, j))],
        out_specs=pl.BlockSpec((1, n, td), lambda b, j: (b, 0, j)),
        compiler_params=_cparams("parallel", "parallel"),
        name="expert_scatter",
    )(jnp.swapaxes(sel, 1, 2), aff, ye)


def _expert_choice(hb, aff, sets, w_gate, w_up, w_down, layer):
    E = w_gate.shape[1]
    sels, xes = [], []
    for row_block, n in sets:
        aff_t = jnp.swapaxes(aff[:, row_block * n:(row_block + 1) * n, :E], 1, 2)
        sel, idx = _topk_slots(aff_t, EC_FACTOR * n // E)
        sels.append(sel)
        xes.append(_gather(idx, hb, row_block, n))
    yes = _expert_ffn(xes, w_gate, w_up, w_down, layer)
    return [_scatter(sel, aff, row_block, ye) for sel, (row_block, n), ye in zip(sels, sets, yes)]


def _rms(x, g):
    return x * lax.rsqrt(jnp.mean(x * x, axis=-1, keepdims=True) + NORM_EPS) * g


def _rope_tables(n_lat, n_ctx, head_dim):
    rows = n_lat // GRID_W
    row = jnp.repeat(jnp.arange(rows), GRID_W).astype(F32)
    col = (jnp.arange(rows * GRID_W) % GRID_W).astype(F32)
    half = head_dim // 2
    inv = ROPE_THETA ** (-jnp.arange(0, half, 2, dtype=F32) / half)
    ar, ac = row[:, None] * inv, col[:, None] * inv
    cos = jnp.concatenate([jnp.cos(ar), jnp.cos(ar), jnp.cos(ac), jnp.cos(ac)], axis=-1)
    sin = jnp.concatenate([-jnp.sin(ar), jnp.sin(ar), -jnp.sin(ac), jnp.sin(ac)], axis=-1)
    cos = jnp.concatenate([cos, jnp.ones((n_ctx, head_dim), F32)], axis=0)
    sin = jnp.concatenate([sin, jnp.zeros((n_ctx, head_dim), F32)], axis=0)
    return cos, sin


def _pad_cols(w, mult):
    return jnp.pad(w, ((0, 0), (0, (-w.shape[1]) % mult)))


def kernel(x, c, ctx, c_ctx, w_mod, b_mod, norm_pre, norm_post, w_in, q_norm, k_norm, w_attn_o, w_pool_group, pool_scale, w_pool_o, rwkv_mu, rwkv_w0, rwkv_w2, rwkv_a0, rwkv_a2, rwkv_g2, rwkv_k_k, rwkv_k_a, rwkv_r_k, rwkv_ln_w, rwkv_ln_b, w_rwkv_o, w_out, w_router, w_exp_gate, w_exp_up, w_exp_down):
    B, n_lat, D = x.shape
    n_ctx = ctx.shape[1]
    T = n_lat + n_ctx
    depth = w_mod.shape[0]
    hd = q_norm.shape[1]
    attn_w = w_attn_o.shape[1]
    pool_w = w_pool_o.shape[1]
    rwkv_w = w_rwkv_o.shape[1]
    n_shift = rwkv_mu.shape[1]
    n_in = w_in.shape[2]
    kv_w = (n_in - attn_w - pool_w - n_shift - N_BRANCH * D) // 2
    head = rwkv_r_k.shape[2]
    col_k = attn_w
    col_v = col_k + kv_w
    col_pool = col_v + kv_w
    col_r = col_pool + pool_w
    col_gate = col_r + n_shift
    hq, hkv = attn_w // hd, kv_w // hd
    assert n_lat % (hq // hkv * hd) == 0 and pool_w % (hq // hkv * hd) == 0 and n_ctx % RWKV_CHUNK == 0

    cos, sin = _rope_tables(n_lat, n_ctx, hd)
    s_all = jnp.concatenate([jax.nn.silu(c), jax.nn.silu(c_ctx)[None]], axis=0)
    s_all = jnp.pad(s_all, ((0, (-s_all.shape[0]) % 8), (0, 0)))
    mod = _modulation(s_all, w_mod, b_mod)[:, :B + 1].reshape(depth, B + 1, 6, D)
    xs = jnp.concatenate([x, ctx], axis=1)
    (h,) = _stream_update(xs, None, None, None, mod[0], norm_pre[0, 0], mod[0], n_lat, T)

    for l in range(depth):
        keep_ctx = l < depth - 1
        wl = w_in[l]
        w_a = jnp.concatenate([wl[:, col_pool:col_r], wl[:, :col_pool]], axis=1).astype(BF16)
        w_b = _pad_cols(wl[:, col_r:col_gate], 3 * V7X_LANES).astype(BF16)
        w_c = wl[:, col_gate:].astype(BF16)
        p_a = _in_proj(h, w_a)
        p_b = _in_proj(h, w_b)
        p_c = _in_proj(h, w_c)

        qkv = (p_a, pool_w, pool_w + attn_w, pool_w + attn_w + kv_w, cos, sin, q_norm[l], k_norm[l], hd, hq, hkv)
        attn = _attention(*qkv, (0, n_lat), (0, T))
        attn = _attention(*qkv, (n_lat, n_ctx), (n_lat, n_ctx), out=attn)
        pool = _pool(p_a, 0, w_pool_group[l].astype(BF16), pool_scale[l], n_lat)
        r, v, kk, lw, cum, kd, bd, g, bonus = _rwkv_prep(
            p_b, n_lat, rwkv_w, head, rwkv_mu[l], rwkv_w0[l], rwkv_w2[l], rwkv_a0[l], rwkv_a2[l],
            rwkv_g2[l], rwkv_k_k[l], rwkv_k_a[l], rwkv_r_k[l])
        y_fwd, y_bwd = _rwkv_scan(r, v, kk, lw, cum, kd, bd, head, n_lat)
        merged = _merge(attn, pool, y_fwd, y_bwd, g, bonus, rwkv_ln_w[l], rwkv_ln_b[l], p_c,
                        w_attn_o[l].astype(BF16), w_pool_o[l].astype(BF16), w_rwkv_o[l].astype(BF16), head)
        xs, hb, aff = _out_proj(merged, w_out[l].astype(BF16), xs, norm_post[l, 0], norm_pre[l, 1],
                                mod[l], w_router[l], n_lat)

        sets = [(0, n_lat)] + ([(n_lat // n_ctx, n_ctx)] if keep_ctx else [])
        mixed = _expert_choice(hb, aff, sets, w_exp_gate, w_exp_up, w_exp_down, l)
        f_lat = mixed[0]
        if keep_ctx:
            f_ctx = mixed[1]
            xs, h = _stream_update(xs, f_lat, f_ctx, norm_post[l, 1], mod[l], norm_pre[l + 1, 0],
                                   mod[l + 1], n_lat, T)
        else:
            (xs,) = _stream_update(xs, f_lat, None, norm_post[l, 1], mod[l], None, None, n_lat, n_lat)
    return xs
```

```python
import functools

import jax
import jax.numpy as jnp
from jax import lax
from jax.experimental import pallas as pl
from jax.experimental.pallas import tpu as pltpu

F32 = jnp.float32
BF16 = jnp.bfloat16
HIGHEST = lax.Precision.HIGHEST

GRID_W = 64
NORM_EPS = 1e-6
ROPE_THETA = 10000.0
POOL_WINDOWS = (2, 4, 8, 16)
GN_EPS = 64e-5
EC_FACTOR = 2
N_DIR = 2
N_BRANCH = 3
RWKV_CHUNK = 64
HALO = 16

LOG2_E = 1.4426950408889634
DECAY_FLOOR = 0.6065306597126334
ATTN_SHIFT_LIMIT = 60.0
FFN_SLABS = 2
TOKEN_RADIX = 64

V7X_LANES = 128
VMEM_LIMIT = 52 * 1024 * 1024

NT = (((1,), (1,)), ((), ()))
TN = (((0,), (0,)), ((), ()))


def _pick(n, cands):
    for c in cands:
        if c <= n and n % c == 0:
            return c
    return n


def _cparams(*sem):
    return pltpu.CompilerParams(dimension_semantics=sem, vmem_limit_bytes=VMEM_LIMIT)


def _bdot(x, y, dn=None):
    x = x.astype(BF16)
    y = y.astype(BF16)
    if dn is None:
        return jnp.dot(x, y, preferred_element_type=F32)
    return lax.dot_general(x, y, dn, preferred_element_type=F32)


def _split_dot(x, y):
    hi = x.astype(BF16)
    lo = x - hi.astype(F32)
    return _bdot(hi, y) + _bdot(lo, y)


def _iota(shape, dim):
    return lax.broadcasted_iota(jnp.int32, shape, dim)


def _row_tile(T, min_parts=8):
    for parts in range(min_parts, T // 16 + 1):
        if T % parts == 0 and (T // parts) % 16 == 0:
            return T // parts
    return T


def _pieces(x):
    p1 = x.astype(BF16)
    r1 = x - p1.astype(F32)
    p2 = r1.astype(BF16)
    p3 = (r1 - p2.astype(F32)).astype(BF16)
    return p1, p2, p3


def _mod_kernel(s_ref, w_ref, b_ref, o_ref):
    s1, s2, s3 = (p.astype(F32) for p in _pieces(s_ref[...]))
    w1, w2, w3 = _pieces(w_ref[0])
    R = s1.shape[0]
    a = _bdot(jnp.concatenate([s1, s2, s3], axis=0), w1)
    b = _bdot(jnp.concatenate([s1, s2], axis=0), w2)
    c = _bdot(s1, w3)

    @pl.when(pl.program_id(1) == 0)
    def _():
        o_ref[0] = jnp.broadcast_to(b_ref[0], o_ref.shape[1:])

    o_ref[0] += (a[:R] + a[R:2 * R] + a[2 * R:]) + (b[:R] + b[R:]) + c


def _modulation(s, w_mod, b_mod):
    L, D, N = w_mod.shape
    R = s.shape[0]
    tk = _pick(D, (128,))
    return pl.pallas_call(
        _mod_kernel,
        out_shape=jax.ShapeDtypeStruct((L, R, N), F32),
        grid=(L, D // tk),
        in_specs=[pl.BlockSpec((R, tk), lambda l, k: (0, k)),
                  pl.BlockSpec((1, tk, N), lambda l, k: (l, k, 0)),
                  pl.BlockSpec((1, 1, N), lambda l, k: (l, 0, 0))],
        out_specs=pl.BlockSpec((1, R, N), lambda l, k: (l, 0, 0)),
        compiler_params=_cparams("parallel", "arbitrary"),
        name="adaln_mod",
    )(s, w_mod, b_mod.reshape(L, 1, N))


def _mod_rows(m_lat_ref, m_ctx_ref, row0, tm, n_lat):
    is_lat = (row0 + _iota((tm, 1), 0)) < n_lat
    return is_lat, m_lat_ref[0], m_ctx_ref[0]


def _modnorm(x, g, is_lat, m_lat, m_ctx, i_shift):
    shift = jnp.where(is_lat, m_lat[i_shift:i_shift + 1], m_ctx[i_shift:i_shift + 1])
    scale = jnp.where(is_lat, m_lat[i_shift + 1:i_shift + 2], m_ctx[i_shift + 1:i_shift + 2])
    y = x * lax.rsqrt(jnp.mean(x * x, axis=-1, keepdims=True) + NORM_EPS) * g
    return y * (1.0 + scale) + shift


def _small_row_tile(T, n_ctx):
    return _pick(n_ctx, (256, 128, 64, 32, 16))


def _stream_kernel(*refs, tm, n_lat, has_f, has_ctx_f, emit_h):
    refs = list(refs)
    x_ref = refs.pop(0)
    fl_ref = refs.pop(0) if has_f else None
    fc_ref = refs.pop(0) if has_ctx_f else None
    gpost_ref = refs.pop(0) if has_f else None
    ml_ref, mc_ref = refs.pop(0), refs.pop(0)
    if emit_h:
        gpre_ref, mln_ref, mcn_ref = refs.pop(0), refs.pop(0), refs.pop(0)
    x_out = refs.pop(0) if has_f else None
    h_out = refs.pop(0) if emit_h else None

    is_lat = (pl.program_id(1) * tm + _iota((tm, 1), 0)) < n_lat
    x = x_ref[0]
    if has_f:
        f = fl_ref[0]
        if has_ctx_f:
            f = jnp.where(is_lat, f, fc_ref[0])
        normed = f * lax.rsqrt(jnp.mean(f * f, axis=-1, keepdims=True) + NORM_EPS) * gpost_ref[...]
        x = x + jnp.where(is_lat, ml_ref[0][5:6], mc_ref[0][5:6]) * normed
        x_out[0] = x
    if emit_h:
        h_out[0] = _modnorm(x, gpre_ref[...], is_lat, mln_ref[0], mcn_ref[0], 0).astype(h_out.dtype)


def _stream_update(xs, f_lat, f_ctx, g_post, mod_l, g_pre_next, mod_next, n_lat, n_rows):
    B, T, D = xs.shape
    n_ctx = T - n_lat
    tm = _small_row_tile(T, n_ctx)
    has_f, has_ctx_f, emit_h = f_lat is not None, f_ctx is not None, g_pre_next is not None
    n_lt = n_lat // tm
    row = pl.BlockSpec((1, tm, D), lambda b, i: (b, i, 0))
    vec = pl.BlockSpec((1, D), lambda b, i: (0, 0))
    m_l = pl.BlockSpec((1, 6, D), lambda b, i: (b, 0, 0))
    m_c = pl.BlockSpec((1, 6, D), lambda b, i: (B, 0, 0))
    args, specs = [xs], [row]
    if has_f:
        args.append(f_lat)
        specs.append(pl.BlockSpec((1, tm, D), lambda b, i: (b, jnp.minimum(i, n_lt - 1), 0)))
    if has_ctx_f:
        args.append(f_ctx)
        specs.append(pl.BlockSpec((1, tm, D), lambda b, i: (b, jnp.maximum(i - n_lt, 0), 0)))
    if has_f:
        args.append(g_post.reshape(1, D))
        specs.append(vec)
    args += [mod_l, mod_l]
    specs += [m_l, m_c]
    if emit_h:
        args += [g_pre_next.reshape(1, D), mod_next, mod_next]
        specs += [vec, m_l, m_c]
    out_shape, out_specs = [], []
    if has_f:
        out_shape.append(jax.ShapeDtypeStruct((B, n_rows, D), F32))
        out_specs.append(row)
    if emit_h:
        out_shape.append(jax.ShapeDtypeStruct((B, n_rows, D), BF16))
        out_specs.append(row)
    kern = functools.partial(_stream_kernel, tm=tm, n_lat=n_lat, has_f=has_f, has_ctx_f=has_ctx_f,
                             emit_h=emit_h)
    return pl.pallas_call(
        kern, out_shape=tuple(out_shape), grid=(B, n_rows // tm), in_specs=specs,
        out_specs=tuple(out_specs), compiler_params=_cparams("parallel", "parallel"),
        name="stream_update",
    )(*args)


def _in_kernel(h_ref, w_ref, o_ref):
    o_ref[0] = _bdot(h_ref[0], w_ref[...]).astype(o_ref.dtype)


def _in_proj(h, w):
    B, T, D = h.shape
    N = w.shape[1]
    tm = _row_tile(T, 4)
    tn = _pick(N, (1280, 1024, 896, 768, 640, 512, 384, 256, 128))
    return pl.pallas_call(
        _in_kernel,
        out_shape=jax.ShapeDtypeStruct((B, T, N), BF16),
        grid=(B, T // tm, N // tn),
        in_specs=[pl.BlockSpec((1, tm, D), lambda b, i, j: (b, i, 0)),
                  pl.BlockSpec((D, tn), lambda b, i, j: (0, j))],
        out_specs=pl.BlockSpec((1, tm, tn), lambda b, i, j: (b, i, j)),
        compiler_params=_cparams("parallel", "parallel", "arbitrary"),
        name="in_proj",
    )(h, w)


def _rot(x, cos, sin):
    qd = x.shape[1] // 4
    first = (_iota((1, x.shape[1]), 1) // qd) % 2 == 0
    swapped = jnp.where(first, pltpu.roll(x, x.shape[1] - qd, 1), pltpu.roll(x, qd, 1))
    return x * cos + swapped * sin


def _head_norm(x, g):
    return x * lax.rsqrt(jnp.mean(x * x, axis=-1, keepdims=True) + NORM_EPS) * g


def _attn_kernel(bound_ref, q_ref, k_ref, v_ref, cq_ref, sq_ref, ck_ref, sk_ref, qn_ref, kn_ref, *rest,
                 hd, group, scale):
    o_ref, kp_ref, vp_ref = rest[-3:]
    unit = jnp.where(_iota((1, hd), 1) == 0, 1.0, 0.0)

    @pl.when(pl.program_id(2) == 0)
    def _():
        k = _rot(_head_norm(k_ref[0].astype(F32), kn_ref[...]), ck_ref[...], sk_ref[...])
        kp_ref[...] = jnp.concatenate([k, jnp.broadcast_to(unit, k.shape)], axis=1).astype(kp_ref.dtype)
        ones = jnp.ones(v_ref.shape[1:], vp_ref.dtype)
        vp_ref[...] = jnp.concatenate([v_ref[0].astype(vp_ref.dtype), ones], axis=1)

    bound = bound_ref[0]

    def attend(shift_in_matmul):
        for g in range(group):
            q = _head_norm(q_ref[0, :, g * hd:(g + 1) * hd].astype(F32), qn_ref[...])
            q = _rot(q, cq_ref[...], sq_ref[...]) * scale
            if shift_in_matmul:
                q_aug = jnp.concatenate([q, jnp.broadcast_to(unit * (-bound), q.shape)], axis=1)
                p = jnp.exp2(_bdot(q_aug, kp_ref[...], NT))
            else:
                s = _bdot(q, kp_ref[:, :hd], NT)
                p = jnp.exp2(s - jnp.max(s, axis=-1, keepdims=True))
            ov = _bdot(p, vp_ref[...])
            o_ref[0, :, g * hd:(g + 1) * hd] = (ov[:, :hd] / ov[:, hd:hd + 1]).astype(o_ref.dtype)

    fast = bound < ATTN_SHIFT_LIMIT
    pl.when(fast)(functools.partial(attend, True))
    pl.when(jnp.logical_not(fast))(functools.partial(attend, False))


def _attention(p_qkv, q_col, k_col, v_col, cos, sin, q_norm, k_norm, hd, hq, hkv, q_rows, key_rows, out=None):
    B, T, _ = p_qkv.shape
    group = hq // hkv
    gw = group * hd
    (q0, nq), (k0, nk) = q_rows, key_rows
    tq = next(t for t in (512, 256, 128, 64, 32, 16) if nq % t == 0 and q0 % t == 0)
    assert k0 % nk == 0
    kern = functools.partial(_attn_kernel, hd=hd, group=group, scale=hd ** -0.5 * LOG2_E)
    qb, kb = q0 // tq, k0 // nk
    tab_q = pl.BlockSpec((tq, hd), lambda b, h, i, s: (i + qb, 0))
    tab_k = pl.BlockSpec((nk, hd), lambda b, h, i, s: (kb, 0))
    vec = pl.BlockSpec((1, hd), lambda b, h, i, s: (0, 0))
    bound = (1.02 * hd * hd ** -0.5 * LOG2_E) * jnp.max(jnp.abs(q_norm)) * jnp.max(jnp.abs(k_norm))
    in_specs = [pl.BlockSpec((1, tq, gw), lambda b, h, i, s: (b, i + qb, q_col // gw + h)),
                pl.BlockSpec((1, nk, hd), lambda b, h, i, s: (b, kb, k_col // hd + h)),
                pl.BlockSpec((1, nk, hd), lambda b, h, i, s: (b, kb, v_col // hd + h)),
                tab_q, tab_q, tab_k, tab_k, vec, vec]
    args = [bound.reshape(1).astype(F32), p_qkv, p_qkv, p_qkv, cos, sin, cos, sin,
            q_norm.reshape(1, hd), k_norm.reshape(1, hd)]
    aliases = {}
    if out is not None:
        in_specs.append(pl.BlockSpec(memory_space=pl.ANY))
        args.append(out)
        aliases = {len(args) - 1: 0}
    grid_spec = pltpu.PrefetchScalarGridSpec(
        num_scalar_prefetch=1, grid=(B, hkv, nq // tq), in_specs=in_specs,
        out_specs=pl.BlockSpec((1, tq, gw), lambda b, h, i, s: (b, i + qb, h)),
        scratch_shapes=[pltpu.VMEM((nk, 2 * hd), BF16), pltpu.VMEM((nk, 2 * hd), BF16)])
    return pl.pallas_call(
        kern,
        out_shape=jax.ShapeDtypeStruct((B, T, hq * hd), BF16),
        grid_spec=grid_spec,
        input_output_aliases=aliases,
        compiler_params=_cparams("parallel", "parallel", "arbitrary"),
        name="attention",
    )(*args)


def _segment_of_tile(i, tt, n_lat_tiles, n_tiles):
    is_lat = i < n_lat_tiles
    ti = jnp.where(is_lat, i, i - n_lat_tiles)
    seg_tiles = jnp.where(is_lat, n_lat_tiles, n_tiles - n_lat_tiles)
    return ti, seg_tiles


def _halo_specs(tt, width, T, col_block):
    hb = tt // HALO
    last = T // HALO - 1
    prev = pl.BlockSpec((1, HALO, width), lambda b, i: (b, jnp.maximum(i * hb - 1, 0), col_block))
    cur = pl.BlockSpec((1, tt, width), lambda b, i: (b, i, col_block))
    nxt = pl.BlockSpec((1, HALO, width), lambda b, i: (b, jnp.minimum((i + 1) * hb, last), col_block))
    return prev, cur, nxt


def _pool_kernel(prev_ref, cur_ref, next_ref, wg_ref, sc_ref, o_ref, *,
                 tt, n_lat_tiles, n_tiles, gw, windows):
    ti, seg_tiles = _segment_of_tile(pl.program_id(1), tt, n_lat_tiles, n_tiles)
    has_prev = ti > 0
    has_next = ti < seg_tiles - 1
    t_seg = seg_tiles * tt
    tpos = ti * tt + _iota((tt, 1), 0)

    cur = cur_ref[0]
    prev = prev_ref[0]
    nxt = next_ref[0]
    d_cur = _iota((tt, tt), 1) - _iota((tt, tt), 0)
    d_halo = _iota((tt, HALO), 1) - _iota((tt, HALO), 0)
    d_prev = d_halo - HALO
    d_next = d_halo + tt
    for g, win in enumerate(windows):
        lo_off = -(win // 2)
        hi_off = win - win // 2 - 1
        sl = slice(g * gw, (g + 1) * gw)
        band_c = ((d_cur >= lo_off) & (d_cur <= hi_off)).astype(BF16)
        band_p = ((d_prev >= lo_off) & (d_prev <= hi_off) & has_prev).astype(BF16)
        band_n = ((d_next >= lo_off) & (d_next <= hi_off) & has_next).astype(BF16)
        ug = cur[:, sl]
        tot = _bdot(band_c, ug) + _bdot(band_p, prev[:, sl]) + _bdot(band_n, nxt[:, sl])
        lo = jnp.maximum(tpos + lo_off, 0)
        hi = jnp.minimum(tpos + hi_off + 1, t_seg)
        pooled = tot / (hi - lo).astype(F32) - ug.astype(F32)
        y = _bdot(pooled, wg_ref[g]) * sc_ref[:, sl]
        o_ref[0, :, sl] = y.astype(o_ref.dtype)


def _pool(p_arr, col_block, w_group, scale, n_lat):
    B, T, _ = p_arr.shape
    G, gw, _ = w_group.shape
    W = G * gw
    tt = _pick(T - n_lat, (256, 128, 64, 32, 16))
    assert n_lat % tt == 0 and tt % HALO == 0 and max(POOL_WINDOWS) <= HALO
    n_tiles = T // tt
    kern = functools.partial(_pool_kernel, tt=tt, n_lat_tiles=n_lat // tt, n_tiles=n_tiles,
                             gw=gw, windows=POOL_WINDOWS)
    prev, cur, nxt = _halo_specs(tt, W, T, col_block)
    return pl.pallas_call(
        kern,
        out_shape=jax.ShapeDtypeStruct((B, T, W), BF16),
        grid=(B, n_tiles),
        in_specs=[prev, cur, nxt,
                  pl.BlockSpec((G, gw, gw), lambda b, i: (0, 0, 0)),
                  pl.BlockSpec((1, W), lambda b, i: (0, 0))],
        out_specs=pl.BlockSpec((1, tt, W), lambda b, i: (b, i, 0)),
        compiler_params=_cparams("parallel", "parallel"),
        name="pool",
    )(p_arr, p_arr, p_arr, w_group, scale.reshape(1, W))


def _head_sum(x, head):
    lanes = x.shape[1]
    blk = min(lanes, V7X_LANES)
    same = (_iota((blk, blk), 0) // head == _iota((blk, blk), 1) // head).astype(BF16)
    parts = [_split_dot(x[:, j:j + blk], same) for j in range(0, lanes, blk)]
    return jnp.concatenate(parts, axis=1) if len(parts) > 1 else parts[0]


def _rwkv_prep_kernel(prev_ref, cur_ref, next_ref, mu_ref, w0_ref, w2_ref, a0_ref, a2_ref, g2_ref,
                      kk_w_ref, ka_ref, rk_ref,
                      r_ref, v_ref, kk_ref, lw_ref, cum_ref, kd_ref, bd_ref, g_ref, bonus_ref, *,
                      tt, n_lat_tiles, n_tiles, W, head, lora, chunk):
    ti, seg_tiles = _segment_of_tile(pl.program_id(1), tt, n_lat_tiles, n_tiles)
    u = cur_ref[0].astype(F32)
    row = _iota((tt, 1), 0)
    before = jnp.where(ti > 0, prev_ref[0, HALO - 1:HALO, :].astype(F32), 0.0)
    after = jnp.where(ti < seg_tiles - 1, next_ref[0, 0:1, :].astype(F32), 0.0)
    down, up = pltpu.roll(u, 1, 0), pltpu.roll(u, tt - 1, 0)
    u_prev = jnp.concatenate([jnp.where(row[:8] == 0, before, down[:8]), down[8:]], axis=0)
    u_next = jnp.concatenate([up[:tt - 8], jnp.where(row[tt - 8:] == tt - 1, after, up[tt - 8:])], axis=0)
    mu = mu_ref[...]
    u = u * (1.0 - mu) + (u_prev + u_next) * (0.5 * mu)

    r, k, v = u[:, :W], u[:, W:2 * W], u[:, 2 * W:3 * W]
    o1 = 3 * W
    o2 = o1 + lora
    o3 = o2 + lora
    w_lin = w0_ref[...] + _bdot(jnp.tanh(u[:, o1:o2]), w2_ref[...])
    lw = (-DECAY_FLOOR) * _sigmoid(w_lin)
    lw_ref[0] = lw
    ri, ci = _iota((tt, tt), 0), _iota((tt, tt), 1)
    same_chunk = ri // chunk == ci // chunk
    p1 = lw.astype(BF16)
    p2 = (lw - p1.astype(F32)).astype(BF16)
    for z, tri in enumerate((same_chunk & (ci <= ri), same_chunk & (ci >= ri))):
        sl = slice(z * W, (z + 1) * W)
        cum_ref[0, :, sl] = _bdot(tri, p1[:, sl]) + _bdot(tri, p2[:, sl])
    a = _sigmoid(a0_ref[...] + _bdot(u[:, o2:o3], a2_ref[...]))
    g_ref[0] = _bdot(_sigmoid(u[:, o3:]), g2_ref[...]).astype(g_ref.dtype)

    kk = k * kk_w_ref[...]
    kk = kk * lax.rsqrt(jnp.maximum(_head_sum(kk * kk, head), 1e-24))
    k_sum = 0.0
    for z in range(N_DIR):
        a_z = a[:, z * W:(z + 1) * W]
        k_z = k * (1.0 + (a_z - 1.0) * ka_ref[...])
        kd_ref[0, :, z * W:(z + 1) * W] = k_z.astype(kd_ref.dtype)
        bd_ref[0, :, z * W:(z + 1) * W] = (kk * a_z).astype(bd_ref.dtype)
        k_sum = k_sum + k_z
    r_ref[0] = r.astype(r_ref.dtype)
    v_ref[0] = v.astype(v_ref.dtype)
    kk_ref[0] = kk.astype(kk_ref.dtype)
    bonus_ref[0] = (_head_sum(r * k_sum * rk_ref[...], head) * v).astype(bonus_ref.dtype)


def _rwkv_prep(p_rwkv, n_lat, W, head, mu, w0, w2, a0, a2, g2, k_k, k_a, r_k):
    B, T, NS = p_rwkv.shape
    lora = N_DIR * w2.shape[1]
    n_gate = NS - 3 * W - 2 * lora
    tt = _pick(T - n_lat, (256, 128, 64, 32, 16))
    n_tiles = T // tt
    def cat(m):
        z = jnp.zeros_like(m[0])
        return jnp.concatenate([jnp.concatenate([m[0], z], axis=1),
                                jnp.concatenate([z, m[1]], axis=1)], axis=0).astype(BF16)
    g2p = jnp.pad(g2, ((0, n_gate - g2.shape[0]), (0, 0))).astype(BF16)
    mup = jnp.pad(mu, (0, NS - mu.shape[0])).reshape(1, NS)
    kern = functools.partial(_rwkv_prep_kernel, tt=tt, n_lat_tiles=n_lat // tt, n_tiles=n_tiles,
                             W=W, head=head, lora=lora, chunk=RWKV_CHUNK)
    assert tt % RWKV_CHUNK == 0
    prev, cur, nxt = _halo_specs(tt, NS, T, 0)
    full = lambda shp: pl.BlockSpec(shp, lambda b, i: (0,) * len(shp))
    tile = lambda w: pl.BlockSpec((1, tt, w), lambda b, i: (b, i, 0))
    sd = lambda w, dt: jax.ShapeDtypeStruct((B, T, w), dt)
    return pl.pallas_call(
        kern,
        out_shape=(sd(W, BF16), sd(W, BF16), sd(W, BF16), sd(2 * W, F32), sd(2 * W, F32), sd(2 * W, BF16),
                   sd(2 * W, BF16), sd(W, BF16), sd(W, BF16)),
        grid=(B, n_tiles),
        in_specs=[prev, cur, nxt, full((1, NS)), full((1, 2 * W)), full((lora, 2 * W)),
                  full((1, 2 * W)), full((lora, 2 * W)), full((n_gate, W)),
                  full((1, W)), full((1, W)), full((1, W))],
        out_specs=(tile(W), tile(W), tile(W), tile(2 * W), tile(2 * W), tile(2 * W), tile(2 * W), tile(W),
                   tile(W)),
        compiler_params=_cparams("parallel", "parallel"),
        name="rwkv_prep",
    )(p_rwkv, p_rwkv, p_rwkv, mup, w0.reshape(1, 2 * W), cat(w2), a0.reshape(1, 2 * W), cat(a2), g2p,
      k_k.reshape(1, W), k_a.reshape(1, W), r_k.reshape(1, W))


def _rwkv_kernel(rf_ref, vf_ref, kkf_ref, rb_ref, vb_ref, kkb_ref, lwf_ref, cumf_ref, kf_ref, bf_ref,
                 lwb_ref, cumb_ref, kb_ref, bb_ref, yf_ref, yb_ref, g_ref, *, C, N):
    @pl.when(pl.program_id(1) == 0)
    def _():
        g_ref[...] = jnp.zeros_like(g_ref)

    W = rf_ref.shape[2]
    PW = 2 * N
    n_pairs = W // PW
    psl = [slice(p * PW, (p + 1) * PW) for p in range(n_pairs)]
    first = _iota((1, PW), 1) < N
    masks = (first, jnp.logical_not(first))
    same_head = (_iota((PW, PW), 0) // N) == (_iota((PW, PW), 1) // N)
    zero = jnp.zeros((C, PW), F32)
    diff = _iota((C, C), 0) - _iota((C, C), 1)

    dirs = []
    for d, (r_ref, v_ref, kk_ref, lw_ref, cum_ref, k_ref, b_ref) in enumerate((
            (rf_ref, vf_ref, kkf_ref, lwf_ref, cumf_ref, kf_ref, bf_ref),
            (rb_ref, vb_ref, kkb_ref, lwb_ref, cumb_ref, kb_ref, bb_ref))):
        order = diff if d == 0 else -diff
        order2 = jnp.concatenate([order, order], axis=1)
        cum = cum_ref[0]
        pc = cum[C - 1:C, :] if d == 0 else cum[0:1, :]
        k = k_ref[0].astype(F32)
        b = b_ref[0].astype(F32)
        p_inv = jnp.exp(-cum)
        p_hat = jnp.exp(pc - cum)
        dirs.append(dict(
            strict2=order2 > 0, incl2=order2 >= 0,
            r_t=r_ref[0].astype(F32) * jnp.exp(cum), k_t=k * p_inv, b_t=b * p_inv,
            a_t=-kk_ref[0].astype(F32) * jnp.exp(cum - lw_ref[0]),
            b_h=b * p_hat, k_h=k * p_hat, p_c=jnp.exp(pc), v=v_ref[0].astype(F32)))

    items = [(d, p) for p in range(n_pairs) for d in range(N_DIR)]
    v_m, upper, lower = [], [], []
    for d, p in items:
        t, sl = dirs[d], psl[p]
        rhs_bk = jnp.concatenate([t["b_t"][:, sl], t["k_t"][:, sl]], axis=0)
        lhs = jnp.concatenate([jnp.where(m, x[:, sl], 0.0) for m in masks for x in (t["a_t"], t["r_t"])],
                              axis=0)
        amat = _bdot(lhs, rhs_bk, NT)
        for sub in range(2):
            upper.append(jnp.where(t["strict2"], amat[2 * sub * C:(2 * sub + 1) * C], 0.0))
            lower.append(jnp.where(t["incl2"], amat[(2 * sub + 1) * C:(2 * sub + 2) * C], 0.0))
            v_m.append(jnp.where(masks[sub], t["v"][:, sl], 0.0))
    pw, x = [], []
    for i, (d, p) in enumerate(items):
        up0, up1 = upper[2 * i], upper[2 * i + 1]
        a_ak = jnp.where(first, pltpu.roll(up0, C, 1), up1)
        w = _bdot(a_ak, jnp.concatenate([v_m[2 * i], v_m[2 * i + 1]], axis=0))
        w = pltpu.roll(w, N, 1)
        a_p = dirs[d]["a_t"][:, psl[p]]
        x += [jnp.where(first, a_p, w), jnp.where(first, w, a_p)]
        pw.append(jnp.where(first, up0, pltpu.roll(up1, C, 1)))
    span = 1
    while span < C:
        span *= 2
        for i in range(len(items)):
            x0, x1 = x[2 * i], x[2 * i + 1]
            rhs = [jnp.concatenate([x0, zero], axis=1), jnp.concatenate([zero, x1], axis=1)]
            if span < C:
                rhs = [jnp.concatenate([jnp.where(m, pw[i], 0.0), xr], axis=1) for m, xr in zip(masks, rhs)]
            res = _bdot(pw[i], jnp.concatenate(rhs, axis=0))
            if span < C:
                pw[i] = res[:, :PW]
                res = res[:, PW:]
            x[2 * i], x[2 * i + 1] = x0 + res[:, :PW], x1 + res[:, PW:]

    g0 = [g_ref[d * n_pairs + p] for d, p in items]
    uv, rg = [], []
    for i, (d, p) in enumerate(items):
        x0, x1 = x[2 * i], x[2 * i + 1]
        ahat = [jnp.where(masks[0], x0, 0.0), jnp.where(masks[1], x1, 0.0)]
        vhat = [jnp.where(masks[0], pltpu.roll(x0, N, 1), 0.0), jnp.where(masks[1], pltpu.roll(x1, N, 1), 0.0)]
        res = _bdot(jnp.concatenate([dirs[d]["r_t"][:, psl[p]]] + ahat, axis=0), g0[i])
        rg.append(res[:C])
        u0 = res[C:2 * C] + vhat[0]
        u1 = res[2 * C:] + vhat[1]
        uv.append(jnp.concatenate([u0, v_m[2 * i], u1, v_m[2 * i + 1]], axis=0).astype(BF16))
    for i, (d, p) in enumerate(items):
        y_ref = yf_ref if d == 0 else yb_ref
        y_ref[0, :, psl[p]] = rg[i] + _bdot(jnp.concatenate([lower[2 * i], lower[2 * i + 1]], axis=1), uv[i])
    for i, (d, p) in enumerate(items):
        t, sl = dirs[d], psl[p]
        bk_h = jnp.concatenate([t["b_h"][:, sl], t["k_h"][:, sl]], axis=0)
        upd = _bdot(jnp.concatenate([bk_h, bk_h], axis=0), uv[i], TN)
        decay = jnp.transpose(jnp.broadcast_to(t["p_c"][:, sl], (PW, PW)))
        g_ref[d * n_pairs + p] = jnp.where(same_head, decay * g0[i] + upd, 0.0)


def _rwkv_scan(r, v, kk, lw, cum, kd, bd, head, n_lat):
    B, T, W = r.shape
    C = RWKV_CHUNK
    assert C == head
    nc = T // C
    nc_ctx = nc - n_lat // C

    def fwd(c):
        return jnp.where(c < nc_ctx, nc - nc_ctx + c, c - nc_ctx)

    def bwd(c):
        return nc - 1 - c

    spec = lambda chunk, col: pl.BlockSpec((1, C, W), lambda b, c: (b, chunk(c), col))
    kern = functools.partial(_rwkv_kernel, C=C, N=head)
    out = jax.ShapeDtypeStruct((B, T, W), F32)
    return pl.pallas_call(
        kern,
        out_shape=(out, out),
        grid=(B, nc),
        in_specs=[spec(fwd, 0)] * 3 + [spec(bwd, 0)] * 3 + [spec(fwd, 0)] * 4 + [spec(bwd, 1)] * 4,
        out_specs=(spec(fwd, 0), spec(bwd, 0)),
        scratch_shapes=[pltpu.VMEM((N_DIR * W // (2 * head), 2 * head, 2 * head), F32)],
        compiler_params=_cparams("parallel", "arbitrary"),
        name="rwkv7_chunk",
    )(r, v, kk, r, v, kk, lw, cum, kd, bd, lw, cum, kd, bd)


def _sigmoid(x):
    return 0.5 * jnp.tanh(0.5 * x) + 0.5


def _merge_kernel(attn_ref, pool_ref, yf_ref, yb_ref, g_ref, bonus_ref, lnw_ref, lnb_ref, gate_ref,
                  wa_ref, wp_ref, wr_ref, o_ref, *, head, D):
    y = yf_ref[0] + yb_ref[0]
    inv_n = 1.0 / head
    dev = y - _head_sum(y, head) * inv_n
    var = _head_sum(dev * dev, head) * inv_n
    yn = dev * lax.rsqrt(var + GN_EPS) * lnw_ref[...] + lnb_ref[...]
    rw = (yn + bonus_ref[0].astype(F32)) * g_ref[0].astype(F32)

    def gate(z):
        return _sigmoid(gate_ref[0, :, z * D:(z + 1) * D].astype(F32))

    out = gate(0) * _bdot(attn_ref[0], wa_ref[...])
    out = out + gate(1) * _bdot(pool_ref[0], wp_ref[...])
    out = out + gate(2) * _bdot(rw, wr_ref[...])
    o_ref[0] = out.astype(o_ref.dtype)


def _merge(attn, pool, y_fwd, y_bwd, g, bonus, ln_w, ln_b, gates, w_a, w_p, w_r, head):
    B, T, _ = attn.shape
    W = g.shape[2]
    D = w_a.shape[1]
    tm = _row_tile(T, 16)
    row = lambda w: pl.BlockSpec((1, tm, w), lambda b, i: (b, i, 0))
    vec = pl.BlockSpec((1, W), lambda b, i: (0, 0))
    wsp = lambda w: pl.BlockSpec(w.shape, lambda b, i: (0, 0), pipeline_mode=pl.Buffered(1))
    return pl.pallas_call(
        functools.partial(_merge_kernel, head=head, D=D),
        out_shape=jax.ShapeDtypeStruct((B, T, D), BF16),
        grid=(B, T // tm),
        in_specs=[row(attn.shape[2]), row(pool.shape[2]), row(W), row(W),
                  row(W), row(W), vec, vec, row(N_BRANCH * D),
                  wsp(w_a), wsp(w_p), wsp(w_r)],
        out_specs=row(D),
        compiler_params=_cparams("parallel", "parallel"),
        name="merge",
    )(attn, pool, y_fwd, y_bwd, g, bonus, ln_w.reshape(1, W), ln_b.reshape(1, W), gates, w_a, w_p, w_r)


def _out_kernel(m_ref, w_ref, x_ref, gpost_ref, gpre_ref, ml_ref, mc_ref, wr_ref,
                x_out, h_out, aff_out, *, tm, n_lat, n_e):
    th = tm // 2
    for half in range(2):
        rows = slice(half * th, (half + 1) * th)
        is_lat, m_lat, m_ctx = _mod_rows(ml_ref, mc_ref, pl.program_id(1) * tm + half * th, th, n_lat)
        mix = _bdot(m_ref[0, rows, :], w_ref[...])
        normed = mix * lax.rsqrt(jnp.mean(mix * mix, axis=-1, keepdims=True) + NORM_EPS) * gpost_ref[...]
        x = x_ref[0, rows, :] + jnp.where(is_lat, m_lat[2:3], m_ctx[2:3]) * normed
        x_out[0, rows, :] = x
        h = _modnorm(x, gpre_ref[...], is_lat, m_lat, m_ctx, 3)
        h_out[0, rows, :] = _pack_halves(h)
        h_hi = h.astype(BF16)
        part = _bdot(h_hi, wr_ref[...]) + _bdot(h - h_hi.astype(F32), wr_ref[...])
        logits = part[:, :V7X_LANES] + part[:, V7X_LANES:]
        logits = jnp.where(_iota(logits.shape, 1) < n_e, logits, -jnp.inf)
        e = jnp.exp(logits - jnp.max(logits, axis=-1, keepdims=True))
        aff_out[0, rows, :] = e / jnp.sum(e, axis=-1, keepdims=True)


def _out_proj(merged, w_out, xs, g_post, g_pre, mod_l, w_router, n_lat):
    B, T, D = xs.shape
    E = w_router.shape[1]
    tm = _pick(T, (544, 272, 256, 128, 320, 64, 32, 16))
    wr = jnp.pad(w_router, ((0, 0), (0, V7X_LANES - E)))
    wr_hi = wr.astype(BF16)
    wr = jnp.concatenate([wr_hi, (wr - wr_hi.astype(F32)).astype(BF16)], axis=1)
    row = lambda w: pl.BlockSpec((1, tm, w), lambda b, i: (b, i, 0))
    vec = pl.BlockSpec((1, D), lambda b, i: (0, 0))
    return pl.pallas_call(
        functools.partial(_out_kernel, tm=tm, n_lat=n_lat, n_e=E),
        out_shape=(jax.ShapeDtypeStruct((B, T, D), F32), jax.ShapeDtypeStruct((B, T, D // 2), jnp.uint32),
                   jax.ShapeDtypeStruct((B, T, V7X_LANES), F32)),
        grid=(B, T // tm),
        in_specs=[row(D), pl.BlockSpec((D, D), lambda b, i: (0, 0)), row(D), vec, vec,
                  pl.BlockSpec((1, 6, D), lambda b, i: (b, 0, 0)),
                  pl.BlockSpec((1, 6, D), lambda b, i: (B, 0, 0)),
                  pl.BlockSpec((D, 2 * V7X_LANES), lambda b, i: (0, 0))],
        out_specs=(row(D), row(D // 2), row(V7X_LANES)),
        compiler_params=_cparams("parallel", "parallel"),
        name="out_proj",
    )(merged, w_out, xs, g_post.reshape(1, D), g_pre.reshape(1, D), mod_l, mod_l, wr)


def _excl_prefix(flags, blk):
    n = flags.shape[-1]
    upper = (_iota((blk, blk), 0) < _iota((blk, blk), 1)).astype(BF16)
    outs = []
    carry = jnp.zeros((flags.shape[0], 1), F32)
    for j in range(n // blk):
        seg = flags[:, j * blk:(j + 1) * blk]
        outs.append(_bdot(seg, upper) + carry)
        carry = carry + jnp.sum(seg, axis=-1, keepdims=True)
    return jnp.concatenate(outs, axis=-1) if len(outs) > 1 else outs[0]


def _topk_kernel(aff_ref, sel_ref, idx_ref, *, cap, blk):
    bits = lax.bitcast_convert_type(aff_ref[0], jnp.int32)
    E = bits.shape[0]

    def body(i, tau):
        cand = tau | jnp.left_shift(jnp.int32(1), 30 - i)
        cnt = jnp.sum((bits >= cand).astype(jnp.int32), axis=-1, keepdims=True)
        return jnp.where(cnt >= cap, cand, tau)

    tau = lax.fori_loop(0, 31, body, jnp.zeros((E, 1), jnp.int32))
    gt = bits > tau
    eq = bits == tau
    need = (cap - jnp.sum(gt.astype(jnp.int32), axis=-1, keepdims=True)).astype(F32)
    eq_rank = _excl_prefix(eq.astype(F32), blk)
    sel = gt | (eq & (eq_rank < need))
    pos = _excl_prefix(sel.astype(F32), blk)
    slots = jnp.where(sel, pos.astype(jnp.int32), -1)
    sel_ref[0] = slots
    n = slots.shape[1]
    tok = _iota((8, n), 1)
    digit = _iota((8, n), 0)
    digits = jnp.where(digit == 0, tok // TOKEN_RADIX, jnp.where(digit == 1, tok % TOKEN_RADIX, 0))
    for e in range(E):
        onehot = slots[e:e + 1, :] == _iota((cap, n), 0)
        d = _bdot(digits.astype(F32), onehot.astype(F32), NT)
        idx_ref[0, e:e + 1, :] = (d[0:1] * TOKEN_RADIX + d[1:2]).astype(jnp.int32)


def _topk_slots(aff_t, cap):
    B, E, n = aff_t.shape
    assert n <= TOKEN_RADIX * 256
    blk = _pick(n, (512, 256, 128))
    return pl.pallas_call(
        functools.partial(_topk_kernel, cap=cap, blk=blk),
        out_shape=(jax.ShapeDtypeStruct((B, E, n), jnp.int32),
                   jax.ShapeDtypeStruct((B, E, cap), jnp.int32)),
        grid=(B,),
        in_specs=[pl.BlockSpec((1, E, n), lambda b: (b, 0, 0))],
        out_specs=(pl.BlockSpec((1, E, n), lambda b: (b, 0, 0)),
                   pl.BlockSpec((1, E, cap), lambda b: (b, 0, 0))),
        compiler_params=_cparams("parallel"),
        name="expert_topk",
    )(aff_t)


def _gather_kernel(idx_ref, h_ref, xe_ref, *, cap, n_e):
    base = (pl.program_id(0) * n_e + pl.program_id(1)) * cap

    def body(s, carry):
        t = idx_ref[base + s]
        xe_ref[0, 0, pl.ds(s, 1), :] = h_ref[0, pl.ds(t, 1), :]
        return carry

    lax.fori_loop(0, cap, body, 0, unroll=8)


def _gather(idx, hp, row_block, n):
    B, E, cap = idx.shape
    dh = hp.shape[2]
    grid_spec = pltpu.PrefetchScalarGridSpec(
        num_scalar_prefetch=1, grid=(B, E),
        in_specs=[pl.BlockSpec((1, n, dh), lambda b, e, idx_ref: (b, row_block, 0))],
        out_specs=pl.BlockSpec((1, 1, cap, dh), lambda b, e, idx_ref: (b, e, 0, 0)))
    return pl.pallas_call(
        functools.partial(_gather_kernel, cap=cap, n_e=E),
        out_shape=jax.ShapeDtypeStruct((B, E, cap, dh), jnp.uint32),
        grid_spec=grid_spec,
        compiler_params=_cparams("parallel", "arbitrary"),
        name="expert_gather",
    )(idx.reshape(-1), hp)


def _pack_halves(h):
    bits = lax.bitcast_convert_type(h.astype(BF16).astype(F32), jnp.uint32)
    half = h.shape[1] // 2
    return (bits[:, :half] >> 16) | (bits[:, half:] & jnp.uint32(0xFFFF0000))


def _unpack_halves(p):
    lo = lax.bitcast_convert_type(p << 16, F32)
    hi = lax.bitcast_convert_type(p & jnp.uint32(0xFFFF0000), F32)
    return lo, hi


def _ffn_kernel(*refs, n_sets):
    xe_refs = refs[:n_sets]
    wg_ref, wu_ref, wd_ref = refs[n_sets:n_sets + 3]
    ye_refs = refs[n_sets + 3:2 * n_sets + 3]
    wgb_ref, wub_ref, wdb_ref = refs[2 * n_sets + 3:]

    @pl.when(pl.program_id(2) == 0)
    def _():
        wgb_ref[...] = wg_ref[0, 0].astype(wgb_ref.dtype)
        wub_ref[...] = wu_ref[0, 0].astype(wub_ref.dtype)
        wdb_ref[...] = wd_ref[0, 0].astype(wdb_ref.dtype)

    for xe_ref, ye_ref in zip(xe_refs, ye_refs):
        lo, hi = _unpack_halves(xe_ref[0, 0])
        half = lo.shape[1]
        gate = _bdot(lo, wgb_ref[:half, :]) + _bdot(hi, wgb_ref[half:, :])
        up = _bdot(lo, wub_ref[:half, :]) + _bdot(hi, wub_ref[half:, :])
        hid = gate * _sigmoid(gate) * up
        ye_ref[0, 0, 0] = _bdot(hid, wdb_ref[...]).astype(ye_ref.dtype)


def _expert_ffn(xes, w_gate, w_up, w_down, layer):
    B, E, _, dh = xes[0].shape
    D = 2 * dh
    FF = w_gate.shape[3]
    fs = FF // FFN_SLABS
    n_sets = len(xes)
    return pl.pallas_call(
        functools.partial(_ffn_kernel, n_sets=n_sets),
        out_shape=tuple(jax.ShapeDtypeStruct((FFN_SLABS, B, E, xe.shape[2], D), BF16) for xe in xes),
        grid=(E, FFN_SLABS, B),
        in_specs=[pl.BlockSpec((1, 1, xe.shape[2], dh), lambda e, s, b: (b, e, 0, 0)) for xe in xes]
        + [pl.BlockSpec((1, 1, D, fs), lambda e, s, b: (layer, e, 0, s)),
           pl.BlockSpec((1, 1, D, fs), lambda e, s, b: (layer, e, 0, s)),
           pl.BlockSpec((1, 1, fs, D), lambda e, s, b: (layer, e, s, 0))],
        out_specs=tuple(pl.BlockSpec((1, 1, 1, xe.shape[2], D), lambda e, s, b: (s, b, e, 0, 0))
                        for xe in xes),
        scratch_shapes=[pltpu.VMEM((D, fs), BF16), pltpu.VMEM((D, fs), BF16), pltpu.VMEM((fs, D), BF16)],
        compiler_params=_cparams("parallel", "parallel", "arbitrary"),
        name="expert_ffn",
    )(*xes, w_gate, w_up, w_down)


def _scatter_kernel(selt_ref, aff_ref, ye_ref, o_ref, *, cap, n_e):
    sel_all = selt_ref[0]
    aff_all = aff_ref[0]
    slot = _iota((1, cap), 1)
    acc = None
    for e in range(n_e):
        onehot = sel_all[:, e:e + 1] == slot
        ye = ye_ref[0, 0, e].astype(F32)
        for s in range(1, ye_ref.shape[0]):
            ye = ye + ye_ref[s, 0, e].astype(F32)
        term = aff_all[:, e:e + 1] * _bdot(onehot, ye)
        acc = term if acc is None else acc + term
    o_ref[0] = acc


def _scatter(sel, aff, row_block, ye):
    B, E, n = sel.shape
    n_slab, cap, D = ye.shape[0], ye.shape[3], ye.shape[4]
    td = _pick(D, (256, 128))
    return pl.pallas_call(
        functools.partial(_scatter_kernel, cap=cap, n_e=E),
        out_shape=jax.ShapeDtypeStruct((B, n, D), F32),
        grid=(B, D // td),
        in_specs=[pl.BlockSpec((1, n, E), lambda b, j: (b, 0, 0)),
                  pl.BlockSpec((1, n, aff.shape[2]), lambda b, j: (b, row_block, 0)),
                  pl.BlockSpec((n_slab, 1, E, cap, td), lambda b, j: (0, b, 0, 0, j))],
        out_specs=pl.BlockSpec((1, n, td), lambda b, j: (b, 0, j)),
        compiler_params=_cparams("parallel", "parallel"),
        name="expert_scatter",
    )(jnp.swapaxes(sel, 1, 2), aff, ye)


def _expert_choice(hb, aff, sets, w_gate, w_up, w_down, layer):
    E = w_gate.shape[1]
    sels, xes = [], []
    for row_block, n in sets:
        aff_t = jnp.swapaxes(aff[:, row_block * n:(row_block + 1) * n, :E], 1, 2)
        sel, idx = _topk_slots(aff_t, EC_FACTOR * n // E)
        sels.append(sel)
        xes.append(_gather(idx, hb, row_block, n))
    yes = _expert_ffn(xes, w_gate, w_up, w_down, layer)
    return [_scatter(sel, aff, row_block, ye) for sel, (row_block, n), ye in zip(sels, sets, yes)]


def _rms(x, g):
    return x * lax.rsqrt(jnp.mean(x * x, axis=-1, keepdims=True) + NORM_EPS) * g


def _rope_tables(n_lat, n_ctx, head_dim):
    rows = n_lat // GRID_W
    row = jnp.repeat(jnp.arange(rows), GRID_W).astype(F32)
    col = (jnp.arange(rows * GRID_W) % GRID_W).astype(F32)
    half = head_dim // 2
    inv = ROPE_THETA ** (-jnp.arange(0, half, 2, dtype=F32) / half)
    ar, ac = row[:, None] * inv, col[:, None] * inv
    cos = jnp.concatenate([jnp.cos(ar), jnp.cos(ar), jnp.cos(ac), jnp.cos(ac)], axis=-1)
    sin = jnp.concatenate([-jnp.sin(ar), jnp.sin(ar), -jnp.sin(ac), jnp.sin(ac)], axis=-1)
    cos = jnp.concatenate([cos, jnp.ones((n_ctx, head_dim), F32)], axis=0)
    sin = jnp.concatenate([sin, jnp.zeros((n_ctx, head_dim), F32)], axis=0)
    return cos, sin


def _pad_cols(w, mult):
    return jnp.pad(w, ((0, 0), (0, (-w.shape[1]) % mult)))


def kernel(x, c, ctx, c_ctx, w_mod, b_mod, norm_pre, norm_post, w_in, q_norm, k_norm, w_attn_o, w_pool_group, pool_scale, w_pool_o, rwkv_mu, rwkv_w0, rwkv_w2, rwkv_a0, rwkv_a2, rwkv_g2, rwkv_k_k, rwkv_k_a, rwkv_r_k, rwkv_ln_w, rwkv_ln_b, w_rwkv_o, w_out, w_router, w_exp_gate, w_exp_up, w_exp_down):
    B, n_lat, D = x.shape
    n_ctx = ctx.shape[1]
    T = n_lat + n_ctx
    depth = w_mod.shape[0]
    hd = q_norm.shape[1]
    attn_w = w_attn_o.shape[1]
    pool_w = w_pool_o.shape[1]
    rwkv_w = w_rwkv_o.shape[1]
    n_shift = rwkv_mu.shape[1]
    n_in = w_in.shape[2]
    kv_w = (n_in - attn_w - pool_w - n_shift - N_BRANCH * D) // 2
    head = rwkv_r_k.shape[2]
    col_k = attn_w
    col_v = col_k + kv_w
    col_pool = col_v + kv_w
    col_r = col_pool + pool_w
    col_gate = col_r + n_shift
    hq, hkv = attn_w // hd, kv_w // hd
    assert n_lat % (hq // hkv * hd) == 0 and pool_w % (hq // hkv * hd) == 0 and n_ctx % RWKV_CHUNK == 0

    cos, sin = _rope_tables(n_lat, n_ctx, hd)
    s_all = jnp.concatenate([jax.nn.silu(c), jax.nn.silu(c_ctx)[None]], axis=0)
    s_all = jnp.pad(s_all, ((0, (-s_all.shape[0]) % 8), (0, 0)))
    mod = _modulation(s_all, w_mod, b_mod)[:, :B + 1].reshape(depth, B + 1, 6, D)
    xs = jnp.concatenate([x, ctx], axis=1)
    (h,) = _stream_update(xs, None, None, None, mod[0], norm_pre[0, 0], mod[0], n_lat, T)

    for l in range(depth):
        keep_ctx = l < depth - 1
        wl = w_in[l]
        w_a = jnp.concatenate([wl[:, col_pool:col_r], wl[:, :col_pool]], axis=1).astype(BF16)
        w_b = _pad_cols(wl[:, col_r:col_gate], 3 * V7X_LANES).astype(BF16)
        w_c = wl[:, col_gate:].astype(BF16)
        p_a = _in_proj(h, w_a)
        p_b = _in_proj(h, w_b)
        p_c = _in_proj(h, w_c)

        qkv = (p_a, pool_w, pool_w + attn_w, pool_w + attn_w + kv_w, cos, sin, q_norm[l], k_norm[l], hd, hq, hkv)
        attn = _attention(*qkv, (0, n_lat), (0, T))
        attn = _attention(*qkv, (n_lat, n_ctx), (n_lat, n_ctx), out=attn)
        pool = _pool(p_a, 0, w_pool_group[l].astype(BF16), pool_scale[l], n_lat)
        r, v, kk, lw, cum, kd, bd, g, bonus = _rwkv_prep(
            p_b, n_lat, rwkv_w, head, rwkv_mu[l], rwkv_w0[l], rwkv_w2[l], rwkv_a0[l], rwkv_a2[l],
            rwkv_g2[l], rwkv_k_k[l], rwkv_k_a[l], rwkv_r_k[l])
        y_fwd, y_bwd = _rwkv_scan(r, v, kk, lw, cum, kd, bd, head, n_lat)
        merged = _merge(attn, pool, y_fwd, y_bwd, g, bonus, rwkv_ln_w[l], rwkv_ln_b[l], p_c,
                        w_attn_o[l].astype(BF16), w_pool_o[l].astype(BF16), w_rwkv_o[l].astype(BF16), head)
        xs, hb, aff = _out_proj(merged, w_out[l].astype(BF16), xs, norm_post[l, 0], norm_pre[l, 1],
                                mod[l], w_router[l], n_lat)

        sets = [(0, n_lat)] + ([(n_lat // n_ctx, n_ctx)] if keep_ctx else [])
        mixed = _expert_choice(hb, aff, sets, w_exp_gate, w_exp_up, w_exp_down, l)
        f_lat = mixed[0]
        if keep_ctx:
            f_ctx = mixed[1]
            xs, h = _stream_update(xs, f_lat, f_ctx, norm_post[l, 1], mod[l], norm_pre[l + 1, 0],
                                   mod[l + 1], n_lat, T)
        else:
            (xs,) = _stream_update(xs, f_lat, None, norm_post[l, 1], mod[l], None, None, n_lat, n_lat)
    return xs
```

```python
import functools

import jax
import jax.numpy as jnp
from jax import lax
from jax.experimental import pallas as pl
from jax.experimental.pallas import tpu as pltpu

F32 = jnp.float32
BF16 = jnp.bfloat16

GRID_W = 64
NORM_EPS = 1e-6
ROPE_THETA = 10000.0
POOL_WINDOWS = (2, 4, 8, 16)
GN_EPS = 64e-5
EC_FACTOR = 2
N_DIR = 2
N_BRANCH = 3
RWKV_CHUNK = 64
HALO = 16

LOG2_E = 1.4426950408889634
DECAY_FLOOR = 0.6065306597126334
ATTN_SHIFT_LIMIT = 60.0
FFN_SLABS = 2
TOKEN_RADIX = 64

V7X_LANES = 128
V7X_VMEM_BYTES = 64 * 1024 * 1024
VMEM_LIMIT = V7X_VMEM_BYTES * 13 // 16

NT = (((1,), (1,)), ((), ()))
TN = (((0,), (0,)), ((), ()))


def _pick(n, cands):
    for c in cands:
        if c <= n and n % c == 0:
            return c
    return n


def _cparams(*sem):
    return pltpu.CompilerParams(dimension_semantics=sem, vmem_limit_bytes=VMEM_LIMIT)


def _bdot(x, y, dn=None):
    x = x.astype(BF16)
    y = y.astype(BF16)
    if dn is None:
        return jnp.dot(x, y, preferred_element_type=F32)
    return lax.dot_general(x, y, dn, preferred_element_type=F32)


def _split_dot(x, y):
    hi = x.astype(BF16)
    lo = x - hi.astype(F32)
    return _bdot(hi, y) + _bdot(lo, y)


def _iota(shape, dim):
    return lax.broadcasted_iota(jnp.int32, shape, dim)


def _row_tile(T, min_parts=8):
    for parts in range(min_parts, T // 16 + 1):
        if T % parts == 0 and (T // parts) % 16 == 0:
            return T // parts
    return T


def _pieces(x):
    p1 = x.astype(BF16)
    r1 = x - p1.astype(F32)
    p2 = r1.astype(BF16)
    p3 = (r1 - p2.astype(F32)).astype(BF16)
    return p1, p2, p3


def _mod_kernel(s_ref, w_ref, b_ref, o_ref):
    s1, s2, s3 = (p.astype(F32) for p in _pieces(s_ref[...]))
    w1, w2, w3 = _pieces(w_ref[0])
    R = s1.shape[0]
    a = _bdot(jnp.concatenate([s1, s2, s3], axis=0), w1)
    b = _bdot(jnp.concatenate([s1, s2], axis=0), w2)
    c = _bdot(s1, w3)

    @pl.when(pl.program_id(1) == 0)
    def _():
        o_ref[0] = jnp.broadcast_to(b_ref[0], o_ref.shape[1:])

    o_ref[0] += (a[:R] + a[R:2 * R] + a[2 * R:]) + (b[:R] + b[R:]) + c


def _modulation(s, w_mod, b_mod):
    L, D, N = w_mod.shape
    R = s.shape[0]
    tk = _pick(D, (128,))
    return pl.pallas_call(
        _mod_kernel,
        out_shape=jax.ShapeDtypeStruct((L, R, N), F32),
        grid=(L, D // tk),
        in_specs=[pl.BlockSpec((R, tk), lambda l, k: (0, k)),
                  pl.BlockSpec((1, tk, N), lambda l, k: (l, k, 0)),
                  pl.BlockSpec((1, 1, N), lambda l, k: (l, 0, 0))],
        out_specs=pl.BlockSpec((1, R, N), lambda l, k: (l, 0, 0)),
        compiler_params=_cparams("parallel", "arbitrary"),
        name="adaln_mod",
    )(s, w_mod, b_mod.reshape(L, 1, N))


def _mod_rows(m_lat_ref, m_ctx_ref, row0, tm, n_lat):
    is_lat = (row0 + _iota((tm, 1), 0)) < n_lat
    return is_lat, m_lat_ref[0], m_ctx_ref[0]


def _modnorm(x, g, is_lat, m_lat, m_ctx, i_shift):
    shift = jnp.where(is_lat, m_lat[i_shift:i_shift + 1], m_ctx[i_shift:i_shift + 1])
    scale = jnp.where(is_lat, m_lat[i_shift + 1:i_shift + 2], m_ctx[i_shift + 1:i_shift + 2])
    y = x * lax.rsqrt(jnp.mean(x * x, axis=-1, keepdims=True) + NORM_EPS) * g
    return y * (1.0 + scale) + shift


def _segment_row_tile(n_ctx):
    return _pick(n_ctx, (256, 128, 64, 32, 16))


def _stream_kernel(*refs, tm, n_lat, has_f, has_ctx_f, emit_h):
    refs = list(refs)
    x_ref = refs.pop(0)
    fl_ref = refs.pop(0) if has_f else None
    fc_ref = refs.pop(0) if has_ctx_f else None
    gpost_ref = refs.pop(0) if has_f else None
    ml_ref, mc_ref = refs.pop(0), refs.pop(0)
    if emit_h:
        gpre_ref, mln_ref, mcn_ref = refs.pop(0), refs.pop(0), refs.pop(0)
    x_out = refs.pop(0) if has_f else None
    h_out = refs.pop(0) if emit_h else None

    is_lat = (pl.program_id(1) * tm + _iota((tm, 1), 0)) < n_lat
    x = x_ref[0]
    if has_f:
        f = fl_ref[0]
        if has_ctx_f:
            f = jnp.where(is_lat, f, fc_ref[0])
        normed = f * lax.rsqrt(jnp.mean(f * f, axis=-1, keepdims=True) + NORM_EPS) * gpost_ref[...]
        x = x + jnp.where(is_lat, ml_ref[0][5:6], mc_ref[0][5:6]) * normed
        x_out[0] = x
    if emit_h:
        h_out[0] = _modnorm(x, gpre_ref[...], is_lat, mln_ref[0], mcn_ref[0], 0).astype(h_out.dtype)


def _stream_update(xs, f_lat, f_ctx, g_post, mod_l, g_pre_next, mod_next, n_lat, n_rows):
    B, T, D = xs.shape
    n_ctx = T - n_lat
    tm = _segment_row_tile(n_ctx)
    has_f, has_ctx_f, emit_h = f_lat is not None, f_ctx is not None, g_pre_next is not None
    n_lt = n_lat // tm
    row = pl.BlockSpec((1, tm, D), lambda b, i: (b, i, 0))
    vec = pl.BlockSpec((1, D), lambda b, i: (0, 0))
    m_l = pl.BlockSpec((1, 6, D), lambda b, i: (b, 0, 0))
    m_c = pl.BlockSpec((1, 6, D), lambda b, i: (B, 0, 0))
    args, specs = [xs], [row]
    if has_f:
        args.append(f_lat)
        specs.append(pl.BlockSpec((1, tm, D), lambda b, i: (b, jnp.minimum(i, n_lt - 1), 0)))
    if has_ctx_f:
        args.append(f_ctx)
        specs.append(pl.BlockSpec((1, tm, D), lambda b, i: (b, jnp.maximum(i - n_lt, 0), 0)))
    if has_f:
        args.append(g_post.reshape(1, D))
        specs.append(vec)
    args += [mod_l, mod_l]
    specs += [m_l, m_c]
    if emit_h:
        args += [g_pre_next.reshape(1, D), mod_next, mod_next]
        specs += [vec, m_l, m_c]
    out_shape, out_specs = [], []
    if has_f:
        out_shape.append(jax.ShapeDtypeStruct((B, n_rows, D), F32))
        out_specs.append(row)
    if emit_h:
        out_shape.append(jax.ShapeDtypeStruct((B, n_rows, D), BF16))
        out_specs.append(row)
    kern = functools.partial(_stream_kernel, tm=tm, n_lat=n_lat, has_f=has_f, has_ctx_f=has_ctx_f,
                             emit_h=emit_h)
    return pl.pallas_call(
        kern, out_shape=tuple(out_shape), grid=(B, n_rows // tm), in_specs=specs,
        out_specs=tuple(out_specs), compiler_params=_cparams("parallel", "parallel"),
        name="stream_update",
    )(*args)


def _in_kernel(h_ref, w_ref, o_ref):
    o_ref[0] = _bdot(h_ref[0], w_ref[...]).astype(o_ref.dtype)


def _in_proj(h, w):
    B, T, D = h.shape
    N = w.shape[1]
    tm = _row_tile(T, 4)
    tn = _pick(N, (1280, 1024, 896, 768, 640, 512, 384, 256, 128))
    return pl.pallas_call(
        _in_kernel,
        out_shape=jax.ShapeDtypeStruct((B, T, N), BF16),
        grid=(B, T // tm, N // tn),
        in_specs=[pl.BlockSpec((1, tm, D), lambda b, i, j: (b, i, 0)),
                  pl.BlockSpec((D, tn), lambda b, i, j: (0, j))],
        out_specs=pl.BlockSpec((1, tm, tn), lambda b, i, j: (b, i, j)),
        compiler_params=_cparams("parallel", "parallel", "arbitrary"),
        name="in_proj",
    )(h, w)


def _rot(x, cos, sin):
    qd = x.shape[1] // 4
    first = (_iota((1, x.shape[1]), 1) // qd) % 2 == 0
    swapped = jnp.where(first, pltpu.roll(x, x.shape[1] - qd, 1), pltpu.roll(x, qd, 1))
    return x * cos + swapped * sin


def _head_norm(x, g):
    return x * lax.rsqrt(jnp.mean(x * x, axis=-1, keepdims=True) + NORM_EPS) * g


def _attn_kernel(bound_ref, q_ref, k_ref, v_ref, cq_ref, sq_ref, ck_ref, sk_ref, qn_ref, kn_ref, *rest,
                 hd, group, scale):
    o_ref, kp_ref, vp_ref = rest[-3:]
    unit = jnp.where(_iota((1, hd), 1) == 0, 1.0, 0.0)

    @pl.when(pl.program_id(2) == 0)
    def _():
        k = _rot(_head_norm(k_ref[0].astype(F32), kn_ref[...]), ck_ref[...], sk_ref[...])
        kp_ref[...] = jnp.concatenate([k, jnp.broadcast_to(unit, k.shape)], axis=1).astype(kp_ref.dtype)
        ones = jnp.ones(v_ref.shape[1:], vp_ref.dtype)
        vp_ref[...] = jnp.concatenate([v_ref[0].astype(vp_ref.dtype), ones], axis=1)

    bound = bound_ref[0]

    def attend(shift_in_matmul):
        for g in range(group):
            q = _head_norm(q_ref[0, :, g * hd:(g + 1) * hd].astype(F32), qn_ref[...])
            q = _rot(q, cq_ref[...], sq_ref[...]) * scale
            if shift_in_matmul:
                q_aug = jnp.concatenate([q, jnp.broadcast_to(unit * (-bound), q.shape)], axis=1)
                p = jnp.exp2(_bdot(q_aug, kp_ref[...], NT))
            else:
                s = _bdot(q, kp_ref[:, :hd], NT)
                p = jnp.exp2(s - jnp.max(s, axis=-1, keepdims=True))
            ov = _bdot(p, vp_ref[...])
            o_ref[0, :, g * hd:(g + 1) * hd] = (ov[:, :hd] / ov[:, hd:hd + 1]).astype(o_ref.dtype)

    fast = bound < ATTN_SHIFT_LIMIT
    pl.when(fast)(functools.partial(attend, True))
    pl.when(jnp.logical_not(fast))(functools.partial(attend, False))


def _attention(p_qkv, q_col, k_col, v_col, cos, sin, q_norm, k_norm, hd, hq, hkv, q_rows, key_rows, out=None):
    B, T, _ = p_qkv.shape
    group = hq // hkv
    gw = group * hd
    (q0, nq), (k0, nk) = q_rows, key_rows
    tq = next(t for t in (512, 256, 128, 64, 32, 16) if nq % t == 0 and q0 % t == 0)
    assert k0 % nk == 0
    kern = functools.partial(_attn_kernel, hd=hd, group=group, scale=hd ** -0.5 * LOG2_E)
    qb, kb = q0 // tq, k0 // nk
    tab_q = pl.BlockSpec((tq, hd), lambda b, h, i, s: (i + qb, 0))
    tab_k = pl.BlockSpec((nk, hd), lambda b, h, i, s: (kb, 0))
    vec = pl.BlockSpec((1, hd), lambda b, h, i, s: (0, 0))
    bound = (1.02 * hd * hd ** -0.5 * LOG2_E) * jnp.max(jnp.abs(q_norm)) * jnp.max(jnp.abs(k_norm))
    in_specs = [pl.BlockSpec((1, tq, gw), lambda b, h, i, s: (b, i + qb, q_col // gw + h)),
                pl.BlockSpec((1, nk, hd), lambda b, h, i, s: (b, kb, k_col // hd + h)),
                pl.BlockSpec((1, nk, hd), lambda b, h, i, s: (b, kb, v_col // hd + h)),
                tab_q, tab_q, tab_k, tab_k, vec, vec]
    args = [bound.reshape(1).astype(F32), p_qkv, p_qkv, p_qkv, cos, sin, cos, sin,
            q_norm.reshape(1, hd), k_norm.reshape(1, hd)]
    aliases = {}
    if out is not None:
        in_specs.append(pl.BlockSpec(memory_space=pl.ANY))
        args.append(out)
        aliases = {len(args) - 1: 0}
    grid_spec = pltpu.PrefetchScalarGridSpec(
        num_scalar_prefetch=1, grid=(B, hkv, nq // tq), in_specs=in_specs,
        out_specs=pl.BlockSpec((1, tq, gw), lambda b, h, i, s: (b, i + qb, h)),
        scratch_shapes=[pltpu.VMEM((nk, 2 * hd), BF16), pltpu.VMEM((nk, 2 * hd), BF16)])
    return pl.pallas_call(
        kern,
        out_shape=jax.ShapeDtypeStruct((B, T, hq * hd), BF16),
        grid_spec=grid_spec,
        input_output_aliases=aliases,
        compiler_params=_cparams("parallel", "parallel", "arbitrary"),
        name="attention",
    )(*args)


def _segment_of_tile(i, tt, n_lat_tiles, n_tiles):
    is_lat = i < n_lat_tiles
    ti = jnp.where(is_lat, i, i - n_lat_tiles)
    seg_tiles = jnp.where(is_lat, n_lat_tiles, n_tiles - n_lat_tiles)
    return ti, seg_tiles


def _halo_specs(tt, width, T, col_block):
    hb = tt // HALO
    last = T // HALO - 1
    prev = pl.BlockSpec((1, HALO, width), lambda b, i: (b, jnp.maximum(i * hb - 1, 0), col_block))
    cur = pl.BlockSpec((1, tt, width), lambda b, i: (b, i, col_block))
    nxt = pl.BlockSpec((1, HALO, width), lambda b, i: (b, jnp.minimum((i + 1) * hb, last), col_block))
    return prev, cur, nxt


def _pool_kernel(prev_ref, cur_ref, next_ref, wg_ref, sc_ref, o_ref, *,
                 tt, n_lat_tiles, n_tiles, gw, windows):
    ti, seg_tiles = _segment_of_tile(pl.program_id(1), tt, n_lat_tiles, n_tiles)
    has_prev = ti > 0
    has_next = ti < seg_tiles - 1
    t_seg = seg_tiles * tt
    tpos = ti * tt + _iota((tt, 1), 0)

    cur = cur_ref[0]
    prev = prev_ref[0]
    nxt = next_ref[0]
    d_cur = _iota((tt, tt), 1) - _iota((tt, tt), 0)
    d_halo = _iota((tt, HALO), 1) - _iota((tt, HALO), 0)
    d_prev = d_halo - HALO
    d_next = d_halo + tt
    for g, win in enumerate(windows):
        lo_off = -(win // 2)
        hi_off = win - win // 2 - 1
        sl = slice(g * gw, (g + 1) * gw)
        band_c = ((d_cur >= lo_off) & (d_cur <= hi_off)).astype(BF16)
        band_p = ((d_prev >= lo_off) & (d_prev <= hi_off) & has_prev).astype(BF16)
        band_n = ((d_next >= lo_off) & (d_next <= hi_off) & has_next).astype(BF16)
        ug = cur[:, sl]
        tot = _bdot(band_c, ug) + _bdot(band_p, prev[:, sl]) + _bdot(band_n, nxt[:, sl])
        lo = jnp.maximum(tpos + lo_off, 0)
        hi = jnp.minimum(tpos + hi_off + 1, t_seg)
        pooled = tot / (hi - lo).astype(F32) - ug.astype(F32)
        y = _bdot(pooled, wg_ref[g]) * sc_ref[:, sl]
        o_ref[0, :, sl] = y.astype(o_ref.dtype)


def _pool(p_arr, col_block, w_group, scale, n_lat):
    B, T, _ = p_arr.shape
    G, gw, _ = w_group.shape
    W = G * gw
    tt = _pick(T - n_lat, (256, 128, 64, 32, 16))
    assert n_lat % tt == 0 and tt % HALO == 0 and max(POOL_WINDOWS) <= HALO
    n_tiles = T // tt
    kern = functools.partial(_pool_kernel, tt=tt, n_lat_tiles=n_lat // tt, n_tiles=n_tiles,
                             gw=gw, windows=POOL_WINDOWS)
    prev, cur, nxt = _halo_specs(tt, W, T, col_block)
    return pl.pallas_call(
        kern,
        out_shape=jax.ShapeDtypeStruct((B, T, W), BF16),
        grid=(B, n_tiles),
        in_specs=[prev, cur, nxt,
                  pl.BlockSpec((G, gw, gw), lambda b, i: (0, 0, 0)),
                  pl.BlockSpec((1, W), lambda b, i: (0, 0))],
        out_specs=pl.BlockSpec((1, tt, W), lambda b, i: (b, i, 0)),
        compiler_params=_cparams("parallel", "parallel"),
        name="pool",
    )(p_arr, p_arr, p_arr, w_group, scale.reshape(1, W))


def _head_sum(x, head):
    lanes = x.shape[1]
    blk = min(lanes, V7X_LANES)
    same = (_iota((blk, blk), 0) // head == _iota((blk, blk), 1) // head).astype(BF16)
    parts = [_split_dot(x[:, j:j + blk], same) for j in range(0, lanes, blk)]
    return jnp.concatenate(parts, axis=1) if len(parts) > 1 else parts[0]


def _rwkv_prep_kernel(prev_ref, cur_ref, next_ref, mu_ref, w0_ref, w2_ref, a0_ref, a2_ref, g2_ref,
                      kk_w_ref, ka_ref, rk_ref,
                      r_ref, v_ref, kk_ref, lw_ref, cum_ref, kd_ref, bd_ref, g_ref, bonus_ref, *,
                      tt, n_lat_tiles, n_tiles, W, head, lora, chunk):
    ti, seg_tiles = _segment_of_tile(pl.program_id(1), tt, n_lat_tiles, n_tiles)
    u = cur_ref[0].astype(F32)
    row = _iota((tt, 1), 0)
    before = jnp.where(ti > 0, prev_ref[0, HALO - 1:HALO, :].astype(F32), 0.0)
    after = jnp.where(ti < seg_tiles - 1, next_ref[0, 0:1, :].astype(F32), 0.0)
    down, up = pltpu.roll(u, 1, 0), pltpu.roll(u, tt - 1, 0)
    u_prev = jnp.concatenate([jnp.where(row[:8] == 0, before, down[:8]), down[8:]], axis=0)
    u_next = jnp.concatenate([up[:tt - 8], jnp.where(row[tt - 8:] == tt - 1, after, up[tt - 8:])], axis=0)
    mu = mu_ref[...]
    u = u * (1.0 - mu) + (u_prev + u_next) * (0.5 * mu)

    r, k, v = u[:, :W], u[:, W:2 * W], u[:, 2 * W:3 * W]
    o1 = 3 * W
    o2 = o1 + lora
    o3 = o2 + lora
    w_lin = w0_ref[...] + _bdot(jnp.tanh(u[:, o1:o2]), w2_ref[...])
    lw = (-DECAY_FLOOR) * _sigmoid(w_lin)
    lw_ref[0] = lw
    ri, ci = _iota((tt, tt), 0), _iota((tt, tt), 1)
    same_chunk = ri // chunk == ci // chunk
    p1 = lw.astype(BF16)
    p2 = (lw - p1.astype(F32)).astype(BF16)
    for z, tri in enumerate((same_chunk & (ci <= ri), same_chunk & (ci >= ri))):
        sl = slice(z * W, (z + 1) * W)
        cum_ref[0, :, sl] = _bdot(tri, p1[:, sl]) + _bdot(tri, p2[:, sl])
    a = _sigmoid(a0_ref[...] + _bdot(u[:, o2:o3], a2_ref[...]))
    g_ref[0] = _bdot(_sigmoid(u[:, o3:]), g2_ref[...]).astype(g_ref.dtype)

    kk = k * kk_w_ref[...]
    kk = kk * lax.rsqrt(jnp.maximum(_head_sum(kk * kk, head), 1e-24))
    k_sum = 0.0
    for z in range(N_DIR):
        a_z = a[:, z * W:(z + 1) * W]
        k_z = k * (1.0 + (a_z - 1.0) * ka_ref[...])
        kd_ref[0, :, z * W:(z + 1) * W] = k_z.astype(kd_ref.dtype)
        bd_ref[0, :, z * W:(z + 1) * W] = (kk * a_z).astype(bd_ref.dtype)
        k_sum = k_sum + k_z
    r_ref[0] = r.astype(r_ref.dtype)
    v_ref[0] = v.astype(v_ref.dtype)
    kk_ref[0] = kk.astype(kk_ref.dtype)
    bonus_ref[0] = (_head_sum(r * k_sum * rk_ref[...], head) * v).astype(bonus_ref.dtype)


def _rwkv_prep(p_rwkv, n_lat, W, head, mu, w0, w2, a0, a2, g2, k_k, k_a, r_k):
    B, T, NS = p_rwkv.shape
    lora = N_DIR * w2.shape[1]
    n_gate = NS - 3 * W - 2 * lora
    tt = _pick(T - n_lat, (256, 128, 64, 32, 16))
    n_tiles = T // tt
    def cat(m):
        z = jnp.zeros_like(m[0])
        return jnp.concatenate([jnp.concatenate([m[0], z], axis=1),
                                jnp.concatenate([z, m[1]], axis=1)], axis=0).astype(BF16)
    g2p = jnp.pad(g2, ((0, n_gate - g2.shape[0]), (0, 0))).astype(BF16)
    mup = jnp.pad(mu, (0, NS - mu.shape[0])).reshape(1, NS)
    kern = functools.partial(_rwkv_prep_kernel, tt=tt, n_lat_tiles=n_lat // tt, n_tiles=n_tiles,
                             W=W, head=head, lora=lora, chunk=RWKV_CHUNK)
    assert tt % RWKV_CHUNK == 0
    prev, cur, nxt = _halo_specs(tt, NS, T, 0)
    full = lambda shp: pl.BlockSpec(shp, lambda b, i: (0,) * len(shp))
    tile = lambda w: pl.BlockSpec((1, tt, w), lambda b, i: (b, i, 0))
    sd = lambda w, dt: jax.ShapeDtypeStruct((B, T, w), dt)
    return pl.pallas_call(
        kern,
        out_shape=(sd(W, BF16), sd(W, BF16), sd(W, BF16), sd(2 * W, F32), sd(2 * W, F32), sd(2 * W, BF16),
                   sd(2 * W, BF16), sd(W, BF16), sd(W, BF16)),
        grid=(B, n_tiles),
        in_specs=[prev, cur, nxt, full((1, NS)), full((1, 2 * W)), full((lora, 2 * W)),
                  full((1, 2 * W)), full((lora, 2 * W)), full((n_gate, W)),
                  full((1, W)), full((1, W)), full((1, W))],
        out_specs=(tile(W), tile(W), tile(W), tile(2 * W), tile(2 * W), tile(2 * W), tile(2 * W), tile(W),
                   tile(W)),
        compiler_params=_cparams("parallel", "parallel"),
        name="rwkv_prep",
    )(p_rwkv, p_rwkv, p_rwkv, mup, w0.reshape(1, 2 * W), cat(w2), a0.reshape(1, 2 * W), cat(a2), g2p,
      k_k.reshape(1, W), k_a.reshape(1, W), r_k.reshape(1, W))


def _rwkv_kernel(rf_ref, vf_ref, kkf_ref, rb_ref, vb_ref, kkb_ref, lwf_ref, cumf_ref, kf_ref, bf_ref,
                 lwb_ref, cumb_ref, kb_ref, bb_ref, yf_ref, yb_ref, g_ref, *, C, N):
    @pl.when(pl.program_id(1) == 0)
    def _():
        g_ref[...] = jnp.zeros_like(g_ref)

    W = rf_ref.shape[2]
    PW = 2 * N
    n_pairs = W // PW
    psl = [slice(p * PW, (p + 1) * PW) for p in range(n_pairs)]
    first = _iota((1, PW), 1) < N
    masks = (first, jnp.logical_not(first))
    same_head = (_iota((PW, PW), 0) // N) == (_iota((PW, PW), 1) // N)
    zero = jnp.zeros((C, PW), F32)
    diff = _iota((C, C), 0) - _iota((C, C), 1)

    dirs = []
    for d, (r_ref, v_ref, kk_ref, lw_ref, cum_ref, k_ref, b_ref) in enumerate((
            (rf_ref, vf_ref, kkf_ref, lwf_ref, cumf_ref, kf_ref, bf_ref),
            (rb_ref, vb_ref, kkb_ref, lwb_ref, cumb_ref, kb_ref, bb_ref))):
        order = diff if d == 0 else -diff
        order2 = jnp.concatenate([order, order], axis=1)
        cum = cum_ref[0]
        pc = cum[C - 1:C, :] if d == 0 else cum[0:1, :]
        k = k_ref[0].astype(F32)
        b = b_ref[0].astype(F32)
        p_inv = jnp.exp(-cum)
        p_hat = jnp.exp(pc - cum)
        dirs.append(dict(
            strict2=order2 > 0, incl2=order2 >= 0,
            r_t=r_ref[0].astype(F32) * jnp.exp(cum), k_t=k * p_inv, b_t=b * p_inv,
            a_t=-kk_ref[0].astype(F32) * jnp.exp(cum - lw_ref[0]),
            b_h=b * p_hat, k_h=k * p_hat, p_c=jnp.exp(pc), v=v_ref[0].astype(F32)))

    items = [(d, p) for p in range(n_pairs) for d in range(N_DIR)]
    v_m, upper, lower = [], [], []
    for d, p in items:
        t, sl = dirs[d], psl[p]
        rhs_bk = jnp.concatenate([t["b_t"][:, sl], t["k_t"][:, sl]], axis=0)
        lhs = jnp.concatenate([jnp.where(m, x[:, sl], 0.0) for m in masks for x in (t["a_t"], t["r_t"])],
                              axis=0)
        amat = _bdot(lhs, rhs_bk, NT)
        for sub in range(2):
            upper.append(jnp.where(t["strict2"], amat[2 * sub * C:(2 * sub + 1) * C], 0.0))
            lower.append(jnp.where(t["incl2"], amat[(2 * sub + 1) * C:(2 * sub + 2) * C], 0.0))
            v_m.append(jnp.where(masks[sub], t["v"][:, sl], 0.0))
    pw, x = [], []
    for i, (d, p) in enumerate(items):
        up0, up1 = upper[2 * i], upper[2 * i + 1]
        a_ak = jnp.where(first, pltpu.roll(up0, C, 1), up1)
        w = _bdot(a_ak, jnp.concatenate([v_m[2 * i], v_m[2 * i + 1]], axis=0))
        w = pltpu.roll(w, N, 1)
        a_p = dirs[d]["a_t"][:, psl[p]]
        x += [jnp.where(first, a_p, w), jnp.where(first, w, a_p)]
        pw.append(jnp.where(first, up0, pltpu.roll(up1, C, 1)))
    span = 1
    while span < C:
        span *= 2
        for i in range(len(items)):
            x0, x1 = x[2 * i], x[2 * i + 1]
            rhs = [jnp.concatenate([x0, zero], axis=1), jnp.concatenate([zero, x1], axis=1)]
            if span < C:
                rhs = [jnp.concatenate([jnp.where(m, pw[i], 0.0), xr], axis=1) for m, xr in zip(masks, rhs)]
            res = _bdot(pw[i], jnp.concatenate(rhs, axis=0))
            if span < C:
                pw[i] = res[:, :PW]
                res = res[:, PW:]
            x[2 * i], x[2 * i + 1] = x0 + res[:, :PW], x1 + res[:, PW:]

    g0 = [g_ref[d * n_pairs + p] for d, p in items]
    uv, rg = [], []
    for i, (d, p) in enumerate(items):
        x0, x1 = x[2 * i], x[2 * i + 1]
        ahat = [jnp.where(masks[0], x0, 0.0), jnp.where(masks[1], x1, 0.0)]
        vhat = [jnp.where(masks[0], pltpu.roll(x0, N, 1), 0.0), jnp.where(masks[1], pltpu.roll(x1, N, 1), 0.0)]
        res = _bdot(jnp.concatenate([dirs[d]["r_t"][:, psl[p]]] + ahat, axis=0), g0[i])
        rg.append(res[:C])
        u0 = res[C:2 * C] + vhat[0]
        u1 = res[2 * C:] + vhat[1]
        uv.append(jnp.concatenate([u0, v_m[2 * i], u1, v_m[2 * i + 1]], axis=0).astype(BF16))
    for i, (d, p) in enumerate(items):
        y_ref = yf_ref if d == 0 else yb_ref
        y_ref[0, :, psl[p]] = rg[i] + _bdot(jnp.concatenate([lower[2 * i], lower[2 * i + 1]], axis=1), uv[i])
    for i, (d, p) in enumerate(items):
        t, sl = dirs[d], psl[p]
        bk_h = jnp.concatenate([t["b_h"][:, sl], t["k_h"][:, sl]], axis=0)
        upd = _bdot(jnp.concatenate([bk_h, bk_h], axis=0), uv[i], TN)
        decay = jnp.transpose(jnp.broadcast_to(t["p_c"][:, sl], (PW, PW)))
        g_ref[d * n_pairs + p] = jnp.where(same_head, decay * g0[i] + upd, 0.0)


def _rwkv_scan(r, v, kk, lw, cum, kd, bd, head, n_lat):
    B, T, W = r.shape
    C = RWKV_CHUNK
    assert C == head
    nc = T // C
    nc_ctx = nc - n_lat // C

    def fwd(c):
        return jnp.where(c < nc_ctx, nc - nc_ctx + c, c - nc_ctx)

    def bwd(c):
        return nc - 1 - c

    spec = lambda chunk, col: pl.BlockSpec((1, C, W), lambda b, c: (b, chunk(c), col))
    kern = functools.partial(_rwkv_kernel, C=C, N=head)
    out = jax.ShapeDtypeStruct((B, T, W), F32)
    return pl.pallas_call(
        kern,
        out_shape=(out, out),
        grid=(B, nc),
        in_specs=[spec(fwd, 0)] * 3 + [spec(bwd, 0)] * 3 + [spec(fwd, 0)] * 4 + [spec(bwd, 1)] * 4,
        out_specs=(spec(fwd, 0), spec(bwd, 0)),
        scratch_shapes=[pltpu.VMEM((N_DIR * W // (2 * head), 2 * head, 2 * head), F32)],
        compiler_params=_cparams("parallel", "arbitrary"),
        name="rwkv7_chunk",
    )(r, v, kk, r, v, kk, lw, cum, kd, bd, lw, cum, kd, bd)


def _sigmoid(x):
    return 0.5 * jnp.tanh(0.5 * x) + 0.5


def _merge_kernel(attn_ref, pool_ref, yf_ref, yb_ref, g_ref, bonus_ref, lnw_ref, lnb_ref, gate_ref,
                  wa_ref, wp_ref, wr_ref, o_ref, *, head, D):
    y = yf_ref[0] + yb_ref[0]
    inv_n = 1.0 / head
    dev = y - _head_sum(y, head) * inv_n
    var = _head_sum(dev * dev, head) * inv_n
    yn = dev * lax.rsqrt(var + GN_EPS) * lnw_ref[...] + lnb_ref[...]
    rw = (yn + bonus_ref[0].astype(F32)) * g_ref[0].astype(F32)

    def gate(z):
        return _sigmoid(gate_ref[0, :, z * D:(z + 1) * D].astype(F32))

    out = gate(0) * _bdot(attn_ref[0], wa_ref[...])
    out = out + gate(1) * _bdot(pool_ref[0], wp_ref[...])
    out = out + gate(2) * _bdot(rw, wr_ref[...])
    o_ref[0] = out.astype(o_ref.dtype)


def _merge(attn, pool, y_fwd, y_bwd, g, bonus, ln_w, ln_b, gates, w_a, w_p, w_r, head):
    B, T, _ = attn.shape
    W = g.shape[2]
    D = w_a.shape[1]
    tm = _row_tile(T, 16)
    row = lambda w: pl.BlockSpec((1, tm, w), lambda b, i: (b, i, 0))
    vec = pl.BlockSpec((1, W), lambda b, i: (0, 0))
    wsp = lambda w: pl.BlockSpec(w.shape, lambda b, i: (0, 0), pipeline_mode=pl.Buffered(1))
    return pl.pallas_call(
        functools.partial(_merge_kernel, head=head, D=D),
        out_shape=jax.ShapeDtypeStruct((B, T, D), BF16),
        grid=(B, T // tm),
        in_specs=[row(attn.shape[2]), row(pool.shape[2]), row(W), row(W),
                  row(W), row(W), vec, vec, row(N_BRANCH * D),
                  wsp(w_a), wsp(w_p), wsp(w_r)],
        out_specs=row(D),
        compiler_params=_cparams("parallel", "parallel"),
        name="merge",
    )(attn, pool, y_fwd, y_bwd, g, bonus, ln_w.reshape(1, W), ln_b.reshape(1, W), gates, w_a, w_p, w_r)


def _out_kernel(m_ref, w_ref, x_ref, gpost_ref, gpre_ref, ml_ref, mc_ref, wr_ref,
                x_out, h_out, aff_out, *, tm, n_lat, n_e):
    th = tm // 2
    for half in range(2):
        rows = slice(half * th, (half + 1) * th)
        is_lat, m_lat, m_ctx = _mod_rows(ml_ref, mc_ref, pl.program_id(1) * tm + half * th, th, n_lat)
        mix = _bdot(m_ref[0, rows, :], w_ref[...])
        normed = mix * lax.rsqrt(jnp.mean(mix * mix, axis=-1, keepdims=True) + NORM_EPS) * gpost_ref[...]
        x = x_ref[0, rows, :] + jnp.where(is_lat, m_lat[2:3], m_ctx[2:3]) * normed
        x_out[0, rows, :] = x
        h = _modnorm(x, gpre_ref[...], is_lat, m_lat, m_ctx, 3)
        h_out[0, rows, :] = _pack_halves(h)
        h_hi = h.astype(BF16)
        part = _bdot(h_hi, wr_ref[...]) + _bdot(h - h_hi.astype(F32), wr_ref[...])
        logits = part[:, :V7X_LANES] + part[:, V7X_LANES:]
        logits = jnp.where(_iota(logits.shape, 1) < n_e, logits, -jnp.inf)
        e = jnp.exp(logits - jnp.max(logits, axis=-1, keepdims=True))
        aff_out[0, rows, :] = e / jnp.sum(e, axis=-1, keepdims=True)


def _out_proj(merged, w_out, xs, g_post, g_pre, mod_l, w_router, n_lat):
    B, T, D = xs.shape
    E = w_router.shape[1]
    tm = _pick(T, (544, 272, 256, 128, 320, 64, 32, 16))
    wr = jnp.pad(w_router, ((0, 0), (0, V7X_LANES - E)))
    wr_hi = wr.astype(BF16)
    wr = jnp.concatenate([wr_hi, (wr - wr_hi.astype(F32)).astype(BF16)], axis=1)
    row = lambda w: pl.BlockSpec((1, tm, w), lambda b, i: (b, i, 0))
    vec = pl.BlockSpec((1, D), lambda b, i: (0, 0))
    return pl.pallas_call(
        functools.partial(_out_kernel, tm=tm, n_lat=n_lat, n_e=E),
        out_shape=(jax.ShapeDtypeStruct((B, T, D), F32), jax.ShapeDtypeStruct((B, T, D // 2), jnp.uint32),
                   jax.ShapeDtypeStruct((B, T, V7X_LANES), F32)),
        grid=(B, T // tm),
        in_specs=[row(D), pl.BlockSpec((D, D), lambda b, i: (0, 0)), row(D), vec, vec,
                  pl.BlockSpec((1, 6, D), lambda b, i: (b, 0, 0)),
                  pl.BlockSpec((1, 6, D), lambda b, i: (B, 0, 0)),
                  pl.BlockSpec((D, 2 * V7X_LANES), lambda b, i: (0, 0))],
        out_specs=(row(D), row(D // 2), row(V7X_LANES)),
        compiler_params=_cparams("parallel", "parallel"),
        name="out_proj",
    )(merged, w_out, xs, g_post.reshape(1, D), g_pre.reshape(1, D), mod_l, mod_l, wr)


def _excl_prefix(flags, blk):
    n = flags.shape[-1]
    upper = (_iota((blk, blk), 0) < _iota((blk, blk), 1)).astype(BF16)
    outs = []
    carry = jnp.zeros((flags.shape[0], 1), F32)
    for j in range(n // blk):
        seg = flags[:, j * blk:(j + 1) * blk]
        outs.append(_bdot(seg, upper) + carry)
        carry = carry + jnp.sum(seg, axis=-1, keepdims=True)
    return jnp.concatenate(outs, axis=-1) if len(outs) > 1 else outs[0]


def _topk_kernel(aff_ref, sel_ref, idx_ref, *, cap, blk):
    bits = lax.bitcast_convert_type(aff_ref[0], jnp.int32)
    E = bits.shape[0]

    def body(i, tau):
        cand = tau | jnp.left_shift(jnp.int32(1), 30 - i)
        cnt = jnp.sum((bits >= cand).astype(jnp.int32), axis=-1, keepdims=True)
        return jnp.where(cnt >= cap, cand, tau)

    tau = lax.fori_loop(0, 31, body, jnp.zeros((E, 1), jnp.int32))
    gt = bits > tau
    eq = bits == tau
    need = (cap - jnp.sum(gt.astype(jnp.int32), axis=-1, keepdims=True)).astype(F32)
    eq_rank = _excl_prefix(eq.astype(F32), blk)
    sel = gt | (eq & (eq_rank < need))
    pos = _excl_prefix(sel.astype(F32), blk)
    slots = jnp.where(sel, pos.astype(jnp.int32), -1)
    sel_ref[0] = slots
    n = slots.shape[1]
    tok = _iota((8, n), 1)
    digit = _iota((8, n), 0)
    digits = jnp.where(digit == 0, tok // TOKEN_RADIX, jnp.where(digit == 1, tok % TOKEN_RADIX, 0))
    for e in range(E):
        onehot = slots[e:e + 1, :] == _iota((cap, n), 0)
        d = _bdot(digits.astype(F32), onehot.astype(F32), NT)
        idx_ref[0, e:e + 1, :] = (d[0:1] * TOKEN_RADIX + d[1:2]).astype(jnp.int32)


def _topk_slots(aff_t, cap):
    B, E, n = aff_t.shape
    assert n <= TOKEN_RADIX * 256
    blk = _pick(n, (512, 256, 128))
    return pl.pallas_call(
        functools.partial(_topk_kernel, cap=cap, blk=blk),
        out_shape=(jax.ShapeDtypeStruct((B, E, n), jnp.int32),
                   jax.ShapeDtypeStruct((B, E, cap), jnp.int32)),
        grid=(B,),
        in_specs=[pl.BlockSpec((1, E, n), lambda b: (b, 0, 0))],
        out_specs=(pl.BlockSpec((1, E, n), lambda b: (b, 0, 0)),
                   pl.BlockSpec((1, E, cap), lambda b: (b, 0, 0))),
        compiler_params=_cparams("parallel"),
        name="expert_topk",
    )(aff_t)


def _gather_kernel(idx_ref, h_ref, xe_ref, *, cap, n_e):
    base = (pl.program_id(0) * n_e + pl.program_id(1)) * cap

    def body(s, carry):
        t = idx_ref[base + s]
        xe_ref[0, 0, pl.ds(s, 1), :] = h_ref[0, pl.ds(t, 1), :]
        return carry

    lax.fori_loop(0, cap, body, 0, unroll=8)


def _gather(idx, hp, row_block, n):
    B, E, cap = idx.shape
    dh = hp.shape[2]
    grid_spec = pltpu.PrefetchScalarGridSpec(
        num_scalar_prefetch=1, grid=(B, E),
        in_specs=[pl.BlockSpec((1, n, dh), lambda b, e, idx_ref: (b, row_block, 0))],
        out_specs=pl.BlockSpec((1, 1, cap, dh), lambda b, e, idx_ref: (b, e, 0, 0)))
    return pl.pallas_call(
        functools.partial(_gather_kernel, cap=cap, n_e=E),
        out_shape=jax.ShapeDtypeStruct((B, E, cap, dh), jnp.uint32),
        grid_spec=grid_spec,
        compiler_params=_cparams("parallel", "arbitrary"),
        name="expert_gather",
    )(idx.reshape(-1), hp)


def _pack_halves(h):
    bits = lax.bitcast_convert_type(h.astype(BF16).astype(F32), jnp.uint32)
    half = h.shape[1] // 2
    return (bits[:, :half] >> 16) | (bits[:, half:] & jnp.uint32(0xFFFF0000))


def _unpack_halves(p):
    lo = lax.bitcast_convert_type(p << 16, F32)
    hi = lax.bitcast_convert_type(p & jnp.uint32(0xFFFF0000), F32)
    return lo, hi


def _ffn_kernel(*refs, n_sets):
    xe_refs = refs[:n_sets]
    wg_ref, wu_ref, wd_ref = refs[n_sets:n_sets + 3]
    ye_refs = refs[n_sets + 3:2 * n_sets + 3]
    wgb_ref, wub_ref, wdb_ref = refs[2 * n_sets + 3:]

    @pl.when(pl.program_id(2) == 0)
    def _():
        wgb_ref[...] = wg_ref[0, 0].astype(wgb_ref.dtype)
        wub_ref[...] = wu_ref[0, 0].astype(wub_ref.dtype)
        wdb_ref[...] = wd_ref[0, 0].astype(wdb_ref.dtype)

    for xe_ref, ye_ref in zip(xe_refs, ye_refs):
        lo, hi = _unpack_halves(xe_ref[0, 0])
        half = lo.shape[1]
        gate = _bdot(lo, wgb_ref[:half, :]) + _bdot(hi, wgb_ref[half:, :])
        up = _bdot(lo, wub_ref[:half, :]) + _bdot(hi, wub_ref[half:, :])
        hid = gate * _sigmoid(gate) * up
        ye_ref[0, 0, 0] = _bdot(hid, wdb_ref[...]).astype(ye_ref.dtype)


def _expert_ffn(xes, w_gate, w_up, w_down, layer):
    B, E, _, dh = xes[0].shape
    D = 2 * dh
    FF = w_gate.shape[3]
    fs = FF // FFN_SLABS
    n_sets = len(xes)
    return pl.pallas_call(
        functools.partial(_ffn_kernel, n_sets=n_sets),
        out_shape=tuple(jax.ShapeDtypeStruct((FFN_SLABS, B, E, xe.shape[2], D), BF16) for xe in xes),
        grid=(E, FFN_SLABS, B),
        in_specs=[pl.BlockSpec((1, 1, xe.shape[2], dh), lambda e, s, b: (b, e, 0, 0)) for xe in xes]
        + [pl.BlockSpec((1, 1, D, fs), lambda e, s, b: (layer, e, 0, s)),
           pl.BlockSpec((1, 1, D, fs), lambda e, s, b: (layer, e, 0, s)),
           pl.BlockSpec((1, 1, fs, D), lambda e, s, b: (layer, e, s, 0))],
        out_specs=tuple(pl.BlockSpec((1, 1, 1, xe.shape[2], D), lambda e, s, b: (s, b, e, 0, 0))
                        for xe in xes),
        scratch_shapes=[pltpu.VMEM((D, fs), BF16), pltpu.VMEM((D, fs), BF16), pltpu.VMEM((fs, D), BF16)],
        compiler_params=_cparams("parallel", "parallel", "arbitrary"),
        name="expert_ffn",
    )(*xes, w_gate, w_up, w_down)


def _scatter_kernel(selt_ref, aff_ref, ye_ref, o_ref, *, cap, n_e):
    sel_all = selt_ref[0]
    aff_all = aff_ref[0]
    slot = _iota((1, cap), 1)
    acc = None
    for e in range(n_e):
        onehot = sel_all[:, e:e + 1] == slot
        ye = ye_ref[0, 0, e].astype(F32)
        for s in range(1, ye_ref.shape[0]):
            ye = ye + ye_ref[s, 0, e].astype(F32)
        term = aff_all[:, e:e + 1] * _bdot(onehot, ye)
        acc = term if acc is None else acc + term
    o_ref[0] = acc


def _scatter(sel, aff, row_block, ye):
    B, E, n = sel.shape
    n_slab, cap, D = ye.shape[0], ye.shape[3], ye.shape[4]
    td = _pick(D, (256, 128))
    return pl.pallas_call(
        functools.partial(_scatter_kernel, cap=cap, n_e=E),
        out_shape=jax.ShapeDtypeStruct((B, n, D), F32),
        grid=(B, D // td),
        in_specs=[pl.BlockSpec((1, n, E), lambda b, j: (b, 0, 0)),
                  pl.BlockSpec((1, n, aff.shape[2]), lambda b, j: (b, row_block, 0)),
                  pl.BlockSpec((n_slab, 1, E, cap, td), lambda b, j: (0, b, 0, 0, j))],
        out_specs=pl.BlockSpec((1, n, td), lambda b, j: (b, 0, j)),
        compiler_params=_cparams("parallel", "parallel"),
        name="expert_scatter",
    )(jnp.swapaxes(sel, 1, 2), aff, ye)


def _expert_choice(hb, aff, sets, w_gate, w_up, w_down, layer):
    E = w_gate.shape[1]
    sels, xes = [], []
    for row_block, n in sets:
        aff_t = jnp.swapaxes(aff[:, row_block * n:(row_block + 1) * n, :E], 1, 2)
        sel, idx = _topk_slots(aff_t, EC_FACTOR * n // E)
        sels.append(sel)
        xes.append(_gather(idx, hb, row_block, n))
    yes = _expert_ffn(xes, w_gate, w_up, w_down, layer)
    return [_scatter(sel, aff, row_block, ye) for sel, (row_block, n), ye in zip(sels, sets, yes)]


def _rope_tables(n_lat, n_ctx, head_dim):
    rows = n_lat // GRID_W
    row = jnp.repeat(jnp.arange(rows), GRID_W).astype(F32)
    col = (jnp.arange(rows * GRID_W) % GRID_W).astype(F32)
    half = head_dim // 2
    inv = ROPE_THETA ** (-jnp.arange(0, half, 2, dtype=F32) / half)
    ar, ac = row[:, None] * inv, col[:, None] * inv
    cos = jnp.concatenate([jnp.cos(ar), jnp.cos(ar), jnp.cos(ac), jnp.cos(ac)], axis=-1)
    sin = jnp.concatenate([-jnp.sin(ar), jnp.sin(ar), -jnp.sin(ac), jnp.sin(ac)], axis=-1)
    cos = jnp.concatenate([cos, jnp.ones((n_ctx, head_dim), F32)], axis=0)
    sin = jnp.concatenate([sin, jnp.zeros((n_ctx, head_dim), F32)], axis=0)
    return cos, sin


def _pad_cols(w, mult):
    return jnp.pad(w, ((0, 0), (0, (-w.shape[1]) % mult)))


def kernel(x, c, ctx, c_ctx, w_mod, b_mod, norm_pre, norm_post, w_in, q_norm, k_norm, w_attn_o, w_pool_group, pool_scale, w_pool_o, rwkv_mu, rwkv_w0, rwkv_w2, rwkv_a0, rwkv_a2, rwkv_g2, rwkv_k_k, rwkv_k_a, rwkv_r_k, rwkv_ln_w, rwkv_ln_b, w_rwkv_o, w_out, w_router, w_exp_gate, w_exp_up, w_exp_down):
    B, n_lat, D = x.shape
    n_ctx = ctx.shape[1]
    T = n_lat + n_ctx
    depth = w_mod.shape[0]
    hd = q_norm.shape[1]
    attn_w = w_attn_o.shape[1]
    pool_w = w_pool_o.shape[1]
    rwkv_w = w_rwkv_o.shape[1]
    n_shift = rwkv_mu.shape[1]
    n_in = w_in.shape[2]
    kv_w = (n_in - attn_w - pool_w - n_shift - N_BRANCH * D) // 2
    head = rwkv_r_k.shape[2]
    col_k = attn_w
    col_v = col_k + kv_w
    col_pool = col_v + kv_w
    col_r = col_pool + pool_w
    col_gate = col_r + n_shift
    hq, hkv = attn_w // hd, kv_w // hd
    assert n_lat % (hq // hkv * hd) == 0 and pool_w % (hq // hkv * hd) == 0 and n_ctx % RWKV_CHUNK == 0

    cos, sin = _rope_tables(n_lat, n_ctx, hd)
    s_all = jnp.concatenate([jax.nn.silu(c), jax.nn.silu(c_ctx)[None]], axis=0)
    s_all = jnp.pad(s_all, ((0, (-s_all.shape[0]) % 8), (0, 0)))
    mod = _modulation(s_all, w_mod, b_mod)[:, :B + 1].reshape(depth, B + 1, 6, D)
    xs = jnp.concatenate([x, ctx], axis=1)
    (h,) = _stream_update(xs, None, None, None, mod[0], norm_pre[0, 0], mod[0], n_lat, T)

    for l in range(depth):
        keep_ctx = l < depth - 1
        wl = w_in[l]
        w_a = jnp.concatenate([wl[:, col_pool:col_r], wl[:, :col_pool]], axis=1).astype(BF16)
        w_b = _pad_cols(wl[:, col_r:col_gate], 3 * V7X_LANES).astype(BF16)
        w_c = wl[:, col_gate:].astype(BF16)
        p_a = _in_proj(h, w_a)
        p_b = _in_proj(h, w_b)
        p_c = _in_proj(h, w_c)

        qkv = (p_a, pool_w, pool_w + attn_w, pool_w + attn_w + kv_w, cos, sin, q_norm[l], k_norm[l], hd, hq, hkv)
        attn = _attention(*qkv, (0, n_lat), (0, T))
        attn = _attention(*qkv, (n_lat, n_ctx), (n_lat, n_ctx), out=attn)
        pool = _pool(p_a, 0, w_pool_group[l].astype(BF16), pool_scale[l], n_lat)
        r, v, kk, lw, cum, kd, bd, g, bonus = _rwkv_prep(
            p_b, n_lat, rwkv_w, head, rwkv_mu[l], rwkv_w0[l], rwkv_w2[l], rwkv_a0[l], rwkv_a2[l],
            rwkv_g2[l], rwkv_k_k[l], rwkv_k_a[l], rwkv_r_k[l])
        y_fwd, y_bwd = _rwkv_scan(r, v, kk, lw, cum, kd, bd, head, n_lat)
        merged = _merge(attn, pool, y_fwd, y_bwd, g, bonus, rwkv_ln_w[l], rwkv_ln_b[l], p_c,
                        w_attn_o[l].astype(BF16), w_pool_o[l].astype(BF16), w_rwkv_o[l].astype(BF16), head)
        xs, hb, aff = _out_proj(merged, w_out[l].astype(BF16), xs, norm_post[l, 0], norm_pre[l, 1],
                                mod[l], w_router[l], n_lat)

        sets = [(0, n_lat)] + ([(n_lat // n_ctx, n_ctx)] if keep_ctx else [])
        mixed = _expert_choice(hb, aff, sets, w_exp_gate, w_exp_up, w_exp_down, l)
        f_lat = mixed[0]
        if keep_ctx:
            f_ctx = mixed[1]
            xs, h = _stream_update(xs, f_lat, f_ctx, norm_post[l, 1], mod[l], norm_pre[l + 1, 0],
                                   mod[l + 1], n_lat, T)
        else:
            (xs,) = _stream_update(xs, f_lat, None, norm_post[l, 1], mod[l], None, None, n_lat, n_lat)
    return xs
```

```python
import functools

import jax
import jax.numpy as jnp
from jax import lax
from jax.experimental import pallas as pl
from jax.experimental.pallas import tpu as pltpu

F32 = jnp.float32
BF16 = jnp.bfloat16

GRID_W = 64
NORM_EPS = 1e-6
ROPE_THETA = 10000.0
POOL_WINDOWS = (2, 4, 8, 16)
GN_EPS = 64e-5
EC_FACTOR = 2
N_DIR = 2
N_BRANCH = 3
RWKV_CHUNK = 64
HALO = 16

LOG2_E = 1.4426950408889634
DECAY_FLOOR = 0.6065306597126334
ATTN_SHIFT_LIMIT = 60.0
FFN_SLABS = 2
TOKEN_RADIX = 64

V7X_LANES = 128
V7X_MXU = 256
V7X_VMEM_BYTES = 64 * 1024 * 1024
VMEM_LIMIT = V7X_VMEM_BYTES * 13 // 16

NT = (((1,), (1,)), ((), ()))
TN = (((0,), (0,)), ((), ()))


def _pick(n, cands):
    for c in cands:
        if c <= n and n % c == 0:
            return c
    return n


def _cparams(*sem):
    return pltpu.CompilerParams(dimension_semantics=sem, vmem_limit_bytes=VMEM_LIMIT)


def _bdot(x, y, dn=None):
    x = x.astype(BF16)
    y = y.astype(BF16)
    if dn is None:
        return jnp.dot(x, y, preferred_element_type=F32)
    return lax.dot_general(x, y, dn, preferred_element_type=F32)


def _split_dot(x, y):
    hi = x.astype(BF16)
    lo = x - hi.astype(F32)
    return _bdot(hi, y) + _bdot(lo, y)


def _iota(shape, dim):
    return lax.broadcasted_iota(jnp.int32, shape, dim)


def _row_tile(T, min_parts=8):
    for parts in range(min_parts, T // 16 + 1):
        if T % parts == 0 and (T // parts) % 16 == 0:
            return T // parts
    return T


def _pieces(x):
    p1 = x.astype(BF16)
    r1 = x - p1.astype(F32)
    p2 = r1.astype(BF16)
    p3 = (r1 - p2.astype(F32)).astype(BF16)
    return p1, p2, p3


def _mod_kernel(s_ref, w_ref, b_ref, o_ref):
    s1, s2, s3 = (p.astype(F32) for p in _pieces(s_ref[...]))
    w1, w2, w3 = _pieces(w_ref[0])
    R = s1.shape[0]
    a = _bdot(jnp.concatenate([s1, s2, s3], axis=0), w1)
    b = _bdot(jnp.concatenate([s1, s2], axis=0), w2)
    c = _bdot(s1, w3)

    @pl.when(pl.program_id(1) == 0)
    def _():
        o_ref[0] = jnp.broadcast_to(b_ref[0], o_ref.shape[1:])

    o_ref[0] += (a[:R] + a[R:2 * R] + a[2 * R:]) + (b[:R] + b[R:]) + c


def _modulation(s, w_mod, b_mod):
    L, D, N = w_mod.shape
    R = s.shape[0]
    tk = _pick(D, (128,))
    return pl.pallas_call(
        _mod_kernel,
        out_shape=jax.ShapeDtypeStruct((L, R, N), F32),
        grid=(L, D // tk),
        in_specs=[pl.BlockSpec((R, tk), lambda l, k: (0, k)),
                  pl.BlockSpec((1, tk, N), lambda l, k: (l, k, 0)),
                  pl.BlockSpec((1, 1, N), lambda l, k: (l, 0, 0))],
        out_specs=pl.BlockSpec((1, R, N), lambda l, k: (l, 0, 0)),
        compiler_params=_cparams("parallel", "arbitrary"),
        name="adaln_mod",
    )(s, w_mod, b_mod.reshape(L, 1, N))


def _mod_rows(m_lat_ref, m_ctx_ref, row0, tm, n_lat):
    is_lat = (row0 + _iota((tm, 1), 0)) < n_lat
    return is_lat, m_lat_ref[0], m_ctx_ref[0]


def _modnorm(x, g, is_lat, m_lat, m_ctx, i_shift):
    shift = jnp.where(is_lat, m_lat[i_shift:i_shift + 1], m_ctx[i_shift:i_shift + 1])
    scale = jnp.where(is_lat, m_lat[i_shift + 1:i_shift + 2], m_ctx[i_shift + 1:i_shift + 2])
    y = x * lax.rsqrt(jnp.mean(x * x, axis=-1, keepdims=True) + NORM_EPS) * g
    return y * (1.0 + scale) + shift


def _segment_row_tile(n_ctx):
    return _pick(n_ctx, (256, 128, 64, 32, 16))


def _stream_kernel(*refs, tm, n_lat, has_f, has_ctx_f, emit_h):
    refs = list(refs)
    x_ref = refs.pop(0)
    fl_ref = refs.pop(0) if has_f else None
    fc_ref = refs.pop(0) if has_ctx_f else None
    gpost_ref = refs.pop(0) if has_f else None
    ml_ref, mc_ref = refs.pop(0), refs.pop(0)
    if emit_h:
        gpre_ref, mln_ref, mcn_ref = refs.pop(0), refs.pop(0), refs.pop(0)
    x_out = refs.pop(0) if has_f else None
    h_out = refs.pop(0) if emit_h else None

    is_lat = (pl.program_id(1) * tm + _iota((tm, 1), 0)) < n_lat
    x = x_ref[0]
    if has_f:
        f = fl_ref[0].astype(F32)
        if has_ctx_f:
            f = jnp.where(is_lat, f, fc_ref[0].astype(F32))
        normed = f * lax.rsqrt(jnp.mean(f * f, axis=-1, keepdims=True) + NORM_EPS) * gpost_ref[...]
        x = x + jnp.where(is_lat, ml_ref[0][5:6], mc_ref[0][5:6]) * normed
        x_out[0] = x
    if emit_h:
        h_out[0] = _modnorm(x, gpre_ref[...], is_lat, mln_ref[0], mcn_ref[0], 0).astype(h_out.dtype)


def _stream_update(xs, f_lat, f_ctx, g_post, mod_l, g_pre_next, mod_next, n_lat, n_rows):
    B, T, D = xs.shape
    n_ctx = T - n_lat
    tm = _segment_row_tile(n_ctx)
    has_f, has_ctx_f, emit_h = f_lat is not None, f_ctx is not None, g_pre_next is not None
    n_lt = n_lat // tm
    row = pl.BlockSpec((1, tm, D), lambda b, i: (b, i, 0))
    vec = pl.BlockSpec((1, D), lambda b, i: (0, 0))
    m_l = pl.BlockSpec((1, 6, D), lambda b, i: (b, 0, 0))
    m_c = pl.BlockSpec((1, 6, D), lambda b, i: (B, 0, 0))
    args, specs = [xs], [row]
    if has_f:
        args.append(f_lat)
        specs.append(pl.BlockSpec((1, tm, D), lambda b, i: (b, jnp.minimum(i, n_lt - 1), 0)))
    if has_ctx_f:
        args.append(f_ctx)
        specs.append(pl.BlockSpec((1, tm, D), lambda b, i: (b, jnp.maximum(i - n_lt, 0), 0)))
    if has_f:
        args.append(g_post.reshape(1, D))
        specs.append(vec)
    args += [mod_l, mod_l]
    specs += [m_l, m_c]
    if emit_h:
        args += [g_pre_next.reshape(1, D), mod_next, mod_next]
        specs += [vec, m_l, m_c]
    out_shape, out_specs = [], []
    if has_f:
        out_shape.append(jax.ShapeDtypeStruct((B, n_rows, D), F32))
        out_specs.append(row)
    if emit_h:
        out_shape.append(jax.ShapeDtypeStruct((B, n_rows, D), BF16))
        out_specs.append(row)
    kern = functools.partial(_stream_kernel, tm=tm, n_lat=n_lat, has_f=has_f, has_ctx_f=has_ctx_f,
                             emit_h=emit_h)
    return pl.pallas_call(
        kern, out_shape=tuple(out_shape), grid=(B, n_rows // tm), in_specs=specs,
        out_specs=tuple(out_specs), compiler_params=_cparams("parallel", "parallel"),
        name="stream_update",
    )(*args)


def _in_kernel(h_ref, w_ref, o_ref):
    o_ref[0] = _bdot(h_ref[0], w_ref[...]).astype(o_ref.dtype)


def _in_proj(h, w):
    B, T, D = h.shape
    N = w.shape[1]
    tm = _row_tile(T, 4)
    tn = _pick(N, (1280, 1024, 896, 768, 640, 512, 384, 256, 128))
    return pl.pallas_call(
        _in_kernel,
        out_shape=jax.ShapeDtypeStruct((B, T, N), BF16),
        grid=(B, T // tm, N // tn),
        in_specs=[pl.BlockSpec((1, tm, D), lambda b, i, j: (b, i, 0)),
                  pl.BlockSpec((D, tn), lambda b, i, j: (0, j))],
        out_specs=pl.BlockSpec((1, tm, tn), lambda b, i, j: (b, i, j)),
        compiler_params=_cparams("parallel", "parallel", "arbitrary"),
        name="in_proj",
    )(h, w)


def _rot(x, cos, sin):
    qd = x.shape[1] // 4
    first = (_iota((1, x.shape[1]), 1) // qd) % 2 == 0
    swapped = jnp.where(first, pltpu.roll(x, x.shape[1] - qd, 1), pltpu.roll(x, qd, 1))
    return x * cos + swapped * sin


def _head_norm(x, g):
    return x * lax.rsqrt(jnp.mean(x * x, axis=-1, keepdims=True) + NORM_EPS) * g


def _attn_kernel(bound_ref, q_ref, k_ref, v_ref, cq_ref, sq_ref, ck_ref, sk_ref, qn_ref, kn_ref, *rest,
                 hd, group, scale):
    o_ref, kp_ref, vp_ref = rest[-3:]
    unit = jnp.where(_iota((1, hd), 1) == 0, 1.0, 0.0)

    @pl.when(pl.program_id(2) == 0)
    def _():
        k = _rot(_head_norm(k_ref[0].astype(F32), kn_ref[...]), ck_ref[...], sk_ref[...])
        kp_ref[...] = jnp.concatenate([k, jnp.broadcast_to(unit, k.shape)], axis=1).astype(kp_ref.dtype)
        ones = jnp.ones(v_ref.shape[1:], vp_ref.dtype)
        vp_ref[...] = jnp.concatenate([v_ref[0].astype(vp_ref.dtype), ones], axis=1)

    bound = bound_ref[0]

    def attend(shift_in_matmul):
        for g in range(group):
            q = _head_norm(q_ref[0, :, g * hd:(g + 1) * hd].astype(F32), qn_ref[...])
            q = _rot(q, cq_ref[...], sq_ref[...]) * scale
            if shift_in_matmul:
                q_aug = jnp.concatenate([q, jnp.broadcast_to(unit * (-bound), q.shape)], axis=1)
                p = jnp.exp2(_bdot(q_aug, kp_ref[...], NT))
            else:
                s = _bdot(q, kp_ref[:, :hd], NT)
                p = jnp.exp2(s - jnp.max(s, axis=-1, keepdims=True))
            ov = _bdot(p, vp_ref[...])
            o_ref[0, :, g * hd:(g + 1) * hd] = (ov[:, :hd] / ov[:, hd:hd + 1]).astype(o_ref.dtype)

    fast = bound < ATTN_SHIFT_LIMIT
    pl.when(fast)(functools.partial(attend, True))
    pl.when(jnp.logical_not(fast))(functools.partial(attend, False))


def _attention(p_qkv, q_col, k_col, v_col, cos, sin, q_norm, k_norm, hd, hq, hkv, q_rows, key_rows, out=None):
    B, T, _ = p_qkv.shape
    group = hq // hkv
    gw = group * hd
    (q0, nq), (k0, nk) = q_rows, key_rows
    tq = next(t for t in (512, 256, 128, 64, 32, 16) if nq % t == 0 and q0 % t == 0)
    assert k0 % nk == 0
    kern = functools.partial(_attn_kernel, hd=hd, group=group, scale=hd ** -0.5 * LOG2_E)
    qb, kb = q0 // tq, k0 // nk
    tab_q = pl.BlockSpec((tq, hd), lambda b, h, i, s: (i + qb, 0))
    tab_k = pl.BlockSpec((nk, hd), lambda b, h, i, s: (kb, 0))
    vec = pl.BlockSpec((1, hd), lambda b, h, i, s: (0, 0))
    bound = (1.02 * hd * hd ** -0.5 * LOG2_E) * jnp.max(jnp.abs(q_norm)) * jnp.max(jnp.abs(k_norm))
    in_specs = [pl.BlockSpec((1, tq, gw), lambda b, h, i, s: (b, i + qb, q_col // gw + h)),
                pl.BlockSpec((1, nk, hd), lambda b, h, i, s: (b, kb, k_col // hd + h)),
                pl.BlockSpec((1, nk, hd), lambda b, h, i, s: (b, kb, v_col // hd + h)),
                tab_q, tab_q, tab_k, tab_k, vec, vec]
    args = [bound.reshape(1).astype(F32), p_qkv, p_qkv, p_qkv, cos, sin, cos, sin,
            q_norm.reshape(1, hd), k_norm.reshape(1, hd)]
    aliases = {}
    if out is not None:
        in_specs.append(pl.BlockSpec(memory_space=pl.ANY))
        args.append(out)
        aliases = {len(args) - 1: 0}
    grid_spec = pltpu.PrefetchScalarGridSpec(
        num_scalar_prefetch=1, grid=(B, hkv, nq // tq), in_specs=in_specs,
        out_specs=pl.BlockSpec((1, tq, gw), lambda b, h, i, s: (b, i + qb, h)),
        scratch_shapes=[pltpu.VMEM((nk, 2 * hd), BF16), pltpu.VMEM((nk, 2 * hd), BF16)])
    return pl.pallas_call(
        kern,
        out_shape=jax.ShapeDtypeStruct((B, T, hq * hd), BF16),
        grid_spec=grid_spec,
        input_output_aliases=aliases,
        compiler_params=_cparams("parallel", "parallel", "arbitrary"),
        name="attention",
    )(*args)


def _segment_of_tile(i, tt, n_lat_tiles, n_tiles):
    is_lat = i < n_lat_tiles
    ti = jnp.where(is_lat, i, i - n_lat_tiles)
    seg_tiles = jnp.where(is_lat, n_lat_tiles, n_tiles - n_lat_tiles)
    return ti, seg_tiles


def _halo_specs(tt, width, T, col_block):
    hb = tt // HALO
    last = T // HALO - 1
    prev = pl.BlockSpec((1, HALO, width), lambda b, i: (b, jnp.maximum(i * hb - 1, 0), col_block))
    cur = pl.BlockSpec((1, tt, width), lambda b, i: (b, i, col_block))
    nxt = pl.BlockSpec((1, HALO, width), lambda b, i: (b, jnp.minimum((i + 1) * hb, last), col_block))
    return prev, cur, nxt


def _pool_kernel(prev_ref, cur_ref, next_ref, wg_ref, sc_ref, o_ref, *,
                 tt, n_lat_tiles, n_tiles, gw, windows):
    ti, seg_tiles = _segment_of_tile(pl.program_id(1), tt, n_lat_tiles, n_tiles)
    has_prev = ti > 0
    has_next = ti < seg_tiles - 1
    t_seg = seg_tiles * tt
    tpos = ti * tt + _iota((tt, 1), 0)

    cur = cur_ref[0]
    prev = prev_ref[0]
    nxt = next_ref[0]
    d_cur = _iota((tt, tt), 1) - _iota((tt, tt), 0)
    d_halo = _iota((tt, HALO), 1) - _iota((tt, HALO), 0)
    d_prev = d_halo - HALO
    d_next = d_halo + tt
    for g, win in enumerate(windows):
        lo_off = -(win // 2)
        hi_off = win - win // 2 - 1
        sl = slice(g * gw, (g + 1) * gw)
        band_c = ((d_cur >= lo_off) & (d_cur <= hi_off)).astype(BF16)
        band_p = ((d_prev >= lo_off) & (d_prev <= hi_off) & has_prev).astype(BF16)
        band_n = ((d_next >= lo_off) & (d_next <= hi_off) & has_next).astype(BF16)
        ug = cur[:, sl]
        tot = _bdot(band_c, ug) + _bdot(band_p, prev[:, sl]) + _bdot(band_n, nxt[:, sl])
        lo = jnp.maximum(tpos + lo_off, 0)
        hi = jnp.minimum(tpos + hi_off + 1, t_seg)
        pooled = tot / (hi - lo).astype(F32) - ug.astype(F32)
        y = _bdot(pooled, wg_ref[g]) * sc_ref[:, sl]
        o_ref[0, :, sl] = y.astype(o_ref.dtype)


def _pool(p_arr, col_block, w_group, scale, n_lat):
    B, T, _ = p_arr.shape
    G, gw, _ = w_group.shape
    W = G * gw
    tt = _pick(T - n_lat, (256, 128, 64, 32, 16))
    assert n_lat % tt == 0 and tt % HALO == 0 and max(POOL_WINDOWS) <= HALO
    n_tiles = T // tt
    kern = functools.partial(_pool_kernel, tt=tt, n_lat_tiles=n_lat // tt, n_tiles=n_tiles,
                             gw=gw, windows=POOL_WINDOWS)
    prev, cur, nxt = _halo_specs(tt, W, T, col_block)
    return pl.pallas_call(
        kern,
        out_shape=jax.ShapeDtypeStruct((B, T, W), BF16),
        grid=(B, n_tiles),
        in_specs=[prev, cur, nxt,
                  pl.BlockSpec((G, gw, gw), lambda b, i: (0, 0, 0)),
                  pl.BlockSpec((1, W), lambda b, i: (0, 0))],
        out_specs=pl.BlockSpec((1, tt, W), lambda b, i: (b, i, 0)),
        compiler_params=_cparams("parallel", "parallel"),
        name="pool",
    )(p_arr, p_arr, p_arr, w_group, scale.reshape(1, W))


def _head_sum(x, head):
    lanes = x.shape[1]
    blk = min(lanes, V7X_MXU)
    same = (_iota((blk, blk), 0) // head == _iota((blk, blk), 1) // head).astype(BF16)
    parts = [_split_dot(x[:, j:j + blk], same) for j in range(0, lanes, blk)]
    return jnp.concatenate(parts, axis=1) if len(parts) > 1 else parts[0]


def _rwkv_prep_kernel(prev_ref, cur_ref, next_ref, mu_ref, w0_ref, w2_ref, a0_ref, a2_ref, g2_ref,
                      kk_w_ref, ka_ref, rk_ref,
                      r_ref, v_ref, kk_ref, lw_ref, cum_ref, kd_ref, bd_ref, g_ref, bonus_ref, *,
                      tt, n_lat_tiles, n_tiles, W, head, lora, chunk):
    ti, seg_tiles = _segment_of_tile(pl.program_id(1), tt, n_lat_tiles, n_tiles)
    u = cur_ref[0].astype(F32)
    row = _iota((tt, 1), 0)
    before = jnp.where(ti > 0, prev_ref[0, HALO - 1:HALO, :].astype(F32), 0.0)
    after = jnp.where(ti < seg_tiles - 1, next_ref[0, 0:1, :].astype(F32), 0.0)
    down, up = pltpu.roll(u, 1, 0), pltpu.roll(u, tt - 1, 0)
    u_prev = jnp.concatenate([jnp.where(row[:8] == 0, before, down[:8]), down[8:]], axis=0)
    u_next = jnp.concatenate([up[:tt - 8], jnp.where(row[tt - 8:] == tt - 1, after, up[tt - 8:])], axis=0)
    mu = mu_ref[...]
    u = u * (1.0 - mu) + (u_prev + u_next) * (0.5 * mu)

    r, k, v = u[:, :W], u[:, W:2 * W], u[:, 2 * W:3 * W]
    o1 = 3 * W
    o2 = o1 + lora
    o3 = o2 + lora
    w_lin = w0_ref[...] + _bdot(jnp.tanh(u[:, o1:o2]), w2_ref[...])
    lw = (-DECAY_FLOOR) * _sigmoid(w_lin)
    lw_ref[0] = lw
    ri, ci = _iota((tt, tt), 0), _iota((tt, tt), 1)
    same_chunk = ri // chunk == ci // chunk
    p1 = lw.astype(BF16)
    p2 = (lw - p1.astype(F32)).astype(BF16)
    for z, tri in enumerate((same_chunk & (ci <= ri), same_chunk & (ci >= ri))):
        sl = slice(z * W, (z + 1) * W)
        cum_ref[0, :, sl] = _bdot(tri, p1[:, sl]) + _bdot(tri, p2[:, sl])
    a = _sigmoid(a0_ref[...] + _bdot(u[:, o2:o3], a2_ref[...]))
    g_ref[0] = _bdot(_sigmoid(u[:, o3:]), g2_ref[...]).astype(g_ref.dtype)

    kk = k * kk_w_ref[...]
    kk = kk * lax.rsqrt(jnp.maximum(_head_sum(kk * kk, head), 1e-24))
    k_sum = 0.0
    for z in range(N_DIR):
        a_z = a[:, z * W:(z + 1) * W]
        k_z = k * (1.0 + (a_z - 1.0) * ka_ref[...])
        kd_ref[0, :, z * W:(z + 1) * W] = k_z.astype(kd_ref.dtype)
        bd_ref[0, :, z * W:(z + 1) * W] = (kk * a_z).astype(bd_ref.dtype)
        k_sum = k_sum + k_z
    r_ref[0] = r.astype(r_ref.dtype)
    v_ref[0] = v.astype(v_ref.dtype)
    kk_ref[0] = kk.astype(kk_ref.dtype)
    bonus_ref[0] = (_head_sum(r * k_sum * rk_ref[...], head) * v).astype(bonus_ref.dtype)


def _rwkv_prep(p_rwkv, n_lat, W, head, mu, w0, w2, a0, a2, g2, k_k, k_a, r_k):
    B, T, NS = p_rwkv.shape
    lora = N_DIR * w2.shape[1]
    n_gate = NS - 3 * W - 2 * lora
    tt = _pick(T - n_lat, (256, 128, 64, 32, 16))
    n_tiles = T // tt
    def cat(m):
        z = jnp.zeros_like(m[0])
        return jnp.concatenate([jnp.concatenate([m[0], z], axis=1),
                                jnp.concatenate([z, m[1]], axis=1)], axis=0).astype(BF16)
    g2p = jnp.pad(g2, ((0, n_gate - g2.shape[0]), (0, 0))).astype(BF16)
    mup = jnp.pad(mu, (0, NS - mu.shape[0])).reshape(1, NS)
    kern = functools.partial(_rwkv_prep_kernel, tt=tt, n_lat_tiles=n_lat // tt, n_tiles=n_tiles,
                             W=W, head=head, lora=lora, chunk=RWKV_CHUNK)
    assert tt % RWKV_CHUNK == 0
    prev, cur, nxt = _halo_specs(tt, NS, T, 0)
    full = lambda shp: pl.BlockSpec(shp, lambda b, i: (0,) * len(shp))
    tile = lambda w: pl.BlockSpec((1, tt, w), lambda b, i: (b, i, 0))
    sd = lambda w, dt: jax.ShapeDtypeStruct((B, T, w), dt)
    return pl.pallas_call(
        kern,
        out_shape=(sd(W, BF16), sd(W, BF16), sd(W, BF16), sd(2 * W, F32), sd(2 * W, F32), sd(2 * W, BF16),
                   sd(2 * W, BF16), sd(W, BF16), sd(W, BF16)),
        grid=(B, n_tiles),
        in_specs=[prev, cur, nxt, full((1, NS)), full((1, 2 * W)), full((lora, 2 * W)),
                  full((1, 2 * W)), full((lora, 2 * W)), full((n_gate, W)),
                  full((1, W)), full((1, W)), full((1, W))],
        out_specs=(tile(W), tile(W), tile(W), tile(2 * W), tile(2 * W), tile(2 * W), tile(2 * W), tile(W),
                   tile(W)),
        compiler_params=_cparams("parallel", "parallel"),
        name="rwkv_prep",
    )(p_rwkv, p_rwkv, p_rwkv, mup, w0.reshape(1, 2 * W), cat(w2), a0.reshape(1, 2 * W), cat(a2), g2p,
      k_k.reshape(1, W), k_a.reshape(1, W), r_k.reshape(1, W))


def _rwkv_kernel(rf_ref, vf_ref, kkf_ref, rb_ref, vb_ref, kkb_ref, lwf_ref, cumf_ref, kf_ref, bf_ref,
                 lwb_ref, cumb_ref, kb_ref, bb_ref, yf_ref, yb_ref, g_ref, *, C, N):
    @pl.when(pl.program_id(1) == 0)
    def _():
        g_ref[...] = jnp.zeros_like(g_ref)

    W = rf_ref.shape[2]
    PW = 2 * N
    n_pairs = W // PW
    psl = [slice(p * PW, (p + 1) * PW) for p in range(n_pairs)]
    first = _iota((1, PW), 1) < N
    masks = (first, jnp.logical_not(first))
    same_head = (_iota((PW, PW), 0) // N) == (_iota((PW, PW), 1) // N)
    zero = jnp.zeros((C, PW), F32)
    diff = _iota((C, C), 0) - _iota((C, C), 1)

    dirs = []
    for d, (r_ref, v_ref, kk_ref, lw_ref, cum_ref, k_ref, b_ref) in enumerate((
            (rf_ref, vf_ref, kkf_ref, lwf_ref, cumf_ref, kf_ref, bf_ref),
            (rb_ref, vb_ref, kkb_ref, lwb_ref, cumb_ref, kb_ref, bb_ref))):
        order = diff if d == 0 else -diff
        order2 = jnp.concatenate([order, order], axis=1)
        cum = cum_ref[0]
        pc = cum[C - 1:C, :] if d == 0 else cum[0:1, :]
        k = k_ref[0].astype(F32)
        b = b_ref[0].astype(F32)
        p_inv = jnp.exp(-cum)
        p_hat = jnp.exp(pc - cum)
        dirs.append(dict(
            strict2=order2 > 0, incl2=order2 >= 0,
            r_t=r_ref[0].astype(F32) * jnp.exp(cum), k_t=k * p_inv, b_t=b * p_inv,
            a_t=-kk_ref[0].astype(F32) * jnp.exp(cum - lw_ref[0]),
            b_h=b * p_hat, k_h=k * p_hat, p_c=jnp.exp(pc), v=v_ref[0].astype(F32)))

    items = [(d, p) for p in range(n_pairs) for d in range(N_DIR)]
    v_m, upper, lower = [], [], []
    for d, p in items:
        t, sl = dirs[d], psl[p]
        rhs_bk = jnp.concatenate([t["b_t"][:, sl], t["k_t"][:, sl]], axis=0)
        lhs = jnp.concatenate([jnp.where(m, x[:, sl], 0.0) for m in masks for x in (t["a_t"], t["r_t"])],
                              axis=0)
        amat = _bdot(lhs, rhs_bk, NT)
        for sub in range(2):
            upper.append(jnp.where(t["strict2"], amat[2 * sub * C:(2 * sub + 1) * C], 0.0))
            lower.append(jnp.where(t["incl2"], amat[(2 * sub + 1) * C:(2 * sub + 2) * C], 0.0))
            v_m.append(jnp.where(masks[sub], t["v"][:, sl], 0.0))
    pw, x = [], []
    for i, (d, p) in enumerate(items):
        up0, up1 = upper[2 * i], upper[2 * i + 1]
        a_ak = jnp.where(first, pltpu.roll(up0, C, 1), up1)
        w = _bdot(a_ak, jnp.concatenate([v_m[2 * i], v_m[2 * i + 1]], axis=0))
        w = pltpu.roll(w, N, 1)
        a_p = dirs[d]["a_t"][:, psl[p]]
        x += [jnp.where(first, a_p, w), jnp.where(first, w, a_p)]
        pw.append(jnp.where(first, up0, pltpu.roll(up1, C, 1)))
    span = 1
    while span < C:
        span *= 2
        for i in range(len(items)):
            x0, x1 = x[2 * i], x[2 * i + 1]
            rhs = [jnp.concatenate([x0, zero], axis=1), jnp.concatenate([zero, x1], axis=1)]
            if span < C:
                rhs = [jnp.concatenate([jnp.where(m, pw[i], 0.0), xr], axis=1) for m, xr in zip(masks, rhs)]
            res = _bdot(pw[i], jnp.concatenate(rhs, axis=0))
            if span < C:
                pw[i] = res[:, :PW]
                res = res[:, PW:]
            x[2 * i], x[2 * i + 1] = x0 + res[:, :PW], x1 + res[:, PW:]

    g0 = [g_ref[d * n_pairs + p] for d, p in items]
    uv, rg = [], []
    for i, (d, p) in enumerate(items):
        x0, x1 = x[2 * i], x[2 * i + 1]
        ahat = [jnp.where(masks[0], x0, 0.0), jnp.where(masks[1], x1, 0.0)]
        vhat = [jnp.where(masks[0], pltpu.roll(x0, N, 1), 0.0), jnp.where(masks[1], pltpu.roll(x1, N, 1), 0.0)]
        res = _bdot(jnp.concatenate([dirs[d]["r_t"][:, psl[p]]] + ahat, axis=0), g0[i])
        rg.append(res[:C])
        u0 = res[C:2 * C] + vhat[0]
        u1 = res[2 * C:] + vhat[1]
        uv.append(jnp.concatenate([u0, v_m[2 * i], u1, v_m[2 * i + 1]], axis=0).astype(BF16))
    for i, (d, p) in enumerate(items):
        y_ref = yf_ref if d == 0 else yb_ref
        y = rg[i] + _bdot(jnp.concatenate([lower[2 * i], lower[2 * i + 1]], axis=1), uv[i])
        y_ref[0, :, psl[p]] = y.astype(y_ref.dtype)
    for i, (d, p) in enumerate(items):
        t, sl = dirs[d], psl[p]
        bk_h = jnp.concatenate([t["b_h"][:, sl], t["k_h"][:, sl]], axis=0)
        upd = _bdot(jnp.concatenate([bk_h, bk_h], axis=0), uv[i], TN)
        decay = jnp.transpose(jnp.broadcast_to(t["p_c"][:, sl], (PW, PW)))
        g_ref[d * n_pairs + p] = jnp.where(same_head, decay * g0[i] + upd, 0.0)


def _rwkv_scan(r, v, kk, lw, cum, kd, bd, head, n_lat):
    B, T, W = r.shape
    C = RWKV_CHUNK
    assert C == head
    nc = T // C
    nc_ctx = nc - n_lat // C

    def fwd(c):
        return jnp.where(c < nc_ctx, nc - nc_ctx + c, c - nc_ctx)

    def bwd(c):
        return nc - 1 - c

    spec = lambda chunk, col: pl.BlockSpec((1, C, W), lambda b, c: (b, chunk(c), col))
    kern = functools.partial(_rwkv_kernel, C=C, N=head)
    out = jax.ShapeDtypeStruct((B, T, W), BF16)
    return pl.pallas_call(
        kern,
        out_shape=(out, out),
        grid=(B, nc),
        in_specs=[spec(fwd, 0)] * 3 + [spec(bwd, 0)] * 3 + [spec(fwd, 0)] * 4 + [spec(bwd, 1)] * 4,
        out_specs=(spec(fwd, 0), spec(bwd, 0)),
        scratch_shapes=[pltpu.VMEM((N_DIR * W // (2 * head), 2 * head, 2 * head), F32)],
        compiler_params=_cparams("parallel", "arbitrary"),
        name="rwkv7_chunk",
    )(r, v, kk, r, v, kk, lw, cum, kd, bd, lw, cum, kd, bd)


def _sigmoid(x):
    return 0.5 * jnp.tanh(0.5 * x) + 0.5


def _merge_kernel(attn_ref, pool_ref, yf_ref, yb_ref, g_ref, bonus_ref, lnw_ref, lnb_ref, gate_ref,
                  wa_ref, wp_ref, wr_ref, o_ref, *, head, D):
    y = yf_ref[0].astype(F32) + yb_ref[0].astype(F32)
    inv_n = 1.0 / head
    dev = y - _head_sum(y, head) * inv_n
    var = _head_sum(dev * dev, head) * inv_n
    yn = dev * lax.rsqrt(var + GN_EPS) * lnw_ref[...] + lnb_ref[...]
    rw = (yn + bonus_ref[0].astype(F32)) * g_ref[0].astype(F32)

    def gate(z):
        return _sigmoid(gate_ref[0, :, z * D:(z + 1) * D].astype(F32))

    out = gate(0) * _bdot(attn_ref[0], wa_ref[...])
    out = out + gate(1) * _bdot(pool_ref[0], wp_ref[...])
    out = out + gate(2) * _bdot(rw, wr_ref[...])
    o_ref[0] = out.astype(o_ref.dtype)


def _merge(attn, pool, y_fwd, y_bwd, g, bonus, ln_w, ln_b, gates, w_a, w_p, w_r, head):
    B, T, _ = attn.shape
    W = g.shape[2]
    D = w_a.shape[1]
    tm = _row_tile(T, 16)
    row = lambda w: pl.BlockSpec((1, tm, w), lambda b, i: (b, i, 0))
    vec = pl.BlockSpec((1, W), lambda b, i: (0, 0))
    wsp = lambda w: pl.BlockSpec(w.shape, lambda b, i: (0, 0), pipeline_mode=pl.Buffered(1))
    return pl.pallas_call(
        functools.partial(_merge_kernel, head=head, D=D),
        out_shape=jax.ShapeDtypeStruct((B, T, D), BF16),
        grid=(B, T // tm),
        in_specs=[row(attn.shape[2]), row(pool.shape[2]), row(W), row(W),
                  row(W), row(W), vec, vec, row(N_BRANCH * D),
                  wsp(w_a), wsp(w_p), wsp(w_r)],
        out_specs=row(D),
        compiler_params=_cparams("parallel", "parallel"),
        name="merge",
    )(attn, pool, y_fwd, y_bwd, g, bonus, ln_w.reshape(1, W), ln_b.reshape(1, W), gates, w_a, w_p, w_r)


def _out_kernel(m_ref, w_ref, x_ref, gpost_ref, gpre_ref, ml_ref, mc_ref, wr_ref,
                x_out, h_out, aff_out, *, tm, n_lat, n_e):
    th = tm // 2
    for half in range(2):
        rows = slice(half * th, (half + 1) * th)
        is_lat, m_lat, m_ctx = _mod_rows(ml_ref, mc_ref, pl.program_id(1) * tm + half * th, th, n_lat)
        mix = _bdot(m_ref[0, rows, :], w_ref[...])
        normed = mix * lax.rsqrt(jnp.mean(mix * mix, axis=-1, keepdims=True) + NORM_EPS) * gpost_ref[...]
        x = x_ref[0, rows, :] + jnp.where(is_lat, m_lat[2:3], m_ctx[2:3]) * normed
        x_out[0, rows, :] = x
        h = _modnorm(x, gpre_ref[...], is_lat, m_lat, m_ctx, 3)
        h_out[0, rows, :] = _pack_halves(h)
        h_hi = h.astype(BF16)
        part = _bdot(h_hi, wr_ref[...]) + _bdot(h - h_hi.astype(F32), wr_ref[...])
        logits = part[:, :V7X_LANES] + part[:, V7X_LANES:]
        logits = jnp.where(_iota(logits.shape, 1) < n_e, logits, -jnp.inf)
        e = jnp.exp(logits - jnp.max(logits, axis=-1, keepdims=True))
        aff_out[0, rows, :] = e / jnp.sum(e, axis=-1, keepdims=True)


def _out_proj(merged, w_out, xs, g_post, g_pre, mod_l, w_router, n_lat):
    B, T, D = xs.shape
    E = w_router.shape[1]
    tm = _pick(T, (544, 272, 256, 128, 320, 64, 32, 16))
    wr = jnp.pad(w_router, ((0, 0), (0, V7X_LANES - E)))
    wr_hi = wr.astype(BF16)
    wr = jnp.concatenate([wr_hi, (wr - wr_hi.astype(F32)).astype(BF16)], axis=1)
    row = lambda w: pl.BlockSpec((1, tm, w), lambda b, i: (b, i, 0))
    vec = pl.BlockSpec((1, D), lambda b, i: (0, 0))
    return pl.pallas_call(
        functools.partial(_out_kernel, tm=tm, n_lat=n_lat, n_e=E),
        out_shape=(jax.ShapeDtypeStruct((B, T, D), F32), jax.ShapeDtypeStruct((B, T, D // 2), jnp.uint32),
                   jax.ShapeDtypeStruct((B, T, V7X_LANES), F32)),
        grid=(B, T // tm),
        in_specs=[row(D), pl.BlockSpec((D, D), lambda b, i: (0, 0)), row(D), vec, vec,
                  pl.BlockSpec((1, 6, D), lambda b, i: (b, 0, 0)),
                  pl.BlockSpec((1, 6, D), lambda b, i: (B, 0, 0)),
                  pl.BlockSpec((D, 2 * V7X_LANES), lambda b, i: (0, 0))],
        out_specs=(row(D), row(D // 2), row(V7X_LANES)),
        compiler_params=_cparams("parallel", "parallel"),
        name="out_proj",
    )(merged, w_out, xs, g_post.reshape(1, D), g_pre.reshape(1, D), mod_l, mod_l, wr)


def _excl_prefix(flags, blk):
    n = flags.shape[-1]
    upper = (_iota((blk, blk), 0) < _iota((blk, blk), 1)).astype(BF16)
    outs = []
    carry = jnp.zeros((flags.shape[0], 1), F32)
    for j in range(n // blk):
        seg = flags[:, j * blk:(j + 1) * blk]
        outs.append(_bdot(seg, upper) + carry)
        carry = carry + jnp.sum(seg, axis=-1, keepdims=True)
    return jnp.concatenate(outs, axis=-1) if len(outs) > 1 else outs[0]


def _topk_kernel(aff_ref, sel_ref, idx_ref, *, cap, blk):
    bits = lax.bitcast_convert_type(aff_ref[0], jnp.int32)
    E = bits.shape[0]

    def body(i, tau):
        cand = tau | jnp.left_shift(jnp.int32(1), 30 - i)
        cnt = jnp.sum((bits >= cand).astype(jnp.int32), axis=-1, keepdims=True)
        return jnp.where(cnt >= cap, cand, tau)

    tau = lax.fori_loop(0, 31, body, jnp.zeros((E, 1), jnp.int32))
    gt = bits > tau
    eq = bits == tau
    need = (cap - jnp.sum(gt.astype(jnp.int32), axis=-1, keepdims=True)).astype(F32)
    eq_rank = _excl_prefix(eq.astype(F32), blk)
    sel = gt | (eq & (eq_rank < need))
    pos = _excl_prefix(sel.astype(F32), blk)
    slots = jnp.where(sel, pos.astype(jnp.int32), -1)
    sel_ref[0] = slots
    n = slots.shape[1]
    tok = _iota((8, n), 1)
    digit = _iota((8, n), 0)
    digits = jnp.where(digit == 0, tok // TOKEN_RADIX, jnp.where(digit == 1, tok % TOKEN_RADIX, 0))
    for e in range(E):
        onehot = slots[e:e + 1, :] == _iota((cap, n), 0)
        d = _bdot(digits.astype(F32), onehot.astype(F32), NT)
        idx_ref[0, e:e + 1, :] = (d[0:1] * TOKEN_RADIX + d[1:2]).astype(jnp.int32)


def _topk_slots(aff_t, cap):
    B, E, n = aff_t.shape
    assert n <= TOKEN_RADIX * 256
    blk = _pick(n, (512, 256, 128))
    return pl.pallas_call(
        functools.partial(_topk_kernel, cap=cap, blk=blk),
        out_shape=(jax.ShapeDtypeStruct((B, E, n), jnp.int32),
                   jax.ShapeDtypeStruct((B, E, cap), jnp.int32)),
        grid=(B,),
        in_specs=[pl.BlockSpec((1, E, n), lambda b: (b, 0, 0))],
        out_specs=(pl.BlockSpec((1, E, n), lambda b: (b, 0, 0)),
                   pl.BlockSpec((1, E, cap), lambda b: (b, 0, 0))),
        compiler_params=_cparams("parallel"),
        name="expert_topk",
    )(aff_t)


def _gather_kernel(idx_ref, h_ref, xe_ref, *, cap, n_e):
    base = (pl.program_id(0) * n_e + pl.program_id(1)) * cap

    def body(s, carry):
        t = idx_ref[base + s]
        xe_ref[0, 0, pl.ds(s, 1), :] = h_ref[0, pl.ds(t, 1), :]
        return carry

    lax.fori_loop(0, cap, body, 0, unroll=8)


def _gather(idx, hp, row_block, n):
    B, E, cap = idx.shape
    dh = hp.shape[2]
    grid_spec = pltpu.PrefetchScalarGridSpec(
        num_scalar_prefetch=1, grid=(B, E),
        in_specs=[pl.BlockSpec((1, n, dh), lambda b, e, idx_ref: (b, row_block, 0))],
        out_specs=pl.BlockSpec((1, 1, cap, dh), lambda b, e, idx_ref: (b, e, 0, 0)))
    return pl.pallas_call(
        functools.partial(_gather_kernel, cap=cap, n_e=E),
        out_shape=jax.ShapeDtypeStruct((B, E, cap, dh), jnp.uint32),
        grid_spec=grid_spec,
        compiler_params=_cparams("parallel", "arbitrary"),
        name="expert_gather",
    )(idx.reshape(-1), hp)


def _pack_halves(h):
    bits = lax.bitcast_convert_type(h.astype(BF16).astype(F32), jnp.uint32)
    half = h.shape[1] // 2
    return (bits[:, :half] >> 16) | (bits[:, half:] & jnp.uint32(0xFFFF0000))


def _unpack_halves(p):
    lo = lax.bitcast_convert_type(p << 16, F32)
    hi = lax.bitcast_convert_type(p & jnp.uint32(0xFFFF0000), F32)
    return lo, hi


def _ffn_kernel(*refs, n_sets):
    xe_refs = refs[:n_sets]
    wg_ref, wu_ref, wd_ref = refs[n_sets:n_sets + 3]
    ye_refs = refs[n_sets + 3:2 * n_sets + 3]
    wgb_ref, wub_ref, wdb_ref = refs[2 * n_sets + 3:]

    @pl.when(pl.program_id(2) == 0)
    def _():
        wgb_ref[...] = wg_ref[0, 0].astype(wgb_ref.dtype)
        wub_ref[...] = wu_ref[0, 0].astype(wub_ref.dtype)
        wdb_ref[...] = wd_ref[0, 0].astype(wdb_ref.dtype)

    for xe_ref, ye_ref in zip(xe_refs, ye_refs):
        lo, hi = _unpack_halves(xe_ref[0, 0])
        half = lo.shape[1]
        gate = _bdot(lo, wgb_ref[:half, :]) + _bdot(hi, wgb_ref[half:, :])
        up = _bdot(lo, wub_ref[:half, :]) + _bdot(hi, wub_ref[half:, :])
        hid = gate * _sigmoid(gate) * up
        ye_ref[0, 0, 0] = _bdot(hid, wdb_ref[...]).astype(ye_ref.dtype)


def _expert_ffn(xes, w_gate, w_up, w_down, layer):
    B, E, _, dh = xes[0].shape
    D = 2 * dh
    FF = w_gate.shape[3]
    fs = FF // FFN_SLABS
    n_sets = len(xes)
    return pl.pallas_call(
        functools.partial(_ffn_kernel, n_sets=n_sets),
        out_shape=tuple(jax.ShapeDtypeStruct((FFN_SLABS, B, E, xe.shape[2], D), BF16) for xe in xes),
        grid=(E, FFN_SLABS, B),
        in_specs=[pl.BlockSpec((1, 1, xe.shape[2], dh), lambda e, s, b: (b, e, 0, 0)) for xe in xes]
        + [pl.BlockSpec((1, 1, D, fs), lambda e, s, b: (layer, e, 0, s)),
           pl.BlockSpec((1, 1, D, fs), lambda e, s, b: (layer, e, 0, s)),
           pl.BlockSpec((1, 1, fs, D), lambda e, s, b: (layer, e, s, 0))],
        out_specs=tuple(pl.BlockSpec((1, 1, 1, xe.shape[2], D), lambda e, s, b: (s, b, e, 0, 0))
                        for xe in xes),
        scratch_shapes=[pltpu.VMEM((D, fs), BF16), pltpu.VMEM((D, fs), BF16), pltpu.VMEM((fs, D), BF16)],
        compiler_params=_cparams("parallel", "parallel", "arbitrary"),
        name="expert_ffn",
    )(*xes, w_gate, w_up, w_down)


def _scatter_kernel(selt_ref, aff_ref, ye_ref, o_ref, *, cap, n_e):
    sel_all = selt_ref[0]
    aff_all = aff_ref[0]
    slot = _iota((1, cap), 1)
    acc = None
    for e in range(n_e):
        onehot = sel_all[:, e:e + 1] == slot
        ye = ye_ref[0, 0, e].astype(F32)
        for s in range(1, ye_ref.shape[0]):
            ye = ye + ye_ref[s, 0, e].astype(F32)
        term = aff_all[:, e:e + 1] * _bdot(onehot, ye)
        acc = term if acc is None else acc + term
    o_ref[0] = acc.astype(o_ref.dtype)


def _scatter(sel, aff, row_block, ye):
    B, E, n = sel.shape
    n_slab, cap, D = ye.shape[0], ye.shape[3], ye.shape[4]
    td = _pick(D, (256, 128))
    return pl.pallas_call(
        functools.partial(_scatter_kernel, cap=cap, n_e=E),
        out_shape=jax.ShapeDtypeStruct((B, n, D), BF16),
        grid=(B, D // td),
        in_specs=[pl.BlockSpec((1, n, E), lambda b, j: (b, 0, 0)),
                  pl.BlockSpec((1, n, aff.shape[2]), lambda b, j: (b, row_block, 0)),
                  pl.BlockSpec((n_slab, 1, E, cap, td), lambda b, j: (0, b, 0, 0, j))],
        out_specs=pl.BlockSpec((1, n, td), lambda b, j: (b, 0, j)),
        compiler_params=_cparams("parallel", "parallel"),
        name="expert_scatter",
    )(jnp.swapaxes(sel, 1, 2), aff, ye)


def _expert_choice(hb, aff, sets, w_gate, w_up, w_down, layer):
    E = w_gate.shape[1]
    sels, xes = [], []
    for row_block, n in sets:
        aff_t = jnp.swapaxes(aff[:, row_block * n:(row_block + 1) * n, :E], 1, 2)
        sel, idx = _topk_slots(aff_t, EC_FACTOR * n // E)
        sels.append(sel)
        xes.append(_gather(idx, hb, row_block, n))
    yes = _expert_ffn(xes, w_gate, w_up, w_down, layer)
    return [_scatter(sel, aff, row_block, ye) for sel, (row_block, n), ye in zip(sels, sets, yes)]


def _rope_tables(n_lat, n_ctx, head_dim):
    rows = n_lat // GRID_W
    row = jnp.repeat(jnp.arange(rows), GRID_W).astype(F32)
    col = (jnp.arange(rows * GRID_W) % GRID_W).astype(F32)
    half = head_dim // 2
    inv = ROPE_THETA ** (-jnp.arange(0, half, 2, dtype=F32) / half)
    ar, ac = row[:, None] * inv, col[:, None] * inv
    cos = jnp.concatenate([jnp.cos(ar), jnp.cos(ar), jnp.cos(ac), jnp.cos(ac)], axis=-1)
    sin = jnp.concatenate([-jnp.sin(ar), jnp.sin(ar), -jnp.sin(ac), jnp.sin(ac)], axis=-1)
    cos = jnp.concatenate([cos, jnp.ones((n_ctx, head_dim), F32)], axis=0)
    sin = jnp.concatenate([sin, jnp.zeros((n_ctx, head_dim), F32)], axis=0)
    return cos, sin


def _pad_cols(w, mult):
    return jnp.pad(w, ((0, 0), (0, (-w.shape[1]) % mult)))


def kernel(x, c, ctx, c_ctx, w_mod, b_mod, norm_pre, norm_post, w_in, q_norm, k_norm, w_attn_o, w_pool_group, pool_scale, w_pool_o, rwkv_mu, rwkv_w0, rwkv_w2, rwkv_a0, rwkv_a2, rwkv_g2, rwkv_k_k, rwkv_k_a, rwkv_r_k, rwkv_ln_w, rwkv_ln_b, w_rwkv_o, w_out, w_router, w_exp_gate, w_exp_up, w_exp_down):
    B, n_lat, D = x.shape
    n_ctx = ctx.shape[1]
    T = n_lat + n_ctx
    depth = w_mod.shape[0]
    hd = q_norm.shape[1]
    attn_w = w_attn_o.shape[1]
    pool_w = w_pool_o.shape[1]
    rwkv_w = w_rwkv_o.shape[1]
    n_shift = rwkv_mu.shape[1]
    n_in = w_in.shape[2]
    kv_w = (n_in - attn_w - pool_w - n_shift - N_BRANCH * D) // 2
    head = rwkv_r_k.shape[2]
    col_k = attn_w
    col_v = col_k + kv_w
    col_pool = col_v + kv_w
    col_r = col_pool + pool_w
    col_gate = col_r + n_shift
    hq, hkv = attn_w // hd, kv_w // hd
    assert n_lat % (hq // hkv * hd) == 0 and pool_w % (hq // hkv * hd) == 0 and n_ctx % RWKV_CHUNK == 0

    cos, sin = _rope_tables(n_lat, n_ctx, hd)
    s_all = jnp.concatenate([jax.nn.silu(c), jax.nn.silu(c_ctx)[None]], axis=0)
    s_all = jnp.pad(s_all, ((0, (-s_all.shape[0]) % 8), (0, 0)))
    mod = _modulation(s_all, w_mod, b_mod)[:, :B + 1].reshape(depth, B + 1, 6, D)
    xs = jnp.concatenate([x, ctx], axis=1)
    (h,) = _stream_update(xs, None, None, None, mod[0], norm_pre[0, 0], mod[0], n_lat, T)

    for l in range(depth):
        keep_ctx = l < depth - 1
        wl = w_in[l]
        w_a = jnp.concatenate([wl[:, col_pool:col_r], wl[:, :col_pool]], axis=1).astype(BF16)
        w_b = _pad_cols(wl[:, col_r:col_gate], 3 * V7X_LANES).astype(BF16)
        w_c = wl[:, col_gate:].astype(BF16)
        p_a = _in_proj(h, w_a)
        p_b = _in_proj(h, w_b)
        p_c = _in_proj(h, w_c)

        qkv = (p_a, pool_w, pool_w + attn_w, pool_w + attn_w + kv_w, cos, sin, q_norm[l], k_norm[l], hd, hq, hkv)
        attn = _attention(*qkv, (0, n_lat), (0, T))
        attn = _attention(*qkv, (n_lat, n_ctx), (n_lat, n_ctx), out=attn)
        pool = _pool(p_a, 0, w_pool_group[l].astype(BF16), pool_scale[l], n_lat)
        r, v, kk, lw, cum, kd, bd, g, bonus = _rwkv_prep(
            p_b, n_lat, rwkv_w, head, rwkv_mu[l], rwkv_w0[l], rwkv_w2[l], rwkv_a0[l], rwkv_a2[l],
            rwkv_g2[l], rwkv_k_k[l], rwkv_k_a[l], rwkv_r_k[l])
        y_fwd, y_bwd = _rwkv_scan(r, v, kk, lw, cum, kd, bd, head, n_lat)
        merged = _merge(attn, pool, y_fwd, y_bwd, g, bonus, rwkv_ln_w[l], rwkv_ln_b[l], p_c,
                        w_attn_o[l].astype(BF16), w_pool_o[l].astype(BF16), w_rwkv_o[l].astype(BF16), head)
        xs, hb, aff = _out_proj(merged, w_out[l].astype(BF16), xs, norm_post[l, 0], norm_pre[l, 1],
                                mod[l], w_router[l], n_lat)

        sets = [(0, n_lat)] + ([(n_lat // n_ctx, n_ctx)] if keep_ctx else [])
        mixed = _expert_choice(hb, aff, sets, w_exp_gate, w_exp_up, w_exp_down, l)
        f_lat = mixed[0]
        if keep_ctx:
            f_ctx = mixed[1]
            xs, h = _stream_update(xs, f_lat, f_ctx, norm_post[l, 1], mod[l], norm_pre[l + 1, 0],
                                   mod[l + 1], n_lat, T)
        else:
            (xs,) = _stream_update(xs, f_lat, None, norm_post[l, 1], mod[l], None, None, n_lat, n_lat)
    return xs
```

```python
import functools

import jax
import jax.numpy as jnp
from jax import lax
from jax.experimental import pallas as pl
from jax.experimental.pallas import tpu as pltpu

F32 = jnp.float32
BF16 = jnp.bfloat16

GRID_W = 64
NORM_EPS = 1e-6
ROPE_THETA = 10000.0
POOL_WINDOWS = (2, 4, 8, 16)
GN_EPS = 64e-5
EC_FACTOR = 2
N_DIR = 2
N_BRANCH = 3
RWKV_CHUNK = 64
HALO = 16

LOG2_E = 1.4426950408889634
DECAY_FLOOR = 0.6065306597126334
ATTN_SHIFT_LIMIT = 60.0
FFN_SLABS = 2
TOKEN_RADIX = 64

V7X_LANES = 128
V7X_MXU = 256
V7X_VMEM_BYTES = 64 * 1024 * 1024
VMEM_LIMIT = V7X_VMEM_BYTES * 13 // 16

NT = (((1,), (1,)), ((), ()))
TN = (((0,), (0,)), ((), ()))


def _pick(n, cands):
    for c in cands:
        if c <= n and n % c == 0:
            return c
    return n


def _cparams(*sem):
    return pltpu.CompilerParams(dimension_semantics=sem, vmem_limit_bytes=VMEM_LIMIT)


def _bdot(x, y, dn=None):
    x = x.astype(BF16)
    y = y.astype(BF16)
    if dn is None:
        return jnp.dot(x, y, preferred_element_type=F32)
    return lax.dot_general(x, y, dn, preferred_element_type=F32)


def _split_dot(x, y):
    hi = x.astype(BF16)
    lo = x - hi.astype(F32)
    return _bdot(hi, y) + _bdot(lo, y)


def _iota(shape, dim):
    return lax.broadcasted_iota(jnp.int32, shape, dim)


def _row_tile(T, min_parts=8):
    for parts in range(min_parts, T // 16 + 1):
        if T % parts == 0 and (T // parts) % 16 == 0:
            return T // parts
    return T


def _pieces(x):
    p1 = x.astype(BF16)
    r1 = x - p1.astype(F32)
    p2 = r1.astype(BF16)
    p3 = (r1 - p2.astype(F32)).astype(BF16)
    return p1, p2, p3


def _mod_kernel(s_ref, w_ref, b_ref, o_ref):
    s1, s2, s3 = (p.astype(F32) for p in _pieces(s_ref[...]))
    w1, w2, w3 = _pieces(w_ref[0])
    R = s1.shape[0]
    a = _bdot(jnp.concatenate([s1, s2, s3], axis=0), w1)
    b = _bdot(jnp.concatenate([s1, s2], axis=0), w2)
    c = _bdot(s1, w3)

    @pl.when(pl.program_id(1) == 0)
    def _():
        o_ref[0] = jnp.broadcast_to(b_ref[0], o_ref.shape[1:])

    o_ref[0] += (a[:R] + a[R:2 * R] + a[2 * R:]) + (b[:R] + b[R:]) + c


def _modulation(s, w_mod, b_mod):
    L, D, N = w_mod.shape
    R = s.shape[0]
    tk = _pick(D, (128,))
    return pl.pallas_call(
        _mod_kernel,
        out_shape=jax.ShapeDtypeStruct((L, R, N), F32),
        grid=(L, D // tk),
        in_specs=[pl.BlockSpec((R, tk), lambda l, k: (0, k)),
                  pl.BlockSpec((1, tk, N), lambda l, k: (l, k, 0)),
                  pl.BlockSpec((1, 1, N), lambda l, k: (l, 0, 0))],
        out_specs=pl.BlockSpec((1, R, N), lambda l, k: (l, 0, 0)),
        compiler_params=_cparams("parallel", "arbitrary"),
        name="adaln_mod",
    )(s, w_mod, b_mod.reshape(L, 1, N))


def _mod_rows(m_lat_ref, m_ctx_ref, row0, tm, n_lat):
    is_lat = (row0 + _iota((tm, 1), 0)) < n_lat
    return is_lat, m_lat_ref[0], m_ctx_ref[0]


def _modnorm(x, g, is_lat, m_lat, m_ctx, i_shift):
    shift = jnp.where(is_lat, m_lat[i_shift:i_shift + 1], m_ctx[i_shift:i_shift + 1])
    scale = jnp.where(is_lat, m_lat[i_shift + 1:i_shift + 2], m_ctx[i_shift + 1:i_shift + 2])
    y = x * lax.rsqrt(jnp.mean(x * x, axis=-1, keepdims=True) + NORM_EPS) * g
    return y * (1.0 + scale) + shift


def _segment_row_tile(n_ctx):
    return _pick(n_ctx, (256, 128, 64, 32, 16))


def _stream_kernel(*refs, tm, n_lat, has_f, has_ctx_f, emit_h):
    refs = list(refs)
    x_ref = refs.pop(0)
    fl_ref = refs.pop(0) if has_f else None
    fc_ref = refs.pop(0) if has_ctx_f else None
    gpost_ref = refs.pop(0) if has_f else None
    ml_ref, mc_ref = refs.pop(0), refs.pop(0)
    if emit_h:
        gpre_ref, mln_ref, mcn_ref = refs.pop(0), refs.pop(0), refs.pop(0)
    x_out = refs.pop(0) if has_f else None
    h_out = refs.pop(0) if emit_h else None

    is_lat = (pl.program_id(1) * tm + _iota((tm, 1), 0)) < n_lat
    x = x_ref[0]
    if has_f:
        f = fl_ref[0]
        if has_ctx_f:
            f = jnp.where(is_lat, f, fc_ref[0])
        normed = f * lax.rsqrt(jnp.mean(f * f, axis=-1, keepdims=True) + NORM_EPS) * gpost_ref[...]
        x = x + jnp.where(is_lat, ml_ref[0][5:6], mc_ref[0][5:6]) * normed
        x_out[0] = x
    if emit_h:
        h_out[0] = _modnorm(x, gpre_ref[...], is_lat, mln_ref[0], mcn_ref[0], 0).astype(h_out.dtype)


def _stream_update(xs, f_lat, f_ctx, g_post, mod_l, g_pre_next, mod_next, n_lat, n_rows):
    B, T, D = xs.shape
    n_ctx = T - n_lat
    tm = _segment_row_tile(n_ctx)
    has_f, has_ctx_f, emit_h = f_lat is not None, f_ctx is not None, g_pre_next is not None
    n_lt = n_lat // tm
    row = pl.BlockSpec((1, tm, D), lambda b, i: (b, i, 0))
    vec = pl.BlockSpec((1, D), lambda b, i: (0, 0))
    m_l = pl.BlockSpec((1, 6, D), lambda b, i: (b, 0, 0))
    m_c = pl.BlockSpec((1, 6, D), lambda b, i: (B, 0, 0))
    args, specs = [xs], [row]
    if has_f:
        args.append(f_lat)
        specs.append(pl.BlockSpec((1, tm, D), lambda b, i: (b, jnp.minimum(i, n_lt - 1), 0)))
    if has_ctx_f:
        args.append(f_ctx)
        specs.append(pl.BlockSpec((1, tm, D), lambda b, i: (b, jnp.maximum(i - n_lt, 0), 0)))
    if has_f:
        args.append(g_post.reshape(1, D))
        specs.append(vec)
    args += [mod_l, mod_l]
    specs += [m_l, m_c]
    if emit_h:
        args += [g_pre_next.reshape(1, D), mod_next, mod_next]
        specs += [vec, m_l, m_c]
    out_shape, out_specs = [], []
    if has_f:
        out_shape.append(jax.ShapeDtypeStruct((B, n_rows, D), F32))
        out_specs.append(row)
    if emit_h:
        out_shape.append(jax.ShapeDtypeStruct((B, n_rows, D), BF16))
        out_specs.append(row)
    kern = functools.partial(_stream_kernel, tm=tm, n_lat=n_lat, has_f=has_f, has_ctx_f=has_ctx_f,
                             emit_h=emit_h)
    return pl.pallas_call(
        kern, out_shape=tuple(out_shape), grid=(B, n_rows // tm), in_specs=specs,
        out_specs=tuple(out_specs), compiler_params=_cparams("parallel", "parallel"),
        name="stream_update",
    )(*args)


def _in_kernel(h_ref, w_ref, o_ref):
    o_ref[0] = _bdot(h_ref[0], w_ref[...]).astype(o_ref.dtype)


def _in_proj(h, w):
    B, T, D = h.shape
    N = w.shape[1]
    tm = _row_tile(T, 4)
    tn = _pick(N, (1280, 1024, 896, 768, 640, 512, 384, 256, 128))
    return pl.pallas_call(
        _in_kernel,
        out_shape=jax.ShapeDtypeStruct((B, T, N), BF16),
        grid=(B, T // tm, N // tn),
        in_specs=[pl.BlockSpec((1, tm, D), lambda b, i, j: (b, i, 0)),
                  pl.BlockSpec((D, tn), lambda b, i, j: (0, j))],
        out_specs=pl.BlockSpec((1, tm, tn), lambda b, i, j: (b, i, j)),
        compiler_params=_cparams("parallel", "parallel", "arbitrary"),
        name="in_proj",
    )(h, w)


def _rot(x, cos, sin):
    qd = x.shape[1] // 4
    first = (_iota((1, x.shape[1]), 1) // qd) % 2 == 0
    swapped = jnp.where(first, pltpu.roll(x, x.shape[1] - qd, 1), pltpu.roll(x, qd, 1))
    return x * cos + swapped * sin


def _head_norm(x, g):
    return x * lax.rsqrt(jnp.mean(x * x, axis=-1, keepdims=True) + NORM_EPS) * g


def _attn_kernel(bound_ref, q_ref, k_ref, v_ref, cq_ref, sq_ref, ck_ref, sk_ref, qn_ref, kn_ref, *rest,
                 hd, group, scale):
    o_ref, kp_ref, vp_ref = rest[-3:]
    unit = jnp.where(_iota((1, hd), 1) == 0, 1.0, 0.0)

    @pl.when(pl.program_id(2) == 0)
    def _():
        k = _rot(_head_norm(k_ref[0].astype(F32), kn_ref[...]), ck_ref[...], sk_ref[...])
        kp_ref[...] = jnp.concatenate([k, jnp.broadcast_to(unit, k.shape)], axis=1).astype(kp_ref.dtype)
        ones = jnp.ones(v_ref.shape[1:], vp_ref.dtype)
        vp_ref[...] = jnp.concatenate([v_ref[0].astype(vp_ref.dtype), ones], axis=1)

    bound = bound_ref[0]

    def attend(shift_in_matmul):
        for g in range(group):
            q = _head_norm(q_ref[0, :, g * hd:(g + 1) * hd].astype(F32), qn_ref[...])
            q = _rot(q, cq_ref[...], sq_ref[...]) * scale
            if shift_in_matmul:
                q_aug = jnp.concatenate([q, jnp.broadcast_to(unit * (-bound), q.shape)], axis=1)
                p = jnp.exp2(_bdot(q_aug, kp_ref[...], NT))
            else:
                s = _bdot(q, kp_ref[:, :hd], NT)
                p = jnp.exp2(s - jnp.max(s, axis=-1, keepdims=True))
            ov = _bdot(p, vp_ref[...])
            o_ref[0, :, g * hd:(g + 1) * hd] = (ov[:, :hd] / ov[:, hd:hd + 1]).astype(o_ref.dtype)

    fast = bound < ATTN_SHIFT_LIMIT
    pl.when(fast)(functools.partial(attend, True))
    pl.when(jnp.logical_not(fast))(functools.partial(attend, False))


def _attention(p_qkv, q_col, k_col, v_col, cos, sin, q_norm, k_norm, hd, hq, hkv, q_rows, key_rows, out=None):
    B, T, _ = p_qkv.shape
    group = hq // hkv
    gw = group * hd
    (q0, nq), (k0, nk) = q_rows, key_rows
    tq = next(t for t in (512, 256, 128, 64, 32, 16) if nq % t == 0 and q0 % t == 0)
    assert k0 % nk == 0
    kern = functools.partial(_attn_kernel, hd=hd, group=group, scale=hd ** -0.5 * LOG2_E)
    qb, kb = q0 // tq, k0 // nk
    tab_q = pl.BlockSpec((tq, hd), lambda b, h, i, s: (i + qb, 0))
    tab_k = pl.BlockSpec((nk, hd), lambda b, h, i, s: (kb, 0))
    vec = pl.BlockSpec((1, hd), lambda b, h, i, s: (0, 0))
    bound = (1.02 * hd * hd ** -0.5 * LOG2_E) * jnp.max(jnp.abs(q_norm)) * jnp.max(jnp.abs(k_norm))
    in_specs = [pl.BlockSpec((1, tq, gw), lambda b, h, i, s: (b, i + qb, q_col // gw + h)),
                pl.BlockSpec((1, nk, hd), lambda b, h, i, s: (b, kb, k_col // hd + h)),
                pl.BlockSpec((1, nk, hd), lambda b, h, i, s: (b, kb, v_col // hd + h)),
                tab_q, tab_q, tab_k, tab_k, vec, vec]
    args = [bound.reshape(1).astype(F32), p_qkv, p_qkv, p_qkv, cos, sin, cos, sin,
            q_norm.reshape(1, hd), k_norm.reshape(1, hd)]
    aliases = {}
    if out is not None:
        in_specs.append(pl.BlockSpec(memory_space=pl.ANY))
        args.append(out)
        aliases = {len(args) - 1: 0}
    grid_spec = pltpu.PrefetchScalarGridSpec(
        num_scalar_prefetch=1, grid=(B, hkv, nq // tq), in_specs=in_specs,
        out_specs=pl.BlockSpec((1, tq, gw), lambda b, h, i, s: (b, i + qb, h)),
        scratch_shapes=[pltpu.VMEM((nk, 2 * hd), BF16), pltpu.VMEM((nk, 2 * hd), BF16)])
    return pl.pallas_call(
        kern,
        out_shape=jax.ShapeDtypeStruct((B, T, hq * hd), BF16),
        grid_spec=grid_spec,
        input_output_aliases=aliases,
        compiler_params=_cparams("parallel", "parallel", "arbitrary"),
        name="attention",
    )(*args)


def _segment_of_tile(i, tt, n_lat_tiles, n_tiles):
    is_lat = i < n_lat_tiles
    ti = jnp.where(is_lat, i, i - n_lat_tiles)
    seg_tiles = jnp.where(is_lat, n_lat_tiles, n_tiles - n_lat_tiles)
    return ti, seg_tiles


def _halo_specs(tt, width, T, col_block):
    hb = tt // HALO
    last = T // HALO - 1
    prev = pl.BlockSpec((1, HALO, width), lambda b, i: (b, jnp.maximum(i * hb - 1, 0), col_block))
    cur = pl.BlockSpec((1, tt, width), lambda b, i: (b, i, col_block))
    nxt = pl.BlockSpec((1, HALO, width), lambda b, i: (b, jnp.minimum((i + 1) * hb, last), col_block))
    return prev, cur, nxt


def _pool_kernel(prev_ref, cur_ref, next_ref, wg_ref, sc_ref, o_ref, *,
                 tt, n_lat_tiles, n_tiles, gw, windows):
    ti, seg_tiles = _segment_of_tile(pl.program_id(1), tt, n_lat_tiles, n_tiles)
    has_prev = ti > 0
    has_next = ti < seg_tiles - 1
    t_seg = seg_tiles * tt
    tpos = ti * tt + _iota((tt, 1), 0)

    cur = cur_ref[0]
    prev = prev_ref[0]
    nxt = next_ref[0]
    d_cur = _iota((tt, tt), 1) - _iota((tt, tt), 0)
    d_halo = _iota((tt, HALO), 1) - _iota((tt, HALO), 0)
    d_prev = d_halo - HALO
    d_next = d_halo + tt
    for g, win in enumerate(windows):
        lo_off = -(win // 2)
        hi_off = win - win // 2 - 1
        sl = slice(g * gw, (g + 1) * gw)
        band_c = ((d_cur >= lo_off) & (d_cur <= hi_off)).astype(BF16)
        band_p = ((d_prev >= lo_off) & (d_prev <= hi_off) & has_prev).astype(BF16)
        band_n = ((d_next >= lo_off) & (d_next <= hi_off) & has_next).astype(BF16)
        ug = cur[:, sl]
        tot = _bdot(band_c, ug) + _bdot(band_p, prev[:, sl]) + _bdot(band_n, nxt[:, sl])
        lo = jnp.maximum(tpos + lo_off, 0)
        hi = jnp.minimum(tpos + hi_off + 1, t_seg)
        pooled = tot / (hi - lo).astype(F32) - ug.astype(F32)
        y = _bdot(pooled, wg_ref[g]) * sc_ref[:, sl]
        o_ref[0, :, sl] = y.astype(o_ref.dtype)


def _pool(p_arr, col_block, w_group, scale, n_lat):
    B, T, _ = p_arr.shape
    G, gw, _ = w_group.shape
    W = G * gw
    tt = _pick(T - n_lat, (256, 128, 64, 32, 16))
    assert n_lat % tt == 0 and tt % HALO == 0 and max(POOL_WINDOWS) <= HALO
    n_tiles = T // tt
    kern = functools.partial(_pool_kernel, tt=tt, n_lat_tiles=n_lat // tt, n_tiles=n_tiles,
                             gw=gw, windows=POOL_WINDOWS)
    prev, cur, nxt = _halo_specs(tt, W, T, col_block)
    return pl.pallas_call(
        kern,
        out_shape=jax.ShapeDtypeStruct((B, T, W), BF16),
        grid=(B, n_tiles),
        in_specs=[prev, cur, nxt,
                  pl.BlockSpec((G, gw, gw), lambda b, i: (0, 0, 0)),
                  pl.BlockSpec((1, W), lambda b, i: (0, 0))],
        out_specs=pl.BlockSpec((1, tt, W), lambda b, i: (b, i, 0)),
        compiler_params=_cparams("parallel", "parallel"),
        name="pool",
    )(p_arr, p_arr, p_arr, w_group, scale.reshape(1, W))


def _head_sum(x, head):
    lanes = x.shape[1]
    blk = min(lanes, V7X_MXU)
    same = (_iota((blk, blk), 0) // head == _iota((blk, blk), 1) // head).astype(BF16)
    parts = [_split_dot(x[:, j:j + blk], same) for j in range(0, lanes, blk)]
    return jnp.concatenate(parts, axis=1) if len(parts) > 1 else parts[0]


def _rwkv_prep_kernel(prev_ref, cur_ref, next_ref, mu_ref, w0_ref, w2_ref, a0_ref, a2_ref, g2_ref,
                      kk_w_ref, ka_ref, rk_ref,
                      r_ref, v_ref, kk_ref, lw_ref, cum_ref, kd_ref, bd_ref, g_ref, bonus_ref, *,
                      tt, n_lat_tiles, n_tiles, W, head, lora, chunk):
    ti, seg_tiles = _segment_of_tile(pl.program_id(1), tt, n_lat_tiles, n_tiles)
    u = cur_ref[0].astype(F32)
    row = _iota((tt, 1), 0)
    before = jnp.where(ti > 0, prev_ref[0, HALO - 1:HALO, :].astype(F32), 0.0)
    after = jnp.where(ti < seg_tiles - 1, next_ref[0, 0:1, :].astype(F32), 0.0)
    down, up = pltpu.roll(u, 1, 0), pltpu.roll(u, tt - 1, 0)
    u_prev = jnp.concatenate([jnp.where(row[:8] == 0, before, down[:8]), down[8:]], axis=0)
    u_next = jnp.concatenate([up[:tt - 8], jnp.where(row[tt - 8:] == tt - 1, after, up[tt - 8:])], axis=0)
    mu = mu_ref[...]
    u = u * (1.0 - mu) + (u_prev + u_next) * (0.5 * mu)

    r, k, v = u[:, :W], u[:, W:2 * W], u[:, 2 * W:3 * W]
    o1 = 3 * W
    o2 = o1 + lora
    o3 = o2 + lora
    w_lin = w0_ref[...] + _bdot(jnp.tanh(u[:, o1:o2]), w2_ref[...])
    lw = (-DECAY_FLOOR) * _sigmoid(w_lin)
    lw_ref[0] = lw
    ri, ci = _iota((tt, tt), 0), _iota((tt, tt), 1)
    same_chunk = ri // chunk == ci // chunk
    p1 = lw.astype(BF16)
    p2 = (lw - p1.astype(F32)).astype(BF16)
    for z, tri in enumerate((same_chunk & (ci <= ri), same_chunk & (ci >= ri))):
        sl = slice(z * W, (z + 1) * W)
        cum_ref[0, :, sl] = _bdot(tri, p1[:, sl]) + _bdot(tri, p2[:, sl])
    a = _sigmoid(a0_ref[...] + _bdot(u[:, o2:o3], a2_ref[...]))
    g_ref[0] = _bdot(_sigmoid(u[:, o3:]), g2_ref[...]).astype(g_ref.dtype)

    kk = k * kk_w_ref[...]
    kk = kk * lax.rsqrt(jnp.maximum(_head_sum(kk * kk, head), 1e-24))
    k_sum = 0.0
    for z in range(N_DIR):
        a_z = a[:, z * W:(z + 1) * W]
        k_z = k * (1.0 + (a_z - 1.0) * ka_ref[...])
        kd_ref[0, :, z * W:(z + 1) * W] = k_z.astype(kd_ref.dtype)
        bd_ref[0, :, z * W:(z + 1) * W] = (kk * a_z).astype(bd_ref.dtype)
        k_sum = k_sum + k_z
    r_ref[0] = r.astype(r_ref.dtype)
    v_ref[0] = v.astype(v_ref.dtype)
    kk_ref[0] = kk.astype(kk_ref.dtype)
    bonus_ref[0] = (_head_sum(r * k_sum * rk_ref[...], head) * v).astype(bonus_ref.dtype)


def _rwkv_prep(p_rwkv, n_lat, W, head, mu, w0, w2, a0, a2, g2, k_k, k_a, r_k):
    B, T, NS = p_rwkv.shape
    lora = N_DIR * w2.shape[1]
    n_gate = NS - 3 * W - 2 * lora
    tt = _pick(T - n_lat, (256, 128, 64, 32, 16))
    n_tiles = T // tt
    def cat(m):
        z = jnp.zeros_like(m[0])
        return jnp.concatenate([jnp.concatenate([m[0], z], axis=1),
                                jnp.concatenate([z, m[1]], axis=1)], axis=0).astype(BF16)
    g2p = jnp.pad(g2, ((0, n_gate - g2.shape[0]), (0, 0))).astype(BF16)
    mup = jnp.pad(mu, (0, NS - mu.shape[0])).reshape(1, NS)
    kern = functools.partial(_rwkv_prep_kernel, tt=tt, n_lat_tiles=n_lat // tt, n_tiles=n_tiles,
                             W=W, head=head, lora=lora, chunk=RWKV_CHUNK)
    assert tt % RWKV_CHUNK == 0
    prev, cur, nxt = _halo_specs(tt, NS, T, 0)
    full = lambda shp: pl.BlockSpec(shp, lambda b, i: (0,) * len(shp))
    tile = lambda w: pl.BlockSpec((1, tt, w), lambda b, i: (b, i, 0))
    sd = lambda w, dt: jax.ShapeDtypeStruct((B, T, w), dt)
    return pl.pallas_call(
        kern,
        out_shape=(sd(W, BF16), sd(W, BF16), sd(W, BF16), sd(2 * W, F32), sd(2 * W, F32), sd(2 * W, BF16),
                   sd(2 * W, BF16), sd(W, BF16), sd(W, BF16)),
        grid=(B, n_tiles),
        in_specs=[prev, cur, nxt, full((1, NS)), full((1, 2 * W)), full((lora, 2 * W)),
                  full((1, 2 * W)), full((lora, 2 * W)), full((n_gate, W)),
                  full((1, W)), full((1, W)), full((1, W))],
        out_specs=(tile(W), tile(W), tile(W), tile(2 * W), tile(2 * W), tile(2 * W), tile(2 * W), tile(W),
                   tile(W)),
        compiler_params=_cparams("parallel", "parallel"),
        name="rwkv_prep",
    )(p_rwkv, p_rwkv, p_rwkv, mup, w0.reshape(1, 2 * W), cat(w2), a0.reshape(1, 2 * W), cat(a2), g2p,
      k_k.reshape(1, W), k_a.reshape(1, W), r_k.reshape(1, W))


def _rwkv_kernel(rf_ref, vf_ref, kkf_ref, rb_ref, vb_ref, kkb_ref, lwf_ref, cumf_ref, kf_ref, bf_ref,
                 lwb_ref, cumb_ref, kb_ref, bb_ref, yf_ref, yb_ref, g_ref, *, C, N):
    @pl.when(pl.program_id(1) == 0)
    def _():
        g_ref[...] = jnp.zeros_like(g_ref)

    W = rf_ref.shape[2]
    PW = 2 * N
    n_pairs = W // PW
    psl = [slice(p * PW, (p + 1) * PW) for p in range(n_pairs)]
    first = _iota((1, PW), 1) < N
    masks = (first, jnp.logical_not(first))
    same_head = (_iota((PW, PW), 0) // N) == (_iota((PW, PW), 1) // N)
    zero = jnp.zeros((C, PW), F32)
    diff = _iota((C, C), 0) - _iota((C, C), 1)

    dirs = []
    for d, (r_ref, v_ref, kk_ref, lw_ref, cum_ref, k_ref, b_ref) in enumerate((
            (rf_ref, vf_ref, kkf_ref, lwf_ref, cumf_ref, kf_ref, bf_ref),
            (rb_ref, vb_ref, kkb_ref, lwb_ref, cumb_ref, kb_ref, bb_ref))):
        order = diff if d == 0 else -diff
        order2 = jnp.concatenate([order, order], axis=1)
        cum = cum_ref[0]
        pc = cum[C - 1:C, :] if d == 0 else cum[0:1, :]
        k = k_ref[0].astype(F32)
        b = b_ref[0].astype(F32)
        p_inv = jnp.exp(-cum)
        p_hat = jnp.exp(pc - cum)
        dirs.append(dict(
            strict2=order2 > 0, incl2=order2 >= 0,
            r_t=r_ref[0].astype(F32) * jnp.exp(cum), k_t=k * p_inv, b_t=b * p_inv,
            a_t=-kk_ref[0].astype(F32) * jnp.exp(cum - lw_ref[0]),
            b_h=b * p_hat, k_h=k * p_hat, p_c=jnp.exp(pc), v=v_ref[0].astype(F32)))

    items = [(d, p) for p in range(n_pairs) for d in range(N_DIR)]
    v_m, upper, lower = [], [], []
    for d, p in items:
        t, sl = dirs[d], psl[p]
        rhs_bk = jnp.concatenate([t["b_t"][:, sl], t["k_t"][:, sl]], axis=0)
        lhs = jnp.concatenate([jnp.where(m, x[:, sl], 0.0) for m in masks for x in (t["a_t"], t["r_t"])],
                              axis=0)
        amat = _bdot(lhs, rhs_bk, NT)
        for sub in range(2):
            upper.append(jnp.where(t["strict2"], amat[2 * sub * C:(2 * sub + 1) * C], 0.0))
            lower.append(jnp.where(t["incl2"], amat[(2 * sub + 1) * C:(2 * sub + 2) * C], 0.0))
            v_m.append(jnp.where(masks[sub], t["v"][:, sl], 0.0))
    pw, x = [], []
    for i, (d, p) in enumerate(items):
        up0, up1 = upper[2 * i], upper[2 * i + 1]
        a_ak = jnp.where(first, pltpu.roll(up0, C, 1), up1)
        w = _bdot(a_ak, jnp.concatenate([v_m[2 * i], v_m[2 * i + 1]], axis=0))
        w = pltpu.roll(w, N, 1)
        a_p = dirs[d]["a_t"][:, psl[p]]
        x += [jnp.where(first, a_p, w), jnp.where(first, w, a_p)]
        pw.append(jnp.where(first, up0, pltpu.roll(up1, C, 1)))
    span = 1
    while span < C:
        span *= 2
        for i in range(len(items)):
            x0, x1 = x[2 * i], x[2 * i + 1]
            rhs = [jnp.concatenate([x0, zero], axis=1), jnp.concatenate([zero, x1], axis=1)]
            if span < C:
                rhs = [jnp.concatenate([jnp.where(m, pw[i], 0.0), xr], axis=1) for m, xr in zip(masks, rhs)]
            res = _bdot(pw[i], jnp.concatenate(rhs, axis=0))
            if span < C:
                pw[i] = res[:, :PW]
                res = res[:, PW:]
            x[2 * i], x[2 * i + 1] = x0 + res[:, :PW], x1 + res[:, PW:]

    g0 = [g_ref[d * n_pairs + p] for d, p in items]
    uv, rg = [], []
    for i, (d, p) in enumerate(items):
        x0, x1 = x[2 * i], x[2 * i + 1]
        ahat = [jnp.where(masks[0], x0, 0.0), jnp.where(masks[1], x1, 0.0)]
        vhat = [jnp.where(masks[0], pltpu.roll(x0, N, 1), 0.0), jnp.where(masks[1], pltpu.roll(x1, N, 1), 0.0)]
        res = _bdot(jnp.concatenate([dirs[d]["r_t"][:, psl[p]]] + ahat, axis=0), g0[i])
        rg.append(res[:C])
        u0 = res[C:2 * C] + vhat[0]
        u1 = res[2 * C:] + vhat[1]
        uv.append(jnp.concatenate([u0, v_m[2 * i], u1, v_m[2 * i + 1]], axis=0).astype(BF16))
    for i, (d, p) in enumerate(items):
        y_ref = yf_ref if d == 0 else yb_ref
        y_ref[0, :, psl[p]] = rg[i] + _bdot(jnp.concatenate([lower[2 * i], lower[2 * i + 1]], axis=1), uv[i])
    for i, (d, p) in enumerate(items):
        t, sl = dirs[d], psl[p]
        bk_h = jnp.concatenate([t["b_h"][:, sl], t["k_h"][:, sl]], axis=0)
        upd = _bdot(jnp.concatenate([bk_h, bk_h], axis=0), uv[i], TN)
        decay = jnp.transpose(jnp.broadcast_to(t["p_c"][:, sl], (PW, PW)))
        g_ref[d * n_pairs + p] = jnp.where(same_head, decay * g0[i] + upd, 0.0)


def _rwkv_scan(r, v, kk, lw, cum, kd, bd, head, n_lat):
    B, T, W = r.shape
    C = RWKV_CHUNK
    assert C == head
    nc = T // C
    nc_ctx = nc - n_lat // C

    def fwd(c):
        return jnp.where(c < nc_ctx, nc - nc_ctx + c, c - nc_ctx)

    def bwd(c):
        return nc - 1 - c

    spec = lambda chunk, col: pl.BlockSpec((1, C, W), lambda b, c: (b, chunk(c), col))
    kern = functools.partial(_rwkv_kernel, C=C, N=head)
    out = jax.ShapeDtypeStruct((B, T, W), F32)
    return pl.pallas_call(
        kern,
        out_shape=(out, out),
        grid=(B, nc),
        in_specs=[spec(fwd, 0)] * 3 + [spec(bwd, 0)] * 3 + [spec(fwd, 0)] * 4 + [spec(bwd, 1)] * 4,
        out_specs=(spec(fwd, 0), spec(bwd, 0)),
        scratch_shapes=[pltpu.VMEM((N_DIR * W // (2 * head), 2 * head, 2 * head), F32)],
        compiler_params=_cparams("parallel", "arbitrary"),
        name="rwkv7_chunk",
    )(r, v, kk, r, v, kk, lw, cum, kd, bd, lw, cum, kd, bd)


def _sigmoid(x):
    return 0.5 * jnp.tanh(0.5 * x) + 0.5


def _merge_kernel(attn_ref, pool_ref, yf_ref, yb_ref, g_ref, bonus_ref, lnw_ref, lnb_ref, gate_ref,
                  wa_ref, wp_ref, wr_ref, o_ref, *, head, D):
    y = yf_ref[0] + yb_ref[0]
    inv_n = 1.0 / head
    dev = y - _head_sum(y, head) * inv_n
    var = _head_sum(dev * dev, head) * inv_n
    yn = dev * lax.rsqrt(var + GN_EPS) * lnw_ref[...] + lnb_ref[...]
    rw = (yn + bonus_ref[0].astype(F32)) * g_ref[0].astype(F32)

    def gate(z):
        return _sigmoid(gate_ref[0, :, z * D:(z + 1) * D].astype(F32))

    out = gate(0) * _bdot(attn_ref[0], wa_ref[...])
    out = out + gate(1) * _bdot(pool_ref[0], wp_ref[...])
    out = out + gate(2) * _bdot(rw, wr_ref[...])
    o_ref[0] = out.astype(o_ref.dtype)


def _merge(attn, pool, y_fwd, y_bwd, g, bonus, ln_w, ln_b, gates, w_a, w_p, w_r, head):
    B, T, _ = attn.shape
    W = g.shape[2]
    D = w_a.shape[1]
    tm = _row_tile(T, 16)
    row = lambda w: pl.BlockSpec((1, tm, w), lambda b, i: (b, i, 0))
    vec = pl.BlockSpec((1, W), lambda b, i: (0, 0))
    wsp = lambda w: pl.BlockSpec(w.shape, lambda b, i: (0, 0), pipeline_mode=pl.Buffered(1))
    return pl.pallas_call(
        functools.partial(_merge_kernel, head=head, D=D),
        out_shape=jax.ShapeDtypeStruct((B, T, D), BF16),
        grid=(B, T // tm),
        in_specs=[row(attn.shape[2]), row(pool.shape[2]), row(W), row(W),
                  row(W), row(W), vec, vec, row(N_BRANCH * D),
                  wsp(w_a), wsp(w_p), wsp(w_r)],
        out_specs=row(D),
        compiler_params=_cparams("parallel", "parallel"),
        name="merge",
    )(attn, pool, y_fwd, y_bwd, g, bonus, ln_w.reshape(1, W), ln_b.reshape(1, W), gates, w_a, w_p, w_r)


def _out_kernel(m_ref, w_ref, x_ref, gpost_ref, gpre_ref, ml_ref, mc_ref, wr_ref,
                x_out, h_out, aff_out, *, tm, n_lat, n_e):
    th = tm // 2
    for half in range(2):
        rows = slice(half * th, (half + 1) * th)
        is_lat, m_lat, m_ctx = _mod_rows(ml_ref, mc_ref, pl.program_id(1) * tm + half * th, th, n_lat)
        mix = _bdot(m_ref[0, rows, :], w_ref[...])
        normed = mix * lax.rsqrt(jnp.mean(mix * mix, axis=-1, keepdims=True) + NORM_EPS) * gpost_ref[...]
        x = x_ref[0, rows, :] + jnp.where(is_lat, m_lat[2:3], m_ctx[2:3]) * normed
        x_out[0, rows, :] = x
        h = _modnorm(x, gpre_ref[...], is_lat, m_lat, m_ctx, 3)
        h_out[0, rows, :] = h
        h_hi = h.astype(BF16)
        part = _bdot(h_hi, wr_ref[...]) + _bdot(h - h_hi.astype(F32), wr_ref[...])
        logits = part[:, :V7X_LANES] + part[:, V7X_LANES:]
        logits = jnp.where(_iota(logits.shape, 1) < n_e, logits, -jnp.inf)
        e = jnp.exp(logits - jnp.max(logits, axis=-1, keepdims=True))
        aff_out[0, rows, :] = e / jnp.sum(e, axis=-1, keepdims=True)


def _out_proj(merged, w_out, xs, g_post, g_pre, mod_l, w_router, n_lat):
    B, T, D = xs.shape
    E = w_router.shape[1]
    tm = _pick(T, (544, 272, 256, 128, 320, 64, 32, 16))
    wr = jnp.pad(w_router, ((0, 0), (0, V7X_LANES - E)))
    wr_hi = wr.astype(BF16)
    wr = jnp.concatenate([wr_hi, (wr - wr_hi.astype(F32)).astype(BF16)], axis=1)
    row = lambda w: pl.BlockSpec((1, tm, w), lambda b, i: (b, i, 0))
    vec = pl.BlockSpec((1, D), lambda b, i: (0, 0))
    return pl.pallas_call(
        functools.partial(_out_kernel, tm=tm, n_lat=n_lat, n_e=E),
        out_shape=(jax.ShapeDtypeStruct((B, T, D), F32), jax.ShapeDtypeStruct((B, T, D), F32),
                   jax.ShapeDtypeStruct((B, T, V7X_LANES), F32)),
        grid=(B, T // tm),
        in_specs=[row(D), pl.BlockSpec((D, D), lambda b, i: (0, 0)), row(D), vec, vec,
                  pl.BlockSpec((1, 6, D), lambda b, i: (b, 0, 0)),
                  pl.BlockSpec((1, 6, D), lambda b, i: (B, 0, 0)),
                  pl.BlockSpec((D, 2 * V7X_LANES), lambda b, i: (0, 0))],
        out_specs=(row(D), row(D), row(V7X_LANES)),
        compiler_params=_cparams("parallel", "parallel"),
        name="out_proj",
    )(merged, w_out, xs, g_post.reshape(1, D), g_pre.reshape(1, D), mod_l, mod_l, wr)


def _excl_prefix(flags, blk):
    n = flags.shape[-1]
    upper = (_iota((blk, blk), 0) < _iota((blk, blk), 1)).astype(BF16)
    outs = []
    carry = jnp.zeros((flags.shape[0], 1), F32)
    for j in range(n // blk):
        seg = flags[:, j * blk:(j + 1) * blk]
        outs.append(_bdot(seg, upper) + carry)
        carry = carry + jnp.sum(seg, axis=-1, keepdims=True)
    return jnp.concatenate(outs, axis=-1) if len(outs) > 1 else outs[0]


def _topk_kernel(aff_ref, sel_ref, idx_ref, *, cap, blk):
    aff = aff_ref[0]
    E = aff.shape[0]

    def body(i, tau):
        cand = tau | jnp.left_shift(jnp.int32(1), 30 - i)
        cnt = jnp.sum((aff >= lax.bitcast_convert_type(cand, F32)).astype(jnp.int32), axis=-1, keepdims=True)
        return jnp.where(cnt >= cap, cand, tau)

    tau = lax.bitcast_convert_type(lax.fori_loop(0, 31, body, jnp.zeros((E, 1), jnp.int32)), F32)
    gt = aff > tau
    eq = aff == tau
    need = (cap - jnp.sum(gt.astype(jnp.int32), axis=-1, keepdims=True)).astype(F32)
    eq_rank = _excl_prefix(eq.astype(F32), blk)
    sel = gt | (eq & (eq_rank < need))
    pos = _excl_prefix(sel.astype(F32), blk)
    slots = jnp.where(sel, pos.astype(jnp.int32), -1)
    sel_ref[0] = slots
    n = slots.shape[1]
    tok = _iota((8, n), 1)
    digit = _iota((8, n), 0)
    digits = jnp.where(digit == 0, tok // TOKEN_RADIX, jnp.where(digit == 1, tok % TOKEN_RADIX, 0))
    for e in range(E):
        onehot = slots[e:e + 1, :] == _iota((cap, n), 0)
        d = _bdot(digits.astype(F32), onehot.astype(F32), NT)
        idx_ref[0, e:e + 1, :] = (d[0:1] * TOKEN_RADIX + d[1:2]).astype(jnp.int32)


def _topk_slots(aff_t, cap):
    B, E, n = aff_t.shape
    assert n <= TOKEN_RADIX * 256
    blk = _pick(n, (512, 256, 128))
    return pl.pallas_call(
        functools.partial(_topk_kernel, cap=cap, blk=blk),
        out_shape=(jax.ShapeDtypeStruct((B, E, n), jnp.int32),
                   jax.ShapeDtypeStruct((B, E, cap), jnp.int32)),
        grid=(B,),
        in_specs=[pl.BlockSpec((1, E, n), lambda b: (b, 0, 0))],
        out_specs=(pl.BlockSpec((1, E, n), lambda b: (b, 0, 0)),
                   pl.BlockSpec((1, E, cap), lambda b: (b, 0, 0))),
        compiler_params=_cparams("parallel"),
        name="expert_topk",
    )(aff_t)


def _gather_kernel(idx_ref, h_ref, xe_ref, rows_ref, *, cap, n_e):
    base = (pl.program_id(0) * n_e + pl.program_id(2)) * cap

    def body(s, carry):
        t = idx_ref[base + s]
        rows_ref[pl.ds(s, 1), :] = h_ref[0, pl.ds(t, 1), :]
        return carry

    lax.fori_loop(0, cap, body, 0, unroll=8)
    xe_ref[0, 0] = rows_ref[...].astype(xe_ref.dtype)


def _gather(idx, h, row_block, n):
    B, E, cap = idx.shape
    D = h.shape[2]
    td = _pick(D, (512, 256, 128))
    grid_spec = pltpu.PrefetchScalarGridSpec(
        num_scalar_prefetch=1, grid=(B, D // td, E),
        in_specs=[pl.BlockSpec((1, n, td), lambda b, j, e, idx_ref: (b, row_block, j))],
        out_specs=pl.BlockSpec((1, 1, cap, td), lambda b, j, e, idx_ref: (b, e, 0, j)),
        scratch_shapes=[pltpu.VMEM((cap, td), F32)])
    return pl.pallas_call(
        functools.partial(_gather_kernel, cap=cap, n_e=E),
        out_shape=jax.ShapeDtypeStruct((B, E, cap, D), BF16),
        grid_spec=grid_spec,
        compiler_params=_cparams("parallel", "parallel", "arbitrary"),
        name="expert_gather",
    )(idx.reshape(-1), h)


def _ffn_kernel(*refs, n_sets):
    xe_refs = refs[:n_sets]
    wg_ref, wu_ref, wd_ref = refs[n_sets:n_sets + 3]
    ye_refs = refs[n_sets + 3:2 * n_sets + 3]
    wgb_ref, wub_ref, wdb_ref = refs[2 * n_sets + 3:]

    @pl.when(pl.program_id(2) == 0)
    def _():
        wgb_ref[...] = wg_ref[0, 0].astype(wgb_ref.dtype)
        wub_ref[...] = wu_ref[0, 0].astype(wub_ref.dtype)
        wdb_ref[...] = wd_ref[0, 0].astype(wdb_ref.dtype)

    for xe_ref, ye_ref in zip(xe_refs, ye_refs):
        xe = xe_ref[0, 0]
        gate = _bdot(xe, wgb_ref[...])
        up = _bdot(xe, wub_ref[...])
        hid = gate * _sigmoid(gate) * up
        ye_ref[0, 0, 0] = _bdot(hid, wdb_ref[...]).astype(ye_ref.dtype)


def _expert_ffn(xes, w_gate, w_up, w_down, layer):
    B, E, _, D = xes[0].shape
    FF = w_gate.shape[3]
    fs = FF // FFN_SLABS
    n_sets = len(xes)
    return pl.pallas_call(
        functools.partial(_ffn_kernel, n_sets=n_sets),
        out_shape=tuple(jax.ShapeDtypeStruct((FFN_SLABS, B, E, xe.shape[2], D), BF16) for xe in xes),
        grid=(E, FFN_SLABS, B),
        in_specs=[pl.BlockSpec((1, 1, xe.shape[2], D), lambda e, s, b: (b, e, 0, 0)) for xe in xes]
        + [pl.BlockSpec((1, 1, D, fs), lambda e, s, b: (layer, e, 0, s)),
           pl.BlockSpec((1, 1, D, fs), lambda e, s, b: (layer, e, 0, s)),
           pl.BlockSpec((1, 1, fs, D), lambda e, s, b: (layer, e, s, 0))],
        out_specs=tuple(pl.BlockSpec((1, 1, 1, xe.shape[2], D), lambda e, s, b: (s, b, e, 0, 0))
                        for xe in xes),
        scratch_shapes=[pltpu.VMEM((D, fs), BF16), pltpu.VMEM((D, fs), BF16), pltpu.VMEM((fs, D), BF16)],
        compiler_params=_cparams("parallel", "parallel", "arbitrary"),
        name="expert_ffn",
    )(*xes, w_gate, w_up, w_down)


def _scatter_kernel(selt_ref, aff_ref, ye_ref, o_ref, *, cap, n_e):
    sel_all = selt_ref[0]
    aff_all = aff_ref[0]
    slot = _iota((1, cap), 1)
    acc = None
    for e in range(n_e):
        onehot = sel_all[:, e:e + 1] == slot
        ye = ye_ref[0, 0, e].astype(F32)
        for s in range(1, ye_ref.shape[0]):
            ye = ye + ye_ref[s, 0, e].astype(F32)
        term = aff_all[:, e:e + 1] * _bdot(onehot, ye)
        acc = term if acc is None else acc + term
    o_ref[0] = acc


def _scatter(sel, aff, row_block, ye):
    B, E, n = sel.shape
    n_slab, cap, D = ye.shape[0], ye.shape[3], ye.shape[4]
    td = _pick(D, (256, 128))
    return pl.pallas_call(
        functools.partial(_scatter_kernel, cap=cap, n_e=E),
        out_shape=jax.ShapeDtypeStruct((B, n, D), F32),
        grid=(B, D // td),
        in_specs=[pl.BlockSpec((1, n, E), lambda b, j: (b, 0, 0)),
                  pl.BlockSpec((1, n, aff.shape[2]), lambda b, j: (b, row_block, 0)),
                  pl.BlockSpec((n_slab, 1, E, cap, td), lambda b, j: (0, b, 0, 0, j))],
        out_specs=pl.BlockSpec((1, n, td), lambda b, j: (b, 0, j)),
        compiler_params=_cparams("parallel", "parallel"),
        name="expert_scatter",
    )(jnp.swapaxes(sel, 1, 2), aff, ye)


def _expert_choice(hb, aff, sets, w_gate, w_up, w_down, layer):
    E = w_gate.shape[1]
    sels, xes = [], []
    for row_block, n in sets:
        aff_t = jnp.swapaxes(aff[:, row_block * n:(row_block + 1) * n, :E], 1, 2)
        sel, idx = _topk_slots(aff_t, EC_FACTOR * n // E)
        sels.append(sel)
        xes.append(_gather(idx, hb, row_block, n))
    yes = _expert_ffn(xes, w_gate, w_up, w_down, layer)
    return [_scatter(sel, aff, row_block, ye) for sel, (row_block, n), ye in zip(sels, sets, yes)]


def _rope_tables(n_lat, n_ctx, head_dim):
    rows = n_lat // GRID_W
    row = jnp.repeat(jnp.arange(rows), GRID_W).astype(F32)
    col = (jnp.arange(rows * GRID_W) % GRID_W).astype(F32)
    half = head_dim // 2
    inv = ROPE_THETA ** (-jnp.arange(0, half, 2, dtype=F32) / half)
    ar, ac = row[:, None] * inv, col[:, None] * inv
    cos = jnp.concatenate([jnp.cos(ar), jnp.cos(ar), jnp.cos(ac), jnp.cos(ac)], axis=-1)
    sin = jnp.concatenate([-jnp.sin(ar), jnp.sin(ar), -jnp.sin(ac), jnp.sin(ac)], axis=-1)
    cos = jnp.concatenate([cos, jnp.ones((n_ctx, head_dim), F32)], axis=0)
    sin = jnp.concatenate([sin, jnp.zeros((n_ctx, head_dim), F32)], axis=0)
    return cos, sin


def _pad_cols(w, mult):
    return jnp.pad(w, ((0, 0), (0, (-w.shape[1]) % mult)))


def kernel(x, c, ctx, c_ctx, w_mod, b_mod, norm_pre, norm_post, w_in, q_norm, k_norm, w_attn_o, w_pool_group, pool_scale, w_pool_o, rwkv_mu, rwkv_w0, rwkv_w2, rwkv_a0, rwkv_a2, rwkv_g2, rwkv_k_k, rwkv_k_a, rwkv_r_k, rwkv_ln_w, rwkv_ln_b, w_rwkv_o, w_out, w_router, w_exp_gate, w_exp_up, w_exp_down):
    B, n_lat, D = x.shape
    n_ctx = ctx.shape[1]
    T = n_lat + n_ctx
    depth = w_mod.shape[0]
    hd = q_norm.shape[1]
    attn_w = w_attn_o.shape[1]
    pool_w = w_pool_o.shape[1]
    rwkv_w = w_rwkv_o.shape[1]
    n_shift = rwkv_mu.shape[1]
    n_in = w_in.shape[2]
    kv_w = (n_in - attn_w - pool_w - n_shift - N_BRANCH * D) // 2
    head = rwkv_r_k.shape[2]
    col_k = attn_w
    col_v = col_k + kv_w
    col_pool = col_v + kv_w
    col_r = col_pool + pool_w
    col_gate = col_r + n_shift
    hq, hkv = attn_w // hd, kv_w // hd
    assert n_lat % (hq // hkv * hd) == 0 and pool_w % (hq // hkv * hd) == 0 and n_ctx % RWKV_CHUNK == 0

    cos, sin = _rope_tables(n_lat, n_ctx, hd)
    s_all = jnp.concatenate([jax.nn.silu(c), jax.nn.silu(c_ctx)[None]], axis=0)
    s_all = jnp.pad(s_all, ((0, (-s_all.shape[0]) % 8), (0, 0)))
    mod = _modulation(s_all, w_mod, b_mod)[:, :B + 1].reshape(depth, B + 1, 6, D)
    xs = jnp.concatenate([x, ctx], axis=1)
    (h,) = _stream_update(xs, None, None, None, mod[0], norm_pre[0, 0], mod[0], n_lat, T)

    for l in range(depth):
        keep_ctx = l < depth - 1
        wl = w_in[l]
        w_a = jnp.concatenate([wl[:, col_pool:col_r], wl[:, :col_pool]], axis=1).astype(BF16)
        w_b = _pad_cols(wl[:, col_r:col_gate], 3 * V7X_LANES).astype(BF16)
        w_c = wl[:, col_gate:].astype(BF16)
        p_a = _in_proj(h, w_a)
        p_b = _in_proj(h, w_b)
        p_c = _in_proj(h, w_c)

        qkv = (p_a, pool_w, pool_w + attn_w, pool_w + attn_w + kv_w, cos, sin, q_norm[l], k_norm[l], hd, hq, hkv)
        attn = _attention(*qkv, (0, n_lat), (0, T))
        attn = _attention(*qkv, (n_lat, n_ctx), (n_lat, n_ctx), out=attn)
        pool = _pool(p_a, 0, w_pool_group[l].astype(BF16), pool_scale[l], n_lat)
        r, v, kk, lw, cum, kd, bd, g, bonus = _rwkv_prep(
            p_b, n_lat, rwkv_w, head, rwkv_mu[l], rwkv_w0[l], rwkv_w2[l], rwkv_a0[l], rwkv_a2[l],
            rwkv_g2[l], rwkv_k_k[l], rwkv_k_a[l], rwkv_r_k[l])
        y_fwd, y_bwd = _rwkv_scan(r, v, kk, lw, cum, kd, bd, head, n_lat)
        merged = _merge(attn, pool, y_fwd, y_bwd, g, bonus, rwkv_ln_w[l], rwkv_ln_b[l], p_c,
                        w_attn_o[l].astype(BF16), w_pool_o[l].astype(BF16), w_rwkv_o[l].astype(BF16), head)
        xs, hb, aff = _out_proj(merged, w_out[l].astype(BF16), xs, norm_post[l, 0], norm_pre[l, 1],
                                mod[l], w_router[l], n_lat)

        sets = [(0, n_lat)] + ([(n_lat // n_ctx, n_ctx)] if keep_ctx else [])
        mixed = _expert_choice(hb, aff, sets, w_exp_gate, w_exp_up, w_exp_down, l)
        f_lat = mixed[0]
        if keep_ctx:
            f_ctx = mixed[1]
            xs, h = _stream_update(xs, f_lat, f_ctx, norm_post[l, 1], mod[l], norm_pre[l + 1, 0],
                                   mod[l + 1], n_lat, T)
        else:
            (xs,) = _stream_update(xs, f_lat, None, norm_post[l, 1], mod[l], None, None, n_lat, n_lat)
    return xs
```
